```python
import jax
import jax.numpy as jnp
from jax import lax
import numpy as np

D_MODEL = 1024
BATCH = 8
SEQ = 2048
DEPTH = 2
DEC_BATCH = 128
DEC_SEQ = 1
PAST_LEN = 16384
PAGE_SIZE = 128

N_EVEN = (DEPTH + 1) // 2
N_ODD = DEPTH // 2
D_PLE = 256
EPS = 1e-6
CONV_W = 4
D_A = 512
N_BLK_A = 8
BW_A = D_A // N_BLK_A
LRU_C = 8.0
H_B = 8
HD_B = 64
D_B = H_B * HD_B
R_W = 64
R_A = 64
R_G = 128
D_BPROJ = 3 * D_B + R_W + R_A + R_G
GN_EPS_B = 64e-5
IN_AB = 2 * D_A + D_BPROJ
H_C = 8
P_C = 64
D_C = H_C * P_C
G_C = 2
HG_C = H_C // G_C
N_C = 128
D_XBC = D_C + 2 * G_C * N_C
CHUNK_C = 64
H_D = 4
DK_D = 64
DV_D = 128
K_D = H_D * DK_D
V_D = H_D * DV_D
R_ALPHA = 16
GATE_NORM = 16.0
CHUNK_D = 32
IN_CD = D_C + D_XBC + H_C + 2 * K_D + 2 * V_D + R_ALPHA
D_FF = 2816
N_EXP = 8
TOP_K = 2
D_FF_E = 1408

kernel_name = 'hybrid_rglru_rwkv7_ssd_gla_step'


def rmsnorm(x, g):
    xf = x.astype(jnp.float32)
    y = xf * lax.rsqrt(jnp.mean(xf * xf, axis=-1, keepdims=True) + EPS)
    return (y * g.astype(jnp.float32)).astype(x.dtype)


def split_cols(t, sizes):
    idx = [int(s) for s in np.cumsum(sizes)[:-1]]
    return jnp.split(t, idx, axis=-1)


def causal_conv(x, buf, w, b):
    L = x.shape[1]
    xx = jnp.concatenate([buf.astype(x.dtype), x], axis=1)
    y = b + sum(xx[:, k:k + L] * w[k] for k in range(CONV_W))
    return y, xx[:, L:]


def to_chunks(t, q):
    bsz, L = t.shape[:2]
    n = -(-L // q)
    t = jnp.pad(t, [(0, 0), (0, n * q - L)] + [(0, 0)] * (t.ndim - 2))
    return jnp.moveaxis(t.reshape((bsz, n, q) + t.shape[2:]), 1, 0)


def from_chunks(t, L):
    n, bsz, q = t.shape[:3]
    return jnp.moveaxis(t, 0, 1).reshape((bsz, n * q) + t.shape[3:])[:, :L]


def rglru(x, h0, w_r, b_r, w_i, b_i, lam):
    bsz, L, _ = x.shape
    xf = x.astype(jnp.float32)
    xb = xf.reshape(bsz, L, N_BLK_A, BW_A)
    r = jax.nn.sigmoid(jnp.einsum('blni,nij->blnj', xb, w_r).reshape(bsz, L, D_A) + b_r)
    i = jax.nn.sigmoid(jnp.einsum('blni,nij->blnj', xb, w_i).reshape(bsz, L, D_A) + b_i)
    log_a = -LRU_C * r * jax.nn.softplus(-lam)
    a = jnp.exp(log_a)
    u = jnp.sqrt(-jnp.expm1(2.0 * log_a)) * (i * xf)

    def step(h, au):
        h = au[0] * h + au[1]
        return h, h

    hT, hs = lax.scan(step, h0.astype(jnp.float32), (a.swapaxes(0, 1), u.swapaxes(0, 1)))
    return hs.swapaxes(0, 1), hT


def rwkv7(proj, shift0, s0, mu, w0, w2, a0, a2, g2, k_k, k_a, r_k, ln_w, ln_b):
    bsz, L, _ = proj.shape
    prev = jnp.concatenate([shift0[:, None].astype(proj.dtype), proj[:, :-1]], axis=1)
    xs = (proj + (prev - proj) * mu).astype(jnp.float32)
    r, k, v, wl, al, gl = split_cols(xs, [D_B, D_B, D_B, R_W, R_A, R_G])
    w_log = -jax.nn.softplus(-(w0 + jnp.tanh(wl) @ w2)) - 0.5
    decay = jnp.exp(-jnp.exp(w_log))
    a = jax.nn.sigmoid(a0 + al @ a2)
    g = jax.nn.sigmoid(gl) @ g2
    hd = lambda t: t.reshape(bsz, L, H_B, HD_B)
    kk = hd(k * k_k)
    kk = kk * lax.rsqrt(jnp.maximum(jnp.sum(kk * kk, axis=-1, keepdims=True), 1e-12))
    k = hd(k * (1.0 + (a - 1.0) * k_a))
    r, v, decay, a = hd(r), hd(v), hd(decay), hd(a)

    def step(S, inp):
        r_t, k_t, v_t, d_t, kk_t, a_t = inp
        S = (S * d_t[:, :, None, :]
             - jnp.einsum('bhvk,bhk->bhv', S, kk_t)[..., None] * (kk_t * a_t)[:, :, None, :]
             + v_t[..., None] * k_t[:, :, None, :])
        return S, jnp.einsum('bhvk,bhk->bhv', S, r_t)

    sT, y = lax.scan(step, s0.astype(jnp.float32),
                     tuple(t.swapaxes(0, 1) for t in (r, k, v, decay, kk, a)))
    y = y.swapaxes(0, 1)
    mean = jnp.mean(y, axis=-1, keepdims=True)
    var = jnp.mean(jnp.square(y - mean), axis=-1, keepdims=True)
    y = ((y - mean) * lax.rsqrt(var + GN_EPS_B)).reshape(bsz, L, D_B) * ln_w + ln_b
    y = y + (jnp.sum(r * k * r_k, axis=-1, keepdims=True) * v).reshape(bsz, L, D_B)
    return y * g, proj[:, -1], sT


def ssd_scan(x, dt, A, Bm, Cm, s0):
    bsz, L = x.shape[:2]
    q = min(CHUNK_C, L)
    x = x.reshape(bsz, L, G_C, HG_C, P_C)
    dt = dt.reshape(bsz, L, G_C, HG_C)
    la = dt * A.reshape(G_C, HG_C)
    causal = jnp.tril(jnp.ones((q, q), dtype=bool))[None, :, :, None, None]

    def step(S, inp):
        x_c, dt_c, la_c, b_c, c_c = inp
        cum = jnp.cumsum(la_c, axis=1)
        seg = jnp.exp(jnp.where(causal, cum[:, :, None] - cum[:, None, :], -jnp.inf))
        cb = jnp.einsum('bign,bjgn->bijg', c_c, b_c)
        y = jnp.einsum('bijg,bijgh,bjgh,bjghp->bighp', cb, seg, dt_c, x_c)
        y = y + jnp.einsum('bign,bghpn->bighp', c_c, S) * jnp.exp(cum)[..., None]
        w_end = jnp.exp(cum[:, -1:] - cum) * dt_c
        S = (S * jnp.exp(cum[:, -1])[..., None, None]
             + jnp.einsum('bjgh,bjghp,bjgn->bghpn', w_end, x_c, b_c))
        return S, y

    sT, y = lax.scan(step, s0.reshape(bsz, G_C, HG_C, P_C, N_C),
                     tuple(to_chunks(t, q) for t in (x, dt, la, Bm, Cm)))
    return from_chunks(y, L).reshape(bsz, L, H_C, P_C), sT.reshape(bsz, H_C, P_C, N_C)


def gla_scan(q, k, v, la, s0):
    bsz, L = q.shape[:2]
    c = min(CHUNK_D, L)
    causal = jnp.tril(jnp.ones((c, c), dtype=bool))

    def step(S, inp):
        q_c, k_c, v_c, la_c = inp
        b = jnp.cumsum(la_c, axis=1)
        qe = q_c * jnp.exp(b)
        att = jnp.where(causal, jnp.einsum('bihd,bjhd->bhij', qe, k_c * jnp.exp(-b)), 0.0)
        o = jnp.einsum('bhij,bjhv->bihv', att, v_c) + jnp.einsum('bihd,bhdv->bihv', qe, S)
        b_end = b[:, -1]
        S = (S * jnp.exp(b_end)[..., None]
             + jnp.einsum('bjhd,bjhv->bhdv', k_c * jnp.exp(b_end[:, None] - b), v_c))
        return S, o

    sT, o = lax.scan(step, s0, tuple(to_chunks(t, c) for t in (q, k, v, la)))
    return from_chunks(o, L), sT


def mixer_ab(u, conv_buf, h0, shift0, wkv0, w_in, w_out, conv_w, conv_b, w_r, b_r, w_i, b_i, lam,
             mu, w0, w2, a0, a2, g2, k_k, k_a, r_k, ln_w, ln_b):
    xa, ga, pb = split_cols(u @ w_in, [D_A, D_A, D_BPROJ])
    xa, conv_new = causal_conv(xa, conv_buf, conv_w, conv_b)
    ha, h_new = rglru(xa, h0, w_r, b_r, w_i, b_i, lam)
    y_a = jax.nn.gelu(ga.astype(jnp.float32)) * ha
    y_b, shift_new, wkv_new = rwkv7(pb, shift0, wkv0, mu, w0, w2, a0, a2, g2, k_k, k_a, r_k, ln_w, ln_b)
    out = jnp.concatenate([y_a, y_b], axis=-1).astype(u.dtype) @ w_out
    return out, conv_new, h_new, shift_new, wkv_new


def mixer_cd(u, conv_buf, ssd0, gla0, w_in, w_out, conv_w, conv_b, dt_bias, a_log, d_skip, ssd_norm,
             alpha_up, alpha_b, gla_norm):
    bsz, L, _ = u.shape
    z, xbc, dt_raw, q, k, v, g, al = split_cols(
        u @ w_in, [D_C, D_XBC, H_C, K_D, K_D, V_D, V_D, R_ALPHA])
    xbc, conv_new = causal_conv(xbc, conv_buf, conv_w, conv_b)
    xbc = jax.nn.silu(xbc.astype(jnp.float32))
    xs, Bm, Cm = split_cols(xbc, [D_C, G_C * N_C, G_C * N_C])
    xs = xs.reshape(bsz, L, H_C, P_C)
    dt = jax.nn.softplus(dt_raw.astype(jnp.float32) + dt_bias)
    A = -jnp.exp(a_log.astype(jnp.float32))
    y, ssd_new = ssd_scan(xs, dt, A, Bm.reshape(bsz, L, G_C, N_C), Cm.reshape(bsz, L, G_C, N_C),
                          ssd0.astype(jnp.float32))
    y = (y + d_skip[:, None] * xs).reshape(bsz, L, D_C)
    y_c = rmsnorm(y * jax.nn.silu(z.astype(jnp.float32)), ssd_norm)
    la = jax.nn.log_sigmoid((al @ alpha_up + alpha_b).astype(jnp.float32)) / GATE_NORM
    heads = lambda t, d: t.astype(jnp.float32).reshape(bsz, L, H_D, d)
    o, gla_new = gla_scan(heads(q, DK_D) * DK_D ** -0.5, heads(k, DK_D), heads(v, DV_D),
                          heads(la, DK_D), gla0.astype(jnp.float32))
    y_d = rmsnorm(o, gla_norm).reshape(bsz, L, V_D) * jax.nn.silu(g.astype(jnp.float32))
    out = jnp.concatenate([y_c, y_d], axis=-1).astype(u.dtype) @ w_out
    return out, conv_new, ssd_new, gla_new


def swiglu(x, wg, wu, wd):
    return (jax.nn.silu(x @ wg) * (x @ wu)) @ wd


def moe_swiglu(x, w_router, wg, wu, wd):
    bsz, L, D = x.shape
    xt = x.reshape(bsz * L, D)
    probs = jax.nn.softmax((xt @ w_router).astype(jnp.float32), axis=-1)
    top_p, top_i = lax.top_k(probs, TOP_K)
    top_p = top_p / jnp.sum(top_p, axis=-1, keepdims=True)
    gates = jnp.einsum('tk,tke->te', top_p,
                       jax.nn.one_hot(top_i, N_EXP, dtype=jnp.float32)).astype(x.dtype)
    out = jnp.zeros_like(xt)
    for e in range(N_EXP):
        out = out + gates[:, e:e + 1] * swiglu(xt, wg[e], wu[e], wd[e])
    return out.reshape(bsz, L, D)


def trunk(x, p, states, prm):
    lru_conv, lru_h, rwkv_shift, rwkv_wkv, ssd_conv, ssd_st, gla_st = states
    new_ab, new_cd = [], []
    h = x
    for i in range(DEPTH):
        j = i // 2
        u = rmsnorm(h, prm['norm_mix'][i])
        if i % 2 == 0:
            mix, *st = mixer_ab(
                u, lru_conv[j], lru_h[j], rwkv_shift[j], rwkv_wkv[j],
                prm['w_in_ab'][j], prm['w_out_ab'][j], prm['lru_conv_w'][j], prm['lru_conv_b'][j],
                prm['lru_w_r'][j], prm['lru_b_r'][j], prm['lru_w_i'][j], prm['lru_b_i'][j],
                prm['lru_lambda'][j], prm['rwkv_mu'][j], prm['rwkv_w0'][j], prm['rwkv_w2'][j],
                prm['rwkv_a0'][j], prm['rwkv_a2'][j], prm['rwkv_g2'][j], prm['rwkv_k_k'][j],
                prm['rwkv_k_a'][j], prm['rwkv_r_k'][j], prm['rwkv_ln_w'][j], prm['rwkv_ln_b'][j])
            new_ab.append(st)
            h = h + mix
            h = h + swiglu(rmsnorm(h, prm['norm_ffn'][i]), prm['ffn_w_gate'][j],
                           prm['ffn_w_up'][j], prm['ffn_w_down'][j])
        else:
            mix, *st = mixer_cd(
                u, ssd_conv[j], ssd_st[j], gla_st[j],
                prm['w_in_cd'][j], prm['w_out_cd'][j], prm['ssd_conv_w'][j], prm['ssd_conv_b'][j],
                prm['ssd_dt_bias'][j], prm['ssd_a_log'][j], prm['ssd_d'][j], prm['ssd_norm_w'][j],
                prm['gla_alpha_up'][j], prm['gla_alpha_b'][j], prm['gla_norm_w'][j])
            new_cd.append(st)
            h = h + mix
            h = h + moe_swiglu(rmsnorm(h, prm['norm_ffn'][i]), prm['moe_router'][j],
                               prm['moe_w_gate'][j], prm['moe_w_up'][j], prm['moe_w_down'][j])
        gate = jax.nn.sigmoid(rmsnorm(h, prm['norm_ple'][i]) @ prm['w_ple_gate'][i])
        h = h + gate * (p[i] @ prm['w_ple_proj'][i])
    y = rmsnorm(h, prm['norm_final'])
    out_ab = [jnp.stack([st[n] for st in new_ab]).astype(ref.dtype)
              for n, ref in enumerate((lru_conv, lru_h, rwkv_shift, rwkv_wkv))]
    out_cd = [jnp.stack([st[n] for st in new_cd]).astype(ref.dtype)
              for n, ref in enumerate((ssd_conv, ssd_st, gla_st))]
    return y, out_ab + out_cd


def setup_inputs(seed: int = 0) -> dict:
    key = jax.random.key(seed)
    keys = jax.random.split(key, 80)
    count = [0]

    def nk():
        count[0] += 1
        return keys[count[0] - 1]

    def nrm(shape, scale):
        return jax.random.normal(nk(), shape, jnp.float32) * scale

    def unif(shape, lo, hi):
        return jax.random.uniform(nk(), shape, jnp.float32, lo, hi)

    def gain(shape):
        return 1.0 + nrm(shape, 0.02)

    D = D_MODEL
    a_init = unif((N_EVEN, D_A), 0.9, 0.999) ** (1.0 / LRU_C)
    lru_lambda = jnp.log(a_init) - jnp.log1p(-a_init)
    dt0 = jnp.exp(unif((N_ODD, H_C), float(np.log(1e-3)), float(np.log(1e-1))))
    ssd_dt_bias = dt0 + jnp.log(-jnp.expm1(-dt0))
    return {
        'x_prompt': nrm((BATCH, SEQ, D), 1.0),
        'x_sample': nrm((DEC_BATCH, DEC_SEQ, D), 1.0),
        'p_prompt': nrm((DEPTH, BATCH, SEQ, D_PLE), 1.0),
        'p_sample': nrm((DEPTH, DEC_BATCH, DEC_SEQ, D_PLE), 1.0),
        'state_lru_conv': nrm((N_EVEN, DEC_BATCH, CONV_W - 1, D_A), 1.0),
        'state_lru_h': nrm((N_EVEN, DEC_BATCH, D_A), 0.5),
        'state_rwkv_shift': nrm((N_EVEN, DEC_BATCH, D_BPROJ), 1.0),
        'state_rwkv_wkv': nrm((N_EVEN, DEC_BATCH, H_B, HD_B, HD_B), 0.3),
        'state_ssd_conv': nrm((N_ODD, DEC_BATCH, CONV_W - 1, D_XBC), 1.0),
        'state_ssd': nrm((N_ODD, DEC_BATCH, H_C, P_C, N_C), 0.1),
        'state_gla': nrm((N_ODD, DEC_BATCH, H_D, DK_D, DV_D), 0.3),
        'norm_mix': gain((DEPTH, D)),
        'norm_ffn': gain((DEPTH, D)),
        'norm_ple': gain((DEPTH, D)),
        'w_ple_gate': nrm((DEPTH, D, D), D ** -0.5),
        'w_ple_proj': nrm((DEPTH, D_PLE, D), D_PLE ** -0.5),
        'norm_final': gain((D,)),
        'w_in_ab': nrm((N_EVEN, D, IN_AB), D ** -0.5),
        'w_out_ab': nrm((N_EVEN, D_A + D_B, D), (D_A + D_B) ** -0.5),
        'lru_conv_w': nrm((N_EVEN, CONV_W, D_A), CONV_W ** -0.5),
        'lru_conv_b': nrm((N_EVEN, D_A), 0.01),
        'lru_w_r': nrm((N_EVEN, N_BLK_A, BW_A, BW_A), BW_A ** -0.5),
        'lru_b_r': nrm((N_EVEN, D_A), 0.01),
        'lru_w_i': nrm((N_EVEN, N_BLK_A, BW_A, BW_A), BW_A ** -0.5),
        'lru_b_i': nrm((N_EVEN, D_A), 0.01),
        'lru_lambda': lru_lambda,
        'rwkv_mu': unif((N_EVEN, D_BPROJ), 0.0, 1.0),
        'rwkv_w0': unif((N_EVEN, D_B), -6.0, -1.0),
        'rwkv_w2': nrm((N_EVEN, R_W, D_B), 0.1),
        'rwkv_a0': nrm((N_EVEN, D_B), 0.1),
        'rwkv_a2': nrm((N_EVEN, R_A, D_B), 0.1),
        'rwkv_g2': nrm((N_EVEN, R_G, D_B), R_G ** -0.5),
        'rwkv_k_k': 0.85 + nrm((N_EVEN, D_B), 0.02),
        'rwkv_k_a': gain((N_EVEN, D_B)),
        'rwkv_r_k': nrm((N_EVEN, H_B, HD_B), 0.1),
        'rwkv_ln_w': gain((N_EVEN, D_B)),
        'rwkv_ln_b': nrm((N_EVEN, D_B), 0.01),
        'ffn_w_gate': nrm((N_EVEN, D, D_FF), D ** -0.5),
        'ffn_w_up': nrm((N_EVEN, D, D_FF), D ** -0.5),
        'ffn_w_down': nrm((N_EVEN, D_FF, D), D_FF ** -0.5),
        'w_in_cd': nrm((N_ODD, D, IN_CD), D ** -0.5),
        'w_out_cd': nrm((N_ODD, D_C + V_D, D), (D_C + V_D) ** -0.5),
        'ssd_conv_w': nrm((N_ODD, CONV_W, D_XBC), CONV_W ** -0.5),
        'ssd_conv_b': nrm((N_ODD, D_XBC), 0.01),
        'ssd_dt_bias': ssd_dt_bias,
        'ssd_a_log': jnp.log(unif((N_ODD, H_C), 1.0, 16.0)),
        'ssd_d': gain((N_ODD, H_C)),
        'ssd_norm_w': gain((N_ODD, D_C)),
        'gla_alpha_up': nrm((N_ODD, R_ALPHA, K_D), R_ALPHA ** -0.5),
        'gla_alpha_b': nrm((N_ODD, K_D), 0.1),
        'gla_norm_w': gain((N_ODD, DV_D)),
        'moe_router': nrm((N_ODD, D, N_EXP), D ** -0.5),
        'moe_w_gate': nrm((N_ODD, N_EXP, D, D_FF_E), D ** -0.5),
        'moe_w_up': nrm((N_ODD, N_EXP, D, D_FF_E), D ** -0.5),
        'moe_w_down': nrm((N_ODD, N_EXP, D_FF_E, D), D_FF_E ** -0.5),
    }


def reference(x_prompt, x_sample, p_prompt, p_sample, state_lru_conv, state_lru_h, state_rwkv_shift,
              state_rwkv_wkv, state_ssd_conv, state_ssd, state_gla, norm_mix, norm_ffn, norm_ple,
              w_ple_gate, w_ple_proj, norm_final, w_in_ab, w_out_ab, lru_conv_w, lru_conv_b, lru_w_r,
              lru_b_r, lru_w_i, lru_b_i, lru_lambda, rwkv_mu, rwkv_w0, rwkv_w2, rwkv_a0, rwkv_a2,
              rwkv_g2, rwkv_k_k, rwkv_k_a, rwkv_r_k, rwkv_ln_w, rwkv_ln_b, ffn_w_gate, ffn_w_up,
              ffn_w_down, w_in_cd, w_out_cd, ssd_conv_w, ssd_conv_b, ssd_dt_bias, ssd_a_log, ssd_d,
              ssd_norm_w, gla_alpha_up, gla_alpha_b, gla_norm_w, moe_router, moe_w_gate, moe_w_up,
              moe_w_down):
    prm = dict(
        norm_mix=norm_mix, norm_ffn=norm_ffn, norm_ple=norm_ple, w_ple_gate=w_ple_gate,
        w_ple_proj=w_ple_proj, norm_final=norm_final, w_in_ab=w_in_ab, w_out_ab=w_out_ab,
        lru_conv_w=lru_conv_w, lru_conv_b=lru_conv_b, lru_w_r=lru_w_r, lru_b_r=lru_b_r,
        lru_w_i=lru_w_i, lru_b_i=lru_b_i, lru_lambda=lru_lambda, rwkv_mu=rwkv_mu, rwkv_w0=rwkv_w0,
        rwkv_w2=rwkv_w2, rwkv_a0=rwkv_a0, rwkv_a2=rwkv_a2, rwkv_g2=rwkv_g2, rwkv_k_k=rwkv_k_k,
        rwkv_k_a=rwkv_k_a, rwkv_r_k=rwkv_r_k, rwkv_ln_w=rwkv_ln_w, rwkv_ln_b=rwkv_ln_b,
        ffn_w_gate=ffn_w_gate, ffn_w_up=ffn_w_up, ffn_w_down=ffn_w_down, w_in_cd=w_in_cd,
        w_out_cd=w_out_cd, ssd_conv_w=ssd_conv_w, ssd_conv_b=ssd_conv_b, ssd_dt_bias=ssd_dt_bias,
        ssd_a_log=ssd_a_log, ssd_d=ssd_d, ssd_norm_w=ssd_norm_w, gla_alpha_up=gla_alpha_up,
        gla_alpha_b=gla_alpha_b, gla_norm_w=gla_norm_w, moe_router=moe_router,
        moe_w_gate=moe_w_gate, moe_w_up=moe_w_up, moe_w_down=moe_w_down)
    states_s = (state_lru_conv, state_lru_h, state_rwkv_shift, state_rwkv_wkv,
                state_ssd_conv, state_ssd, state_gla)
    n_prompt = x_prompt.shape[0]
    states_p = tuple(jnp.zeros((s.shape[0], n_prompt) + s.shape[2:], s.dtype) for s in states_s)
    y_prompt, st_p = trunk(x_prompt, p_prompt, states_p, prm)
    y_sample, st_s = trunk(x_sample, p_sample, states_s, prm)
    lc_p, lh_p, rs_p, rw_p, sc_p, ss_p, gs_p = st_p
    lc_s, lh_s, rs_s, rw_s, sc_s, ss_s, gs_s = st_s
    return (y_prompt, y_sample, lc_p, lh_p, rs_p, rw_p, sc_p, ss_p, gs_p,
            lc_s, lh_s, rs_s, rw_s, sc_s, ss_s, gs_s)
```

```python
import functools
import math

import jax
import jax.numpy as jnp
from jax import lax
from jax.experimental import pallas as pl
from jax.experimental.pallas import tpu as pltpu

F32, BF16 = jnp.float32, jnp.bfloat16

D_MODEL = 1024
DEPTH = 2
D_PLE = 256
EPS = 1e-6
CONV_W = 4
D_A = 512
N_BLK_A = 8
BW_A = D_A // N_BLK_A
LRU_C = 8.0
H_B = 8
HD_B = 64
D_B = H_B * HD_B
R_W = 64
R_A = 64
R_G = 128
D_BPROJ = 3 * D_B + R_W + R_A + R_G
GN_EPS_B = 64e-5
IN_AB = 2 * D_A + D_BPROJ
H_C = 8
P_C = 64
D_C = H_C * P_C
G_C = 2
HG_C = H_C // G_C
N_C = 128
D_XBC = D_C + 2 * G_C * N_C
H_D = 4
DK_D = 64
DV_D = 128
K_D = H_D * DK_D
V_D = H_D * DV_D
R_ALPHA = 16
GATE_NORM = 16.0
D_FF = 2816
N_EXP = 8
D_FF_E = 1408
IN_CD_PAD = D_C + D_XBC + 2 * K_D + 2 * V_D + 128

LANES = 128
SUBLANES = 8
VMEM_LIMIT = 56 * 1024 * 1024

G_ROWS = SUBLANES
T_CHUNK = 16
TL_BLOCK = 64
TM_TOK = 1024
TM_FFN = 512
LRU_TL = 256

NN = ((1,), (0,))
NT = ((1,), (1,))
TN = ((0,), (0,))


def _cparams(sem):
    return pltpu.CompilerParams(dimension_semantics=sem, vmem_limit_bytes=VMEM_LIMIT)


def _rms(x, g):
    return x * lax.rsqrt(jnp.mean(x * x, axis=-1, keepdims=True) + EPS) * g


def _sigmoid(x):
    return jax.nn.sigmoid(x)


def _silu(x):
    return x * jax.nn.sigmoid(x)


def _softplus(x):
    return jnp.maximum(x, 0.0) + jnp.log1p(jnp.exp(-jnp.abs(x)))


def _gelu_tanh(x):
    c = math.sqrt(2.0 / math.pi)
    return 0.5 * x * (1.0 + jnp.tanh(c * (x + 0.044715 * (x * x * x))))


def _d(a, b, dims=NN):
    return lax.dot_general(a, b, (dims, ((), ())), preferred_element_type=F32)


def _mm(a, b, dims=NN):
    return _d(a.astype(BF16), b.astype(BF16), dims)


def _split2(x):
    hi = x.astype(BF16)
    lo = (x - hi.astype(F32)).astype(BF16)
    return hi, lo


def _mm3(a, b, dims=NN):
    ah, al = _split2(a)
    bh, bl = _split2(b)
    return _d(ah, bh, dims) + (_d(ah, bl, dims) + _d(al, bh, dims))


def _mm_sel(sel, x, pieces, sel_left=True, dims=NN):
    out = None
    rest = x
    for i in range(pieces):
        p = rest.astype(BF16)
        if i + 1 < pieces:
            rest = rest - p.astype(F32)
        t = _d(sel, p, dims) if sel_left else _d(p, sel, dims)
        out = t if out is None else out + t
    return out


def _chunk_masks(G, T):
    R = G * T
    ri = lax.broadcasted_iota(jnp.int32, (R, R), 0)
    ci = lax.broadcasted_iota(jnp.int32, (R, R), 1)
    same = (ri // T) == (ci // T)
    incl = same & (ci <= ri)
    strict = same & (ci < ri)
    last = ci == (ri // T) * T + (T - 1)
    return incl, strict, last


def _to_bf16_mask(m):
    return jnp.where(m, 1.0, 0.0).astype(BF16)


def _tile_sel(W, G):
    r = lax.broadcasted_iota(jnp.int32, (W, G * W), 0)
    c = lax.broadcasted_iota(jnp.int32, (W, G * W), 1)
    return _to_bf16_mask((c % W) == r)


def _batch_lane_mask(n_stack, G, T, W):
    R = G * T
    r = lax.broadcasted_iota(jnp.int32, (n_stack * R, G * W), 0)
    c = lax.broadcasted_iota(jnp.int32, (n_stack * R, G * W), 1)
    return ((r % R) // T) == (c // W)


def _head_sum_sel(width, head):
    r = lax.broadcasted_iota(jnp.int32, (width, width), 0)
    c = lax.broadcasted_iota(jnp.int32, (width, width), 1)
    return _to_bf16_mask((r // head) == (c // head))


def _norm_mm_kernel(x_ref, g_ref, w_ref, o_ref, xn_ref):
    @pl.when(pl.program_id(1) == 0)
    def _():
        xn_ref[...] = _rms(x_ref[...], g_ref[...]).astype(BF16)

    o_ref[...] = _d(xn_ref[...], w_ref[...])


def _norm_mm(x, g, w, tn):
    M, K = x.shape
    N = w.shape[1]
    tm = min(TM_TOK, M)
    return pl.pallas_call(
        _norm_mm_kernel,
        grid=(M // tm, N // tn),
        in_specs=[pl.BlockSpec((tm, K), lambda i, j: (i, 0)),
                  pl.BlockSpec((1, K), lambda i, j: (0, 0)),
                  pl.BlockSpec((K, tn), lambda i, j: (0, j))],
        out_specs=pl.BlockSpec((tm, tn), lambda i, j: (i, j)),
        out_shape=jax.ShapeDtypeStruct((M, N), F32),
        scratch_shapes=[pltpu.VMEM((tm, K), BF16)],
        compiler_params=_cparams(("parallel", "arbitrary")),
    )(x, g, w)


def _out_proj_kernel(n_in, *refs):
    h_ref = refs[0]
    ys = refs[1:1 + n_in]
    ws = refs[1 + n_in:1 + 2 * n_in]
    o_ref = refs[1 + 2 * n_in]
    acc = h_ref[...]
    for y_ref, w_ref in zip(ys, ws):
        acc = acc + _mm(y_ref[...], w_ref[...])
    o_ref[...] = acc


def _out_proj(h, ys, ws):
    M, D = h.shape
    tm = min(TM_TOK, M)
    n_in = len(ys)
    in_specs = [pl.BlockSpec((tm, D), lambda i: (i, 0))]
    in_specs += [pl.BlockSpec((tm, y.shape[1]), lambda i: (i, 0)) for y in ys]
    in_specs += [pl.BlockSpec(w.shape, lambda i: (0, 0)) for w in ws]
    return pl.pallas_call(
        functools.partial(_out_proj_kernel, n_in),
        grid=(M // tm,),
        in_specs=in_specs,
        out_specs=pl.BlockSpec((tm, D), lambda i: (i, 0)),
        out_shape=jax.ShapeDtypeStruct((M, D), F32),
        compiler_params=_cparams(("parallel",)),
    )(h, *ys, *ws)


def _ffn_kernel(h_ref, g_ref, wg_ref, wu_ref, wd_ref, o_ref, xn_ref, acc_ref):
    j = pl.program_id(1)

    @pl.when(j == 0)
    def _():
        xn_ref[...] = _rms(h_ref[...], g_ref[...]).astype(BF16)
        acc_ref[...] = jnp.zeros_like(acc_ref)

    xn = xn_ref[...]
    act = _silu(_d(xn, wg_ref[...])) * _d(xn, wu_ref[...])
    acc_ref[...] += _mm(act, wd_ref[...])

    @pl.when(j == pl.num_programs(1) - 1)
    def _():
        o_ref[...] = h_ref[...] + acc_ref[...]


def _ffn(h, g, wg, wu, wd, tf):
    M, D = h.shape
    FF = wg.shape[1]
    tm = min(TM_FFN, M)
    return pl.pallas_call(
        _ffn_kernel,
        grid=(M // tm, FF // tf),
        in_specs=[pl.BlockSpec((tm, D), lambda i, j: (i, 0)),
                  pl.BlockSpec((1, D), lambda i, j: (0, 0)),
                  pl.BlockSpec((D, tf), lambda i, j: (0, j)),
                  pl.BlockSpec((D, tf), lambda i, j: (0, j)),
                  pl.BlockSpec((tf, D), lambda i, j: (j, 0))],
        out_specs=pl.BlockSpec((tm, D), lambda i, j: (i, 0)),
        out_shape=jax.ShapeDtypeStruct((M, D), F32),
        scratch_shapes=[pltpu.VMEM((tm, D), BF16), pltpu.VMEM((tm, D), F32)],
        compiler_params=_cparams(("parallel", "arbitrary")),
    )(h, g, wg, wu, wd)


def _router_kernel(h_ref, g_ref, wr_ref, gates_ref):
    xn = _rms(h_ref[...], g_ref[...])
    logits = _mm3(xn, wr_ref[...])
    lane = lax.broadcasted_iota(jnp.int32, logits.shape, 1)
    valid = lane < N_EXP
    logits = jnp.where(valid, logits, -1e30)
    m = jnp.max(logits, axis=-1, keepdims=True)
    e = jnp.where(valid, jnp.exp(logits - m), 0.0)
    p = e / jnp.sum(e, axis=-1, keepdims=True)
    lane_f = lane.astype(F32)
    m1 = jnp.max(p, axis=-1, keepdims=True)
    i1 = jnp.min(jnp.where(p == m1, lane_f, float(LANES)), axis=-1, keepdims=True)
    p2 = jnp.where((lane_f == i1) | (lane >= N_EXP), -1.0, p)
    m2 = jnp.max(p2, axis=-1, keepdims=True)
    i2 = jnp.min(jnp.where(p2 == m2, lane_f, float(LANES)), axis=-1, keepdims=True)
    tot = m1 + m2
    gates_ref[...] = jnp.where(lane_f == i1, m1 / tot, jnp.where(lane_f == i2, m2 / tot, 0.0))


def _router(h, g, wr):
    M, D = h.shape
    tm = min(TM_TOK, M)
    return pl.pallas_call(
        _router_kernel,
        grid=(M // tm,),
        in_specs=[pl.BlockSpec((tm, D), lambda i: (i, 0)),
                  pl.BlockSpec((1, D), lambda i: (0, 0)),
                  pl.BlockSpec((D, LANES), lambda i: (0, 0))],
        out_specs=pl.BlockSpec((tm, LANES), lambda i: (i, 0)),
        out_shape=jax.ShapeDtypeStruct((M, LANES), F32),
        compiler_params=_cparams(("parallel",)),
    )(h, g, wr)


def _moe_kernel(h_ref, g_ref, gates_ref, wg_ref, wu_ref, wd_ref, o_ref, xn_ref, acc_ref):
    e = pl.program_id(1)
    j = pl.program_id(2)

    @pl.when((e == 0) & (j == 0))
    def _():
        xn_ref[...] = _rms(h_ref[...], g_ref[...]).astype(BF16)
        acc_ref[...] = jnp.zeros_like(acc_ref)

    gates = gates_ref[...]
    lane = lax.broadcasted_iota(jnp.int32, gates.shape, 1)
    gcol = jnp.sum(jnp.where(lane == e, gates, 0.0), axis=-1, keepdims=True)
    xn = xn_ref[...]
    act = _silu(_d(xn, wg_ref[...])) * _d(xn, wu_ref[...])
    acc_ref[...] += _mm(act * gcol, wd_ref[...])

    @pl.when((e == pl.num_programs(1) - 1) & (j == pl.num_programs(2) - 1))
    def _():
        o_ref[...] = h_ref[...] + acc_ref[...]


def _moe(h, g, gates, wg, wu, wd, tf):
    M, D = h.shape
    E, _, FF = wg.shape
    tm = min(TM_FFN, M)
    return pl.pallas_call(
        _moe_kernel,
        grid=(M // tm, E, FF // tf),
        in_specs=[pl.BlockSpec((tm, D), lambda i, e, j: (i, 0)),
                  pl.BlockSpec((1, D), lambda i, e, j: (0, 0)),
                  pl.BlockSpec((tm, LANES), lambda i, e, j: (i, 0)),
                  pl.BlockSpec((None, D, tf), lambda i, e, j: (e, 0, j)),
                  pl.BlockSpec((None, D, tf), lambda i, e, j: (e, 0, j)),
                  pl.BlockSpec((None, tf, D), lambda i, e, j: (e, j, 0))],
        out_specs=pl.BlockSpec((tm, D), lambda i, e, j: (i, 0)),
        out_shape=jax.ShapeDtypeStruct((M, D), F32),
        scratch_shapes=[pltpu.VMEM((tm, D), BF16), pltpu.VMEM((tm, D), F32)],
        compiler_params=_cparams(("parallel", "arbitrary", "arbitrary")),
    )(h, g, gates, wg, wu, wd)


def _ple_kernel(final, h_ref, g_ref, wg_ref, p_ref, wp_ref, gf_ref, o_ref):
    h = h_ref[...]
    gate = _sigmoid(_mm(_rms(h, g_ref[...]), wg_ref[...]))
    out = h + gate * _mm(p_ref[...], wp_ref[...])
    if final:
        out = _rms(out, gf_ref[...])
    o_ref[...] = out


def _ple(h, g, wg, p, wp, gf, final):
    M, D = h.shape
    tm = min(TM_TOK, M)
    return pl.pallas_call(
        functools.partial(_ple_kernel, final),
        grid=(M // tm,),
        in_specs=[pl.BlockSpec((tm, D), lambda i: (i, 0)),
                  pl.BlockSpec((1, D), lambda i: (0, 0)),
                  pl.BlockSpec((D, D), lambda i: (0, 0)),
                  pl.BlockSpec((tm, D_PLE), lambda i: (i, 0)),
                  pl.BlockSpec((D_PLE, D), lambda i: (0, 0)),
                  pl.BlockSpec((1, D), lambda i: (0, 0))],
        out_specs=pl.BlockSpec((tm, D), lambda i: (i, 0)),
        out_shape=jax.ShapeDtypeStruct((M, D), F32),
        compiler_params=_cparams(("parallel",)),
    )(h, g, wg, p, wp, gf)


def _lru_gates(xc, ga_unused, wr, br, wi, bi, lam):
    r = _sigmoid(_mm(xc, wr) + br)
    i = _sigmoid(_mm(xc, wi) + bi)
    log_a = -LRU_C * r * _softplus(-lam)
    a = jnp.exp(log_a)
    u = jnp.sqrt(jnp.tanh(-log_a) * (a * a + 1.0)) * (i * xc)
    return a, u


def _lru_seq_kernel(TL, proj_ref, cs_ref, h0_ref, cw_ref, cb_ref, wr_ref, br_ref, wi_ref, bi_ref, lam_ref,
                    y_ref, hT_ref, pad_ref, a_ref, u_ref, hs_ref, hc_ref):
    ti = pl.program_id(0)
    G = pad_ref.shape[0]

    @pl.when(ti == 0)
    def _():
        pad_ref[:, SUBLANES - (CONV_W - 1):SUBLANES, :] = cs_ref[...]
        hc_ref[...] = h0_ref[...]

    pad_ref[:, SUBLANES:, :] = proj_ref[:, :, :D_A]
    cw = cw_ref[...]
    xc = cb_ref[...][None]
    for k in range(CONV_W):
        off = SUBLANES - (CONV_W - 1) + k
        xc = xc + pad_ref[:, off:off + TL, :] * cw[k:k + 1, :][None]
    pad_ref[:, :SUBLANES, :] = pad_ref[:, TL:TL + SUBLANES, :]

    a, u = _lru_gates(xc.reshape(G * TL, D_A), None, wr_ref[...], br_ref[...], wi_ref[...], bi_ref[...],
                      lam_ref[...])
    a_ref[...] = a.reshape(G, TL, D_A)
    u_ref[...] = u.reshape(G, TL, D_A)

    def step(t, h):
        h = a_ref[:, pl.ds(t, 1), :] * h + u_ref[:, pl.ds(t, 1), :]
        hs_ref[:, pl.ds(t, 1), :] = h
        return h

    hc_ref[...] = lax.fori_loop(0, TL, step, hc_ref[...], unroll=8)
    y_ref[...] = _gelu_tanh(proj_ref[:, :, D_A:]) * hs_ref[...]

    @pl.when(ti == pl.num_programs(0) - 1)
    def _():
        hT_ref[...] = hc_ref[...]


def _lru_step_kernel(xa_ref, ga_ref, cs_ref, h0_ref, cw_ref, cb_ref, wr_ref, br_ref, wi_ref, bi_ref, lam_ref,
                     y_ref, hT_ref):
    cw = cw_ref[...]
    x = xa_ref[...]
    xc = cb_ref[...] + x * cw[CONV_W - 1:CONV_W, :]
    for k in range(CONV_W - 1):
        xc = xc + cs_ref[k] * cw[k:k + 1, :]
    a, u = _lru_gates(xc, None, wr_ref[...], br_ref[...], wi_ref[...], bi_ref[...], lam_ref[...])
    h = a * h0_ref[...] + u
    hT_ref[...] = h
    y_ref[...] = _gelu_tanh(ga_ref[...]) * h


def _lru(proj, conv_state, h0, cw, cb, wr, br, wi, bi, lam):
    B, L, _ = proj.shape
    small = [cw, cb, wr, br, wi, bi, lam]
    if L == 1:
        proj2 = proj.reshape(B, -1)
        cs = jnp.transpose(conv_state, (1, 0, 2))
        tb = min(B, 128)
        full = lambda s: pl.BlockSpec(s.shape, lambda i: (0,) * s.ndim)
        y, hT = pl.pallas_call(
            _lru_step_kernel,
            grid=(B // tb,),
            in_specs=[pl.BlockSpec((tb, D_A), lambda i: (i, 0)),
                      pl.BlockSpec((tb, D_A), lambda i: (i, 1)),
                      pl.BlockSpec((CONV_W - 1, tb, D_A), lambda i: (0, i, 0)),
                      pl.BlockSpec((tb, D_A), lambda i: (i, 0))] + [full(s) for s in small],
            out_specs=[pl.BlockSpec((tb, D_A), lambda i: (i, 0)), pl.BlockSpec((tb, D_A), lambda i: (i, 0))],
            out_shape=[jax.ShapeDtypeStruct((B, D_A), F32), jax.ShapeDtypeStruct((B, D_A), F32)],
            compiler_params=_cparams(("parallel",)),
        )(proj2, proj2, cs, h0, *small)
        return y.reshape(B, 1, D_A), hT
    TL = min(LRU_TL, L)
    G = B
    full = lambda s: pl.BlockSpec(s.shape, lambda i: (0,) * s.ndim)
    y, hT = pl.pallas_call(
        functools.partial(_lru_seq_kernel, TL),
        grid=(L // TL,),
        in_specs=[pl.BlockSpec((G, TL, 2 * D_A), lambda i: (0, i, 0)),
                  pl.BlockSpec((G, CONV_W - 1, D_A), lambda i: (0, 0, 0)),
                  pl.BlockSpec((G, 1, D_A), lambda i: (0, 0, 0))] + [full(s) for s in small],
        out_specs=[pl.BlockSpec((G, TL, D_A), lambda i: (0, i, 0)),
                   pl.BlockSpec((G, 1, D_A), lambda i: (0, 0, 0))],
        out_shape=[jax.ShapeDtypeStruct((B, L, D_A), F32), jax.ShapeDtypeStruct((B, 1, D_A), F32)],
        scratch_shapes=[pltpu.VMEM((G, SUBLANES + TL, D_A), F32),
                        pltpu.VMEM((G, TL, D_A), F32), pltpu.VMEM((G, TL, D_A), F32),
                        pltpu.VMEM((G, TL, D_A), F32), pltpu.VMEM((G, 1, D_A), F32)],
        compiler_params=_cparams(("arbitrary",)),
    )(proj, conv_state, h0.reshape(B, 1, D_A), *small)
    return y, hT.reshape(B, D_A)


def _rwkv_prep(xs, w0, w2p, a0, a2p, g2, k_k, k_a, r_k, hsel):
    r = xs[:, 0:D_B]
    k = xs[:, D_B:2 * D_B]
    v = xs[:, 2 * D_B:3 * D_B]
    wa = xs[:, 3 * D_B:3 * D_B + R_W + R_A]
    gl = xs[:, 3 * D_B + R_W + R_A:]
    w_log = -_softplus(-(w0 + _mm(jnp.tanh(wa), w2p))) - 0.5
    logd = -jnp.exp(w_log)
    a = _sigmoid(a0 + _mm(wa, a2p))
    g = _mm(_sigmoid(gl), g2)
    kk = k * k_k
    kk = kk * lax.rsqrt(jnp.maximum(_mm_sel(hsel, kk * kk, 3, sel_left=False), 1e-12))
    k2 = k * (1.0 + (a - 1.0) * k_a)
    beta = kk * a
    bonus = _mm_sel(hsel, r * k2 * r_k, 3, sel_left=False) * v
    return r, k2, v, logd, kk, beta, g, bonus


def _rwkv_post(y, bonus, g, ln_w, ln_b, hsel):
    mean = _mm_sel(hsel, y, 3, sel_left=False) * (1.0 / HD_B)
    d = y - mean
    var = _mm_sel(hsel, d * d, 3, sel_left=False) * (1.0 / HD_B)
    yn = d * lax.rsqrt(var + GN_EPS_B) * ln_w + ln_b
    return (yn + bonus) * g


def _rwkv_chunk(G, T, ops, st_ref, masks, tile64, emask4, store_y):
    r, k2, v, logd, kk, beta = ops
    incl, strict, lmask_bf, lastmask_bf = masks
    R = G * T
    cum = _mm_sel(lmask_bf, logd, 3)
    cend = _mm_sel(lastmask_bf, cum, 3)
    ecn = jnp.exp(-cum)
    P = jnp.exp(cum - logd) * kk
    Q = ecn * beta
    Kt = ecn * k2
    Rt = jnp.exp(cum) * r
    eend = jnp.exp(cend)
    eye = None
    if T > 1:
        ri = lax.broadcasted_iota(jnp.int32, (R, R), 0)
        ci = lax.broadcasted_iota(jnp.int32, (R, R), 1)
        eye = jnp.where(ri == ci, 1.0, 0.0)
    for h in range(H_B):
        sl = slice(h * HD_B, (h + 1) * HD_B)
        Ph, Qh, Kh, Rh, Vh = P[:, sl], Q[:, sl], Kt[:, sl], Rt[:, sl], v[:, sl]
        AM = _mm3(jnp.concatenate([Ph, Rh], axis=0), jnp.concatenate([Qh, Kh], axis=0), NT)
        Mqr = jnp.where(incl, AM[R:, :R], 0.0)
        Mkr = jnp.where(incl, AM[R:, R:], 0.0)
        X4e = _mm_sel(tile64, jnp.concatenate([Ph, Rh, Qh, Kh], axis=0), 2, sel_left=False)
        X4e = jnp.where(emask4, X4e, 0.0)
        WT = st_ref[h]
        PWRW = _mm3(X4e[:2 * R], WT, NT)
        if T > 1:
            Aqp = jnp.where(strict, AM[:R, :R], 0.0)
            Akp = jnp.where(strict, AM[:R, R:], 0.0)
            Npow = -Aqp
            inv = eye + Npow
            for _ in range(int(math.log2(T)) - 1):
                Npow = _mm3(Npow, Npow)
                inv = inv + _mm3(inv, Npow)
            E = _mm3(inv, PWRW[:R] + _mm3(Akp, Vh))
        else:
            E = PWRW[:R]
        Y = PWRW[R:] + _mm3(Mkr, Vh) - _mm3(Mqr, E)
        dW = _mm3(jnp.concatenate([-E, Vh], axis=0), X4e[2 * R:], TN)
        erow = jnp.sum(jnp.where(emask4[:R], _mm_sel(tile64, eend[:, sl], 3, sel_left=False), 0.0),
                       axis=0, keepdims=True) * (1.0 / T)
        st_ref[h] = (WT + dW) * erow
        store_y(h, Y)


def _rwkv_kernel(G, T, NC, seq, proj_ref, sh_ref, s0_ref, mu_ref, w0_ref, w2_ref, a0_ref, a2_ref, g2_ref,
                 kk_ref, ka_ref, rk_ref, lnw_ref, lnb_ref, y_ref, sT_ref, *scratch):
    ti = pl.program_id(1)
    TL = T * NC
    R = G * T
    st_ref = scratch[0]
    hsel = _head_sum_sel(D_B, HD_B)

    @pl.when(ti == 0)
    def _():
        st_ref[...] = s0_ref[...]

    if seq:
        pad_ref, ops_ref, gb_ref, ys_ref = scratch[1:]

        @pl.when(ti == 0)
        def _():
            pad_ref[:, SUBLANES - 1:SUBLANES, :] = sh_ref[...]

        pad_ref[:, SUBLANES:, :] = proj_ref[:, :, 2 * D_A:]
        pb = pad_ref[:, SUBLANES:, :]
        prev = pad_ref[:, SUBLANES - 1:SUBLANES - 1 + TL, :]
        xs = (pb + (prev - pb) * mu_ref[...][None]).reshape(G * TL, D_BPROJ)
        pad_ref[:, :SUBLANES, :] = pad_ref[:, TL:TL + SUBLANES, :]
    else:
        pb = proj_ref[:, 2 * D_A:]
        xs = pb + (sh_ref[...] - pb) * mu_ref[...]

    r, k2, v, logd, kk, beta, g, bonus = _rwkv_prep(
        xs, w0_ref[...], w2_ref[...], a0_ref[...], a2_ref[...], g2_ref[...], kk_ref[...], ka_ref[...],
        rk_ref[...], hsel)

    incl, strict, last = _chunk_masks(G, T)
    masks = (incl, strict, _to_bf16_mask(incl), _to_bf16_mask(last))
    tile64 = _tile_sel(HD_B, G)
    emask4 = _batch_lane_mask(4, G, T, HD_B)

    if seq:
        for n, val in enumerate((r, k2, v, logd, kk, beta)):
            ops_ref[n] = val.reshape(G, TL, D_B)
        gb_ref[0] = g.reshape(G, TL, D_B)
        gb_ref[1] = bonus.reshape(G, TL, D_B)

        def chunk(c, carry):
            t0 = pl.multiple_of(c * T, T)
            ops = tuple(ops_ref[n, :, pl.ds(t0, T), :].reshape(R, D_B) for n in range(6))

            def store_y(h, Y):
                ys_ref[:, pl.ds(t0, T), h * HD_B:(h + 1) * HD_B] = Y.reshape(G, T, HD_B)

            _rwkv_chunk(G, T, ops, st_ref, masks, tile64, emask4, store_y)
            return carry

        lax.fori_loop(0, NC, chunk, 0)
        y = ys_ref[...].reshape(G * TL, D_B)
        out = _rwkv_post(y, gb_ref[1].reshape(G * TL, D_B), gb_ref[0].reshape(G * TL, D_B),
                         lnw_ref[...], lnb_ref[...], hsel)
        y_ref[...] = out.reshape(G, TL, D_B)
    else:
        ys_ref = scratch[1]

        def store_y(h, Y):
            ys_ref[:, h * HD_B:(h + 1) * HD_B] = Y

        _rwkv_chunk(G, T, (r, k2, v, logd, kk, beta), st_ref, masks, tile64, emask4, store_y)
        y_ref[...] = _rwkv_post(ys_ref[...], bonus, g, lnw_ref[...], lnb_ref[...], hsel)

    @pl.when(ti == pl.num_programs(1) - 1)
    def _():
        sT_ref[...] = st_ref[...]


def _state_to_lanes(s, G):
    B, H, X, Y = s.shape
    return s.reshape(B // G, G, H, X, Y).transpose(0, 2, 3, 1, 4).reshape(B // G, H, X, G * Y)


def _state_from_lanes(s, G):
    NB, H, X, GY = s.shape
    Y = GY // G
    return s.reshape(NB, H, X, G, Y).transpose(0, 3, 1, 2, 4).reshape(NB * G, H, X, Y)


def _rwkv(proj, shift0, s0, mu, w0, w2p, a0, a2p, g2, k_k, k_a, r_k, ln_w, ln_b):
    B, L, _ = proj.shape
    G = G_ROWS
    NB = B // G
    small = [mu, w0, w2p, a0, a2p, g2, k_k, k_a, r_k, ln_w, ln_b]
    full = lambda s: pl.BlockSpec(s.shape, lambda b, i: (0,) * s.ndim)
    s0l = _state_to_lanes(s0, G)
    st_spec = pl.BlockSpec((None, H_B, HD_B, G * HD_B), lambda b, i: (b, 0, 0, 0))
    st_shape = jax.ShapeDtypeStruct(s0l.shape, F32)
    if L == 1:
        T, NC = 1, 1
        kern = functools.partial(_rwkv_kernel, G, T, NC, False)
        y, sT = pl.pallas_call(
            kern,
            grid=(NB, 1),
            in_specs=[pl.BlockSpec((G, IN_AB), lambda b, i: (b, 0)),
                      pl.BlockSpec((G, D_BPROJ), lambda b, i: (b, 0)), st_spec] + [full(s) for s in small],
            out_specs=[pl.BlockSpec((G, D_B), lambda b, i: (b, 0)), st_spec],
            out_shape=[jax.ShapeDtypeStruct((B, D_B), F32), st_shape],
            scratch_shapes=[pltpu.VMEM((H_B, HD_B, G * HD_B), F32), pltpu.VMEM((G, D_B), F32)],
            compiler_params=_cparams(("parallel", "arbitrary")),
        )(proj.reshape(B, IN_AB), shift0, s0l, *small)
        return y.reshape(B, 1, D_B), _state_from_lanes(sT, G)
    T = min(T_CHUNK, L)
    TL = min(TL_BLOCK, L)
    NC = TL // T
    kern = functools.partial(_rwkv_kernel, G, T, NC, True)
    y, sT = pl.pallas_call(
        kern,
        grid=(NB, L // TL),
        in_specs=[pl.BlockSpec((G, TL, IN_AB), lambda b, i: (b, i, 0)),
                  pl.BlockSpec((G, 1, D_BPROJ), lambda b, i: (b, 0, 0)), st_spec] + [full(s) for s in small],
        out_specs=[pl.BlockSpec((G, TL, D_B), lambda b, i: (b, i, 0)), st_spec],
        out_shape=[jax.ShapeDtypeStruct((B, L, D_B), F32), st_shape],
        scratch_shapes=[pltpu.VMEM((H_B, HD_B, G * HD_B), F32),
                        pltpu.VMEM((G, SUBLANES + TL, D_BPROJ), F32),
                        pltpu.VMEM((6, G, TL, D_B), F32),
                        pltpu.VMEM((2, G, TL, D_B), F32),
                        pltpu.VMEM((G, TL, D_B), F32)],
        compiler_params=_cparams(("parallel", "arbitrary")),
    )(proj, shift0.reshape(B, 1, D_BPROJ), s0l, *small)
    return y, _state_from_lanes(sT, G)


CD_Z, CD_XBC, CD_Q, CD_K, CD_V, CD_G, CD_S = 0, 512, 1536, 1792, 2048, 2560, 3072


def _ssd_chunk(G, T, xs, Bm, Cm, dt_all, la_all, st_ref, masks, tileN, emaskN, store_y):
    incl, lmask_bf, lastmask_bf = masks
    cum = _mm_sel(lmask_bf, la_all, 3)
    cend = _mm_sel(lastmask_bf, cum, 3)
    cumT = cum.T
    ecum = jnp.exp(cum)
    wend = jnp.exp(cend - cum) * dt_all
    dend = jnp.exp(cend)
    for g in range(G_C):
        Bg = Bm[:, g * N_C:(g + 1) * N_C]
        Cg = Cm[:, g * N_C:(g + 1) * N_C]
        CB = _mm(Cg, Bg, NT)
        Bge = jnp.where(emaskN, _mm(Bg, tileN), 0.0)
        Cge = jnp.where(emaskN, _mm(Cg, tileN), 0.0)
        for hh in range(HG_C):
            h = g * HG_C + hh
            xh = xs[:, h * P_C:(h + 1) * P_C]
            seg = jnp.exp(jnp.where(incl, cum[:, h:h + 1] - cumT[h:h + 1, :], -1e30))
            ST = st_ref[h]
            y = _mm(CB * seg, xh * dt_all[:, h:h + 1]) + ecum[:, h:h + 1] * _mm(Cge, ST, NT)
            drow = jnp.sum(jnp.where(emaskN, dend[:, h:h + 1], 0.0), axis=0, keepdims=True) * (1.0 / T)
            st_ref[h] = ST * drow + _mm(xh * wend[:, h:h + 1], Bge, TN)
            store_y(h, y)


def _gla_chunk(G, T, q, k, v, la, st_ref, masks, tile64, emask1, store_o):
    incl, lmask_bf, lastmask_bf = masks
    bc = _mm_sel(lmask_bf, la, 3)
    bend = _mm_sel(lastmask_bf, bc, 3)
    qe = q * jnp.exp(bc)
    kn = k * jnp.exp(-bc)
    kw = k * jnp.exp(bend - bc)
    dend = jnp.exp(bend)
    for h in range(H_D):
        sl = slice(h * DK_D, (h + 1) * DK_D)
        vh = v[:, h * DV_D:(h + 1) * DV_D]
        att = jnp.where(incl, _mm(qe[:, sl], kn[:, sl], NT), 0.0)
        qee = jnp.where(emask1, _mm(qe[:, sl], tile64), 0.0)
        kwe = jnp.where(emask1, _mm(kw[:, sl], tile64), 0.0)
        ST = st_ref[h]
        o = _mm(att, vh) + _mm(qee, ST, NT)
        drow = jnp.sum(jnp.where(emask1, _mm_sel(tile64, dend[:, sl], 3, sel_left=False), 0.0),
                       axis=0, keepdims=True) * (1.0 / T)
        st_ref[h] = ST * drow + _mm(vh, kwe, TN)
        store_o(h, o)


def _cd_prep_small(small, dtb, a_neg, aup, ab):
    dt_all = _softplus(small + dtb)
    la_all = dt_all * a_neg
    la_gla = -_softplus(-(_mm(small, aup) + ab)) * (1.0 / GATE_NORM)
    return dt_all, la_all, la_gla


def _cd_post(y, xs, z, o, gg, dskip, ssd_norm, gla_norm):
    y_c = _rms((y + dskip * xs) * _silu(z), ssd_norm)
    outs = [y_c]
    for h in range(H_D):
        sl = slice(h * DV_D, (h + 1) * DV_D)
        outs.append(_rms(o[:, sl], gla_norm) * _silu(gg[:, sl]))
    return outs


def _cd_kernel(G, T, NC, seq, proj_ref, cs_ref, ssd0_ref, gla0_ref, cw_ref, cb_ref, dtb_ref, aneg_ref, dskip_ref,
               ssdn_ref, aup_ref, ab_ref, glan_ref, y_ref, ssdT_ref, glaT_ref, *scratch):
    ti = pl.program_id(1)
    TL = T * NC
    R = G * T
    sst_ref, gst_ref = scratch[0], scratch[1]

    @pl.when(ti == 0)
    def _():
        sst_ref[...] = ssd0_ref[...]
        gst_ref[...] = gla0_ref[...]

    cw = cw_ref[...]
    if seq:
        pad_ref, xbc_ref, sm_ref, ys_ref, os_ref = scratch[2:]

        @pl.when(ti == 0)
        def _():
            pad_ref[:, SUBLANES - (CONV_W - 1):SUBLANES, :] = cs_ref[...]

        pad_ref[:, SUBLANES:, :] = proj_ref[:, :, CD_XBC:CD_XBC + D_XBC]
        xc = cb_ref[...][None]
        for kq in range(CONV_W):
            off = SUBLANES - (CONV_W - 1) + kq
            xc = xc + pad_ref[:, off:off + TL, :] * cw[kq:kq + 1, :][None]
        pad_ref[:, :SUBLANES, :] = pad_ref[:, TL:TL + SUBLANES, :]
        xbc_ref[...] = _silu(xc)
        small = proj_ref[:, :, CD_S:].reshape(G * TL, LANES)
    else:
        x = proj_ref[:, CD_XBC:CD_XBC + D_XBC]
        xc = cb_ref[...] + x * cw[CONV_W - 1:CONV_W, :]
        for kq in range(CONV_W - 1):
            xc = xc + cs_ref[kq] * cw[kq:kq + 1, :]
        xbc = _silu(xc)
        small = proj_ref[:, CD_S:]

    dt_all, la_all, la_gla = _cd_prep_small(small, dtb_ref[...], aneg_ref[...], aup_ref[...], ab_ref[...])

    incl, _, last = _chunk_masks(G, T)
    masks = (incl, _to_bf16_mask(incl), _to_bf16_mask(last))
    tileN = _tile_sel(N_C, G)
    tile64 = _tile_sel(DK_D, G)
    emaskN = _batch_lane_mask(1, G, T, N_C)
    emask1 = _batch_lane_mask(1, G, T, DK_D)
    scale = DK_D ** -0.5

    if seq:
        sm_ref[0] = dt_all.reshape(G, TL, LANES)
        sm_ref[1] = la_all.reshape(G, TL, LANES)
        sm_ref[2] = la_gla[:, :LANES].reshape(G, TL, LANES)
        sm_ref[3] = la_gla[:, LANES:].reshape(G, TL, LANES)

        def chunk(c, carry):
            t0 = pl.multiple_of(c * T, T)
            rows = lambda ref, lo, hi: ref[:, pl.ds(t0, T), lo:hi].reshape(R, hi - lo)
            xs = rows(xbc_ref, 0, D_C)
            Bm = rows(xbc_ref, D_C, D_C + G_C * N_C)
            Cm = rows(xbc_ref, D_C + G_C * N_C, D_XBC)
            dtc = sm_ref[0, :, pl.ds(t0, T), :].reshape(R, LANES)
            lac = sm_ref[1, :, pl.ds(t0, T), :].reshape(R, LANES)

            def store_y(h, yv):
                ys_ref[:, pl.ds(t0, T), h * P_C:(h + 1) * P_C] = yv.reshape(G, T, P_C)

            _ssd_chunk(G, T, xs, Bm, Cm, dtc, lac, sst_ref, masks, tileN, emaskN, store_y)

            q = rows(proj_ref, CD_Q, CD_Q + K_D) * scale
            k = rows(proj_ref, CD_K, CD_K + K_D)
            v = rows(proj_ref, CD_V, CD_V + V_D)
            lag = jnp.concatenate([sm_ref[2, :, pl.ds(t0, T), :].reshape(R, LANES),
                                   sm_ref[3, :, pl.ds(t0, T), :].reshape(R, LANES)], axis=1)

            def store_o(h, ov):
                os_ref[:, pl.ds(t0, T), h * DV_D:(h + 1) * DV_D] = ov.reshape(G, T, DV_D)

            _gla_chunk(G, T, q, k, v, lag, gst_ref, masks, tile64, emask1, store_o)
            return carry

        lax.fori_loop(0, NC, chunk, 0)
        flat = lambda a: a.reshape(G * TL, a.shape[-1])
        outs = _cd_post(flat(ys_ref[...]), flat(xbc_ref[:, :, :D_C]), flat(proj_ref[:, :, CD_Z:CD_Z + D_C]),
                        flat(os_ref[...]), flat(proj_ref[:, :, CD_G:CD_G + V_D]),
                        dskip_ref[...], ssdn_ref[...], glan_ref[...])
        y_ref[:, :, :D_C] = outs[0].reshape(G, TL, D_C)
        for h in range(H_D):
            y_ref[:, :, D_C + h * DV_D:D_C + (h + 1) * DV_D] = outs[1 + h].reshape(G, TL, DV_D)
    else:
        ys_ref, os_ref = scratch[2:]

        def store_y(h, yv):
            ys_ref[:, h * P_C:(h + 1) * P_C] = yv

        def store_o(h, ov):
            os_ref[:, h * DV_D:(h + 1) * DV_D] = ov

        _ssd_chunk(G, T, xbc[:, :D_C], xbc[:, D_C:D_C + G_C * N_C], xbc[:, D_C + G_C * N_C:], dt_all, la_all,
                   sst_ref, masks, tileN, emaskN, store_y)
        _gla_chunk(G, T, proj_ref[:, CD_Q:CD_Q + K_D] * scale, proj_ref[:, CD_K:CD_K + K_D],
                   proj_ref[:, CD_V:CD_V + V_D], la_gla, gst_ref, masks, tile64, emask1, store_o)
        outs = _cd_post(ys_ref[...], xbc[:, :D_C], proj_ref[:, CD_Z:CD_Z + D_C], os_ref[...],
                        proj_ref[:, CD_G:CD_G + V_D], dskip_ref[...], ssdn_ref[...], glan_ref[...])
        y_ref[:, :D_C] = outs[0]
        for h in range(H_D):
            y_ref[:, D_C + h * DV_D:D_C + (h + 1) * DV_D] = outs[1 + h]

    @pl.when(ti == pl.num_programs(1) - 1)
    def _():
        ssdT_ref[...] = sst_ref[...]
        glaT_ref[...] = gst_ref[...]


def _cd(proj, conv_state, ssd0, gla0, cw, cb, dtb, a_neg, dskip, ssd_norm, aup, ab, gla_norm):
    B, L, _ = proj.shape
    G = G_ROWS
    NB = B // G
    small = [cw, cb, dtb, a_neg, dskip, ssd_norm, aup, ab, gla_norm]
    full = lambda s: pl.BlockSpec(s.shape, lambda b, i: (0,) * s.ndim)
    ssd0l = _state_to_lanes(ssd0, G)
    gla0l = _state_to_lanes(jnp.swapaxes(gla0, 2, 3), G)
    ssd_spec = pl.BlockSpec((None, H_C, P_C, G * N_C), lambda b, i: (b, 0, 0, 0))
    gla_spec = pl.BlockSpec((None, H_D, DV_D, G * DK_D), lambda b, i: (b, 0, 0, 0))
    st_shapes = [jax.ShapeDtypeStruct(ssd0l.shape, F32), jax.ShapeDtypeStruct(gla0l.shape, F32)]
    st_scratch = [pltpu.VMEM((H_C, P_C, G * N_C), F32), pltpu.VMEM((H_D, DV_D, G * DK_D), F32)]
    DO = D_C + V_D
    if L == 1:
        kern = functools.partial(_cd_kernel, G, 1, 1, False)
        y, ssdT, glaT = pl.pallas_call(
            kern,
            grid=(NB, 1),
            in_specs=[pl.BlockSpec((G, IN_CD_PAD), lambda b, i: (b, 0)),
                      pl.BlockSpec((CONV_W - 1, G, D_XBC), lambda b, i: (0, b, 0)),
                      ssd_spec, gla_spec] + [full(s) for s in small],
            out_specs=[pl.BlockSpec((G, DO), lambda b, i: (b, 0)), ssd_spec, gla_spec],
            out_shape=[jax.ShapeDtypeStruct((B, DO), F32)] + st_shapes,
            scratch_shapes=st_scratch + [pltpu.VMEM((G, D_C), F32), pltpu.VMEM((G, V_D), F32)],
            compiler_params=_cparams(("parallel", "arbitrary")),
        )(proj.reshape(B, IN_CD_PAD), jnp.transpose(conv_state, (1, 0, 2)), ssd0l, gla0l, *small)
        y = y.reshape(B, 1, DO)
    else:
        T = min(T_CHUNK, L)
        TL = min(TL_BLOCK, L)
        NC = TL // T
        kern = functools.partial(_cd_kernel, G, T, NC, True)
        y, ssdT, glaT = pl.pallas_call(
            kern,
            grid=(NB, L // TL),
            in_specs=[pl.BlockSpec((G, TL, IN_CD_PAD), lambda b, i: (b, i, 0)),
                      pl.BlockSpec((G, CONV_W - 1, D_XBC), lambda b, i: (b, 0, 0)),
                      ssd_spec, gla_spec] + [full(s) for s in small],
            out_specs=[pl.BlockSpec((G, TL, DO), lambda b, i: (b, i, 0)), ssd_spec, gla_spec],
            out_shape=[jax.ShapeDtypeStruct((B, L, DO), F32)] + st_shapes,
            scratch_shapes=st_scratch + [pltpu.VMEM((G, SUBLANES + TL, D_XBC), F32),
                                         pltpu.VMEM((G, TL, D_XBC), F32),
                                         pltpu.VMEM((4, G, TL, LANES), F32),
                                         pltpu.VMEM((G, TL, D_C), F32),
                                         pltpu.VMEM((G, TL, V_D), F32)],
            compiler_params=_cparams(("parallel", "arbitrary")),
        )(proj, conv_state, ssd0l, gla0l, *small)
    ssd_new = _state_from_lanes(ssdT, G)
    gla_new = jnp.swapaxes(_state_from_lanes(glaT, G), 2, 3)
    return y, ssd_new, gla_new


def _block_diag(w):
    eye = jnp.eye(N_BLK_A, dtype=w.dtype)
    return jnp.einsum('nij,nm->nimj', w, eye).reshape(D_A, D_A)


def _row(v):
    return v.reshape(1, -1).astype(F32)


def _prep_params(p):
    q = {}
    j = 0
    q['w_in_ab'] = p['w_in_ab'][j].astype(BF16)
    q['w_out_ab'] = p['w_out_ab'][j].astype(BF16)
    q['lru'] = [p['lru_conv_w'][j], _row(p['lru_conv_b'][j]),
                _block_diag(p['lru_w_r'][j]).astype(BF16), _row(p['lru_b_r'][j]),
                _block_diag(p['lru_w_i'][j]).astype(BF16), _row(p['lru_b_i'][j]), _row(p['lru_lambda'][j])]
    zw = jnp.zeros((R_W, D_B), F32)
    w2p = jnp.concatenate([p['rwkv_w2'][j], zw], axis=0).astype(BF16)
    a2p = jnp.concatenate([zw, p['rwkv_a2'][j]], axis=0).astype(BF16)
    q['rwkv'] = [_row(p['rwkv_mu'][j]), _row(p['rwkv_w0'][j]), w2p, _row(p['rwkv_a0'][j]), a2p,
                 p['rwkv_g2'][j].astype(BF16), _row(p['rwkv_k_k'][j]), _row(p['rwkv_k_a'][j]),
                 _row(p['rwkv_r_k'][j]), _row(p['rwkv_ln_w'][j]), _row(p['rwkv_ln_b'][j])]
    q['ffn'] = [p['ffn_w_gate'][j].astype(BF16), p['ffn_w_up'][j].astype(BF16), p['ffn_w_down'][j].astype(BF16)]
    w = p['w_in_cd'][j]
    o_z, o_xbc, o_dt = 0, D_C, D_C + D_XBC
    o_q = o_dt + H_C
    o_k = o_q + K_D
    o_v = o_k + K_D
    o_g = o_v + V_D
    o_al = o_g + V_D
    pad = jnp.zeros((D_MODEL, LANES - H_C - R_ALPHA), F32)
    q['w_in_cd'] = jnp.concatenate([w[:, o_z:o_dt], w[:, o_q:o_al], w[:, o_dt:o_q], w[:, o_al:], pad],
                                   axis=1).astype(BF16)
    q['w_out_cd'] = p['w_out_cd'][j].astype(BF16)
    lane_pad = lambda v: jnp.concatenate([v.astype(F32), jnp.zeros((LANES - v.shape[0],), F32)]).reshape(1, LANES)
    aup = jnp.zeros((LANES, K_D), F32).at[H_C:H_C + R_ALPHA].set(p['gla_alpha_up'][j]).astype(BF16)
    q['cd'] = [p['ssd_conv_w'][j], _row(p['ssd_conv_b'][j]), lane_pad(p['ssd_dt_bias'][j]),
               lane_pad(-jnp.exp(p['ssd_a_log'][j].astype(F32))), _row(jnp.repeat(p['ssd_d'][j], P_C)),
               _row(p['ssd_norm_w'][j]), aup, _row(p['gla_alpha_b'][j]), _row(p['gla_norm_w'][j])]
    q['router'] = jnp.concatenate([p['moe_router'][j], jnp.zeros((D_MODEL, LANES - N_EXP), F32)], axis=1)
    q['moe'] = [p['moe_w_gate'][j].astype(BF16), p['moe_w_up'][j].astype(BF16), p['moe_w_down'][j].astype(BF16)]
    q['norm_mix'] = [_row(p['norm_mix'][i]) for i in range(DEPTH)]
    q['norm_ffn'] = [_row(p['norm_ffn'][i]) for i in range(DEPTH)]
    q['norm_ple'] = [_row(p['norm_ple'][i]) for i in range(DEPTH)]
    q['w_ple_gate'] = [p['w_ple_gate'][i].astype(BF16) for i in range(DEPTH)]
    q['w_ple_proj'] = [p['w_ple_proj'][i].astype(BF16) for i in range(DEPTH)]
    q['norm_final'] = _row(p['norm_final'])
    return q


def _trunk(x, p, states, q):
    lru_conv, lru_h, rwkv_shift, rwkv_wkv, ssd_conv, ssd_st, gla_st = states
    B, L, D = x.shape
    M = B * L
    h = x.reshape(M, D)
    p2 = p.reshape(DEPTH, M, D_PLE)

    proj = _norm_mm(h, q['norm_mix'][0], q['w_in_ab'], IN_AB // 2).reshape(B, L, IN_AB)
    y_a, h_new = _lru(proj, lru_conv[0], lru_h[0], *q['lru'])
    y_b, wkv_new = _rwkv(proj, rwkv_shift[0], rwkv_wkv[0], *q['rwkv'])
    if L >= CONV_W - 1:
        conv_new = proj[:, L - (CONV_W - 1):, :D_A]
    else:
        conv_new = jnp.concatenate([lru_conv[0][:, L:], proj[:, :, :D_A]], axis=1)
    shift_new = proj[:, L - 1, 2 * D_A:]
    w_out = q['w_out_ab']
    h = _out_proj(h, [y_a.reshape(M, D_A), y_b.reshape(M, D_B)], [w_out[:D_A], w_out[D_A:]])
    h = _ffn(h, q['norm_ffn'][0], *q['ffn'], D_FF // 2)
    h = _ple(h, q['norm_ple'][0], q['w_ple_gate'][0], p2[0], q['w_ple_proj'][0], q['norm_final'], False)

    proj = _norm_mm(h, q['norm_mix'][1], q['w_in_cd'], IN_CD_PAD // 5).reshape(B, L, IN_CD_PAD)
    y_cd, ssd_new, gla_new = _cd(proj, ssd_conv[0], ssd_st[0], gla_st[0], *q['cd'])
    if L >= CONV_W - 1:
        sconv_new = proj[:, L - (CONV_W - 1):, CD_XBC:CD_XBC + D_XBC]
    else:
        sconv_new = jnp.concatenate([ssd_conv[0][:, L:], proj[:, :, CD_XBC:CD_XBC + D_XBC]], axis=1)
    h = _out_proj(h, [y_cd.reshape(M, D_C + V_D)], [q['w_out_cd']])
    gates = _router(h, q['norm_ffn'][1], q['router'])
    h = _moe(h, q['norm_ffn'][1], gates, *q['moe'], D_FF_E)
    y = _ple(h, q['norm_ple'][1], q['w_ple_gate'][1], p2[1], q['w_ple_proj'][1], q['norm_final'], True)

    new_states = (conv_new[None], h_new[None], shift_new[None], wkv_new[None],
                  sconv_new[None], ssd_new[None], gla_new[None])
    return y.reshape(B, L, D), new_states


def kernel(x_prompt, x_sample, p_prompt, p_sample, state_lru_conv, state_lru_h, state_rwkv_shift,
           state_rwkv_wkv, state_ssd_conv, state_ssd, state_gla, norm_mix, norm_ffn, norm_ple,
           w_ple_gate, w_ple_proj, norm_final, w_in_ab, w_out_ab, lru_conv_w, lru_conv_b, lru_w_r,
           lru_b_r, lru_w_i, lru_b_i, lru_lambda, rwkv_mu, rwkv_w0, rwkv_w2, rwkv_a0, rwkv_a2,
           rwkv_g2, rwkv_k_k, rwkv_k_a, rwkv_r_k, rwkv_ln_w, rwkv_ln_b, ffn_w_gate, ffn_w_up,
           ffn_w_down, w_in_cd, w_out_cd, ssd_conv_w, ssd_conv_b, ssd_dt_bias, ssd_a_log, ssd_d,
           ssd_norm_w, gla_alpha_up, gla_alpha_b, gla_norm_w, moe_router, moe_w_gate, moe_w_up,
           moe_w_down):
    prm = dict(
        norm_mix=norm_mix, norm_ffn=norm_ffn, norm_ple=norm_ple, w_ple_gate=w_ple_gate,
        w_ple_proj=w_ple_proj, norm_final=norm_final, w_in_ab=w_in_ab, w_out_ab=w_out_ab,
        lru_conv_w=lru_conv_w, lru_conv_b=lru_conv_b, lru_w_r=lru_w_r, lru_b_r=lru_b_r,
        lru_w_i=lru_w_i, lru_b_i=lru_b_i, lru_lambda=lru_lambda, rwkv_mu=rwkv_mu, rwkv_w0=rwkv_w0,
        rwkv_w2=rwkv_w2, rwkv_a0=rwkv_a0, rwkv_a2=rwkv_a2, rwkv_g2=rwkv_g2, rwkv_k_k=rwkv_k_k,
        rwkv_k_a=rwkv_k_a, rwkv_r_k=rwkv_r_k, rwkv_ln_w=rwkv_ln_w, rwkv_ln_b=rwkv_ln_b,
        ffn_w_gate=ffn_w_gate, ffn_w_up=ffn_w_up, ffn_w_down=ffn_w_down, w_in_cd=w_in_cd,
        w_out_cd=w_out_cd, ssd_conv_w=ssd_conv_w, ssd_conv_b=ssd_conv_b, ssd_dt_bias=ssd_dt_bias,
        ssd_a_log=ssd_a_log, ssd_d=ssd_d, ssd_norm_w=ssd_norm_w, gla_alpha_up=gla_alpha_up,
        gla_alpha_b=gla_alpha_b, gla_norm_w=gla_norm_w, moe_router=moe_router,
        moe_w_gate=moe_w_gate, moe_w_up=moe_w_up, moe_w_down=moe_w_down)
    q = _prep_params(prm)
    states_s = (state_lru_conv, state_lru_h, state_rwkv_shift, state_rwkv_wkv,
                state_ssd_conv, state_ssd, state_gla)
    n_prompt = x_prompt.shape[0]
    states_p = tuple(jnp.zeros((s.shape[0], n_prompt) + s.shape[2:], s.dtype) for s in states_s)
    y_p, st_p = _trunk(x_prompt, p_prompt, states_p, q)
    y_s, st_s = _trunk(x_sample, p_sample, states_s, q)
    return (y_p, y_s) + tuple(st_p) + tuple(st_s)
```

```python
import functools
import math

import jax
import jax.numpy as jnp
from jax import lax
from jax.experimental import pallas as pl
from jax.experimental.pallas import tpu as pltpu

F32, BF16 = jnp.float32, jnp.bfloat16

D_MODEL = 1024
DEPTH = 2
D_PLE = 256
EPS = 1e-6
CONV_W = 4
D_A = 512
N_BLK_A = 8
BW_A = D_A // N_BLK_A
LRU_C = 8.0
H_B = 8
HD_B = 64
D_B = H_B * HD_B
R_W = 64
R_A = 64
R_G = 128
D_BPROJ = 3 * D_B + R_W + R_A + R_G
GN_EPS_B = 64e-5
IN_AB = 2 * D_A + D_BPROJ
H_C = 8
P_C = 64
D_C = H_C * P_C
G_C = 2
HG_C = H_C // G_C
N_C = 128
D_XBC = D_C + 2 * G_C * N_C
H_D = 4
DK_D = 64
DV_D = 128
K_D = H_D * DK_D
V_D = H_D * DV_D
R_ALPHA = 16
GATE_NORM = 16.0
D_FF = 2816
N_EXP = 8
D_FF_E = 1408
IN_CD_PAD = D_C + D_XBC + 2 * K_D + 2 * V_D + 128

LANES = 128
SUBLANES = 8
VMEM_LIMIT = 56 * 1024 * 1024

G_ROWS = SUBLANES
T_CHUNK = 16
TL_BLOCK = 64
TM_TOK = 1024
TM_FFN = 512
LRU_TL = 256

NN = ((1,), (0,))
NT = ((1,), (1,))
TN = ((0,), (0,))


def _cparams(sem):
    return pltpu.CompilerParams(dimension_semantics=sem, vmem_limit_bytes=VMEM_LIMIT)


def _rms(x, g):
    return x * lax.rsqrt(jnp.mean(x * x, axis=-1, keepdims=True) + EPS) * g


def _sigmoid(x):
    return jax.nn.sigmoid(x)


def _silu(x):
    return x * jax.nn.sigmoid(x)


def _softplus(x):
    return jnp.maximum(x, 0.0) + jnp.log1p(jnp.exp(-jnp.abs(x)))


def _gelu_tanh(x):
    c = math.sqrt(2.0 / math.pi)
    return 0.5 * x * (1.0 + jnp.tanh(c * (x + 0.044715 * (x * x * x))))


def _d(a, b, dims=NN):
    return lax.dot_general(a, b, (dims, ((), ())), preferred_element_type=F32)


def _mm(a, b, dims=NN):
    return _d(a.astype(BF16), b.astype(BF16), dims)


def _split2(x):
    hi = x.astype(BF16)
    lo = (x - hi.astype(F32)).astype(BF16)
    return hi, lo


def _mm3(a, b, dims=NN):
    ah, al = _split2(a)
    bh, bl = _split2(b)
    return _d(ah, bh, dims) + (_d(ah, bl, dims) + _d(al, bh, dims))


def _mm_sel(sel, x, pieces, sel_left=True, dims=NN):
    out = None
    rest = x
    for i in range(pieces):
        p = rest.astype(BF16)
        if i + 1 < pieces:
            rest = rest - p.astype(F32)
        t = _d(sel, p, dims) if sel_left else _d(p, sel, dims)
        out = t if out is None else out + t
    return out


def _chunk_masks(G, T):
    R = G * T
    ri = lax.broadcasted_iota(jnp.int32, (R, R), 0)
    ci = lax.broadcasted_iota(jnp.int32, (R, R), 1)
    same = (ri // T) == (ci // T)
    incl = same & (ci <= ri)
    strict = same & (ci < ri)
    last = ci == (ri // T) * T + (T - 1)
    return incl, strict, last


def _to_bf16_mask(m):
    return jnp.where(m, 1.0, 0.0).astype(BF16)


def _tile_sel(W, G):
    r = lax.broadcasted_iota(jnp.int32, (W, G * W), 0)
    c = lax.broadcasted_iota(jnp.int32, (W, G * W), 1)
    return _to_bf16_mask((c % W) == r)


def _batch_lane_mask(n_stack, G, T, W):
    R = G * T
    r = lax.broadcasted_iota(jnp.int32, (n_stack * R, G * W), 0)
    c = lax.broadcasted_iota(jnp.int32, (n_stack * R, G * W), 1)
    return ((r % R) // T) == (c // W)


def _head_sum_sel(width, head):
    r = lax.broadcasted_iota(jnp.int32, (width, width), 0)
    c = lax.broadcasted_iota(jnp.int32, (width, width), 1)
    return _to_bf16_mask((r // head) == (c // head))


def _norm_mm_kernel(x_ref, g_ref, w_ref, o_ref, xn_ref):
    @pl.when(pl.program_id(1) == 0)
    def _():
        xn_ref[...] = _rms(x_ref[...], g_ref[...]).astype(BF16)

    o_ref[...] = _d(xn_ref[...], w_ref[...])


def _norm_mm(x, g, w, tn):
    M, K = x.shape
    N = w.shape[1]
    tm = min(TM_TOK, M)
    return pl.pallas_call(
        _norm_mm_kernel,
        grid=(M // tm, N // tn),
        in_specs=[pl.BlockSpec((tm, K), lambda i, j: (i, 0)),
                  pl.BlockSpec((1, K), lambda i, j: (0, 0)),
                  pl.BlockSpec((K, tn), lambda i, j: (0, j))],
        out_specs=pl.BlockSpec((tm, tn), lambda i, j: (i, j)),
        out_shape=jax.ShapeDtypeStruct((M, N), F32),
        scratch_shapes=[pltpu.VMEM((tm, K), BF16)],
        compiler_params=_cparams(("parallel", "arbitrary")),
    )(x, g, w)


def _out_proj_kernel(n_in, *refs):
    h_ref = refs[0]
    ys = refs[1:1 + n_in]
    ws = refs[1 + n_in:1 + 2 * n_in]
    o_ref = refs[1 + 2 * n_in]
    acc = h_ref[...]
    for y_ref, w_ref in zip(ys, ws):
        acc = acc + _mm(y_ref[...], w_ref[...])
    o_ref[...] = acc


def _out_proj(h, ys, ws):
    M, D = h.shape
    tm = min(TM_TOK, M)
    n_in = len(ys)
    in_specs = [pl.BlockSpec((tm, D), lambda i: (i, 0))]
    in_specs += [pl.BlockSpec((tm, y.shape[1]), lambda i: (i, 0)) for y in ys]
    in_specs += [pl.BlockSpec(w.shape, lambda i: (0, 0)) for w in ws]
    return pl.pallas_call(
        functools.partial(_out_proj_kernel, n_in),
        grid=(M // tm,),
        in_specs=in_specs,
        out_specs=pl.BlockSpec((tm, D), lambda i: (i, 0)),
        out_shape=jax.ShapeDtypeStruct((M, D), F32),
        compiler_params=_cparams(("parallel",)),
    )(h, *ys, *ws)


def _ffn_kernel(h_ref, g_ref, wg_ref, wu_ref, wd_ref, o_ref, xn_ref, acc_ref):
    j = pl.program_id(1)

    @pl.when(j == 0)
    def _():
        xn_ref[...] = _rms(h_ref[...], g_ref[...]).astype(BF16)
        acc_ref[...] = jnp.zeros_like(acc_ref)

    xn = xn_ref[...]
    act = _silu(_d(xn, wg_ref[...])) * _d(xn, wu_ref[...])
    acc_ref[...] += _mm(act, wd_ref[...])

    @pl.when(j == pl.num_programs(1) - 1)
    def _():
        o_ref[...] = h_ref[...] + acc_ref[...]


def _ffn(h, g, wg, wu, wd, tf):
    M, D = h.shape
    FF = wg.shape[1]
    tm = min(TM_FFN, M)
    return pl.pallas_call(
        _ffn_kernel,
        grid=(M // tm, FF // tf),
        in_specs=[pl.BlockSpec((tm, D), lambda i, j: (i, 0)),
                  pl.BlockSpec((1, D), lambda i, j: (0, 0)),
                  pl.BlockSpec((D, tf), lambda i, j: (0, j)),
                  pl.BlockSpec((D, tf), lambda i, j: (0, j)),
                  pl.BlockSpec((tf, D), lambda i, j: (j, 0))],
        out_specs=pl.BlockSpec((tm, D), lambda i, j: (i, 0)),
        out_shape=jax.ShapeDtypeStruct((M, D), F32),
        scratch_shapes=[pltpu.VMEM((tm, D), BF16), pltpu.VMEM((tm, D), F32)],
        compiler_params=_cparams(("parallel", "arbitrary")),
    )(h, g, wg, wu, wd)


def _router_kernel(h_ref, g_ref, wr_ref, gates_ref):
    xn = _rms(h_ref[...], g_ref[...])
    logits = _mm3(xn, wr_ref[...])
    lane = lax.broadcasted_iota(jnp.int32, logits.shape, 1)
    valid = lane < N_EXP
    logits = jnp.where(valid, logits, -1e30)
    m = jnp.max(logits, axis=-1, keepdims=True)
    e = jnp.where(valid, jnp.exp(logits - m), 0.0)
    p = e / jnp.sum(e, axis=-1, keepdims=True)
    lane_f = lane.astype(F32)
    m1 = jnp.max(p, axis=-1, keepdims=True)
    i1 = jnp.min(jnp.where(p == m1, lane_f, float(LANES)), axis=-1, keepdims=True)
    p2 = jnp.where((lane_f == i1) | (lane >= N_EXP), -1.0, p)
    m2 = jnp.max(p2, axis=-1, keepdims=True)
    i2 = jnp.min(jnp.where(p2 == m2, lane_f, float(LANES)), axis=-1, keepdims=True)
    tot = m1 + m2
    gates_ref[...] = jnp.where(lane_f == i1, m1 / tot, jnp.where(lane_f == i2, m2 / tot, 0.0))


def _router(h, g, wr):
    M, D = h.shape
    tm = min(TM_TOK, M)
    return pl.pallas_call(
        _router_kernel,
        grid=(M // tm,),
        in_specs=[pl.BlockSpec((tm, D), lambda i: (i, 0)),
                  pl.BlockSpec((1, D), lambda i: (0, 0)),
                  pl.BlockSpec((D, LANES), lambda i: (0, 0))],
        out_specs=pl.BlockSpec((tm, LANES), lambda i: (i, 0)),
        out_shape=jax.ShapeDtypeStruct((M, LANES), F32),
        compiler_params=_cparams(("parallel",)),
    )(h, g, wr)


def _moe_kernel(h_ref, g_ref, gates_ref, wg_ref, wu_ref, wd_ref, o_ref, xn_ref, acc_ref):
    e = pl.program_id(1)
    j = pl.program_id(2)

    @pl.when((e == 0) & (j == 0))
    def _():
        xn_ref[...] = _rms(h_ref[...], g_ref[...]).astype(BF16)
        acc_ref[...] = jnp.zeros_like(acc_ref)

    gates = gates_ref[...]
    lane = lax.broadcasted_iota(jnp.int32, gates.shape, 1)
    gcol = jnp.sum(jnp.where(lane == e, gates, 0.0), axis=-1, keepdims=True)
    xn = xn_ref[...]
    act = _silu(_d(xn, wg_ref[...])) * _d(xn, wu_ref[...])
    acc_ref[...] += _mm(act * gcol, wd_ref[...])

    @pl.when((e == pl.num_programs(1) - 1) & (j == pl.num_programs(2) - 1))
    def _():
        o_ref[...] = h_ref[...] + acc_ref[...]


def _moe(h, g, gates, wg, wu, wd, tf):
    M, D = h.shape
    E, _, FF = wg.shape
    tm = min(TM_FFN, M)
    return pl.pallas_call(
        _moe_kernel,
        grid=(M // tm, E, FF // tf),
        in_specs=[pl.BlockSpec((tm, D), lambda i, e, j: (i, 0)),
                  pl.BlockSpec((1, D), lambda i, e, j: (0, 0)),
                  pl.BlockSpec((tm, LANES), lambda i, e, j: (i, 0)),
                  pl.BlockSpec((None, D, tf), lambda i, e, j: (e, 0, j)),
                  pl.BlockSpec((None, D, tf), lambda i, e, j: (e, 0, j)),
                  pl.BlockSpec((None, tf, D), lambda i, e, j: (e, j, 0))],
        out_specs=pl.BlockSpec((tm, D), lambda i, e, j: (i, 0)),
        out_shape=jax.ShapeDtypeStruct((M, D), F32),
        scratch_shapes=[pltpu.VMEM((tm, D), BF16), pltpu.VMEM((tm, D), F32)],
        compiler_params=_cparams(("parallel", "arbitrary", "arbitrary")),
    )(h, g, gates, wg, wu, wd)


def _ple_kernel(final, h_ref, g_ref, wg_ref, p_ref, wp_ref, gf_ref, o_ref):
    h = h_ref[...]
    gate = _sigmoid(_mm(_rms(h, g_ref[...]), wg_ref[...]))
    out = h + gate * _mm(p_ref[...], wp_ref[...])
    if final:
        out = _rms(out, gf_ref[...])
    o_ref[...] = out


def _ple(h, g, wg, p, wp, gf, final):
    M, D = h.shape
    tm = min(TM_TOK, M)
    return pl.pallas_call(
        functools.partial(_ple_kernel, final),
        grid=(M // tm,),
        in_specs=[pl.BlockSpec((tm, D), lambda i: (i, 0)),
                  pl.BlockSpec((1, D), lambda i: (0, 0)),
                  pl.BlockSpec((D, D), lambda i: (0, 0)),
                  pl.BlockSpec((tm, D_PLE), lambda i: (i, 0)),
                  pl.BlockSpec((D_PLE, D), lambda i: (0, 0)),
                  pl.BlockSpec((1, D), lambda i: (0, 0))],
        out_specs=pl.BlockSpec((tm, D), lambda i: (i, 0)),
        out_shape=jax.ShapeDtypeStruct((M, D), F32),
        compiler_params=_cparams(("parallel",)),
    )(h, g, wg, p, wp, gf)


def _lru_gates(xc, ga_unused, wr, br, wi, bi, lam):
    r = _sigmoid(_mm(xc, wr) + br)
    i = _sigmoid(_mm(xc, wi) + bi)
    log_a = -LRU_C * r * _softplus(-lam)
    a = jnp.exp(log_a)
    u = jnp.sqrt(jnp.tanh(-log_a) * (a * a + 1.0)) * (i * xc)
    return a, u


def _lru_seq_kernel(TL, proj_ref, cs_ref, h0_ref, cw_ref, cb_ref, wr_ref, br_ref, wi_ref, bi_ref, lam_ref,
                    y_ref, hT_ref, pad_ref, a_ref, u_ref, hs_ref, hc_ref):
    ti = pl.program_id(0)
    G = pad_ref.shape[0]

    @pl.when(ti == 0)
    def _():
        pad_ref[:, SUBLANES - (CONV_W - 1):SUBLANES, :] = cs_ref[...]
        hc_ref[...] = h0_ref[...]

    pad_ref[:, SUBLANES:, :] = proj_ref[:, :, :D_A]
    cw = cw_ref[...]
    xc = cb_ref[...][None]
    for k in range(CONV_W):
        off = SUBLANES - (CONV_W - 1) + k
        xc = xc + pad_ref[:, off:off + TL, :] * cw[k:k + 1, :][None]
    pad_ref[:, :SUBLANES, :] = pad_ref[:, TL:TL + SUBLANES, :]

    a, u = _lru_gates(xc.reshape(G * TL, D_A), None, wr_ref[...], br_ref[...], wi_ref[...], bi_ref[...],
                      lam_ref[...])
    a_ref[...] = a.reshape(G, TL, D_A)
    u_ref[...] = u.reshape(G, TL, D_A)

    def step(t, h):
        h = a_ref[:, pl.ds(t, 1), :] * h + u_ref[:, pl.ds(t, 1), :]
        hs_ref[:, pl.ds(t, 1), :] = h
        return h

    hc_ref[...] = lax.fori_loop(0, TL, step, hc_ref[...], unroll=8)
    y_ref[...] = _gelu_tanh(proj_ref[:, :, D_A:]) * hs_ref[...]

    @pl.when(ti == pl.num_programs(0) - 1)
    def _():
        hT_ref[...] = hc_ref[...]


def _lru_step_kernel(xa_ref, ga_ref, cs_ref, h0_ref, cw_ref, cb_ref, wr_ref, br_ref, wi_ref, bi_ref, lam_ref,
                     y_ref, hT_ref):
    cw = cw_ref[...]
    x = xa_ref[...]
    xc = cb_ref[...] + x * cw[CONV_W - 1:CONV_W, :]
    for k in range(CONV_W - 1):
        xc = xc + cs_ref[k] * cw[k:k + 1, :]
    a, u = _lru_gates(xc, None, wr_ref[...], br_ref[...], wi_ref[...], bi_ref[...], lam_ref[...])
    h = a * h0_ref[...] + u
    hT_ref[...] = h
    y_ref[...] = _gelu_tanh(ga_ref[...]) * h


def _lru(proj, conv_state, h0, cw, cb, wr, br, wi, bi, lam):
    B, L, _ = proj.shape
    small = [cw, cb, wr, br, wi, bi, lam]
    if L == 1:
        proj2 = proj.reshape(B, -1)
        cs = jnp.transpose(conv_state, (1, 0, 2))
        tb = min(B, 128)
        full = lambda s: pl.BlockSpec(s.shape, lambda i: (0,) * s.ndim)
        y, hT = pl.pallas_call(
            _lru_step_kernel,
            grid=(B // tb,),
            in_specs=[pl.BlockSpec((tb, D_A), lambda i: (i, 0)),
                      pl.BlockSpec((tb, D_A), lambda i: (i, 1)),
                      pl.BlockSpec((CONV_W - 1, tb, D_A), lambda i: (0, i, 0)),
                      pl.BlockSpec((tb, D_A), lambda i: (i, 0))] + [full(s) for s in small],
            out_specs=[pl.BlockSpec((tb, D_A), lambda i: (i, 0)), pl.BlockSpec((tb, D_A), lambda i: (i, 0))],
            out_shape=[jax.ShapeDtypeStruct((B, D_A), F32), jax.ShapeDtypeStruct((B, D_A), F32)],
            compiler_params=_cparams(("parallel",)),
        )(proj2, proj2, cs, h0, *small)
        return y.reshape(B, 1, D_A), hT
    TL = min(LRU_TL, L)
    G = B
    full = lambda s: pl.BlockSpec(s.shape, lambda i: (0,) * s.ndim)
    y, hT = pl.pallas_call(
        functools.partial(_lru_seq_kernel, TL),
        grid=(L // TL,),
        in_specs=[pl.BlockSpec((G, TL, 2 * D_A), lambda i: (0, i, 0)),
                  pl.BlockSpec((G, CONV_W - 1, D_A), lambda i: (0, 0, 0)),
                  pl.BlockSpec((G, 1, D_A), lambda i: (0, 0, 0))] + [full(s) for s in small],
        out_specs=[pl.BlockSpec((G, TL, D_A), lambda i: (0, i, 0)),
                   pl.BlockSpec((G, 1, D_A), lambda i: (0, 0, 0))],
        out_shape=[jax.ShapeDtypeStruct((B, L, D_A), F32), jax.ShapeDtypeStruct((B, 1, D_A), F32)],
        scratch_shapes=[pltpu.VMEM((G, SUBLANES + TL, D_A), F32),
                        pltpu.VMEM((G, TL, D_A), F32), pltpu.VMEM((G, TL, D_A), F32),
                        pltpu.VMEM((G, TL, D_A), F32), pltpu.VMEM((G, 1, D_A), F32)],
        compiler_params=_cparams(("arbitrary",)),
    )(proj, conv_state, h0.reshape(B, 1, D_A), *small)
    return y, hT.reshape(B, D_A)


def _rwkv_prep(xs, w0, w2p, a0, a2p, g2, k_k, k_a, r_k, hsel):
    r = xs[:, 0:D_B]
    k = xs[:, D_B:2 * D_B]
    v = xs[:, 2 * D_B:3 * D_B]
    wa = xs[:, 3 * D_B:3 * D_B + R_W + R_A]
    gl = xs[:, 3 * D_B + R_W + R_A:]
    w_log = -_softplus(-(w0 + _mm(jnp.tanh(wa), w2p))) - 0.5
    logd = -jnp.exp(w_log)
    a = _sigmoid(a0 + _mm(wa, a2p))
    g = _mm(_sigmoid(gl), g2)
    kk = k * k_k
    kk = kk * lax.rsqrt(jnp.maximum(_mm_sel(hsel, kk * kk, 3, sel_left=False), 1e-12))
    k2 = k * (1.0 + (a - 1.0) * k_a)
    beta = kk * a
    bonus = _mm_sel(hsel, r * k2 * r_k, 3, sel_left=False) * v
    return r, k2, v, logd, kk, beta, g, bonus


def _rwkv_post(y, bonus, g, ln_w, ln_b, hsel):
    mean = _mm_sel(hsel, y, 3, sel_left=False) * (1.0 / HD_B)
    d = y - mean
    var = _mm_sel(hsel, d * d, 3, sel_left=False) * (1.0 / HD_B)
    yn = d * lax.rsqrt(var + GN_EPS_B) * ln_w + ln_b
    return (yn + bonus) * g


def _mmp(a, b, dims, passes):
    return _mm3(a, b, dims) if passes == 3 else _mm(a, b, dims)


RWKV_PASSES = dict(am=1, state=1, inv=1, apply=1, dw=1)


def _rwkv_chunk(G, T, ops, st_ref, masks, emask, store_y):
    r, k2, v, logd, kk, beta = ops
    incl, strict, lmask_bf, lastmask_bf = masks
    R = G * T
    ps = RWKV_PASSES
    cum = _mm_sel(lmask_bf, logd, 3)
    cend = _mm_sel(lastmask_bf, cum, 3)
    ecn = jnp.exp(-cum)
    P = jnp.exp(cum - logd) * kk
    Q = ecn * beta
    Kt = ecn * k2
    Rt = jnp.exp(cum) * r
    eend = jnp.exp(cend)
    lo = lax.broadcasted_iota(jnp.int32, (R, LANES), 1) < HD_B
    row_lo = lax.broadcasted_iota(jnp.int32, (LANES, G * HD_B), 0) < HD_B
    eye = None
    if T > 1:
        ri = lax.broadcasted_iota(jnp.int32, (R, R), 0)
        ci = lax.broadcasted_iota(jnp.int32, (R, R), 1)
        eye = jnp.where(ri == ci, 1.0, 0.0)
    n_rep = G * HD_B // LANES
    NP = H_B // 2
    heads = [(c, hh) for c in range(NP) for hh in range(2)]
    own = {0: lo, 1: ~lo}

    def tiled(x, xr, hh):
        t = jnp.where(lo, x, xr) if hh == 0 else jnp.where(lo, xr, x)
        return jnp.where(emask, jnp.concatenate([t] * n_rep, axis=1), 0.0)

    pair = []
    for c in range(NP):
        sl = slice(c * LANES, (c + 1) * LANES)
        vals = (P[:, sl], Rt[:, sl], Q[:, sl], Kt[:, sl], eend[:, sl])
        pair.append(dict(x=vals, xr=[pltpu.roll(x, HD_B, 1) for x in vals], V=v[:, sl]))
    AM = [_mmp(jnp.concatenate([jnp.where(own[hh], pair[c]['x'][0], 0.0),
                                jnp.where(own[hh], pair[c]['x'][1], 0.0)], axis=0),
               jnp.concatenate([pair[c]['x'][2], pair[c]['x'][3]], axis=0), NT, ps['am']) for c, hh in heads]
    Mqr = [jnp.where(incl, a[R:, :R], 0.0) for a in AM]
    Mkr = [jnp.where(incl, a[R:, R:], 0.0) for a in AM]
    MV = [_mmp(m, pair[c]['V'], NN, ps['apply']) for m, (c, hh) in zip(Mkr, heads)]
    if T > 1:
        Npow = [-jnp.where(strict, a[:R, :R], 0.0) for a in AM]
        Akp = [jnp.where(strict, a[:R, R:], 0.0) for a in AM]
        AV = [_mmp(m, pair[c]['V'], NN, ps['apply']) for m, (c, hh) in zip(Akp, heads)]
        inv = [eye + n for n in Npow]
        for _ in range(int(math.log2(T)) - 1):
            Npow = [_mmp(n, n, NN, ps['inv']) for n in Npow]
            inv = [i + _mmp(i, n, NN, ps['inv']) for i, n in zip(inv, Npow)]
    ex = [[tiled(x, xr, hh) for x, xr in zip(pair[c]['x'], pair[c]['xr'])] for c, hh in heads]
    WT = [st_ref[c] for c in range(NP)]
    PWRW = [_mmp(jnp.concatenate([e[0], e[1]], axis=0), WT[c], NT, ps['state']) for e, (c, hh) in zip(ex, heads)]
    if T > 1:
        E = [_mmp(i, pw[:R] + av, NN, ps['apply']) for i, pw, av in zip(inv, PWRW, AV)]
    else:
        E = [pw[:R] for pw in PWRW]
    Y = [pw[R:] + mv - _mmp(m, e, NN, ps['apply']) for pw, mv, m, e in zip(PWRW, MV, Mqr, E)]
    for c in range(NP):
        lhs = jnp.concatenate([jnp.where(own[hh], x, 0.0) for hh in range(2)
                               for x in (-E[2 * c + hh], pair[c]['V'])], axis=0)
        rhs = jnp.concatenate([ex[2 * c + hh][j] for hh in range(2) for j in (2, 3)], axis=0)
        dW = _mmp(lhs, rhs, TN, ps['dw'])
        erow = [jnp.sum(ex[2 * c + hh][4], axis=0, keepdims=True) * (1.0 / T) for hh in range(2)]
        st_ref[c] = (WT[c] + dW) * jnp.where(row_lo, erow[0], erow[1])
        store_y(c, jnp.where(lo, Y[2 * c], Y[2 * c + 1]))


def _rwkv_kernel(G, T, NC, seq, proj_ref, sh_ref, s0_ref, mu_ref, w0_ref, w2_ref, a0_ref, a2_ref, g2_ref,
                 kk_ref, ka_ref, rk_ref, lnw_ref, lnb_ref, y_ref, sT_ref, *scratch):
    ti = pl.program_id(1)
    TL = T * NC
    R = G * T
    st_ref = scratch[0]
    hsel = _head_sum_sel(D_B, HD_B)

    @pl.when(ti == 0)
    def _():
        st_ref[...] = s0_ref[...]

    if seq:
        pad_ref, ops_ref, gb_ref, ys_ref = scratch[1:]

        @pl.when(ti == 0)
        def _():
            pad_ref[:, SUBLANES - 1:SUBLANES, :] = sh_ref[...]

        pad_ref[:, SUBLANES:, :] = proj_ref[:, :, 2 * D_A:]
        pb = pad_ref[:, SUBLANES:, :]
        prev = pad_ref[:, SUBLANES - 1:SUBLANES - 1 + TL, :]
        xs = (pb + (prev - pb) * mu_ref[...][None]).reshape(G * TL, D_BPROJ)
        pad_ref[:, :SUBLANES, :] = pad_ref[:, TL:TL + SUBLANES, :]
    else:
        pb = proj_ref[:, 2 * D_A:]
        xs = pb + (sh_ref[...] - pb) * mu_ref[...]

    r, k2, v, logd, kk, beta, g, bonus = _rwkv_prep(
        xs, w0_ref[...], w2_ref[...], a0_ref[...], a2_ref[...], g2_ref[...], kk_ref[...], ka_ref[...],
        rk_ref[...], hsel)

    incl, strict, last = _chunk_masks(G, T)
    masks = (incl, strict, _to_bf16_mask(incl), _to_bf16_mask(last))
    emask = _batch_lane_mask(1, G, T, HD_B)

    if seq:
        for n, val in enumerate((r, k2, v, logd, kk, beta)):
            ops_ref[n] = val.reshape(G, TL, D_B)
        gb_ref[0] = g.reshape(G, TL, D_B)
        gb_ref[1] = bonus.reshape(G, TL, D_B)

        def chunk(c, carry):
            t0 = pl.multiple_of(c * T, T)
            ops = tuple(ops_ref[n, :, pl.ds(t0, T), :].reshape(R, D_B) for n in range(6))

            def store_y(hp, Y):
                ys_ref[:, pl.ds(t0, T), hp * LANES:(hp + 1) * LANES] = Y.reshape(G, T, LANES)

            _rwkv_chunk(G, T, ops, st_ref, masks, emask, store_y)
            return carry

        lax.fori_loop(0, NC, chunk, 0)
        y = ys_ref[...].reshape(G * TL, D_B)
        out = _rwkv_post(y, gb_ref[1].reshape(G * TL, D_B), gb_ref[0].reshape(G * TL, D_B),
                         lnw_ref[...], lnb_ref[...], hsel)
        y_ref[...] = out.reshape(G, TL, D_B)
    else:
        ys_ref = scratch[1]

        def store_y(hp, Y):
            ys_ref[:, hp * LANES:(hp + 1) * LANES] = Y

        _rwkv_chunk(G, T, (r, k2, v, logd, kk, beta), st_ref, masks, emask, store_y)
        y_ref[...] = _rwkv_post(ys_ref[...], bonus, g, lnw_ref[...], lnb_ref[...], hsel)

    @pl.when(ti == pl.num_programs(1) - 1)
    def _():
        sT_ref[...] = st_ref[...]


def _state_to_lanes(s, G):
    B, H, X, Y = s.shape
    return s.reshape(B // G, G, H, X, Y).transpose(0, 2, 3, 1, 4).reshape(B // G, H, X, G * Y)


def _state_from_lanes(s, G):
    NB, H, X, GY = s.shape
    Y = GY // G
    return s.reshape(NB, H, X, G, Y).transpose(0, 3, 1, 2, 4).reshape(NB * G, H, X, Y)


def _rwkv(proj, shift0, s0, mu, w0, w2p, a0, a2p, g2, k_k, k_a, r_k, ln_w, ln_b):
    B, L, _ = proj.shape
    G = G_ROWS
    NB = B // G
    small = [mu, w0, w2p, a0, a2p, g2, k_k, k_a, r_k, ln_w, ln_b]
    full = lambda s: pl.BlockSpec(s.shape, lambda b, i: (0,) * s.ndim)
    st_dims = (H_B // 2, 2 * HD_B, G * HD_B)
    s0l = _state_to_lanes(s0, G).reshape((NB,) + st_dims)
    st_spec = pl.BlockSpec((None,) + st_dims, lambda b, i: (b, 0, 0, 0))
    st_shape = jax.ShapeDtypeStruct(s0l.shape, F32)
    unpair = lambda s: _state_from_lanes(s.reshape(NB, H_B, HD_B, G * HD_B), G)
    if L == 1:
        T, NC = 1, 1
        kern = functools.partial(_rwkv_kernel, G, T, NC, False)
        y, sT = pl.pallas_call(
            kern,
            grid=(NB, 1),
            in_specs=[pl.BlockSpec((G, IN_AB), lambda b, i: (b, 0)),
                      pl.BlockSpec((G, D_BPROJ), lambda b, i: (b, 0)), st_spec] + [full(s) for s in small],
            out_specs=[pl.BlockSpec((G, D_B), lambda b, i: (b, 0)), st_spec],
            out_shape=[jax.ShapeDtypeStruct((B, D_B), F32), st_shape],
            scratch_shapes=[pltpu.VMEM(st_dims, F32), pltpu.VMEM((G, D_B), F32)],
            compiler_params=_cparams(("parallel", "arbitrary")),
        )(proj.reshape(B, IN_AB), shift0, s0l, *small)
        return y.reshape(B, 1, D_B), unpair(sT)
    T = min(T_CHUNK, L)
    TL = min(TL_BLOCK, L)
    NC = TL // T
    kern = functools.partial(_rwkv_kernel, G, T, NC, True)
    y, sT = pl.pallas_call(
        kern,
        grid=(NB, L // TL),
        in_specs=[pl.BlockSpec((G, TL, IN_AB), lambda b, i: (b, i, 0)),
                  pl.BlockSpec((G, 1, D_BPROJ), lambda b, i: (b, 0, 0)), st_spec] + [full(s) for s in small],
        out_specs=[pl.BlockSpec((G, TL, D_B), lambda b, i: (b, i, 0)), st_spec],
        out_shape=[jax.ShapeDtypeStruct((B, L, D_B), F32), st_shape],
        scratch_shapes=[pltpu.VMEM(st_dims, F32),
                        pltpu.VMEM((G, SUBLANES + TL, D_BPROJ), F32),
                        pltpu.VMEM((6, G, TL, D_B), F32),
                        pltpu.VMEM((2, G, TL, D_B), F32),
                        pltpu.VMEM((G, TL, D_B), F32)],
        compiler_params=_cparams(("parallel", "arbitrary")),
    )(proj, shift0.reshape(B, 1, D_BPROJ), s0l, *small)
    return y, unpair(sT)


CD_Z, CD_XBC, CD_Q, CD_K, CD_V, CD_G, CD_S = 0, 512, 1536, 1792, 2048, 2560, 3072


def _ssd_chunk(G, T, xs, Bm, Cm, dt_all, la_all, st_ref, masks, tileN, emaskN, store_y):
    incl, lmask_bf, lastmask_bf = masks
    cum = _mm_sel(lmask_bf, la_all, 3)
    cend = _mm_sel(lastmask_bf, cum, 3)
    cumT = cum.T
    ecum = jnp.exp(cum)
    wend = jnp.exp(cend - cum) * dt_all
    dend = jnp.exp(cend)
    for g in range(G_C):
        Bg = Bm[:, g * N_C:(g + 1) * N_C]
        Cg = Cm[:, g * N_C:(g + 1) * N_C]
        CB = _mm(Cg, Bg, NT)
        Bge = jnp.where(emaskN, _mm(Bg, tileN), 0.0)
        Cge = jnp.where(emaskN, _mm(Cg, tileN), 0.0)
        for hh in range(HG_C):
            h = g * HG_C + hh
            xh = xs[:, h * P_C:(h + 1) * P_C]
            seg = jnp.exp(jnp.where(incl, cum[:, h:h + 1] - cumT[h:h + 1, :], -1e30))
            ST = st_ref[h]
            y = _mm(CB * seg, xh * dt_all[:, h:h + 1]) + ecum[:, h:h + 1] * _mm(Cge, ST, NT)
            drow = jnp.sum(jnp.where(emaskN, dend[:, h:h + 1], 0.0), axis=0, keepdims=True) * (1.0 / T)
            st_ref[h] = ST * drow + _mm(xh * wend[:, h:h + 1], Bge, TN)
            store_y(h, y)


def _gla_chunk(G, T, q, k, v, la, st_ref, masks, tile64, emask1, store_o):
    incl, lmask_bf, lastmask_bf = masks
    bc = _mm_sel(lmask_bf, la, 3)
    bend = _mm_sel(lastmask_bf, bc, 3)
    qe = q * jnp.exp(bc)
    kn = k * jnp.exp(-bc)
    kw = k * jnp.exp(bend - bc)
    dend = jnp.exp(bend)
    for h in range(H_D):
        sl = slice(h * DK_D, (h + 1) * DK_D)
        vh = v[:, h * DV_D:(h + 1) * DV_D]
        att = jnp.where(incl, _mm(qe[:, sl], kn[:, sl], NT), 0.0)
        qee = jnp.where(emask1, _mm(qe[:, sl], tile64), 0.0)
        kwe = jnp.where(emask1, _mm(kw[:, sl], tile64), 0.0)
        ST = st_ref[h]
        o = _mm(att, vh) + _mm(qee, ST, NT)
        drow = jnp.sum(jnp.where(emask1, _mm_sel(tile64, dend[:, sl], 3, sel_left=False), 0.0),
                       axis=0, keepdims=True) * (1.0 / T)
        st_ref[h] = ST * drow + _mm(vh, kwe, TN)
        store_o(h, o)


def _cd_prep_small(small, dtb, a_neg, aup, ab):
    dt_all = _softplus(small + dtb)
    la_all = dt_all * a_neg
    la_gla = -_softplus(-(_mm(small, aup) + ab)) * (1.0 / GATE_NORM)
    return dt_all, la_all, la_gla


def _cd_post(y, xs, z, o, gg, dskip, ssd_norm, gla_norm):
    y_c = _rms((y + dskip * xs) * _silu(z), ssd_norm)
    outs = [y_c]
    for h in range(H_D):
        sl = slice(h * DV_D, (h + 1) * DV_D)
        outs.append(_rms(o[:, sl], gla_norm) * _silu(gg[:, sl]))
    return outs


def _cd_kernel(G, T, NC, seq, proj_ref, cs_ref, ssd0_ref, gla0_ref, cw_ref, cb_ref, dtb_ref, aneg_ref, dskip_ref,
               ssdn_ref, aup_ref, ab_ref, glan_ref, y_ref, ssdT_ref, glaT_ref, *scratch):
    ti = pl.program_id(1)
    TL = T * NC
    R = G * T
    sst_ref, gst_ref = scratch[0], scratch[1]

    @pl.when(ti == 0)
    def _():
        sst_ref[...] = ssd0_ref[...]
        gst_ref[...] = gla0_ref[...]

    cw = cw_ref[...]
    if seq:
        pad_ref, xbc_ref, sm_ref, ys_ref, os_ref = scratch[2:]

        @pl.when(ti == 0)
        def _():
            pad_ref[:, SUBLANES - (CONV_W - 1):SUBLANES, :] = cs_ref[...]

        pad_ref[:, SUBLANES:, :] = proj_ref[:, :, CD_XBC:CD_XBC + D_XBC]
        xc = cb_ref[...][None]
        for kq in range(CONV_W):
            off = SUBLANES - (CONV_W - 1) + kq
            xc = xc + pad_ref[:, off:off + TL, :] * cw[kq:kq + 1, :][None]
        pad_ref[:, :SUBLANES, :] = pad_ref[:, TL:TL + SUBLANES, :]
        xbc_ref[...] = _silu(xc)
        small = proj_ref[:, :, CD_S:].reshape(G * TL, LANES)
    else:
        x = proj_ref[:, CD_XBC:CD_XBC + D_XBC]
        xc = cb_ref[...] + x * cw[CONV_W - 1:CONV_W, :]
        for kq in range(CONV_W - 1):
            xc = xc + cs_ref[kq] * cw[kq:kq + 1, :]
        xbc = _silu(xc)
        small = proj_ref[:, CD_S:]

    dt_all, la_all, la_gla = _cd_prep_small(small, dtb_ref[...], aneg_ref[...], aup_ref[...], ab_ref[...])

    incl, _, last = _chunk_masks(G, T)
    masks = (incl, _to_bf16_mask(incl), _to_bf16_mask(last))
    tileN = _tile_sel(N_C, G)
    tile64 = _tile_sel(DK_D, G)
    emaskN = _batch_lane_mask(1, G, T, N_C)
    emask1 = _batch_lane_mask(1, G, T, DK_D)
    scale = DK_D ** -0.5

    if seq:
        sm_ref[0] = dt_all.reshape(G, TL, LANES)
        sm_ref[1] = la_all.reshape(G, TL, LANES)
        sm_ref[2] = la_gla[:, :LANES].reshape(G, TL, LANES)
        sm_ref[3] = la_gla[:, LANES:].reshape(G, TL, LANES)

        def chunk(c, carry):
            t0 = pl.multiple_of(c * T, T)
            rows = lambda ref, lo, hi: ref[:, pl.ds(t0, T), lo:hi].reshape(R, hi - lo)
            xs = rows(xbc_ref, 0, D_C)
            Bm = rows(xbc_ref, D_C, D_C + G_C * N_C)
            Cm = rows(xbc_ref, D_C + G_C * N_C, D_XBC)
            dtc = sm_ref[0, :, pl.ds(t0, T), :].reshape(R, LANES)
            lac = sm_ref[1, :, pl.ds(t0, T), :].reshape(R, LANES)

            def store_y(h, yv):
                ys_ref[:, pl.ds(t0, T), h * P_C:(h + 1) * P_C] = yv.reshape(G, T, P_C)

            _ssd_chunk(G, T, xs, Bm, Cm, dtc, lac, sst_ref, masks, tileN, emaskN, store_y)

            q = rows(proj_ref, CD_Q, CD_Q + K_D) * scale
            k = rows(proj_ref, CD_K, CD_K + K_D)
            v = rows(proj_ref, CD_V, CD_V + V_D)
            lag = jnp.concatenate([sm_ref[2, :, pl.ds(t0, T), :].reshape(R, LANES),
                                   sm_ref[3, :, pl.ds(t0, T), :].reshape(R, LANES)], axis=1)

            def store_o(h, ov):
                os_ref[:, pl.ds(t0, T), h * DV_D:(h + 1) * DV_D] = ov.reshape(G, T, DV_D)

            _gla_chunk(G, T, q, k, v, lag, gst_ref, masks, tile64, emask1, store_o)
            return carry

        lax.fori_loop(0, NC, chunk, 0)
        flat = lambda a: a.reshape(G * TL, a.shape[-1])
        outs = _cd_post(flat(ys_ref[...]), flat(xbc_ref[:, :, :D_C]), flat(proj_ref[:, :, CD_Z:CD_Z + D_C]),
                        flat(os_ref[...]), flat(proj_ref[:, :, CD_G:CD_G + V_D]),
                        dskip_ref[...], ssdn_ref[...], glan_ref[...])
        y_ref[:, :, :D_C] = outs[0].reshape(G, TL, D_C)
        for h in range(H_D):
            y_ref[:, :, D_C + h * DV_D:D_C + (h + 1) * DV_D] = outs[1 + h].reshape(G, TL, DV_D)
    else:
        ys_ref, os_ref = scratch[2:]

        def store_y(h, yv):
            ys_ref[:, h * P_C:(h + 1) * P_C] = yv

        def store_o(h, ov):
            os_ref[:, h * DV_D:(h + 1) * DV_D] = ov

        _ssd_chunk(G, T, xbc[:, :D_C], xbc[:, D_C:D_C + G_C * N_C], xbc[:, D_C + G_C * N_C:], dt_all, la_all,
                   sst_ref, masks, tileN, emaskN, store_y)
        _gla_chunk(G, T, proj_ref[:, CD_Q:CD_Q + K_D] * scale, proj_ref[:, CD_K:CD_K + K_D],
                   proj_ref[:, CD_V:CD_V + V_D], la_gla, gst_ref, masks, tile64, emask1, store_o)
        outs = _cd_post(ys_ref[...], xbc[:, :D_C], proj_ref[:, CD_Z:CD_Z + D_C], os_ref[...],
                        proj_ref[:, CD_G:CD_G + V_D], dskip_ref[...], ssdn_ref[...], glan_ref[...])
        y_ref[:, :D_C] = outs[0]
        for h in range(H_D):
            y_ref[:, D_C + h * DV_D:D_C + (h + 1) * DV_D] = outs[1 + h]

    @pl.when(ti == pl.num_programs(1) - 1)
    def _():
        ssdT_ref[...] = sst_ref[...]
        glaT_ref[...] = gst_ref[...]


def _cd(proj, conv_state, ssd0, gla0, cw, cb, dtb, a_neg, dskip, ssd_norm, aup, ab, gla_norm):
    B, L, _ = proj.shape
    G = G_ROWS
    NB = B // G
    small = [cw, cb, dtb, a_neg, dskip, ssd_norm, aup, ab, gla_norm]
    full = lambda s: pl.BlockSpec(s.shape, lambda b, i: (0,) * s.ndim)
    ssd0l = _state_to_lanes(ssd0, G)
    gla0l = _state_to_lanes(jnp.swapaxes(gla0, 2, 3), G)
    ssd_spec = pl.BlockSpec((None, H_C, P_C, G * N_C), lambda b, i: (b, 0, 0, 0))
    gla_spec = pl.BlockSpec((None, H_D, DV_D, G * DK_D), lambda b, i: (b, 0, 0, 0))
    st_shapes = [jax.ShapeDtypeStruct(ssd0l.shape, F32), jax.ShapeDtypeStruct(gla0l.shape, F32)]
    st_scratch = [pltpu.VMEM((H_C, P_C, G * N_C), F32), pltpu.VMEM((H_D, DV_D, G * DK_D), F32)]
    DO = D_C + V_D
    if L == 1:
        kern = functools.partial(_cd_kernel, G, 1, 1, False)
        y, ssdT, glaT = pl.pallas_call(
            kern,
            grid=(NB, 1),
            in_specs=[pl.BlockSpec((G, IN_CD_PAD), lambda b, i: (b, 0)),
                      pl.BlockSpec((CONV_W - 1, G, D_XBC), lambda b, i: (0, b, 0)),
                      ssd_spec, gla_spec] + [full(s) for s in small],
            out_specs=[pl.BlockSpec((G, DO), lambda b, i: (b, 0)), ssd_spec, gla_spec],
            out_shape=[jax.ShapeDtypeStruct((B, DO), F32)] + st_shapes,
            scratch_shapes=st_scratch + [pltpu.VMEM((G, D_C), F32), pltpu.VMEM((G, V_D), F32)],
            compiler_params=_cparams(("parallel", "arbitrary")),
        )(proj.reshape(B, IN_CD_PAD), jnp.transpose(conv_state, (1, 0, 2)), ssd0l, gla0l, *small)
        y = y.reshape(B, 1, DO)
    else:
        T = min(T_CHUNK, L)
        TL = min(TL_BLOCK, L)
        NC = TL // T
        kern = functools.partial(_cd_kernel, G, T, NC, True)
        y, ssdT, glaT = pl.pallas_call(
            kern,
            grid=(NB, L // TL),
            in_specs=[pl.BlockSpec((G, TL, IN_CD_PAD), lambda b, i: (b, i, 0)),
                      pl.BlockSpec((G, CONV_W - 1, D_XBC), lambda b, i: (b, 0, 0)),
                      ssd_spec, gla_spec] + [full(s) for s in small],
            out_specs=[pl.BlockSpec((G, TL, DO), lambda b, i: (b, i, 0)), ssd_spec, gla_spec],
            out_shape=[jax.ShapeDtypeStruct((B, L, DO), F32)] + st_shapes,
            scratch_shapes=st_scratch + [pltpu.VMEM((G, SUBLANES + TL, D_XBC), F32),
                                         pltpu.VMEM((G, TL, D_XBC), F32),
                                         pltpu.VMEM((4, G, TL, LANES), F32),
                                         pltpu.VMEM((G, TL, D_C), F32),
                                         pltpu.VMEM((G, TL, V_D), F32)],
            compiler_params=_cparams(("parallel", "arbitrary")),
        )(proj, conv_state, ssd0l, gla0l, *small)
    ssd_new = _state_from_lanes(ssdT, G)
    gla_new = jnp.swapaxes(_state_from_lanes(glaT, G), 2, 3)
    return y, ssd_new, gla_new


def _block_diag(w):
    eye = jnp.eye(N_BLK_A, dtype=w.dtype)
    return jnp.einsum('nij,nm->nimj', w, eye).reshape(D_A, D_A)


def _row(v):
    return v.reshape(1, -1).astype(F32)


def _prep_params(p):
    q = {}
    j = 0
    q['w_in_ab'] = p['w_in_ab'][j].astype(BF16)
    q['w_out_ab'] = p['w_out_ab'][j].astype(BF16)
    q['lru'] = [p['lru_conv_w'][j], _row(p['lru_conv_b'][j]),
                _block_diag(p['lru_w_r'][j]).astype(BF16), _row(p['lru_b_r'][j]),
                _block_diag(p['lru_w_i'][j]).astype(BF16), _row(p['lru_b_i'][j]), _row(p['lru_lambda'][j])]
    zw = jnp.zeros((R_W, D_B), F32)
    w2p = jnp.concatenate([p['rwkv_w2'][j], zw], axis=0).astype(BF16)
    a2p = jnp.concatenate([zw, p['rwkv_a2'][j]], axis=0).astype(BF16)
    q['rwkv'] = [_row(p['rwkv_mu'][j]), _row(p['rwkv_w0'][j]), w2p, _row(p['rwkv_a0'][j]), a2p,
                 p['rwkv_g2'][j].astype(BF16), _row(p['rwkv_k_k'][j]), _row(p['rwkv_k_a'][j]),
                 _row(p['rwkv_r_k'][j]), _row(p['rwkv_ln_w'][j]), _row(p['rwkv_ln_b'][j])]
    q['ffn'] = [p['ffn_w_gate'][j].astype(BF16), p['ffn_w_up'][j].astype(BF16), p['ffn_w_down'][j].astype(BF16)]
    w = p['w_in_cd'][j]
    o_z, o_xbc, o_dt = 0, D_C, D_C + D_XBC
    o_q = o_dt + H_C
    o_k = o_q + K_D
    o_v = o_k + K_D
    o_g = o_v + V_D
    o_al = o_g + V_D
    pad = jnp.zeros((D_MODEL, LANES - H_C - R_ALPHA), F32)
    q['w_in_cd'] = jnp.concatenate([w[:, o_z:o_dt], w[:, o_q:o_al], w[:, o_dt:o_q], w[:, o_al:], pad],
                                   axis=1).astype(BF16)
    q['w_out_cd'] = p['w_out_cd'][j].astype(BF16)
    lane_pad = lambda v: jnp.concatenate([v.astype(F32), jnp.zeros((LANES - v.shape[0],), F32)]).reshape(1, LANES)
    aup = jnp.zeros((LANES, K_D), F32).at[H_C:H_C + R_ALPHA].set(p['gla_alpha_up'][j]).astype(BF16)
    q['cd'] = [p['ssd_conv_w'][j], _row(p['ssd_conv_b'][j]), lane_pad(p['ssd_dt_bias'][j]),
               lane_pad(-jnp.exp(p['ssd_a_log'][j].astype(F32))), _row(jnp.repeat(p['ssd_d'][j], P_C)),
               _row(p['ssd_norm_w'][j]), aup, _row(p['gla_alpha_b'][j]), _row(p['gla_norm_w'][j])]
    q['router'] = jnp.concatenate([p['moe_router'][j], jnp.zeros((D_MODEL, LANES - N_EXP), F32)], axis=1)
    q['moe'] = [p['moe_w_gate'][j].astype(BF16), p['moe_w_up'][j].astype(BF16), p['moe_w_down'][j].astype(BF16)]
    q['norm_mix'] = [_row(p['norm_mix'][i]) for i in range(DEPTH)]
    q['norm_ffn'] = [_row(p['norm_ffn'][i]) for i in range(DEPTH)]
    q['norm_ple'] = [_row(p['norm_ple'][i]) for i in range(DEPTH)]
    q['w_ple_gate'] = [p['w_ple_gate'][i].astype(BF16) for i in range(DEPTH)]
    q['w_ple_proj'] = [p['w_ple_proj'][i].astype(BF16) for i in range(DEPTH)]
    q['norm_final'] = _row(p['norm_final'])
    return q


def _trunk(x, p, states, q):
    lru_conv, lru_h, rwkv_shift, rwkv_wkv, ssd_conv, ssd_st, gla_st = states
    B, L, D = x.shape
    M = B * L
    h = x.reshape(M, D)
    p2 = p.reshape(DEPTH, M, D_PLE)

    proj = _norm_mm(h, q['norm_mix'][0], q['w_in_ab'], IN_AB // 2).reshape(B, L, IN_AB)
    y_a, h_new = _lru(proj, lru_conv[0], lru_h[0], *q['lru'])
    y_b, wkv_new = _rwkv(proj, rwkv_shift[0], rwkv_wkv[0], *q['rwkv'])
    if L >= CONV_W - 1:
        conv_new = proj[:, L - (CONV_W - 1):, :D_A]
    else:
        conv_new = jnp.concatenate([lru_conv[0][:, L:], proj[:, :, :D_A]], axis=1)
    shift_new = proj[:, L - 1, 2 * D_A:]
    w_out = q['w_out_ab']
    h = _out_proj(h, [y_a.reshape(M, D_A), y_b.reshape(M, D_B)], [w_out[:D_A], w_out[D_A:]])
    h = _ffn(h, q['norm_ffn'][0], *q['ffn'], D_FF // 2)
    h = _ple(h, q['norm_ple'][0], q['w_ple_gate'][0], p2[0], q['w_ple_proj'][0], q['norm_final'], False)

    proj = _norm_mm(h, q['norm_mix'][1], q['w_in_cd'], IN_CD_PAD // 5).reshape(B, L, IN_CD_PAD)
    y_cd, ssd_new, gla_new = _cd(proj, ssd_conv[0], ssd_st[0], gla_st[0], *q['cd'])
    if L >= CONV_W - 1:
        sconv_new = proj[:, L - (CONV_W - 1):, CD_XBC:CD_XBC + D_XBC]
    else:
        sconv_new = jnp.concatenate([ssd_conv[0][:, L:], proj[:, :, CD_XBC:CD_XBC + D_XBC]], axis=1)
    h = _out_proj(h, [y_cd.reshape(M, D_C + V_D)], [q['w_out_cd']])
    gates = _router(h, q['norm_ffn'][1], q['router'])
    h = _moe(h, q['norm_ffn'][1], gates, *q['moe'], D_FF_E)
    y = _ple(h, q['norm_ple'][1], q['w_ple_gate'][1], p2[1], q['w_ple_proj'][1], q['norm_final'], True)

    new_states = (conv_new[None], h_new[None], shift_new[None], wkv_new[None],
                  sconv_new[None], ssd_new[None], gla_new[None])
    return y.reshape(B, L, D), new_states


def kernel(x_prompt, x_sample, p_prompt, p_sample, state_lru_conv, state_lru_h, state_rwkv_shift,
           state_rwkv_wkv, state_ssd_conv, state_ssd, state_gla, norm_mix, norm_ffn, norm_ple,
           w_ple_gate, w_ple_proj, norm_final, w_in_ab, w_out_ab, lru_conv_w, lru_conv_b, lru_w_r,
           lru_b_r, lru_w_i, lru_b_i, lru_lambda, rwkv_mu, rwkv_w0, rwkv_w2, rwkv_a0, rwkv_a2,
           rwkv_g2, rwkv_k_k, rwkv_k_a, rwkv_r_k, rwkv_ln_w, rwkv_ln_b, ffn_w_gate, ffn_w_up,
           ffn_w_down, w_in_cd, w_out_cd, ssd_conv_w, ssd_conv_b, ssd_dt_bias, ssd_a_log, ssd_d,
           ssd_norm_w, gla_alpha_up, gla_alpha_b, gla_norm_w, moe_router, moe_w_gate, moe_w_up,
           moe_w_down):
    prm = dict(
        norm_mix=norm_mix, norm_ffn=norm_ffn, norm_ple=norm_ple, w_ple_gate=w_ple_gate,
        w_ple_proj=w_ple_proj, norm_final=norm_final, w_in_ab=w_in_ab, w_out_ab=w_out_ab,
        lru_conv_w=lru_conv_w, lru_conv_b=lru_conv_b, lru_w_r=lru_w_r, lru_b_r=lru_b_r,
        lru_w_i=lru_w_i, lru_b_i=lru_b_i, lru_lambda=lru_lambda, rwkv_mu=rwkv_mu, rwkv_w0=rwkv_w0,
        rwkv_w2=rwkv_w2, rwkv_a0=rwkv_a0, rwkv_a2=rwkv_a2, rwkv_g2=rwkv_g2, rwkv_k_k=rwkv_k_k,
        rwkv_k_a=rwkv_k_a, rwkv_r_k=rwkv_r_k, rwkv_ln_w=rwkv_ln_w, rwkv_ln_b=rwkv_ln_b,
        ffn_w_gate=ffn_w_gate, ffn_w_up=ffn_w_up, ffn_w_down=ffn_w_down, w_in_cd=w_in_cd,
        w_out_cd=w_out_cd, ssd_conv_w=ssd_conv_w, ssd_conv_b=ssd_conv_b, ssd_dt_bias=ssd_dt_bias,
        ssd_a_log=ssd_a_log, ssd_d=ssd_d, ssd_norm_w=ssd_norm_w, gla_alpha_up=gla_alpha_up,
        gla_alpha_b=gla_alpha_b, gla_norm_w=gla_norm_w, moe_router=moe_router,
        moe_w_gate=moe_w_gate, moe_w_up=moe_w_up, moe_w_down=moe_w_down)
    q = _prep_params(prm)
    states_s = (state_lru_conv, state_lru_h, state_rwkv_shift, state_rwkv_wkv,
                state_ssd_conv, state_ssd, state_gla)
    n_prompt = x_prompt.shape[0]
    states_p = tuple(jnp.zeros((s.shape[0], n_prompt) + s.shape[2:], s.dtype) for s in states_s)
    y_p, st_p = _trunk(x_prompt, p_prompt, states_p, q)
    y_s, st_s = _trunk(x_sample, p_sample, states_s, q)
    return (y_p, y_s) + tuple(st_p) + tuple(st_s)
```

```python
import functools
import math

import jax
import jax.numpy as jnp
from jax import lax
from jax.experimental import pallas as pl
from jax.experimental.pallas import tpu as pltpu

F32, BF16 = jnp.float32, jnp.bfloat16

D_MODEL = 1024
DEPTH = 2
D_PLE = 256
EPS = 1e-6
CONV_W = 4
D_A = 512
N_BLK_A = 8
BW_A = D_A // N_BLK_A
LRU_C = 8.0
H_B = 8
HD_B = 64
D_B = H_B * HD_B
R_W = 64
R_A = 64
R_G = 128
D_BPROJ = 3 * D_B + R_W + R_A + R_G
GN_EPS_B = 64e-5
IN_AB = 2 * D_A + D_BPROJ
H_C = 8
P_C = 64
D_C = H_C * P_C
G_C = 2
HG_C = H_C // G_C
N_C = 128
D_XBC = D_C + 2 * G_C * N_C
H_D = 4
DK_D = 64
DV_D = 128
K_D = H_D * DK_D
V_D = H_D * DV_D
R_ALPHA = 16
GATE_NORM = 16.0
D_FF = 2816
N_EXP = 8
D_FF_E = 1408
IN_CD_PAD = D_C + D_XBC + 2 * K_D + 2 * V_D + 128

LANES = 128
SUBLANES = 8
VMEM_LIMIT = 56 * 1024 * 1024

G_ROWS = SUBLANES
T_CHUNK = 16
TL_BLOCK = 64
TM_TOK = 1024
TM_FFN = 512
TM_MOE = 1024
MOE_RB = 128
MOE_SORT_ROWS = 256
MOE_UNSORT_COLS = 512
TOP_K = 2
LRU_TL = 256

NN = ((1,), (0,))
NT = ((1,), (1,))
TN = ((0,), (0,))


def _cparams(sem):
    return pltpu.CompilerParams(dimension_semantics=sem, vmem_limit_bytes=VMEM_LIMIT)


def _rms(x, g):
    return x * lax.rsqrt(jnp.mean(x * x, axis=-1, keepdims=True) + EPS) * g


def _sigmoid(x):
    return jax.nn.sigmoid(x)


def _silu(x):
    return x * jax.nn.sigmoid(x)


def _softplus(x):
    return jnp.maximum(x, 0.0) + jnp.log1p(jnp.exp(-jnp.abs(x)))


def _gelu_tanh(x):
    c = math.sqrt(2.0 / math.pi)
    return 0.5 * x * (1.0 + jnp.tanh(c * (x + 0.044715 * (x * x * x))))


def _d(a, b, dims=NN):
    return lax.dot_general(a, b, (dims, ((), ())), preferred_element_type=F32)


def _mm(a, b, dims=NN):
    return _d(a.astype(BF16), b.astype(BF16), dims)


def _split2(x):
    hi = x.astype(BF16)
    lo = (x - hi.astype(F32)).astype(BF16)
    return hi, lo


def _mm3(a, b, dims=NN):
    ah, al = _split2(a)
    bh, bl = _split2(b)
    return _d(ah, bh, dims) + (_d(ah, bl, dims) + _d(al, bh, dims))


def _mm_sel(sel, x, pieces, sel_left=True, dims=NN):
    out = None
    rest = x
    for i in range(pieces):
        p = rest.astype(BF16)
        if i + 1 < pieces:
            rest = rest - p.astype(F32)
        t = _d(sel, p, dims) if sel_left else _d(p, sel, dims)
        out = t if out is None else out + t
    return out


def _chunk_masks(G, T):
    R = G * T
    ri = lax.broadcasted_iota(jnp.int32, (R, R), 0)
    ci = lax.broadcasted_iota(jnp.int32, (R, R), 1)
    same = (ri // T) == (ci // T)
    incl = same & (ci <= ri)
    strict = same & (ci < ri)
    last = ci == (ri // T) * T + (T - 1)
    return incl, strict, last


def _to_bf16_mask(m):
    return jnp.where(m, 1.0, 0.0).astype(BF16)


def _tile_sel(W, G):
    r = lax.broadcasted_iota(jnp.int32, (W, G * W), 0)
    c = lax.broadcasted_iota(jnp.int32, (W, G * W), 1)
    return _to_bf16_mask((c % W) == r)


def _batch_lane_mask(n_stack, G, T, W):
    R = G * T
    r = lax.broadcasted_iota(jnp.int32, (n_stack * R, G * W), 0)
    c = lax.broadcasted_iota(jnp.int32, (n_stack * R, G * W), 1)
    return ((r % R) // T) == (c // W)


def _head_sum_sel(width, head):
    r = lax.broadcasted_iota(jnp.int32, (width, width), 0)
    c = lax.broadcasted_iota(jnp.int32, (width, width), 1)
    return _to_bf16_mask((r // head) == (c // head))


def _norm_mm_kernel(x_ref, g_ref, w_ref, o_ref, xn_ref):
    @pl.when(pl.program_id(1) == 0)
    def _():
        xn_ref[...] = _rms(x_ref[...], g_ref[...]).astype(BF16)

    o_ref[...] = _d(xn_ref[...], w_ref[...])


def _norm_mm(x, g, w, tn):
    M, K = x.shape
    N = w.shape[1]
    tm = min(TM_TOK, M)
    return pl.pallas_call(
        _norm_mm_kernel,
        grid=(M // tm, N // tn),
        in_specs=[pl.BlockSpec((tm, K), lambda i, j: (i, 0)),
                  pl.BlockSpec((1, K), lambda i, j: (0, 0)),
                  pl.BlockSpec((K, tn), lambda i, j: (0, j))],
        out_specs=pl.BlockSpec((tm, tn), lambda i, j: (i, j)),
        out_shape=jax.ShapeDtypeStruct((M, N), F32),
        scratch_shapes=[pltpu.VMEM((tm, K), BF16)],
        compiler_params=_cparams(("parallel", "arbitrary")),
    )(x, g, w)


def _out_proj_kernel(n_in, *refs):
    h_ref = refs[0]
    ys = refs[1:1 + n_in]
    ws = refs[1 + n_in:1 + 2 * n_in]
    o_ref = refs[1 + 2 * n_in]
    acc = h_ref[...]
    for y_ref, w_ref in zip(ys, ws):
        acc = acc + _mm(y_ref[...], w_ref[...])
    o_ref[...] = acc


def _out_proj(h, ys, ws):
    M, D = h.shape
    tm = min(TM_TOK, M)
    n_in = len(ys)
    in_specs = [pl.BlockSpec((tm, D), lambda i: (i, 0))]
    in_specs += [pl.BlockSpec((tm, y.shape[1]), lambda i: (i, 0)) for y in ys]
    in_specs += [pl.BlockSpec(w.shape, lambda i: (0, 0)) for w in ws]
    return pl.pallas_call(
        functools.partial(_out_proj_kernel, n_in),
        grid=(M // tm,),
        in_specs=in_specs,
        out_specs=pl.BlockSpec((tm, D), lambda i: (i, 0)),
        out_shape=jax.ShapeDtypeStruct((M, D), F32),
        compiler_params=_cparams(("parallel",)),
    )(h, *ys, *ws)


def _ffn_kernel(h_ref, g_ref, wg_ref, wu_ref, wd_ref, o_ref, xn_ref, acc_ref):
    j = pl.program_id(1)

    @pl.when(j == 0)
    def _():
        xn_ref[...] = _rms(h_ref[...], g_ref[...]).astype(BF16)
        acc_ref[...] = jnp.zeros_like(acc_ref)

    xn = xn_ref[...]
    act = _silu(_d(xn, wg_ref[...])) * _d(xn, wu_ref[...])
    acc_ref[...] += _mm(act, wd_ref[...])

    @pl.when(j == pl.num_programs(1) - 1)
    def _():
        o_ref[...] = h_ref[...] + acc_ref[...]


def _ffn(h, g, wg, wu, wd, tf):
    M, D = h.shape
    FF = wg.shape[1]
    tm = min(TM_FFN, M)
    return pl.pallas_call(
        _ffn_kernel,
        grid=(M // tm, FF // tf),
        in_specs=[pl.BlockSpec((tm, D), lambda i, j: (i, 0)),
                  pl.BlockSpec((1, D), lambda i, j: (0, 0)),
                  pl.BlockSpec((D, tf), lambda i, j: (0, j)),
                  pl.BlockSpec((D, tf), lambda i, j: (0, j)),
                  pl.BlockSpec((tf, D), lambda i, j: (j, 0))],
        out_specs=pl.BlockSpec((tm, D), lambda i, j: (i, 0)),
        out_shape=jax.ShapeDtypeStruct((M, D), F32),
        scratch_shapes=[pltpu.VMEM((tm, D), BF16), pltpu.VMEM((tm, D), F32)],
        compiler_params=_cparams(("parallel", "arbitrary")),
    )(h, g, wg, wu, wd)


def _router_kernel(h_ref, g_ref, wr_ref, route_ref, routeT_ref, meta_ref):
    xn = _rms(h_ref[...], g_ref[...])
    logits = _mm3(xn, wr_ref[...])
    lane = lax.broadcasted_iota(jnp.int32, logits.shape, 1)
    valid = lane < N_EXP
    logits = jnp.where(valid, logits, -1e30)
    m = jnp.max(logits, axis=-1, keepdims=True)
    e = jnp.where(valid, jnp.exp(logits - m), 0.0)
    p = e / jnp.sum(e, axis=-1, keepdims=True)
    lane_f = lane.astype(F32)
    m1 = jnp.max(p, axis=-1, keepdims=True)
    i1 = jnp.min(jnp.where(p == m1, lane_f, float(LANES)), axis=-1, keepdims=True)
    p2 = jnp.where((lane_f == i1) | (lane >= N_EXP), -1.0, p)
    m2 = jnp.max(p2, axis=-1, keepdims=True)
    i2 = jnp.min(jnp.where(p2 == m2, lane_f, float(LANES)), axis=-1, keepdims=True)
    tot = m1 + m2
    tm = p.shape[0]
    sel1 = lane_f == i1
    sel2 = lane_f == i2
    assign = jnp.where(sel1 | sel2, 1.0, 0.0)
    ri = lax.broadcasted_iota(jnp.int32, (tm, tm), 0)
    ci = lax.broadcasted_iota(jnp.int32, (tm, tm), 1)
    rank = _d(_to_bf16_mask(ci < ri), assign.astype(BF16))
    cnt = jnp.sum(assign, axis=0, keepdims=True)
    padded = jnp.floor((cnt + (MOE_RB - 1)) * (1.0 / MOE_RB)) * MOE_RB
    er = lax.broadcasted_iota(jnp.int32, (LANES, LANES), 0)
    ec = lax.broadcasted_iota(jnp.int32, (LANES, LANES), 1)
    off = _mm_sel(_to_bf16_mask(er < ec), jnp.broadcast_to(padded, (SUBLANES, LANES)), 2, sel_left=False)[0:1]
    pos = off + rank
    dest1 = jnp.sum(jnp.where(sel1, pos, 0.0), axis=-1, keepdims=True)
    dest2 = jnp.sum(jnp.where(sel2, pos, 0.0), axis=-1, keepdims=True)
    route = jnp.where(lane == 0, dest1, jnp.where(lane == 1, dest2,
                      jnp.where(lane == 2, m1 / tot, jnp.where(lane == 3, m2 / tot, 0.0))))
    route_ref[...] = route
    routeT_ref[...] = route.T[:SUBLANES]
    mrow = lax.broadcasted_iota(jnp.int32, (SUBLANES, LANES), 0)
    meta_ref[...] = jnp.where(mrow == 0, padded * (1.0 / MOE_RB), jnp.where(mrow == 1, off * (1.0 / MOE_RB), 0.0))


def _router(h, g, wr):
    M, D = h.shape
    tm = min(TM_MOE, M)
    nt = M // tm
    return pl.pallas_call(
        _router_kernel,
        grid=(nt,),
        in_specs=[pl.BlockSpec((tm, D), lambda i: (i, 0)),
                  pl.BlockSpec((1, D), lambda i: (0, 0)),
                  pl.BlockSpec((D, LANES), lambda i: (0, 0))],
        out_specs=[pl.BlockSpec((tm, LANES), lambda i: (i, 0)),
                   pl.BlockSpec((None, SUBLANES, tm), lambda i: (i, 0, 0)),
                   pl.BlockSpec((None, SUBLANES, LANES), lambda i: (i, 0, 0))],
        out_shape=[jax.ShapeDtypeStruct((M, LANES), F32),
                   jax.ShapeDtypeStruct((nt, SUBLANES, tm), F32),
                   jax.ShapeDtypeStruct((nt, SUBLANES, LANES), F32)],
        compiler_params=_cparams(("parallel",)),
    )(h, g, wr)


def _moe_kernel(nblk_ref, offb_ref, h_ref, g_ref, route_ref, routeT_ref, wg_ref, wu_ref, wd_ref, o_ref,
                xs_ref, ys_ref, gs_ref):
    i = pl.program_id(0)
    e = pl.program_id(1)
    tm = h_ref.shape[0]
    cap = xs_ref.shape[0]

    @pl.when(e == 0)
    def _():
        xn = _rms(h_ref[...], g_ref[...]).astype(BF16)
        d1, d2 = routeT_ref[0:1, :], routeT_ref[1:2, :]
        g1, g2 = routeT_ref[2:3, :], routeT_ref[3:4, :]
        row0 = lax.broadcasted_iota(jnp.int32, (MOE_SORT_ROWS, tm), 0).astype(F32)
        for rb in range(cap // MOE_SORT_ROWS):
            lo, hi = rb * MOE_SORT_ROWS, (rb + 1) * MOE_SORT_ROWS
            rows = row0 + float(lo)
            m1 = rows == d1
            m2 = rows == d2
            xs_ref[lo:hi, :] = _d(_to_bf16_mask(m1 | m2), xn).astype(BF16)
            gate = jnp.sum(jnp.where(m1, g1, 0.0) + jnp.where(m2, g2, 0.0), axis=-1, keepdims=True)
            gs_ref[lo:hi, :] = jnp.broadcast_to(gate, (MOE_SORT_ROWS, LANES))
        ys_ref[...] = jnp.zeros_like(ys_ref)

    base = offb_ref[i, e]

    def block(j, carry):
        r0 = pl.multiple_of((base + j) * MOE_RB, MOE_RB)
        x = xs_ref[pl.ds(r0, MOE_RB), :]
        act = _silu(_d(x, wg_ref[...])) * _d(x, wu_ref[...])
        act = act * gs_ref[pl.ds(r0, MOE_RB), 0:1]
        ys_ref[pl.ds(r0, MOE_RB), :] = _mm(act, wd_ref[...]).astype(BF16)
        return carry

    lax.fori_loop(0, nblk_ref[i, e], block, 0)

    @pl.when(e == pl.num_programs(1) - 1)
    def _():
        d1, d2 = route_ref[:, 0:1], route_ref[:, 1:2]
        col0 = lax.broadcasted_iota(jnp.int32, (tm, MOE_UNSORT_COLS), 1).astype(F32)
        acc = h_ref[...]
        for cb in range(cap // MOE_UNSORT_COLS):
            lo, hi = cb * MOE_UNSORT_COLS, (cb + 1) * MOE_UNSORT_COLS
            cols = col0 + float(lo)
            acc = acc + _d(_to_bf16_mask((cols == d1) | (cols == d2)), ys_ref[lo:hi, :])
        o_ref[...] = acc


def _moe(h, g, route, routeT, meta, wg, wu, wd):
    M, D = h.shape
    E, _, FF = wg.shape
    tm = min(TM_MOE, M)
    nt = M // tm
    cap = TOP_K * tm + E * MOE_RB
    cap = -(-cap // MOE_UNSORT_COLS) * MOE_UNSORT_COLS
    meta_i = meta.astype(jnp.int32)
    grid_spec = pltpu.PrefetchScalarGridSpec(
        num_scalar_prefetch=2,
        grid=(nt, E),
        in_specs=[pl.BlockSpec((tm, D), lambda i, e, nb, ob: (i, 0), pipeline_mode=pl.Buffered(1)),
                  pl.BlockSpec((1, D), lambda i, e, nb, ob: (0, 0)),
                  pl.BlockSpec((tm, LANES), lambda i, e, nb, ob: (i, 0)),
                  pl.BlockSpec((None, SUBLANES, tm), lambda i, e, nb, ob: (i, 0, 0)),
                  pl.BlockSpec((None, D, FF), lambda i, e, nb, ob: (e, 0, 0)),
                  pl.BlockSpec((None, D, FF), lambda i, e, nb, ob: (e, 0, 0)),
                  pl.BlockSpec((None, FF, D), lambda i, e, nb, ob: (e, 0, 0))],
        out_specs=pl.BlockSpec((tm, D), lambda i, e, nb, ob: (i, 0)),
        scratch_shapes=[pltpu.VMEM((cap, D), BF16), pltpu.VMEM((cap, D), BF16), pltpu.VMEM((cap, LANES), F32)])
    return pl.pallas_call(
        _moe_kernel,
        grid_spec=grid_spec,
        out_shape=jax.ShapeDtypeStruct((M, D), F32),
        compiler_params=_cparams(("parallel", "arbitrary")),
    )(meta_i[:, 0, :], meta_i[:, 1, :], h, g, route, routeT, wg, wu, wd)


def _ple_kernel(final, h_ref, g_ref, wg_ref, p_ref, wp_ref, gf_ref, o_ref):
    h = h_ref[...]
    gate = _sigmoid(_mm(_rms(h, g_ref[...]), wg_ref[...]))
    out = h + gate * _mm(p_ref[...], wp_ref[...])
    if final:
        out = _rms(out, gf_ref[...])
    o_ref[...] = out


def _ple(h, g, wg, p, wp, gf, final):
    M, D = h.shape
    tm = min(TM_TOK, M)
    return pl.pallas_call(
        functools.partial(_ple_kernel, final),
        grid=(M // tm,),
        in_specs=[pl.BlockSpec((tm, D), lambda i: (i, 0)),
                  pl.BlockSpec((1, D), lambda i: (0, 0)),
                  pl.BlockSpec((D, D), lambda i: (0, 0)),
                  pl.BlockSpec((tm, D_PLE), lambda i: (i, 0)),
                  pl.BlockSpec((D_PLE, D), lambda i: (0, 0)),
                  pl.BlockSpec((1, D), lambda i: (0, 0))],
        out_specs=pl.BlockSpec((tm, D), lambda i: (i, 0)),
        out_shape=jax.ShapeDtypeStruct((M, D), F32),
        compiler_params=_cparams(("parallel",)),
    )(h, g, wg, p, wp, gf)


def _lru_gates(xc, ga_unused, wr, br, wi, bi, lam):
    r = _sigmoid(_mm(xc, wr) + br)
    i = _sigmoid(_mm(xc, wi) + bi)
    log_a = -LRU_C * r * _softplus(-lam)
    a = jnp.exp(log_a)
    u = jnp.sqrt(jnp.tanh(-log_a) * (a * a + 1.0)) * (i * xc)
    return a, u


def _lru_seq_kernel(TL, proj_ref, cs_ref, h0_ref, cw_ref, cb_ref, wr_ref, br_ref, wi_ref, bi_ref, lam_ref,
                    y_ref, hT_ref, pad_ref, a_ref, u_ref, hs_ref, hc_ref):
    ti = pl.program_id(0)
    G = pad_ref.shape[0]

    @pl.when(ti == 0)
    def _():
        pad_ref[:, SUBLANES - (CONV_W - 1):SUBLANES, :] = cs_ref[...]
        hc_ref[...] = h0_ref[...]

    pad_ref[:, SUBLANES:, :] = proj_ref[:, :, :D_A]
    cw = cw_ref[...]
    xc = cb_ref[...][None]
    for k in range(CONV_W):
        off = SUBLANES - (CONV_W - 1) + k
        xc = xc + pad_ref[:, off:off + TL, :] * cw[k:k + 1, :][None]
    pad_ref[:, :SUBLANES, :] = pad_ref[:, TL:TL + SUBLANES, :]

    a, u = _lru_gates(xc.reshape(G * TL, D_A), None, wr_ref[...], br_ref[...], wi_ref[...], bi_ref[...],
                      lam_ref[...])
    a_ref[...] = a.reshape(G, TL, D_A)
    u_ref[...] = u.reshape(G, TL, D_A)

    def step(t, h):
        h = a_ref[:, pl.ds(t, 1), :] * h + u_ref[:, pl.ds(t, 1), :]
        hs_ref[:, pl.ds(t, 1), :] = h
        return h

    hc_ref[...] = lax.fori_loop(0, TL, step, hc_ref[...], unroll=8)
    y_ref[...] = _gelu_tanh(proj_ref[:, :, D_A:]) * hs_ref[...]

    @pl.when(ti == pl.num_programs(0) - 1)
    def _():
        hT_ref[...] = hc_ref[...]


def _lru_step_kernel(xa_ref, ga_ref, cs_ref, h0_ref, cw_ref, cb_ref, wr_ref, br_ref, wi_ref, bi_ref, lam_ref,
                     y_ref, hT_ref):
    cw = cw_ref[...]
    x = xa_ref[...]
    xc = cb_ref[...] + x * cw[CONV_W - 1:CONV_W, :]
    for k in range(CONV_W - 1):
        xc = xc + cs_ref[k] * cw[k:k + 1, :]
    a, u = _lru_gates(xc, None, wr_ref[...], br_ref[...], wi_ref[...], bi_ref[...], lam_ref[...])
    h = a * h0_ref[...] + u
    hT_ref[...] = h
    y_ref[...] = _gelu_tanh(ga_ref[...]) * h


def _lru(proj, conv_state, h0, cw, cb, wr, br, wi, bi, lam):
    B, L, _ = proj.shape
    small = [cw, cb, wr, br, wi, bi, lam]
    if L == 1:
        proj2 = proj.reshape(B, -1)
        cs = jnp.transpose(conv_state, (1, 0, 2))
        tb = min(B, 128)
        full = lambda s: pl.BlockSpec(s.shape, lambda i: (0,) * s.ndim)
        y, hT = pl.pallas_call(
            _lru_step_kernel,
            grid=(B // tb,),
            in_specs=[pl.BlockSpec((tb, D_A), lambda i: (i, 0)),
                      pl.BlockSpec((tb, D_A), lambda i: (i, 1)),
                      pl.BlockSpec((CONV_W - 1, tb, D_A), lambda i: (0, i, 0)),
                      pl.BlockSpec((tb, D_A), lambda i: (i, 0))] + [full(s) for s in small],
            out_specs=[pl.BlockSpec((tb, D_A), lambda i: (i, 0)), pl.BlockSpec((tb, D_A), lambda i: (i, 0))],
            out_shape=[jax.ShapeDtypeStruct((B, D_A), F32), jax.ShapeDtypeStruct((B, D_A), F32)],
            compiler_params=_cparams(("parallel",)),
        )(proj2, proj2, cs, h0, *small)
        return y.reshape(B, 1, D_A), hT
    TL = min(LRU_TL, L)
    G = B
    full = lambda s: pl.BlockSpec(s.shape, lambda i: (0,) * s.ndim)
    y, hT = pl.pallas_call(
        functools.partial(_lru_seq_kernel, TL),
        grid=(L // TL,),
        in_specs=[pl.BlockSpec((G, TL, 2 * D_A), lambda i: (0, i, 0)),
                  pl.BlockSpec((G, CONV_W - 1, D_A), lambda i: (0, 0, 0)),
                  pl.BlockSpec((G, 1, D_A), lambda i: (0, 0, 0))] + [full(s) for s in small],
        out_specs=[pl.BlockSpec((G, TL, D_A), lambda i: (0, i, 0)),
                   pl.BlockSpec((G, 1, D_A), lambda i: (0, 0, 0))],
        out_shape=[jax.ShapeDtypeStruct((B, L, D_A), F32), jax.ShapeDtypeStruct((B, 1, D_A), F32)],
        scratch_shapes=[pltpu.VMEM((G, SUBLANES + TL, D_A), F32),
                        pltpu.VMEM((G, TL, D_A), F32), pltpu.VMEM((G, TL, D_A), F32),
                        pltpu.VMEM((G, TL, D_A), F32), pltpu.VMEM((G, 1, D_A), F32)],
        compiler_params=_cparams(("arbitrary",)),
    )(proj, conv_state, h0.reshape(B, 1, D_A), *small)
    return y, hT.reshape(B, D_A)


def _rwkv_prep(xs, w0, w2p, a0, a2p, g2, k_k, k_a, r_k, hsel):
    r = xs[:, 0:D_B]
    k = xs[:, D_B:2 * D_B]
    v = xs[:, 2 * D_B:3 * D_B]
    wa = xs[:, 3 * D_B:3 * D_B + R_W + R_A]
    gl = xs[:, 3 * D_B + R_W + R_A:]
    w_log = -_softplus(-(w0 + _mm(jnp.tanh(wa), w2p))) - 0.5
    logd = -jnp.exp(w_log)
    a = _sigmoid(a0 + _mm(wa, a2p))
    g = _mm(_sigmoid(gl), g2)
    kk = k * k_k
    kk = kk * lax.rsqrt(jnp.maximum(_mm_sel(hsel, kk * kk, 3, sel_left=False), 1e-12))
    k2 = k * (1.0 + (a - 1.0) * k_a)
    beta = kk * a
    bonus = _mm_sel(hsel, r * k2 * r_k, 3, sel_left=False) * v
    return r, k2, v, logd, kk, beta, g, bonus


def _rwkv_post(y, bonus, g, ln_w, ln_b, hsel):
    mean = _mm_sel(hsel, y, 3, sel_left=False) * (1.0 / HD_B)
    d = y - mean
    var = _mm_sel(hsel, d * d, 3, sel_left=False) * (1.0 / HD_B)
    yn = d * lax.rsqrt(var + GN_EPS_B) * ln_w + ln_b
    return (yn + bonus) * g


def _mmp(a, b, dims, passes):
    return _mm3(a, b, dims) if passes == 3 else _mm(a, b, dims)


RWKV_PASSES = dict(am=1, state=1, inv=1, apply=1, dw=1)


def _rwkv_chunk(G, T, ops, st_ref, masks, emask, store_y):
    r, k2, v, logd, kk, beta = ops
    incl, strict, lmask_bf, lastmask_bf = masks
    R = G * T
    ps = RWKV_PASSES
    cum = _mm_sel(lmask_bf, logd, 3)
    cend = _mm_sel(lastmask_bf, cum, 3)
    ecn = jnp.exp(-cum)
    P = jnp.exp(cum - logd) * kk
    Q = ecn * beta
    Kt = ecn * k2
    Rt = jnp.exp(cum) * r
    eend = jnp.exp(cend)
    lo = lax.broadcasted_iota(jnp.int32, (R, LANES), 1) < HD_B
    row_lo = lax.broadcasted_iota(jnp.int32, (LANES, G * HD_B), 0) < HD_B
    eye = None
    if T > 1:
        ri = lax.broadcasted_iota(jnp.int32, (R, R), 0)
        ci = lax.broadcasted_iota(jnp.int32, (R, R), 1)
        eye = jnp.where(ri == ci, 1.0, 0.0)
    n_rep = G * HD_B // LANES
    NP = H_B // 2
    heads = [(c, hh) for c in range(NP) for hh in range(2)]
    own = {0: lo, 1: ~lo}

    def tiled(x, xr, hh):
        t = jnp.where(lo, x, xr) if hh == 0 else jnp.where(lo, xr, x)
        return jnp.where(emask, jnp.concatenate([t] * n_rep, axis=1), 0.0)

    pair = []
    for c in range(NP):
        sl = slice(c * LANES, (c + 1) * LANES)
        vals = (P[:, sl], Rt[:, sl], Q[:, sl], Kt[:, sl], eend[:, sl])
        pair.append(dict(x=vals, xr=[pltpu.roll(x, HD_B, 1) for x in vals], V=v[:, sl]))
    AM = [_mmp(jnp.concatenate([jnp.where(own[hh], pair[c]['x'][0], 0.0),
                                jnp.where(own[hh], pair[c]['x'][1], 0.0)], axis=0),
               jnp.concatenate([pair[c]['x'][2], pair[c]['x'][3]], axis=0), NT, ps['am']) for c, hh in heads]
    Mqr = [jnp.where(incl, a[R:, :R], 0.0) for a in AM]
    Mkr = [jnp.where(incl, a[R:, R:], 0.0) for a in AM]
    MV = [_mmp(m, pair[c]['V'], NN, ps['apply']) for m, (c, hh) in zip(Mkr, heads)]
    if T > 1:
        Npow = [-jnp.where(strict, a[:R, :R], 0.0) for a in AM]
        Akp = [jnp.where(strict, a[:R, R:], 0.0) for a in AM]
        AV = [_mmp(m, pair[c]['V'], NN, ps['apply']) for m, (c, hh) in zip(Akp, heads)]
        inv = [eye + n for n in Npow]
        for _ in range(int(math.log2(T)) - 1):
            Npow = [_mmp(n, n, NN, ps['inv']) for n in Npow]
            inv = [i + _mmp(i, n, NN, ps['inv']) for i, n in zip(inv, Npow)]
    ex = [[tiled(x, xr, hh) for x, xr in zip(pair[c]['x'], pair[c]['xr'])] for c, hh in heads]
    WT = [st_ref[c] for c in range(NP)]
    PWRW = [_mmp(jnp.concatenate([e[0], e[1]], axis=0), WT[c], NT, ps['state']) for e, (c, hh) in zip(ex, heads)]
    if T > 1:
        E = [_mmp(i, pw[:R] + av, NN, ps['apply']) for i, pw, av in zip(inv, PWRW, AV)]
    else:
        E = [pw[:R] for pw in PWRW]
    Y = [pw[R:] + mv - _mmp(m, e, NN, ps['apply']) for pw, mv, m, e in zip(PWRW, MV, Mqr, E)]
    for c in range(NP):
        lhs = jnp.concatenate([jnp.where(own[hh], x, 0.0) for hh in range(2)
                               for x in (-E[2 * c + hh], pair[c]['V'])], axis=0)
        rhs = jnp.concatenate([ex[2 * c + hh][j] for hh in range(2) for j in (2, 3)], axis=0)
        dW = _mmp(lhs, rhs, TN, ps['dw'])
        erow = [jnp.sum(ex[2 * c + hh][4], axis=0, keepdims=True) * (1.0 / T) for hh in range(2)]
        st_ref[c] = (WT[c] + dW) * jnp.where(row_lo, erow[0], erow[1])
        store_y(c, jnp.where(lo, Y[2 * c], Y[2 * c + 1]))


def _rwkv_kernel(G, T, NC, seq, proj_ref, sh_ref, s0_ref, mu_ref, w0_ref, w2_ref, a0_ref, a2_ref, g2_ref,
                 kk_ref, ka_ref, rk_ref, lnw_ref, lnb_ref, y_ref, sT_ref, *scratch):
    ti = pl.program_id(1)
    TL = T * NC
    R = G * T
    st_ref = scratch[0]
    hsel = _head_sum_sel(D_B, HD_B)

    @pl.when(ti == 0)
    def _():
        st_ref[...] = s0_ref[...]

    if seq:
        pad_ref, ops_ref, gb_ref, ys_ref = scratch[1:]

        @pl.when(ti == 0)
        def _():
            pad_ref[:, SUBLANES - 1:SUBLANES, :] = sh_ref[...]

        pad_ref[:, SUBLANES:, :] = proj_ref[:, :, 2 * D_A:]
        pb = pad_ref[:, SUBLANES:, :]
        prev = pad_ref[:, SUBLANES - 1:SUBLANES - 1 + TL, :]
        xs = (pb + (prev - pb) * mu_ref[...][None]).reshape(G * TL, D_BPROJ)
        pad_ref[:, :SUBLANES, :] = pad_ref[:, TL:TL + SUBLANES, :]
    else:
        pb = proj_ref[:, 2 * D_A:]
        xs = pb + (sh_ref[...] - pb) * mu_ref[...]

    r, k2, v, logd, kk, beta, g, bonus = _rwkv_prep(
        xs, w0_ref[...], w2_ref[...], a0_ref[...], a2_ref[...], g2_ref[...], kk_ref[...], ka_ref[...],
        rk_ref[...], hsel)

    incl, strict, last = _chunk_masks(G, T)
    masks = (incl, strict, _to_bf16_mask(incl), _to_bf16_mask(last))
    emask = _batch_lane_mask(1, G, T, HD_B)

    if seq:
        for n, val in enumerate((r, k2, v, logd, kk, beta)):
            ops_ref[n] = val.reshape(G, TL, D_B)
        gb_ref[0] = g.reshape(G, TL, D_B)
        gb_ref[1] = bonus.reshape(G, TL, D_B)

        def chunk(c, carry):
            t0 = pl.multiple_of(c * T, T)
            ops = tuple(ops_ref[n, :, pl.ds(t0, T), :].reshape(R, D_B) for n in range(6))

            def store_y(hp, Y):
                ys_ref[:, pl.ds(t0, T), hp * LANES:(hp + 1) * LANES] = Y.reshape(G, T, LANES)

            _rwkv_chunk(G, T, ops, st_ref, masks, emask, store_y)
            return carry

        lax.fori_loop(0, NC, chunk, 0)
        y = ys_ref[...].reshape(G * TL, D_B)
        out = _rwkv_post(y, gb_ref[1].reshape(G * TL, D_B), gb_ref[0].reshape(G * TL, D_B),
                         lnw_ref[...], lnb_ref[...], hsel)
        y_ref[...] = out.reshape(G, TL, D_B)
    else:
        ys_ref = scratch[1]

        def store_y(hp, Y):
            ys_ref[:, hp * LANES:(hp + 1) * LANES] = Y

        _rwkv_chunk(G, T, (r, k2, v, logd, kk, beta), st_ref, masks, emask, store_y)
        y_ref[...] = _rwkv_post(ys_ref[...], bonus, g, lnw_ref[...], lnb_ref[...], hsel)

    @pl.when(ti == pl.num_programs(1) - 1)
    def _():
        sT_ref[...] = st_ref[...]


def _state_to_lanes(s, G):
    B, H, X, Y = s.shape
    return s.reshape(B // G, G, H, X, Y).transpose(0, 2, 3, 1, 4).reshape(B // G, H, X, G * Y)


def _state_from_lanes(s, G):
    NB, H, X, GY = s.shape
    Y = GY // G
    return s.reshape(NB, H, X, G, Y).transpose(0, 3, 1, 2, 4).reshape(NB * G, H, X, Y)


def _rwkv(proj, shift0, s0, mu, w0, w2p, a0, a2p, g2, k_k, k_a, r_k, ln_w, ln_b):
    B, L, _ = proj.shape
    G = G_ROWS
    NB = B // G
    small = [mu, w0, w2p, a0, a2p, g2, k_k, k_a, r_k, ln_w, ln_b]
    full = lambda s: pl.BlockSpec(s.shape, lambda b, i: (0,) * s.ndim)
    st_dims = (H_B // 2, 2 * HD_B, G * HD_B)
    s0l = _state_to_lanes(s0, G).reshape((NB,) + st_dims)
    st_spec = pl.BlockSpec((None,) + st_dims, lambda b, i: (b, 0, 0, 0))
    st_shape = jax.ShapeDtypeStruct(s0l.shape, F32)
    unpair = lambda s: _state_from_lanes(s.reshape(NB, H_B, HD_B, G * HD_B), G)
    if L == 1:
        T, NC = 1, 1
        kern = functools.partial(_rwkv_kernel, G, T, NC, False)
        y, sT = pl.pallas_call(
            kern,
            grid=(NB, 1),
            in_specs=[pl.BlockSpec((G, IN_AB), lambda b, i: (b, 0)),
                      pl.BlockSpec((G, D_BPROJ), lambda b, i: (b, 0)), st_spec] + [full(s) for s in small],
            out_specs=[pl.BlockSpec((G, D_B), lambda b, i: (b, 0)), st_spec],
            out_shape=[jax.ShapeDtypeStruct((B, D_B), F32), st_shape],
            scratch_shapes=[pltpu.VMEM(st_dims, F32), pltpu.VMEM((G, D_B), F32)],
            compiler_params=_cparams(("parallel", "arbitrary")),
        )(proj.reshape(B, IN_AB), shift0, s0l, *small)
        return y.reshape(B, 1, D_B), unpair(sT)
    T = min(T_CHUNK, L)
    TL = min(TL_BLOCK, L)
    NC = TL // T
    kern = functools.partial(_rwkv_kernel, G, T, NC, True)
    y, sT = pl.pallas_call(
        kern,
        grid=(NB, L // TL),
        in_specs=[pl.BlockSpec((G, TL, IN_AB), lambda b, i: (b, i, 0)),
                  pl.BlockSpec((G, 1, D_BPROJ), lambda b, i: (b, 0, 0)), st_spec] + [full(s) for s in small],
        out_specs=[pl.BlockSpec((G, TL, D_B), lambda b, i: (b, i, 0)), st_spec],
        out_shape=[jax.ShapeDtypeStruct((B, L, D_B), F32), st_shape],
        scratch_shapes=[pltpu.VMEM(st_dims, F32),
                        pltpu.VMEM((G, SUBLANES + TL, D_BPROJ), F32),
                        pltpu.VMEM((6, G, TL, D_B), F32),
                        pltpu.VMEM((2, G, TL, D_B), F32),
                        pltpu.VMEM((G, TL, D_B), F32)],
        compiler_params=_cparams(("parallel", "arbitrary")),
    )(proj, shift0.reshape(B, 1, D_BPROJ), s0l, *small)
    return y, unpair(sT)


CD_Z, CD_XBC, CD_Q, CD_K, CD_V, CD_G, CD_S = 0, 512, 1536, 1792, 2048, 2560, 3072


def _ssd_chunk(G, T, xs, Bm, Cm, dt_all, la_all, st_ref, masks, emaskN, store_y):
    incl, lmask_bf, lastmask_bf = masks
    R = G * T
    cum = _mm_sel(lmask_bf, la_all, 3)
    cend = _mm_sel(lastmask_bf, cum, 3)
    cumT = cum.T
    wend = jnp.exp(cend - cum) * dt_all
    ecum = jnp.exp(cum)
    dend = jnp.exp(cend)
    lo = lax.broadcasted_iota(jnp.int32, (R, LANES), 1) < P_C

    def per_head(x):
        return jnp.concatenate([jnp.where(lo, x[:, 2 * c:2 * c + 1], x[:, 2 * c + 1:2 * c + 2])
                                for c in range(H_C // 2)], axis=1)

    tile = lambda x: jnp.where(emaskN, jnp.concatenate([x] * G, axis=1), 0.0)
    xdt = xs * per_head(dt_all)
    xw = xs * per_head(wend)
    ecf = per_head(ecum)
    Bg = [Bm[:, g * N_C:(g + 1) * N_C] for g in range(G_C)]
    Cg = [Cm[:, g * N_C:(g + 1) * N_C] for g in range(G_C)]
    CB = [_mm(c, b, NT) for c, b in zip(Cg, Bg)]
    ST = [st_ref[g] for g in range(G_C)]
    CS = [_mm(tile(c), s, NT) for c, s in zip(Cg, ST)]
    W = [CB[h // HG_C] * jnp.exp(jnp.where(incl, cum[:, h:h + 1] - cumT[h:h + 1, :], -1e30))
         for h in range(H_C)]
    intra = [_mm(W[h], xdt[:, (h // 2) * LANES:(h // 2 + 1) * LANES]) for h in range(H_C)]
    drow = [jnp.sum(jnp.where(emaskN, dend[:, h:h + 1], 0.0), axis=0, keepdims=True) * (1.0 / T)
            for h in range(H_C)]
    for c in range(H_C // 2):
        sl = slice(c * LANES, (c + 1) * LANES)
        gsl = slice((c % (HG_C // 2)) * LANES, (c % (HG_C // 2) + 1) * LANES)
        store_y(c, jnp.where(lo, intra[2 * c], intra[2 * c + 1]) + ecf[:, sl] * CS[c // (HG_C // 2)][:, gsl])
    for g in range(G_C):
        W_g = HG_C * P_C
        dS = _mm(xw[:, g * W_g:(g + 1) * W_g], tile(Bg[g]), TN)
        st_ref[g] = jnp.concatenate(
            [ST[g][hh * P_C:(hh + 1) * P_C] * drow[g * HG_C + hh] + dS[hh * P_C:(hh + 1) * P_C]
             for hh in range(HG_C)], axis=0)


def _gla_chunk(G, T, q, k, v, la, st_ref, masks, emask, store_o):
    incl, lmask_bf, lastmask_bf = masks
    R = G * T
    bc = _mm_sel(lmask_bf, la, 3)
    bend = _mm_sel(lastmask_bf, bc, 3)
    qe = q * jnp.exp(bc)
    kn = k * jnp.exp(-bc)
    kw = k * jnp.exp(bend - bc)
    dend = jnp.exp(bend)
    lo = lax.broadcasted_iota(jnp.int32, (R, LANES), 1) < DK_D
    own = {0: lo, 1: ~lo}
    n_rep = G * DK_D // LANES

    def tiled(x, xr, hh):
        t = jnp.where(lo, x, xr) if hh == 0 else jnp.where(lo, xr, x)
        return jnp.where(emask, jnp.concatenate([t] * n_rep, axis=1), 0.0)

    heads = [(c, hh) for c in range(H_D // 2) for hh in range(2)]
    pair = []
    for c in range(H_D // 2):
        sl = slice(c * LANES, (c + 1) * LANES)
        vals = (qe[:, sl], kw[:, sl], dend[:, sl])
        pair.append(dict(x=vals, xr=[pltpu.roll(x, DK_D, 1) for x in vals], kn=kn[:, sl]))
    vh = [v[:, h * DV_D:(h + 1) * DV_D] for h in range(H_D)]
    att = [jnp.where(incl, _mm(jnp.where(own[hh], pair[c]['x'][0], 0.0), pair[c]['kn'], NT), 0.0)
           for c, hh in heads]
    ex = [[tiled(x, xr, hh) for x, xr in zip(pair[c]['x'], pair[c]['xr'])] for c, hh in heads]
    ST = [st_ref[h] for h in range(H_D)]
    o = [_mm(a, vv) + _mm(e[0], s, NT) for a, vv, e, s in zip(att, vh, ex, ST)]
    for h in range(H_D):
        drow = jnp.sum(ex[h][2], axis=0, keepdims=True) * (1.0 / T)
        st_ref[h] = ST[h] * drow + _mm(vh[h], ex[h][1], TN)
        store_o(h, o[h])


def _cd_prep_small(small, dtb, a_neg, aup, ab):
    dt_all = _softplus(small + dtb)
    la_all = dt_all * a_neg
    la_gla = -_softplus(-(_mm(small, aup) + ab)) * (1.0 / GATE_NORM)
    return dt_all, la_all, la_gla


def _cd_post(y, xs, z, o, gg, dskip, ssd_norm, gla_norm):
    y_c = _rms((y + dskip * xs) * _silu(z), ssd_norm)
    outs = [y_c]
    for h in range(H_D):
        sl = slice(h * DV_D, (h + 1) * DV_D)
        outs.append(_rms(o[:, sl], gla_norm) * _silu(gg[:, sl]))
    return outs


def _cd_kernel(G, T, NC, seq, proj_ref, cs_ref, ssd0_ref, gla0_ref, cw_ref, cb_ref, dtb_ref, aneg_ref, dskip_ref,
               ssdn_ref, aup_ref, ab_ref, glan_ref, y_ref, ssdT_ref, glaT_ref, *scratch):
    ti = pl.program_id(1)
    TL = T * NC
    R = G * T
    sst_ref, gst_ref = scratch[0], scratch[1]

    @pl.when(ti == 0)
    def _():
        sst_ref[...] = ssd0_ref[...]
        gst_ref[...] = gla0_ref[...]

    cw = cw_ref[...]
    if seq:
        pad_ref, xbc_ref, sm_ref, ys_ref, os_ref = scratch[2:]

        @pl.when(ti == 0)
        def _():
            pad_ref[:, SUBLANES - (CONV_W - 1):SUBLANES, :] = cs_ref[...]

        pad_ref[:, SUBLANES:, :] = proj_ref[:, :, CD_XBC:CD_XBC + D_XBC]
        xc = cb_ref[...][None]
        for kq in range(CONV_W):
            off = SUBLANES - (CONV_W - 1) + kq
            xc = xc + pad_ref[:, off:off + TL, :] * cw[kq:kq + 1, :][None]
        pad_ref[:, :SUBLANES, :] = pad_ref[:, TL:TL + SUBLANES, :]
        xbc_ref[...] = _silu(xc)
        small = proj_ref[:, :, CD_S:].reshape(G * TL, LANES)
    else:
        x = proj_ref[:, CD_XBC:CD_XBC + D_XBC]
        xc = cb_ref[...] + x * cw[CONV_W - 1:CONV_W, :]
        for kq in range(CONV_W - 1):
            xc = xc + cs_ref[kq] * cw[kq:kq + 1, :]
        xbc = _silu(xc)
        small = proj_ref[:, CD_S:]

    dt_all, la_all, la_gla = _cd_prep_small(small, dtb_ref[...], aneg_ref[...], aup_ref[...], ab_ref[...])

    incl, _, last = _chunk_masks(G, T)
    masks = (incl, _to_bf16_mask(incl), _to_bf16_mask(last))
    emaskN = _batch_lane_mask(1, G, T, N_C)
    emask1 = _batch_lane_mask(1, G, T, DK_D)
    scale = DK_D ** -0.5

    if seq:
        sm_ref[0] = dt_all.reshape(G, TL, LANES)
        sm_ref[1] = la_all.reshape(G, TL, LANES)
        sm_ref[2] = la_gla[:, :LANES].reshape(G, TL, LANES)
        sm_ref[3] = la_gla[:, LANES:].reshape(G, TL, LANES)

        def chunk(c, carry):
            t0 = pl.multiple_of(c * T, T)
            rows = lambda ref, lo, hi: ref[:, pl.ds(t0, T), lo:hi].reshape(R, hi - lo)
            xs = rows(xbc_ref, 0, D_C)
            Bm = rows(xbc_ref, D_C, D_C + G_C * N_C)
            Cm = rows(xbc_ref, D_C + G_C * N_C, D_XBC)
            dtc = sm_ref[0, :, pl.ds(t0, T), :].reshape(R, LANES)
            lac = sm_ref[1, :, pl.ds(t0, T), :].reshape(R, LANES)

            def store_y(hp, yv):
                ys_ref[:, pl.ds(t0, T), hp * LANES:(hp + 1) * LANES] = yv.reshape(G, T, LANES)

            _ssd_chunk(G, T, xs, Bm, Cm, dtc, lac, sst_ref, masks, emaskN, store_y)

            q = rows(proj_ref, CD_Q, CD_Q + K_D) * scale
            k = rows(proj_ref, CD_K, CD_K + K_D)
            v = rows(proj_ref, CD_V, CD_V + V_D)
            lag = jnp.concatenate([sm_ref[2, :, pl.ds(t0, T), :].reshape(R, LANES),
                                   sm_ref[3, :, pl.ds(t0, T), :].reshape(R, LANES)], axis=1)

            def store_o(h, ov):
                os_ref[:, pl.ds(t0, T), h * DV_D:(h + 1) * DV_D] = ov.reshape(G, T, DV_D)

            _gla_chunk(G, T, q, k, v, lag, gst_ref, masks, emask1, store_o)
            return carry

        lax.fori_loop(0, NC, chunk, 0)
        flat = lambda a: a.reshape(G * TL, a.shape[-1])
        outs = _cd_post(flat(ys_ref[...]), flat(xbc_ref[:, :, :D_C]), flat(proj_ref[:, :, CD_Z:CD_Z + D_C]),
                        flat(os_ref[...]), flat(proj_ref[:, :, CD_G:CD_G + V_D]),
                        dskip_ref[...], ssdn_ref[...], glan_ref[...])
        y_ref[:, :, :D_C] = outs[0].reshape(G, TL, D_C)
        for h in range(H_D):
            y_ref[:, :, D_C + h * DV_D:D_C + (h + 1) * DV_D] = outs[1 + h].reshape(G, TL, DV_D)
    else:
        ys_ref, os_ref = scratch[2:]

        def store_y(hp, yv):
            ys_ref[:, hp * LANES:(hp + 1) * LANES] = yv

        def store_o(h, ov):
            os_ref[:, h * DV_D:(h + 1) * DV_D] = ov

        _ssd_chunk(G, T, xbc[:, :D_C], xbc[:, D_C:D_C + G_C * N_C], xbc[:, D_C + G_C * N_C:], dt_all, la_all,
                   sst_ref, masks, emaskN, store_y)
        _gla_chunk(G, T, proj_ref[:, CD_Q:CD_Q + K_D] * scale, proj_ref[:, CD_K:CD_K + K_D],
                   proj_ref[:, CD_V:CD_V + V_D], la_gla, gst_ref, masks, emask1, store_o)
        outs = _cd_post(ys_ref[...], xbc[:, :D_C], proj_ref[:, CD_Z:CD_Z + D_C], os_ref[...],
                        proj_ref[:, CD_G:CD_G + V_D], dskip_ref[...], ssdn_ref[...], glan_ref[...])
        y_ref[:, :D_C] = outs[0]
        for h in range(H_D):
            y_ref[:, D_C + h * DV_D:D_C + (h + 1) * DV_D] = outs[1 + h]

    @pl.when(ti == pl.num_programs(1) - 1)
    def _():
        ssdT_ref[...] = sst_ref[...]
        glaT_ref[...] = gst_ref[...]


def _cd(proj, conv_state, ssd0, gla0, cw, cb, dtb, a_neg, dskip, ssd_norm, aup, ab, gla_norm):
    B, L, _ = proj.shape
    G = G_ROWS
    NB = B // G
    small = [cw, cb, dtb, a_neg, dskip, ssd_norm, aup, ab, gla_norm]
    full = lambda s: pl.BlockSpec(s.shape, lambda b, i: (0,) * s.ndim)
    ssd_dims = (G_C, HG_C * P_C, G * N_C)
    ssd0l = _state_to_lanes(ssd0, G).reshape((NB,) + ssd_dims)
    gla0l = _state_to_lanes(jnp.swapaxes(gla0, 2, 3), G)
    ssd_spec = pl.BlockSpec((None,) + ssd_dims, lambda b, i: (b, 0, 0, 0))
    gla_spec = pl.BlockSpec((None, H_D, DV_D, G * DK_D), lambda b, i: (b, 0, 0, 0))
    st_shapes = [jax.ShapeDtypeStruct(ssd0l.shape, F32), jax.ShapeDtypeStruct(gla0l.shape, F32)]
    st_scratch = [pltpu.VMEM(ssd_dims, F32), pltpu.VMEM((H_D, DV_D, G * DK_D), F32)]
    DO = D_C + V_D
    if L == 1:
        kern = functools.partial(_cd_kernel, G, 1, 1, False)
        y, ssdT, glaT = pl.pallas_call(
            kern,
            grid=(NB, 1),
            in_specs=[pl.BlockSpec((G, IN_CD_PAD), lambda b, i: (b, 0)),
                      pl.BlockSpec((CONV_W - 1, G, D_XBC), lambda b, i: (0, b, 0)),
                      ssd_spec, gla_spec] + [full(s) for s in small],
            out_specs=[pl.BlockSpec((G, DO), lambda b, i: (b, 0)), ssd_spec, gla_spec],
            out_shape=[jax.ShapeDtypeStruct((B, DO), F32)] + st_shapes,
            scratch_shapes=st_scratch + [pltpu.VMEM((G, D_C), F32), pltpu.VMEM((G, V_D), F32)],
            compiler_params=_cparams(("parallel", "arbitrary")),
        )(proj.reshape(B, IN_CD_PAD), jnp.transpose(conv_state, (1, 0, 2)), ssd0l, gla0l, *small)
        y = y.reshape(B, 1, DO)
    else:
        T = min(T_CHUNK, L)
        TL = min(TL_BLOCK, L)
        NC = TL // T
        kern = functools.partial(_cd_kernel, G, T, NC, True)
        y, ssdT, glaT = pl.pallas_call(
            kern,
            grid=(NB, L // TL),
            in_specs=[pl.BlockSpec((G, TL, IN_CD_PAD), lambda b, i: (b, i, 0)),
                      pl.BlockSpec((G, CONV_W - 1, D_XBC), lambda b, i: (b, 0, 0)),
                      ssd_spec, gla_spec] + [full(s) for s in small],
            out_specs=[pl.BlockSpec((G, TL, DO), lambda b, i: (b, i, 0)), ssd_spec, gla_spec],
            out_shape=[jax.ShapeDtypeStruct((B, L, DO), F32)] + st_shapes,
            scratch_shapes=st_scratch + [pltpu.VMEM((G, SUBLANES + TL, D_XBC), F32),
                                         pltpu.VMEM((G, TL, D_XBC), F32),
                                         pltpu.VMEM((4, G, TL, LANES), F32),
                                         pltpu.VMEM((G, TL, D_C), F32),
                                         pltpu.VMEM((G, TL, V_D), F32)],
            compiler_params=_cparams(("parallel", "arbitrary")),
        )(proj, conv_state, ssd0l, gla0l, *small)
    ssd_new = _state_from_lanes(ssdT.reshape(NB, H_C, P_C, G * N_C), G)
    gla_new = jnp.swapaxes(_state_from_lanes(glaT, G), 2, 3)
    return y, ssd_new, gla_new


def _block_diag(w):
    eye = jnp.eye(N_BLK_A, dtype=w.dtype)
    return jnp.einsum('nij,nm->nimj', w, eye).reshape(D_A, D_A)


def _row(v):
    return v.reshape(1, -1).astype(F32)


def _prep_params(p):
    q = {}
    j = 0
    q['w_in_ab'] = p['w_in_ab'][j].astype(BF16)
    q['w_out_ab'] = p['w_out_ab'][j].astype(BF16)
    q['lru'] = [p['lru_conv_w'][j], _row(p['lru_conv_b'][j]),
                _block_diag(p['lru_w_r'][j]).astype(BF16), _row(p['lru_b_r'][j]),
                _block_diag(p['lru_w_i'][j]).astype(BF16), _row(p['lru_b_i'][j]), _row(p['lru_lambda'][j])]
    zw = jnp.zeros((R_W, D_B), F32)
    w2p = jnp.concatenate([p['rwkv_w2'][j], zw], axis=0).astype(BF16)
    a2p = jnp.concatenate([zw, p['rwkv_a2'][j]], axis=0).astype(BF16)
    q['rwkv'] = [_row(p['rwkv_mu'][j]), _row(p['rwkv_w0'][j]), w2p, _row(p['rwkv_a0'][j]), a2p,
                 p['rwkv_g2'][j].astype(BF16), _row(p['rwkv_k_k'][j]), _row(p['rwkv_k_a'][j]),
                 _row(p['rwkv_r_k'][j]), _row(p['rwkv_ln_w'][j]), _row(p['rwkv_ln_b'][j])]
    q['ffn'] = [p['ffn_w_gate'][j].astype(BF16), p['ffn_w_up'][j].astype(BF16), p['ffn_w_down'][j].astype(BF16)]
    w = p['w_in_cd'][j]
    o_z, o_xbc, o_dt = 0, D_C, D_C + D_XBC
    o_q = o_dt + H_C
    o_k = o_q + K_D
    o_v = o_k + K_D
    o_g = o_v + V_D
    o_al = o_g + V_D
    pad = jnp.zeros((D_MODEL, LANES - H_C - R_ALPHA), F32)
    q['w_in_cd'] = jnp.concatenate([w[:, o_z:o_dt], w[:, o_q:o_al], w[:, o_dt:o_q], w[:, o_al:], pad],
                                   axis=1).astype(BF16)
    q['w_out_cd'] = p['w_out_cd'][j].astype(BF16)
    lane_pad = lambda v: jnp.concatenate([v.astype(F32), jnp.zeros((LANES - v.shape[0],), F32)]).reshape(1, LANES)
    aup = jnp.zeros((LANES, K_D), F32).at[H_C:H_C + R_ALPHA].set(p['gla_alpha_up'][j]).astype(BF16)
    q['cd'] = [p['ssd_conv_w'][j], _row(p['ssd_conv_b'][j]), lane_pad(p['ssd_dt_bias'][j]),
               lane_pad(-jnp.exp(p['ssd_a_log'][j].astype(F32))), _row(jnp.repeat(p['ssd_d'][j], P_C)),
               _row(p['ssd_norm_w'][j]), aup, _row(p['gla_alpha_b'][j]), _row(p['gla_norm_w'][j])]
    q['router'] = jnp.concatenate([p['moe_router'][j], jnp.zeros((D_MODEL, LANES - N_EXP), F32)], axis=1)
    q['moe'] = [p['moe_w_gate'][j].astype(BF16), p['moe_w_up'][j].astype(BF16), p['moe_w_down'][j].astype(BF16)]
    q['norm_mix'] = [_row(p['norm_mix'][i]) for i in range(DEPTH)]
    q['norm_ffn'] = [_row(p['norm_ffn'][i]) for i in range(DEPTH)]
    q['norm_ple'] = [_row(p['norm_ple'][i]) for i in range(DEPTH)]
    q['w_ple_gate'] = [p['w_ple_gate'][i].astype(BF16) for i in range(DEPTH)]
    q['w_ple_proj'] = [p['w_ple_proj'][i].astype(BF16) for i in range(DEPTH)]
    q['norm_final'] = _row(p['norm_final'])
    return q


def _trunk(x, p, states, q):
    lru_conv, lru_h, rwkv_shift, rwkv_wkv, ssd_conv, ssd_st, gla_st = states
    B, L, D = x.shape
    M = B * L
    h = x.reshape(M, D)
    p2 = p.reshape(DEPTH, M, D_PLE)

    proj = _norm_mm(h, q['norm_mix'][0], q['w_in_ab'], IN_AB // 2).reshape(B, L, IN_AB)
    y_a, h_new = _lru(proj, lru_conv[0], lru_h[0], *q['lru'])
    y_b, wkv_new = _rwkv(proj, rwkv_shift[0], rwkv_wkv[0], *q['rwkv'])
    if L >= CONV_W - 1:
        conv_new = proj[:, L - (CONV_W - 1):, :D_A]
    else:
        conv_new = jnp.concatenate([lru_conv[0][:, L:], proj[:, :, :D_A]], axis=1)
    shift_new = proj[:, L - 1, 2 * D_A:]
    w_out = q['w_out_ab']
    h = _out_proj(h, [y_a.reshape(M, D_A), y_b.reshape(M, D_B)], [w_out[:D_A], w_out[D_A:]])
    h = _ffn(h, q['norm_ffn'][0], *q['ffn'], D_FF // 2)
    h = _ple(h, q['norm_ple'][0], q['w_ple_gate'][0], p2[0], q['w_ple_proj'][0], q['norm_final'], False)

    proj = _norm_mm(h, q['norm_mix'][1], q['w_in_cd'], IN_CD_PAD // 5).reshape(B, L, IN_CD_PAD)
    y_cd, ssd_new, gla_new = _cd(proj, ssd_conv[0], ssd_st[0], gla_st[0], *q['cd'])
    if L >= CONV_W - 1:
        sconv_new = proj[:, L - (CONV_W - 1):, CD_XBC:CD_XBC + D_XBC]
    else:
        sconv_new = jnp.concatenate([ssd_conv[0][:, L:], proj[:, :, CD_XBC:CD_XBC + D_XBC]], axis=1)
    h = _out_proj(h, [y_cd.reshape(M, D_C + V_D)], [q['w_out_cd']])
    route, routeT, meta = _router(h, q['norm_ffn'][1], q['router'])
    h = _moe(h, q['norm_ffn'][1], route, routeT, meta, *q['moe'])
    y = _ple(h, q['norm_ple'][1], q['w_ple_gate'][1], p2[1], q['w_ple_proj'][1], q['norm_final'], True)

    new_states = (conv_new[None], h_new[None], shift_new[None], wkv_new[None],
                  sconv_new[None], ssd_new[None], gla_new[None])
    return y.reshape(B, L, D), new_states


def kernel(x_prompt, x_sample, p_prompt, p_sample, state_lru_conv, state_lru_h, state_rwkv_shift,
           state_rwkv_wkv, state_ssd_conv, state_ssd, state_gla, norm_mix, norm_ffn, norm_ple,
           w_ple_gate, w_ple_proj, norm_final, w_in_ab, w_out_ab, lru_conv_w, lru_conv_b, lru_w_r,
           lru_b_r, lru_w_i, lru_b_i, lru_lambda, rwkv_mu, rwkv_w0, rwkv_w2, rwkv_a0, rwkv_a2,
           rwkv_g2, rwkv_k_k, rwkv_k_a, rwkv_r_k, rwkv_ln_w, rwkv_ln_b, ffn_w_gate, ffn_w_up,
           ffn_w_down, w_in_cd, w_out_cd, ssd_conv_w, ssd_conv_b, ssd_dt_bias, ssd_a_log, ssd_d,
           ssd_norm_w, gla_alpha_up, gla_alpha_b, gla_norm_w, moe_router, moe_w_gate, moe_w_up,
           moe_w_down):
    prm = dict(
        norm_mix=norm_mix, norm_ffn=norm_ffn, norm_ple=norm_ple, w_ple_gate=w_ple_gate,
        w_ple_proj=w_ple_proj, norm_final=norm_final, w_in_ab=w_in_ab, w_out_ab=w_out_ab,
        lru_conv_w=lru_conv_w, lru_conv_b=lru_conv_b, lru_w_r=lru_w_r, lru_b_r=lru_b_r,
        lru_w_i=lru_w_i, lru_b_i=lru_b_i, lru_lambda=lru_lambda, rwkv_mu=rwkv_mu, rwkv_w0=rwkv_w0,
        rwkv_w2=rwkv_w2, rwkv_a0=rwkv_a0, rwkv_a2=rwkv_a2, rwkv_g2=rwkv_g2, rwkv_k_k=rwkv_k_k,
        rwkv_k_a=rwkv_k_a, rwkv_r_k=rwkv_r_k, rwkv_ln_w=rwkv_ln_w, rwkv_ln_b=rwkv_ln_b,
        ffn_w_gate=ffn_w_gate, ffn_w_up=ffn_w_up, ffn_w_down=ffn_w_down, w_in_cd=w_in_cd,
        w_out_cd=w_out_cd, ssd_conv_w=ssd_conv_w, ssd_conv_b=ssd_conv_b, ssd_dt_bias=ssd_dt_bias,
        ssd_a_log=ssd_a_log, ssd_d=ssd_d, ssd_norm_w=ssd_norm_w, gla_alpha_up=gla_alpha_up,
        gla_alpha_b=gla_alpha_b, gla_norm_w=gla_norm_w, moe_router=moe_router,
        moe_w_gate=moe_w_gate, moe_w_up=moe_w_up, moe_w_down=moe_w_down)
    q = _prep_params(prm)
    states_s = (state_lru_conv, state_lru_h, state_rwkv_shift, state_rwkv_wkv,
                state_ssd_conv, state_ssd, state_gla)
    n_prompt = x_prompt.shape[0]
    states_p = tuple(jnp.zeros((s.shape[0], n_prompt) + s.shape[2:], s.dtype) for s in states_s)
    y_p, st_p = _trunk(x_prompt, p_prompt, states_p, q)
    y_s, st_s = _trunk(x_sample, p_sample, states_s, q)
    return (y_p, y_s) + tuple(st_p) + tuple(st_s)
```

```python
import functools
import math

import jax
import jax.numpy as jnp
from jax import lax
from jax.experimental import pallas as pl
from jax.experimental.pallas import tpu as pltpu

F32, BF16 = jnp.float32, jnp.bfloat16

D_MODEL = 1024
DEPTH = 2
D_PLE = 256
EPS = 1e-6
CONV_W = 4
D_A = 512
N_BLK_A = 8
BW_A = D_A // N_BLK_A
LRU_C = 8.0
H_B = 8
HD_B = 64
D_B = H_B * HD_B
R_W = 64
R_A = 64
R_G = 128
D_BPROJ = 3 * D_B + R_W + R_A + R_G
GN_EPS_B = 64e-5
IN_AB = 2 * D_A + D_BPROJ
H_C = 8
P_C = 64
D_C = H_C * P_C
G_C = 2
HG_C = H_C // G_C
N_C = 128
D_XBC = D_C + 2 * G_C * N_C
H_D = 4
DK_D = 64
DV_D = 128
K_D = H_D * DK_D
V_D = H_D * DV_D
R_ALPHA = 16
GATE_NORM = 16.0
D_FF = 2816
N_EXP = 8
D_FF_E = 1408
IN_CD_PAD = D_C + D_XBC + 2 * K_D + 2 * V_D + 128

LANES = 128
SUBLANES = 8
VMEM_LIMIT = 56 * 1024 * 1024

G_ROWS = SUBLANES
T_CHUNK = 16
TL_BLOCK = 64
TM_TOK = 1024
TM_FFN = 512
TM_MOE = 1024
MOE_RB = 128
MOE_SORT_ROWS = 256
MOE_UNSORT_COLS = 512
TOP_K = 2
HEAD_SUM_PIECES = 2
CONV_COLS = 256
LRU_TL = 256

NN = ((1,), (0,))
NT = ((1,), (1,))
TN = ((0,), (0,))


def _cparams(sem):
    return pltpu.CompilerParams(dimension_semantics=sem, vmem_limit_bytes=VMEM_LIMIT)


def _rms(x, g):
    return x * lax.rsqrt(jnp.mean(x * x, axis=-1, keepdims=True) + EPS) * g


def _sigmoid(x):
    return jax.nn.sigmoid(x)


def _silu(x):
    return x * jax.nn.sigmoid(x)


def _softplus(x):
    return jnp.maximum(x, 0.0) + jnp.log1p(jnp.exp(-jnp.abs(x)))


def _gelu_tanh(x):
    c = math.sqrt(2.0 / math.pi)
    return 0.5 * x * (1.0 + jnp.tanh(c * (x + 0.044715 * (x * x * x))))


def _d(a, b, dims=NN):
    return lax.dot_general(a, b, (dims, ((), ())), preferred_element_type=F32)


def _mm(a, b, dims=NN):
    return _d(a.astype(BF16), b.astype(BF16), dims)


def _split2(x):
    hi = x.astype(BF16)
    lo = (x - hi.astype(F32)).astype(BF16)
    return hi, lo


def _mm3(a, b, dims=NN):
    ah, al = _split2(a)
    bh, bl = _split2(b)
    return _d(ah, bh, dims) + (_d(ah, bl, dims) + _d(al, bh, dims))


def _mm_sel(sel, x, pieces, sel_left=True, dims=NN):
    out = None
    rest = x
    for i in range(pieces):
        p = rest.astype(BF16)
        if i + 1 < pieces:
            rest = rest - p.astype(F32)
        t = _d(sel, p, dims) if sel_left else _d(p, sel, dims)
        out = t if out is None else out + t
    return out


def _chunk_masks(G, T):
    R = G * T
    ri = lax.broadcasted_iota(jnp.int32, (R, R), 0)
    ci = lax.broadcasted_iota(jnp.int32, (R, R), 1)
    same = (ri // T) == (ci // T)
    incl = same & (ci <= ri)
    strict = same & (ci < ri)
    last = ci == (ri // T) * T + (T - 1)
    return incl, strict, last


def _to_bf16_mask(m):
    return jnp.where(m, 1.0, 0.0).astype(BF16)


def _tile_sel(W, G):
    r = lax.broadcasted_iota(jnp.int32, (W, G * W), 0)
    c = lax.broadcasted_iota(jnp.int32, (W, G * W), 1)
    return _to_bf16_mask((c % W) == r)


def _batch_lane_mask(n_stack, G, T, W):
    R = G * T
    r = lax.broadcasted_iota(jnp.int32, (n_stack * R, G * W), 0)
    c = lax.broadcasted_iota(jnp.int32, (n_stack * R, G * W), 1)
    return ((r % R) // T) == (c // W)


def _head_sum_sel(width, head):
    r = lax.broadcasted_iota(jnp.int32, (width, width), 0)
    c = lax.broadcasted_iota(jnp.int32, (width, width), 1)
    return _to_bf16_mask((r // head) == (c // head))


def _norm_mm_kernel(x_ref, g_ref, w_ref, o_ref, xn_ref):
    @pl.when(pl.program_id(1) == 0)
    def _():
        xn_ref[...] = _rms(x_ref[...], g_ref[...]).astype(BF16)

    o_ref[...] = _d(xn_ref[...], w_ref[...])


def _norm_mm(x, g, w, tn):
    M, K = x.shape
    N = w.shape[1]
    tm = min(TM_TOK, M)
    return pl.pallas_call(
        _norm_mm_kernel,
        grid=(M // tm, N // tn),
        in_specs=[pl.BlockSpec((tm, K), lambda i, j: (i, 0)),
                  pl.BlockSpec((1, K), lambda i, j: (0, 0)),
                  pl.BlockSpec((K, tn), lambda i, j: (0, j))],
        out_specs=pl.BlockSpec((tm, tn), lambda i, j: (i, j)),
        out_shape=jax.ShapeDtypeStruct((M, N), F32),
        scratch_shapes=[pltpu.VMEM((tm, K), BF16)],
        compiler_params=_cparams(("parallel", "arbitrary")),
    )(x, g, w)


def _out_proj_kernel(n_in, *refs):
    h_ref = refs[0]
    ys = refs[1:1 + n_in]
    ws = refs[1 + n_in:1 + 2 * n_in]
    o_ref = refs[1 + 2 * n_in]
    acc = h_ref[...]
    for y_ref, w_ref in zip(ys, ws):
        acc = acc + _mm(y_ref[...], w_ref[...])
    o_ref[...] = acc


def _out_proj(h, ys, ws):
    M, D = h.shape
    tm = min(TM_TOK, M)
    n_in = len(ys)
    in_specs = [pl.BlockSpec((tm, D), lambda i: (i, 0))]
    in_specs += [pl.BlockSpec((tm, y.shape[1]), lambda i: (i, 0)) for y in ys]
    in_specs += [pl.BlockSpec(w.shape, lambda i: (0, 0)) for w in ws]
    return pl.pallas_call(
        functools.partial(_out_proj_kernel, n_in),
        grid=(M // tm,),
        in_specs=in_specs,
        out_specs=pl.BlockSpec((tm, D), lambda i: (i, 0)),
        out_shape=jax.ShapeDtypeStruct((M, D), F32),
        compiler_params=_cparams(("parallel",)),
    )(h, *ys, *ws)


def _ffn_kernel(h_ref, g_ref, wg_ref, wu_ref, wd_ref, o_ref, xn_ref, acc_ref):
    j = pl.program_id(1)

    @pl.when(j == 0)
    def _():
        xn_ref[...] = _rms(h_ref[...], g_ref[...]).astype(BF16)
        acc_ref[...] = jnp.zeros_like(acc_ref)

    xn = xn_ref[...]
    act = _silu(_d(xn, wg_ref[...])) * _d(xn, wu_ref[...])
    acc_ref[...] += _mm(act, wd_ref[...])

    @pl.when(j == pl.num_programs(1) - 1)
    def _():
        o_ref[...] = h_ref[...] + acc_ref[...]


def _ffn(h, g, wg, wu, wd, tf):
    M, D = h.shape
    FF = wg.shape[1]
    tm = min(TM_FFN, M)
    return pl.pallas_call(
        _ffn_kernel,
        grid=(M // tm, FF // tf),
        in_specs=[pl.BlockSpec((tm, D), lambda i, j: (i, 0)),
                  pl.BlockSpec((1, D), lambda i, j: (0, 0)),
                  pl.BlockSpec((D, tf), lambda i, j: (0, j)),
                  pl.BlockSpec((D, tf), lambda i, j: (0, j)),
                  pl.BlockSpec((tf, D), lambda i, j: (j, 0))],
        out_specs=pl.BlockSpec((tm, D), lambda i, j: (i, 0)),
        out_shape=jax.ShapeDtypeStruct((M, D), F32),
        scratch_shapes=[pltpu.VMEM((tm, D), BF16), pltpu.VMEM((tm, D), F32)],
        compiler_params=_cparams(("parallel", "arbitrary")),
    )(h, g, wg, wu, wd)


def _router_kernel(h_ref, g_ref, wr_ref, route_ref, routeT_ref, meta_ref):
    xn = _rms(h_ref[...], g_ref[...])
    logits = _mm3(xn, wr_ref[...])
    lane = lax.broadcasted_iota(jnp.int32, logits.shape, 1)
    valid = lane < N_EXP
    logits = jnp.where(valid, logits, -1e30)
    m = jnp.max(logits, axis=-1, keepdims=True)
    e = jnp.where(valid, jnp.exp(logits - m), 0.0)
    p = e / jnp.sum(e, axis=-1, keepdims=True)
    lane_f = lane.astype(F32)
    m1 = jnp.max(p, axis=-1, keepdims=True)
    i1 = jnp.min(jnp.where(p == m1, lane_f, float(LANES)), axis=-1, keepdims=True)
    p2 = jnp.where((lane_f == i1) | (lane >= N_EXP), -1.0, p)
    m2 = jnp.max(p2, axis=-1, keepdims=True)
    i2 = jnp.min(jnp.where(p2 == m2, lane_f, float(LANES)), axis=-1, keepdims=True)
    tot = m1 + m2
    tm = p.shape[0]
    sel1 = lane_f == i1
    sel2 = lane_f == i2
    assign = jnp.where(sel1 | sel2, 1.0, 0.0)
    ri = lax.broadcasted_iota(jnp.int32, (tm, tm), 0)
    ci = lax.broadcasted_iota(jnp.int32, (tm, tm), 1)
    rank = _d(_to_bf16_mask(ci < ri), assign.astype(BF16))
    cnt = jnp.sum(assign, axis=0, keepdims=True)
    padded = jnp.floor((cnt + (MOE_RB - 1)) * (1.0 / MOE_RB)) * MOE_RB
    er = lax.broadcasted_iota(jnp.int32, (LANES, LANES), 0)
    ec = lax.broadcasted_iota(jnp.int32, (LANES, LANES), 1)
    off = _mm_sel(_to_bf16_mask(er < ec), jnp.broadcast_to(padded, (SUBLANES, LANES)), 2, sel_left=False)[0:1]
    pos = off + rank
    dest1 = jnp.sum(jnp.where(sel1, pos, 0.0), axis=-1, keepdims=True)
    dest2 = jnp.sum(jnp.where(sel2, pos, 0.0), axis=-1, keepdims=True)
    route = jnp.where(lane == 0, dest1, jnp.where(lane == 1, dest2,
                      jnp.where(lane == 2, m1 / tot, jnp.where(lane == 3, m2 / tot, 0.0))))
    route_ref[...] = route
    routeT_ref[...] = route.T[:SUBLANES]
    mrow = lax.broadcasted_iota(jnp.int32, (SUBLANES, LANES), 0)
    meta_ref[...] = jnp.where(mrow == 0, padded * (1.0 / MOE_RB), jnp.where(mrow == 1, off * (1.0 / MOE_RB), 0.0))


def _router(h, g, wr):
    M, D = h.shape
    tm = min(TM_MOE, M)
    nt = M // tm
    return pl.pallas_call(
        _router_kernel,
        grid=(nt,),
        in_specs=[pl.BlockSpec((tm, D), lambda i: (i, 0)),
                  pl.BlockSpec((1, D), lambda i: (0, 0)),
                  pl.BlockSpec((D, LANES), lambda i: (0, 0))],
        out_specs=[pl.BlockSpec((tm, LANES), lambda i: (i, 0)),
                   pl.BlockSpec((None, SUBLANES, tm), lambda i: (i, 0, 0)),
                   pl.BlockSpec((None, SUBLANES, LANES), lambda i: (i, 0, 0))],
        out_shape=[jax.ShapeDtypeStruct((M, LANES), F32),
                   jax.ShapeDtypeStruct((nt, SUBLANES, tm), F32),
                   jax.ShapeDtypeStruct((nt, SUBLANES, LANES), F32)],
        compiler_params=_cparams(("parallel",)),
    )(h, g, wr)


def _moe_kernel(nblk_ref, offb_ref, h_ref, g_ref, route_ref, routeT_ref, wg_ref, wu_ref, wd_ref, o_ref,
                xs_ref, ys_ref, gs_ref, xn_ref):
    i = pl.program_id(0)
    e = pl.program_id(1)
    tm = h_ref.shape[0]
    cap = xs_ref.shape[0]

    last_e = pl.num_programs(1) - 1
    used_rows = (offb_ref[i, last_e] + nblk_ref[i, last_e]) * MOE_RB

    @pl.when(e == 0)
    def _():
        xn_ref[...] = _rms(h_ref[...], g_ref[...]).astype(BF16)
        ys_ref[...] = jnp.zeros_like(ys_ref)
        row0 = lax.broadcasted_iota(jnp.int32, (MOE_SORT_ROWS, tm), 0).astype(F32)
        for rb in range(cap // MOE_SORT_ROWS):
            lo, hi = rb * MOE_SORT_ROWS, (rb + 1) * MOE_SORT_ROWS

            @pl.when(lo < used_rows)
            def _():
                d1, d2 = routeT_ref[0:1, :], routeT_ref[1:2, :]
                g1, g2 = routeT_ref[2:3, :], routeT_ref[3:4, :]
                rows = row0 + float(lo)
                m1 = rows == d1
                m2 = rows == d2
                xs_ref[lo:hi, :] = _d(_to_bf16_mask(m1 | m2), xn_ref[...]).astype(BF16)
                gate = jnp.sum(jnp.where(m1, g1, 0.0) + jnp.where(m2, g2, 0.0), axis=-1, keepdims=True)
                gs_ref[lo:hi, :] = jnp.broadcast_to(gate, (MOE_SORT_ROWS, LANES))

    base = offb_ref[i, e]

    def block(j, carry):
        r0 = pl.multiple_of((base + j) * MOE_RB, MOE_RB)
        x = xs_ref[pl.ds(r0, MOE_RB), :]
        act = _silu(_d(x, wg_ref[...])) * _d(x, wu_ref[...])
        act = act * gs_ref[pl.ds(r0, MOE_RB), 0:1]
        ys_ref[pl.ds(r0, MOE_RB), :] = _mm(act, wd_ref[...]).astype(BF16)
        return carry

    lax.fori_loop(0, nblk_ref[i, e], block, 0)

    @pl.when(e == last_e)
    def _():
        o_ref[...] = h_ref[...]
        col0 = lax.broadcasted_iota(jnp.int32, (tm, MOE_UNSORT_COLS), 1).astype(F32)
        for cb in range(cap // MOE_UNSORT_COLS):
            lo, hi = cb * MOE_UNSORT_COLS, (cb + 1) * MOE_UNSORT_COLS

            @pl.when(lo < used_rows)
            def _():
                d1, d2 = route_ref[:, 0:1], route_ref[:, 1:2]
                cols = col0 + float(lo)
                o_ref[...] += _d(_to_bf16_mask((cols == d1) | (cols == d2)), ys_ref[lo:hi, :])


def _moe(h, g, route, routeT, meta, wg, wu, wd):
    M, D = h.shape
    E, _, FF = wg.shape
    tm = min(TM_MOE, M)
    nt = M // tm
    cap = TOP_K * tm + E * MOE_RB
    cap = -(-cap // MOE_UNSORT_COLS) * MOE_UNSORT_COLS
    meta_i = meta.astype(jnp.int32)
    grid_spec = pltpu.PrefetchScalarGridSpec(
        num_scalar_prefetch=2,
        grid=(nt, E),
        in_specs=[pl.BlockSpec((tm, D), lambda i, e, nb, ob: (i, 0), pipeline_mode=pl.Buffered(1)),
                  pl.BlockSpec((1, D), lambda i, e, nb, ob: (0, 0)),
                  pl.BlockSpec((tm, LANES), lambda i, e, nb, ob: (i, 0)),
                  pl.BlockSpec((None, SUBLANES, tm), lambda i, e, nb, ob: (i, 0, 0)),
                  pl.BlockSpec((None, D, FF), lambda i, e, nb, ob: (e, 0, 0)),
                  pl.BlockSpec((None, D, FF), lambda i, e, nb, ob: (e, 0, 0)),
                  pl.BlockSpec((None, FF, D), lambda i, e, nb, ob: (e, 0, 0))],
        out_specs=pl.BlockSpec((tm, D), lambda i, e, nb, ob: (i, 0)),
        scratch_shapes=[pltpu.VMEM((cap, D), BF16), pltpu.VMEM((cap, D), BF16), pltpu.VMEM((cap, LANES), F32),
                        pltpu.VMEM((tm, D), BF16)])
    return pl.pallas_call(
        _moe_kernel,
        grid_spec=grid_spec,
        out_shape=jax.ShapeDtypeStruct((M, D), F32),
        compiler_params=_cparams(("parallel", "arbitrary")),
    )(meta_i[:, 0, :], meta_i[:, 1, :], h, g, route, routeT, wg, wu, wd)


def _ple_kernel(final, h_ref, g_ref, wg_ref, p_ref, wp_ref, gf_ref, o_ref):
    h = h_ref[...]
    gate = _sigmoid(_mm(_rms(h, g_ref[...]), wg_ref[...]))
    out = h + gate * _mm(p_ref[...], wp_ref[...])
    if final:
        out = _rms(out, gf_ref[...])
    o_ref[...] = out


def _ple(h, g, wg, p, wp, gf, final):
    M, D = h.shape
    tm = min(TM_TOK, M)
    return pl.pallas_call(
        functools.partial(_ple_kernel, final),
        grid=(M // tm,),
        in_specs=[pl.BlockSpec((tm, D), lambda i: (i, 0)),
                  pl.BlockSpec((1, D), lambda i: (0, 0)),
                  pl.BlockSpec((D, D), lambda i: (0, 0)),
                  pl.BlockSpec((tm, D_PLE), lambda i: (i, 0)),
                  pl.BlockSpec((D_PLE, D), lambda i: (0, 0)),
                  pl.BlockSpec((1, D), lambda i: (0, 0))],
        out_specs=pl.BlockSpec((tm, D), lambda i: (i, 0)),
        out_shape=jax.ShapeDtypeStruct((M, D), F32),
        compiler_params=_cparams(("parallel",)),
    )(h, g, wg, p, wp, gf)


def _lru_gates(xc, ga_unused, wr, br, wi, bi, lam):
    r = _sigmoid(_mm(xc, wr) + br)
    i = _sigmoid(_mm(xc, wi) + bi)
    log_a = -LRU_C * r * _softplus(-lam)
    a = jnp.exp(log_a)
    u = jnp.sqrt(jnp.tanh(-log_a) * (a * a + 1.0)) * (i * xc)
    return a, u


def _lru_seq_kernel(TL, proj_ref, cs_ref, h0_ref, cw_ref, cb_ref, wr_ref, br_ref, wi_ref, bi_ref, lam_ref,
                    y_ref, hT_ref, pad_ref, a_ref, u_ref, hs_ref, hc_ref):
    ti = pl.program_id(0)
    G = pad_ref.shape[0]

    @pl.when(ti == 0)
    def _():
        pad_ref[:, SUBLANES - (CONV_W - 1):SUBLANES, :] = cs_ref[...]
        hc_ref[...] = h0_ref[...]

    pad_ref[:, SUBLANES:, :] = proj_ref[:, :, :D_A]
    cw = cw_ref[...]
    xc = cb_ref[...][None]
    for k in range(CONV_W):
        off = SUBLANES - (CONV_W - 1) + k
        xc = xc + pad_ref[:, off:off + TL, :] * cw[k:k + 1, :][None]
    pad_ref[:, :SUBLANES, :] = pad_ref[:, TL:TL + SUBLANES, :]

    a, u = _lru_gates(xc.reshape(G * TL, D_A), None, wr_ref[...], br_ref[...], wi_ref[...], bi_ref[...],
                      lam_ref[...])
    a_ref[...] = a.reshape(G, TL, D_A)
    u_ref[...] = u.reshape(G, TL, D_A)

    def step(t, h):
        h = a_ref[:, pl.ds(t, 1), :] * h + u_ref[:, pl.ds(t, 1), :]
        hs_ref[:, pl.ds(t, 1), :] = h
        return h

    hc_ref[...] = lax.fori_loop(0, TL, step, hc_ref[...], unroll=8)
    y_ref[...] = _gelu_tanh(proj_ref[:, :, D_A:]) * hs_ref[...]

    @pl.when(ti == pl.num_programs(0) - 1)
    def _():
        hT_ref[...] = hc_ref[...]


def _lru_step_kernel(xa_ref, ga_ref, cs_ref, h0_ref, cw_ref, cb_ref, wr_ref, br_ref, wi_ref, bi_ref, lam_ref,
                     y_ref, hT_ref):
    cw = cw_ref[...]
    x = xa_ref[...]
    xc = cb_ref[...] + x * cw[CONV_W - 1:CONV_W, :]
    for k in range(CONV_W - 1):
        xc = xc + cs_ref[k] * cw[k:k + 1, :]
    a, u = _lru_gates(xc, None, wr_ref[...], br_ref[...], wi_ref[...], bi_ref[...], lam_ref[...])
    h = a * h0_ref[...] + u
    hT_ref[...] = h
    y_ref[...] = _gelu_tanh(ga_ref[...]) * h


def _lru(proj, conv_state, h0, cw, cb, wr, br, wi, bi, lam):
    B, L, _ = proj.shape
    small = [cw, cb, wr, br, wi, bi, lam]
    if L == 1:
        proj2 = proj.reshape(B, -1)
        cs = jnp.transpose(conv_state, (1, 0, 2))
        tb = min(B, 128)
        full = lambda s: pl.BlockSpec(s.shape, lambda i: (0,) * s.ndim)
        y, hT = pl.pallas_call(
            _lru_step_kernel,
            grid=(B // tb,),
            in_specs=[pl.BlockSpec((tb, D_A), lambda i: (i, 0)),
                      pl.BlockSpec((tb, D_A), lambda i: (i, 1)),
                      pl.BlockSpec((CONV_W - 1, tb, D_A), lambda i: (0, i, 0)),
                      pl.BlockSpec((tb, D_A), lambda i: (i, 0))] + [full(s) for s in small],
            out_specs=[pl.BlockSpec((tb, D_A), lambda i: (i, 0)), pl.BlockSpec((tb, D_A), lambda i: (i, 0))],
            out_shape=[jax.ShapeDtypeStruct((B, D_A), F32), jax.ShapeDtypeStruct((B, D_A), F32)],
            compiler_params=_cparams(("parallel",)),
        )(proj2, proj2, cs, h0, *small)
        return y.reshape(B, 1, D_A), hT
    TL = min(LRU_TL, L)
    G = B
    full = lambda s: pl.BlockSpec(s.shape, lambda i: (0,) * s.ndim)
    y, hT = pl.pallas_call(
        functools.partial(_lru_seq_kernel, TL),
        grid=(L // TL,),
        in_specs=[pl.BlockSpec((G, TL, 2 * D_A), lambda i: (0, i, 0)),
                  pl.BlockSpec((G, CONV_W - 1, D_A), lambda i: (0, 0, 0)),
                  pl.BlockSpec((G, 1, D_A), lambda i: (0, 0, 0))] + [full(s) for s in small],
        out_specs=[pl.BlockSpec((G, TL, D_A), lambda i: (0, i, 0)),
                   pl.BlockSpec((G, 1, D_A), lambda i: (0, 0, 0))],
        out_shape=[jax.ShapeDtypeStruct((B, L, D_A), F32), jax.ShapeDtypeStruct((B, 1, D_A), F32)],
        scratch_shapes=[pltpu.VMEM((G, SUBLANES + TL, D_A), F32),
                        pltpu.VMEM((G, TL, D_A), F32), pltpu.VMEM((G, TL, D_A), F32),
                        pltpu.VMEM((G, TL, D_A), F32), pltpu.VMEM((G, 1, D_A), F32)],
        compiler_params=_cparams(("arbitrary",)),
    )(proj, conv_state, h0.reshape(B, 1, D_A), *small)
    return y, hT.reshape(B, D_A)


def _rwkv_prep(xs, w0, w2p, a0, a2p, g2, k_k, k_a, r_k, hsel):
    r = xs[:, 0:D_B]
    k = xs[:, D_B:2 * D_B]
    v = xs[:, 2 * D_B:3 * D_B]
    wa = xs[:, 3 * D_B:3 * D_B + R_W + R_A]
    gl = xs[:, 3 * D_B + R_W + R_A:]
    w_log = -_softplus(-(w0 + _mm(jnp.tanh(wa), w2p))) - 0.5
    logd = -jnp.exp(w_log)
    a = _sigmoid(a0 + _mm(wa, a2p))
    g = _mm(_sigmoid(gl), g2)
    kk = k * k_k
    kk = kk * lax.rsqrt(jnp.maximum(_mm_sel(hsel, kk * kk, HEAD_SUM_PIECES, sel_left=False), 1e-12))
    k2 = k * (1.0 + (a - 1.0) * k_a)
    beta = kk * a
    bonus = _mm_sel(hsel, r * k2 * r_k, HEAD_SUM_PIECES, sel_left=False) * v
    return r, k2, v, logd, kk, beta, g, bonus


def _rwkv_post(y, bonus, g, ln_w, ln_b, hsel):
    mean = _mm_sel(hsel, y, HEAD_SUM_PIECES, sel_left=False) * (1.0 / HD_B)
    d = y - mean
    var = _mm_sel(hsel, d * d, HEAD_SUM_PIECES, sel_left=False) * (1.0 / HD_B)
    yn = d * lax.rsqrt(var + GN_EPS_B) * ln_w + ln_b
    return (yn + bonus) * g


def _mmp(a, b, dims, passes):
    return _mm3(a, b, dims) if passes == 3 else _mm(a, b, dims)


RWKV_PASSES = dict(am=1, state=1, inv=1, apply=1, dw=1)


def _rwkv_chunk(G, T, ops, st_ref, masks, emask, store_y):
    r, k2, v, logd, kk, beta = ops
    incl, strict, lmask_bf, lastmask_bf = masks
    R = G * T
    ps = RWKV_PASSES
    cum = _mm_sel(lmask_bf, logd, 3)
    cend = _mm_sel(lastmask_bf, cum, 3)
    ecn = jnp.exp(-cum)
    P = jnp.exp(cum - logd) * kk
    Q = ecn * beta
    Kt = ecn * k2
    Rt = jnp.exp(cum) * r
    eend = jnp.exp(cend)
    lo = lax.broadcasted_iota(jnp.int32, (R, LANES), 1) < HD_B
    row_lo = lax.broadcasted_iota(jnp.int32, (LANES, G * HD_B), 0) < HD_B
    eye = None
    if T > 1:
        ri = lax.broadcasted_iota(jnp.int32, (R, R), 0)
        ci = lax.broadcasted_iota(jnp.int32, (R, R), 1)
        eye = jnp.where(ri == ci, 1.0, 0.0)
    n_rep = G * HD_B // LANES
    NP = H_B // 2
    heads = [(c, hh) for c in range(NP) for hh in range(2)]
    own = {0: lo, 1: ~lo}

    def tiled(x, xr, hh):
        t = jnp.where(lo, x, xr) if hh == 0 else jnp.where(lo, xr, x)
        return jnp.where(emask, jnp.concatenate([t] * n_rep, axis=1), 0.0)

    pair = []
    for c in range(NP):
        sl = slice(c * LANES, (c + 1) * LANES)
        vals = (P[:, sl], Rt[:, sl], Q[:, sl], Kt[:, sl], eend[:, sl])
        pair.append(dict(x=vals, xr=[pltpu.roll(x, HD_B, 1) for x in vals], V=v[:, sl]))
    AM = [_mmp(jnp.concatenate([jnp.where(own[hh], pair[c]['x'][0], 0.0),
                                jnp.where(own[hh], pair[c]['x'][1], 0.0)], axis=0),
               jnp.concatenate([pair[c]['x'][2], pair[c]['x'][3]], axis=0), NT, ps['am']) for c, hh in heads]
    Mqr = [jnp.where(incl, a[R:, :R], 0.0) for a in AM]
    Mkr = [jnp.where(incl, a[R:, R:], 0.0) for a in AM]
    MV = [_mmp(m, pair[c]['V'], NN, ps['apply']) for m, (c, hh) in zip(Mkr, heads)]
    if T > 1:
        Npow = [-jnp.where(strict, a[:R, :R], 0.0) for a in AM]
        Akp = [jnp.where(strict, a[:R, R:], 0.0) for a in AM]
        AV = [_mmp(m, pair[c]['V'], NN, ps['apply']) for m, (c, hh) in zip(Akp, heads)]
        inv = [eye + n for n in Npow]
        for _ in range(int(math.log2(T)) - 1):
            Npow = [_mmp(n, n, NN, ps['inv']) for n in Npow]
            inv = [i + _mmp(i, n, NN, ps['inv']) for i, n in zip(inv, Npow)]
    ex = [[tiled(x, xr, hh) for x, xr in zip(pair[c]['x'], pair[c]['xr'])] for c, hh in heads]
    WT = [st_ref[c] for c in range(NP)]
    PWRW = [_mmp(jnp.concatenate([e[0], e[1]], axis=0), WT[c], NT, ps['state']) for e, (c, hh) in zip(ex, heads)]
    if T > 1:
        E = [_mmp(i, pw[:R] + av, NN, ps['apply']) for i, pw, av in zip(inv, PWRW, AV)]
    else:
        E = [pw[:R] for pw in PWRW]
    Y = [pw[R:] + mv - _mmp(m, e, NN, ps['apply']) for pw, mv, m, e in zip(PWRW, MV, Mqr, E)]
    for c in range(NP):
        lhs = jnp.concatenate([jnp.where(own[hh], x, 0.0) for hh in range(2)
                               for x in (-E[2 * c + hh], pair[c]['V'])], axis=0)
        rhs = jnp.concatenate([ex[2 * c + hh][j] for hh in range(2) for j in (2, 3)], axis=0)
        dW = _mmp(lhs, rhs, TN, ps['dw'])
        erow = [jnp.sum(ex[2 * c + hh][4], axis=0, keepdims=True) * (1.0 / T) for hh in range(2)]
        st_ref[c] = (WT[c] + dW) * jnp.where(row_lo, erow[0], erow[1])
        store_y(c, jnp.where(lo, Y[2 * c], Y[2 * c + 1]))


def _rwkv_kernel(G, T, NC, seq, proj_ref, sh_ref, s0_ref, mu_ref, w0_ref, w2_ref, a0_ref, a2_ref, g2_ref,
                 kk_ref, ka_ref, rk_ref, lnw_ref, lnb_ref, y_ref, sT_ref, *scratch):
    ti = pl.program_id(1)
    TL = T * NC
    R = G * T
    st_ref = scratch[0]
    hsel = _head_sum_sel(D_B, HD_B)

    @pl.when(ti == 0)
    def _():
        for h in range(H_B):
            r0 = (h % 2) * HD_B
            for j in range(G // 2):
                st_ref[h // 2, r0:r0 + HD_B, j * LANES:(j + 1) * LANES] = jnp.concatenate(
                    [s0_ref[2 * j, h], s0_ref[2 * j + 1, h]], axis=1)

    if seq:
        pad_ref, ops_ref, gb_ref, ys_ref = scratch[1:]

        @pl.when(ti == 0)
        def _():
            pad_ref[:, SUBLANES - 1:SUBLANES, :] = sh_ref[...]

        pad_ref[:, SUBLANES:, :] = proj_ref[:, :, 2 * D_A:]
        pb = pad_ref[:, SUBLANES:, :]
        prev = pad_ref[:, SUBLANES - 1:SUBLANES - 1 + TL, :]
        xs = (pb + (prev - pb) * mu_ref[...][None]).reshape(G * TL, D_BPROJ)
        pad_ref[:, :SUBLANES, :] = pad_ref[:, TL:TL + SUBLANES, :]
    else:
        pb = proj_ref[:, 2 * D_A:]
        xs = pb + (sh_ref[...] - pb) * mu_ref[...]

    r, k2, v, logd, kk, beta, g, bonus = _rwkv_prep(
        xs, w0_ref[...], w2_ref[...], a0_ref[...], a2_ref[...], g2_ref[...], kk_ref[...], ka_ref[...],
        rk_ref[...], hsel)

    incl, strict, last = _chunk_masks(G, T)
    masks = (incl, strict, _to_bf16_mask(incl), _to_bf16_mask(last))
    emask = _batch_lane_mask(1, G, T, HD_B)

    if seq:
        for n, val in enumerate((r, k2, v, logd, kk, beta)):
            ops_ref[n] = val.reshape(G, TL, D_B)
        gb_ref[0] = g.reshape(G, TL, D_B)
        gb_ref[1] = bonus.reshape(G, TL, D_B)

        def chunk(c, carry):
            t0 = pl.multiple_of(c * T, T)
            ops = tuple(ops_ref[n, :, pl.ds(t0, T), :].reshape(R, D_B) for n in range(6))

            def store_y(hp, Y):
                ys_ref[:, pl.ds(t0, T), hp * LANES:(hp + 1) * LANES] = Y.reshape(G, T, LANES)

            _rwkv_chunk(G, T, ops, st_ref, masks, emask, store_y)
            return carry

        lax.fori_loop(0, NC, chunk, 0)
        y = ys_ref[...].reshape(G * TL, D_B)
        out = _rwkv_post(y, gb_ref[1].reshape(G * TL, D_B), gb_ref[0].reshape(G * TL, D_B),
                         lnw_ref[...], lnb_ref[...], hsel)
        y_ref[...] = out.reshape(G, TL, D_B)
    else:
        ys_ref = scratch[1]

        def store_y(hp, Y):
            ys_ref[:, hp * LANES:(hp + 1) * LANES] = Y

        _rwkv_chunk(G, T, (r, k2, v, logd, kk, beta), st_ref, masks, emask, store_y)
        y_ref[...] = _rwkv_post(ys_ref[...], bonus, g, lnw_ref[...], lnb_ref[...], hsel)

    @pl.when(ti == pl.num_programs(1) - 1)
    def _():
        for h in range(H_B):
            r0 = (h % 2) * HD_B
            for b in range(G):
                sT_ref[b, h] = st_ref[h // 2, r0:r0 + HD_B, b * HD_B:(b + 1) * HD_B]


def _state_to_lanes(s, G):
    B, H, X, Y = s.shape
    return s.reshape(B // G, G, H, X, Y).transpose(0, 2, 3, 1, 4).reshape(B // G, H, X, G * Y)


def _state_from_lanes(s, G):
    NB, H, X, GY = s.shape
    Y = GY // G
    return s.reshape(NB, H, X, G, Y).transpose(0, 3, 1, 2, 4).reshape(NB * G, H, X, Y)


def _rwkv(proj, shift0, s0, mu, w0, w2p, a0, a2p, g2, k_k, k_a, r_k, ln_w, ln_b):
    B, L, _ = proj.shape
    G = G_ROWS
    NB = B // G
    small = [mu, w0, w2p, a0, a2p, g2, k_k, k_a, r_k, ln_w, ln_b]
    full = lambda s: pl.BlockSpec(s.shape, lambda b, i: (0,) * s.ndim)
    st_dims = (H_B // 2, 2 * HD_B, G * HD_B)
    s0l = s0
    st_spec = pl.BlockSpec((G, H_B, HD_B, HD_B), lambda b, i: (b, 0, 0, 0))
    st_shape = jax.ShapeDtypeStruct(s0.shape, F32)
    unpair = lambda s: s
    if L == 1:
        T, NC = 1, 1
        kern = functools.partial(_rwkv_kernel, G, T, NC, False)
        y, sT = pl.pallas_call(
            kern,
            grid=(NB, 1),
            in_specs=[pl.BlockSpec((G, IN_AB), lambda b, i: (b, 0)),
                      pl.BlockSpec((G, D_BPROJ), lambda b, i: (b, 0)), st_spec] + [full(s) for s in small],
            out_specs=[pl.BlockSpec((G, D_B), lambda b, i: (b, 0)), st_spec],
            out_shape=[jax.ShapeDtypeStruct((B, D_B), F32), st_shape],
            scratch_shapes=[pltpu.VMEM(st_dims, F32), pltpu.VMEM((G, D_B), F32)],
            compiler_params=_cparams(("parallel", "arbitrary")),
        )(proj.reshape(B, IN_AB), shift0, s0l, *small)
        return y.reshape(B, 1, D_B), unpair(sT)
    T = min(T_CHUNK, L)
    TL = min(TL_BLOCK, L)
    NC = TL // T
    kern = functools.partial(_rwkv_kernel, G, T, NC, True)
    y, sT = pl.pallas_call(
        kern,
        grid=(NB, L // TL),
        in_specs=[pl.BlockSpec((G, TL, IN_AB), lambda b, i: (b, i, 0)),
                  pl.BlockSpec((G, 1, D_BPROJ), lambda b, i: (b, 0, 0)), st_spec] + [full(s) for s in small],
        out_specs=[pl.BlockSpec((G, TL, D_B), lambda b, i: (b, i, 0)), st_spec],
        out_shape=[jax.ShapeDtypeStruct((B, L, D_B), F32), st_shape],
        scratch_shapes=[pltpu.VMEM(st_dims, F32),
                        pltpu.VMEM((G, SUBLANES + TL, D_BPROJ), F32),
                        pltpu.VMEM((6, G, TL, D_B), F32),
                        pltpu.VMEM((2, G, TL, D_B), F32),
                        pltpu.VMEM((G, TL, D_B), F32)],
        compiler_params=_cparams(("parallel", "arbitrary")),
    )(proj, shift0.reshape(B, 1, D_BPROJ), s0l, *small)
    return y, unpair(sT)


CD_Z, CD_XBC, CD_Q, CD_K, CD_V, CD_G, CD_S = 0, 512, 1536, 1792, 2048, 2560, 3072


def _ssd_chunk(G, T, xs, Bm, Cm, dt_all, la_all, st_ref, masks, emaskN, store_y):
    incl, lmask_bf, lastmask_bf = masks
    R = G * T
    cum = _mm_sel(lmask_bf, la_all, 3)
    cend = _mm_sel(lastmask_bf, cum, 3)
    cumT = cum.T
    wend = jnp.exp(cend - cum) * dt_all
    ecum = jnp.exp(cum)
    dend = jnp.exp(cend)
    lo = lax.broadcasted_iota(jnp.int32, (R, LANES), 1) < P_C

    def per_head(x):
        return jnp.concatenate([jnp.where(lo, x[:, 2 * c:2 * c + 1], x[:, 2 * c + 1:2 * c + 2])
                                for c in range(H_C // 2)], axis=1)

    tile = lambda x: jnp.where(emaskN, jnp.concatenate([x] * G, axis=1), 0.0)
    xdt = xs * per_head(dt_all)
    xw = xs * per_head(wend)
    ecf = per_head(ecum)
    Bg = [Bm[:, g * N_C:(g + 1) * N_C] for g in range(G_C)]
    Cg = [Cm[:, g * N_C:(g + 1) * N_C] for g in range(G_C)]
    CB = [_mm(c, b, NT) for c, b in zip(Cg, Bg)]
    ST = [st_ref[g] for g in range(G_C)]
    CS = [_mm(tile(c), s, NT) for c, s in zip(Cg, ST)]
    W = [CB[h // HG_C] * jnp.exp(jnp.where(incl, cum[:, h:h + 1] - cumT[h:h + 1, :], -1e30))
         for h in range(H_C)]
    intra = [_mm(W[h], xdt[:, (h // 2) * LANES:(h // 2 + 1) * LANES]) for h in range(H_C)]
    drow = [jnp.sum(jnp.where(emaskN, dend[:, h:h + 1], 0.0), axis=0, keepdims=True) * (1.0 / T)
            for h in range(H_C)]
    for c in range(H_C // 2):
        sl = slice(c * LANES, (c + 1) * LANES)
        gsl = slice((c % (HG_C // 2)) * LANES, (c % (HG_C // 2) + 1) * LANES)
        store_y(c, jnp.where(lo, intra[2 * c], intra[2 * c + 1]) + ecf[:, sl] * CS[c // (HG_C // 2)][:, gsl])
    for g in range(G_C):
        W_g = HG_C * P_C
        dS = _mm(xw[:, g * W_g:(g + 1) * W_g], tile(Bg[g]), TN)
        st_ref[g] = jnp.concatenate(
            [ST[g][hh * P_C:(hh + 1) * P_C] * drow[g * HG_C + hh] + dS[hh * P_C:(hh + 1) * P_C]
             for hh in range(HG_C)], axis=0)


def _gla_chunk(G, T, q, k, v, la, st_ref, masks, emask, store_o):
    incl, lmask_bf, lastmask_bf = masks
    R = G * T
    bc = _mm_sel(lmask_bf, la, 3)
    bend = _mm_sel(lastmask_bf, bc, 3)
    qe = q * jnp.exp(bc)
    kn = k * jnp.exp(-bc)
    kw = k * jnp.exp(bend - bc)
    dend = jnp.exp(bend)
    lo = lax.broadcasted_iota(jnp.int32, (R, LANES), 1) < DK_D
    own = {0: lo, 1: ~lo}
    n_rep = G * DK_D // LANES

    def tiled(x, xr, hh):
        t = jnp.where(lo, x, xr) if hh == 0 else jnp.where(lo, xr, x)
        return jnp.where(emask, jnp.concatenate([t] * n_rep, axis=1), 0.0)

    heads = [(c, hh) for c in range(H_D // 2) for hh in range(2)]
    pair = []
    for c in range(H_D // 2):
        sl = slice(c * LANES, (c + 1) * LANES)
        vals = (qe[:, sl], kw[:, sl], dend[:, sl])
        pair.append(dict(x=vals, xr=[pltpu.roll(x, DK_D, 1) for x in vals], kn=kn[:, sl]))
    vh = [v[:, h * DV_D:(h + 1) * DV_D] for h in range(H_D)]
    att = [jnp.where(incl, _mm(jnp.where(own[hh], pair[c]['x'][0], 0.0), pair[c]['kn'], NT), 0.0)
           for c, hh in heads]
    ex = [[tiled(x, xr, hh) for x, xr in zip(pair[c]['x'], pair[c]['xr'])] for c, hh in heads]
    ST = [st_ref[h] for h in range(H_D)]
    o = [_mm(a, vv) + _mm(e[0], s, NT) for a, vv, e, s in zip(att, vh, ex, ST)]
    for h in range(H_D):
        drow = jnp.sum(ex[h][2], axis=0, keepdims=True) * (1.0 / T)
        st_ref[h] = ST[h] * drow + _mm(vh[h], ex[h][1], TN)
        store_o(h, o[h])


def _cd_prep_small(small, dtb, a_neg, aup, ab):
    dt_all = _softplus(small + dtb)
    la_all = dt_all * a_neg
    la_gla = -_softplus(-(_mm(small, aup) + ab)) * (1.0 / GATE_NORM)
    return dt_all, la_all, la_gla


def _cd_post(y, xs, z, o, gg, dskip, ssd_norm, gla_norm):
    y_c = _rms((y + dskip * xs) * _silu(z), ssd_norm)
    outs = [y_c]
    for h in range(H_D):
        sl = slice(h * DV_D, (h + 1) * DV_D)
        outs.append(_rms(o[:, sl], gla_norm) * _silu(gg[:, sl]))
    return outs


def _cd_kernel(G, T, NC, seq, proj_ref, cs_ref, ssd0_ref, gla0_ref, cw_ref, cb_ref, dtb_ref, aneg_ref, dskip_ref,
               ssdn_ref, aup_ref, ab_ref, glan_ref, y_ref, ssdT_ref, glaT_ref, *scratch):
    ti = pl.program_id(1)
    TL = T * NC
    R = G * T
    sst_ref, gst_ref = scratch[0], scratch[1]

    @pl.when(ti == 0)
    def _():
        for h in range(H_C):
            r0 = (h % HG_C) * P_C
            for b in range(G):
                sst_ref[h // HG_C, r0:r0 + P_C, b * N_C:(b + 1) * N_C] = ssd0_ref[b, h]
        for h in range(H_D):
            for j in range(G // 2):
                two = jnp.concatenate([gla0_ref[2 * j, h], gla0_ref[2 * j + 1, h]], axis=0)
                gst_ref[h, :, j * LANES:(j + 1) * LANES] = two.T

    cw = cw_ref[...]
    if seq:
        pad_ref, xbc_ref, sm_ref, ys_ref, os_ref = scratch[2:]

        @pl.when(ti == 0)
        def _():
            pad_ref[:, SUBLANES - (CONV_W - 1):SUBLANES, :] = cs_ref[...]

        pad_ref[:, SUBLANES:, :] = proj_ref[:, :, CD_XBC:CD_XBC + D_XBC]

        def conv_b(b, carry):
            for c0 in range(0, D_XBC, CONV_COLS):
                cols = slice(c0, c0 + CONV_COLS)
                acc = cb_ref[:, cols]
                for kq in range(CONV_W):
                    off = SUBLANES - (CONV_W - 1) + kq
                    acc = acc + pad_ref[b, off:off + TL, cols] * cw_ref[kq:kq + 1, cols]
                xbc_ref[b, :, cols] = _silu(acc)
            return carry

        lax.fori_loop(0, G, conv_b, 0)
        pad_ref[:, :SUBLANES, :] = pad_ref[:, TL:TL + SUBLANES, :]
        small = proj_ref[:, :, CD_S:].reshape(G * TL, LANES)
    else:
        x = proj_ref[:, CD_XBC:CD_XBC + D_XBC]
        xc = cb_ref[...] + x * cw[CONV_W - 1:CONV_W, :]
        for kq in range(CONV_W - 1):
            xc = xc + cs_ref[kq] * cw[kq:kq + 1, :]
        xbc = _silu(xc)
        small = proj_ref[:, CD_S:]

    dt_all, la_all, la_gla = _cd_prep_small(small, dtb_ref[...], aneg_ref[...], aup_ref[...], ab_ref[...])

    incl, _, last = _chunk_masks(G, T)
    masks = (incl, _to_bf16_mask(incl), _to_bf16_mask(last))
    emaskN = _batch_lane_mask(1, G, T, N_C)
    emask1 = _batch_lane_mask(1, G, T, DK_D)
    scale = DK_D ** -0.5

    if seq:
        sm_ref[0] = dt_all.reshape(G, TL, LANES)
        sm_ref[1] = la_all.reshape(G, TL, LANES)
        sm_ref[2] = la_gla[:, :LANES].reshape(G, TL, LANES)
        sm_ref[3] = la_gla[:, LANES:].reshape(G, TL, LANES)

        def chunk(c, carry):
            t0 = pl.multiple_of(c * T, T)
            rows = lambda ref, lo, hi: ref[:, pl.ds(t0, T), lo:hi].reshape(R, hi - lo)
            xs = rows(xbc_ref, 0, D_C)
            Bm = rows(xbc_ref, D_C, D_C + G_C * N_C)
            Cm = rows(xbc_ref, D_C + G_C * N_C, D_XBC)
            dtc = sm_ref[0, :, pl.ds(t0, T), :].reshape(R, LANES)
            lac = sm_ref[1, :, pl.ds(t0, T), :].reshape(R, LANES)

            def store_y(hp, yv):
                ys_ref[:, pl.ds(t0, T), hp * LANES:(hp + 1) * LANES] = yv.reshape(G, T, LANES)

            _ssd_chunk(G, T, xs, Bm, Cm, dtc, lac, sst_ref, masks, emaskN, store_y)

            q = rows(proj_ref, CD_Q, CD_Q + K_D) * scale
            k = rows(proj_ref, CD_K, CD_K + K_D)
            v = rows(proj_ref, CD_V, CD_V + V_D)
            lag = jnp.concatenate([sm_ref[2, :, pl.ds(t0, T), :].reshape(R, LANES),
                                   sm_ref[3, :, pl.ds(t0, T), :].reshape(R, LANES)], axis=1)

            def store_o(h, ov):
                os_ref[:, pl.ds(t0, T), h * DV_D:(h + 1) * DV_D] = ov.reshape(G, T, DV_D)

            _gla_chunk(G, T, q, k, v, lag, gst_ref, masks, emask1, store_o)
            return carry

        lax.fori_loop(0, NC, chunk, 0)
        def post_b(b, carry):
            outs = _cd_post(ys_ref[b], xbc_ref[b, :, :D_C], proj_ref[b, :, CD_Z:CD_Z + D_C],
                            os_ref[b], proj_ref[b, :, CD_G:CD_G + V_D],
                            dskip_ref[...], ssdn_ref[...], glan_ref[...])
            y_ref[b, :, :D_C] = outs[0]
            for h in range(H_D):
                y_ref[b, :, D_C + h * DV_D:D_C + (h + 1) * DV_D] = outs[1 + h]
            return carry

        lax.fori_loop(0, G, post_b, 0)
    else:
        ys_ref, os_ref = scratch[2:]

        def store_y(hp, yv):
            ys_ref[:, hp * LANES:(hp + 1) * LANES] = yv

        def store_o(h, ov):
            os_ref[:, h * DV_D:(h + 1) * DV_D] = ov

        _ssd_chunk(G, T, xbc[:, :D_C], xbc[:, D_C:D_C + G_C * N_C], xbc[:, D_C + G_C * N_C:], dt_all, la_all,
                   sst_ref, masks, emaskN, store_y)
        _gla_chunk(G, T, proj_ref[:, CD_Q:CD_Q + K_D] * scale, proj_ref[:, CD_K:CD_K + K_D],
                   proj_ref[:, CD_V:CD_V + V_D], la_gla, gst_ref, masks, emask1, store_o)
        outs = _cd_post(ys_ref[...], xbc[:, :D_C], proj_ref[:, CD_Z:CD_Z + D_C], os_ref[...],
                        proj_ref[:, CD_G:CD_G + V_D], dskip_ref[...], ssdn_ref[...], glan_ref[...])
        y_ref[:, :D_C] = outs[0]
        for h in range(H_D):
            y_ref[:, D_C + h * DV_D:D_C + (h + 1) * DV_D] = outs[1 + h]

    @pl.when(ti == pl.num_programs(1) - 1)
    def _():
        for h in range(H_C):
            r0 = (h % HG_C) * P_C
            for b in range(G):
                ssdT_ref[b, h] = sst_ref[h // HG_C, r0:r0 + P_C, b * N_C:(b + 1) * N_C]
        for h in range(H_D):
            for j in range(G // 2):
                two = gst_ref[h, :, j * LANES:(j + 1) * LANES].T
                glaT_ref[2 * j, h] = two[:DK_D]
                glaT_ref[2 * j + 1, h] = two[DK_D:]


def _cd(proj, conv_state, ssd0, gla0, cw, cb, dtb, a_neg, dskip, ssd_norm, aup, ab, gla_norm):
    B, L, _ = proj.shape
    G = G_ROWS
    NB = B // G
    small = [cw, cb, dtb, a_neg, dskip, ssd_norm, aup, ab, gla_norm]
    full = lambda s: pl.BlockSpec(s.shape, lambda b, i: (0,) * s.ndim)
    ssd_dims = (G_C, HG_C * P_C, G * N_C)
    ssd0l, gla0l = ssd0, gla0
    ssd_spec = pl.BlockSpec((G, H_C, P_C, N_C), lambda b, i: (b, 0, 0, 0))
    gla_spec = pl.BlockSpec((G, H_D, DK_D, DV_D), lambda b, i: (b, 0, 0, 0))
    st_shapes = [jax.ShapeDtypeStruct(ssd0.shape, F32), jax.ShapeDtypeStruct(gla0.shape, F32)]
    st_scratch = [pltpu.VMEM(ssd_dims, F32), pltpu.VMEM((H_D, DV_D, G * DK_D), F32)]
    DO = D_C + V_D
    if L == 1:
        kern = functools.partial(_cd_kernel, G, 1, 1, False)
        y, ssdT, glaT = pl.pallas_call(
            kern,
            grid=(NB, 1),
            in_specs=[pl.BlockSpec((G, IN_CD_PAD), lambda b, i: (b, 0)),
                      pl.BlockSpec((CONV_W - 1, G, D_XBC), lambda b, i: (0, b, 0)),
                      ssd_spec, gla_spec] + [full(s) for s in small],
            out_specs=[pl.BlockSpec((G, DO), lambda b, i: (b, 0)), ssd_spec, gla_spec],
            out_shape=[jax.ShapeDtypeStruct((B, DO), F32)] + st_shapes,
            scratch_shapes=st_scratch + [pltpu.VMEM((G, D_C), F32), pltpu.VMEM((G, V_D), F32)],
            compiler_params=_cparams(("parallel", "arbitrary")),
        )(proj.reshape(B, IN_CD_PAD), jnp.transpose(conv_state, (1, 0, 2)), ssd0l, gla0l, *small)
        y = y.reshape(B, 1, DO)
    else:
        T = min(T_CHUNK, L)
        TL = min(TL_BLOCK, L)
        NC = TL // T
        kern = functools.partial(_cd_kernel, G, T, NC, True)
        y, ssdT, glaT = pl.pallas_call(
            kern,
            grid=(NB, L // TL),
            in_specs=[pl.BlockSpec((G, TL, IN_CD_PAD), lambda b, i: (b, i, 0)),
                      pl.BlockSpec((G, CONV_W - 1, D_XBC), lambda b, i: (b, 0, 0)),
                      ssd_spec, gla_spec] + [full(s) for s in small],
            out_specs=[pl.BlockSpec((G, TL, DO), lambda b, i: (b, i, 0)), ssd_spec, gla_spec],
            out_shape=[jax.ShapeDtypeStruct((B, L, DO), F32)] + st_shapes,
            scratch_shapes=st_scratch + [pltpu.VMEM((G, SUBLANES + TL, D_XBC), F32),
                                         pltpu.VMEM((G, TL, D_XBC), F32),
                                         pltpu.VMEM((4, G, TL, LANES), F32),
                                         pltpu.VMEM((G, TL, D_C), F32),
                                         pltpu.VMEM((G, TL, V_D), F32)],
            compiler_params=_cparams(("parallel", "arbitrary")),
        )(proj, conv_state, ssd0l, gla0l, *small)
    return y, ssdT, glaT


def _block_diag(w):
    eye = jnp.eye(N_BLK_A, dtype=w.dtype)
    return jnp.einsum('nij,nm->nimj', w, eye).reshape(D_A, D_A)


def _row(v):
    return v.reshape(1, -1).astype(F32)


def _prep_params(p):
    q = {}
    j = 0
    q['w_in_ab'] = p['w_in_ab'][j].astype(BF16)
    q['w_out_ab'] = p['w_out_ab'][j].astype(BF16)
    q['lru'] = [p['lru_conv_w'][j], _row(p['lru_conv_b'][j]),
                _block_diag(p['lru_w_r'][j]).astype(BF16), _row(p['lru_b_r'][j]),
                _block_diag(p['lru_w_i'][j]).astype(BF16), _row(p['lru_b_i'][j]), _row(p['lru_lambda'][j])]
    zw = jnp.zeros((R_W, D_B), F32)
    w2p = jnp.concatenate([p['rwkv_w2'][j], zw], axis=0).astype(BF16)
    a2p = jnp.concatenate([zw, p['rwkv_a2'][j]], axis=0).astype(BF16)
    q['rwkv'] = [_row(p['rwkv_mu'][j]), _row(p['rwkv_w0'][j]), w2p, _row(p['rwkv_a0'][j]), a2p,
                 p['rwkv_g2'][j].astype(BF16), _row(p['rwkv_k_k'][j]), _row(p['rwkv_k_a'][j]),
                 _row(p['rwkv_r_k'][j]), _row(p['rwkv_ln_w'][j]), _row(p['rwkv_ln_b'][j])]
    q['ffn'] = [p['ffn_w_gate'][j].astype(BF16), p['ffn_w_up'][j].astype(BF16), p['ffn_w_down'][j].astype(BF16)]
    w = p['w_in_cd'][j]
    o_z, o_xbc, o_dt = 0, D_C, D_C + D_XBC
    o_q = o_dt + H_C
    o_k = o_q + K_D
    o_v = o_k + K_D
    o_g = o_v + V_D
    o_al = o_g + V_D
    pad = jnp.zeros((D_MODEL, LANES - H_C - R_ALPHA), F32)
    q['w_in_cd'] = jnp.concatenate([w[:, o_z:o_dt], w[:, o_q:o_al], w[:, o_dt:o_q], w[:, o_al:], pad],
                                   axis=1).astype(BF16)
    q['w_out_cd'] = p['w_out_cd'][j].astype(BF16)
    lane_pad = lambda v: jnp.concatenate([v.astype(F32), jnp.zeros((LANES - v.shape[0],), F32)]).reshape(1, LANES)
    aup = jnp.zeros((LANES, K_D), F32).at[H_C:H_C + R_ALPHA].set(p['gla_alpha_up'][j]).astype(BF16)
    q['cd'] = [p['ssd_conv_w'][j], _row(p['ssd_conv_b'][j]), lane_pad(p['ssd_dt_bias'][j]),
               lane_pad(-jnp.exp(p['ssd_a_log'][j].astype(F32))), _row(jnp.repeat(p['ssd_d'][j], P_C)),
               _row(p['ssd_norm_w'][j]), aup, _row(p['gla_alpha_b'][j]), _row(p['gla_norm_w'][j])]
    q['router'] = jnp.concatenate([p['moe_router'][j], jnp.zeros((D_MODEL, LANES - N_EXP), F32)], axis=1)
    q['moe'] = [p['moe_w_gate'][j].astype(BF16), p['moe_w_up'][j].astype(BF16), p['moe_w_down'][j].astype(BF16)]
    q['norm_mix'] = [_row(p['norm_mix'][i]) for i in range(DEPTH)]
    q['norm_ffn'] = [_row(p['norm_ffn'][i]) for i in range(DEPTH)]
    q['norm_ple'] = [_row(p['norm_ple'][i]) for i in range(DEPTH)]
    q['w_ple_gate'] = [p['w_ple_gate'][i].astype(BF16) for i in range(DEPTH)]
    q['w_ple_proj'] = [p['w_ple_proj'][i].astype(BF16) for i in range(DEPTH)]
    q['norm_final'] = _row(p['norm_final'])
    return q


def _trunk(x, p, states, q):
    lru_conv, lru_h, rwkv_shift, rwkv_wkv, ssd_conv, ssd_st, gla_st = states
    B, L, D = x.shape
    M = B * L
    h = x.reshape(M, D)
    p2 = p.reshape(DEPTH, M, D_PLE)

    proj = _norm_mm(h, q['norm_mix'][0], q['w_in_ab'], IN_AB // 2).reshape(B, L, IN_AB)
    y_a, h_new = _lru(proj, lru_conv[0], lru_h[0], *q['lru'])
    y_b, wkv_new = _rwkv(proj, rwkv_shift[0], rwkv_wkv[0], *q['rwkv'])
    if L >= CONV_W - 1:
        conv_new = proj[:, L - (CONV_W - 1):, :D_A]
    else:
        conv_new = jnp.concatenate([lru_conv[0][:, L:], proj[:, :, :D_A]], axis=1)
    shift_new = proj[:, L - 1, 2 * D_A:]
    w_out = q['w_out_ab']
    h = _out_proj(h, [y_a.reshape(M, D_A), y_b.reshape(M, D_B)], [w_out[:D_A], w_out[D_A:]])
    h = _ffn(h, q['norm_ffn'][0], *q['ffn'], D_FF // 2)
    h = _ple(h, q['norm_ple'][0], q['w_ple_gate'][0], p2[0], q['w_ple_proj'][0], q['norm_final'], False)

    proj = _norm_mm(h, q['norm_mix'][1], q['w_in_cd'], IN_CD_PAD // 5).reshape(B, L, IN_CD_PAD)
    y_cd, ssd_new, gla_new = _cd(proj, ssd_conv[0], ssd_st[0], gla_st[0], *q['cd'])
    if L >= CONV_W - 1:
        sconv_new = proj[:, L - (CONV_W - 1):, CD_XBC:CD_XBC + D_XBC]
    else:
        sconv_new = jnp.concatenate([ssd_conv[0][:, L:], proj[:, :, CD_XBC:CD_XBC + D_XBC]], axis=1)
    h = _out_proj(h, [y_cd.reshape(M, D_C + V_D)], [q['w_out_cd']])
    route, routeT, meta = _router(h, q['norm_ffn'][1], q['router'])
    h = _moe(h, q['norm_ffn'][1], route, routeT, meta, *q['moe'])
    y = _ple(h, q['norm_ple'][1], q['w_ple_gate'][1], p2[1], q['w_ple_proj'][1], q['norm_final'], True)

    new_states = (conv_new[None], h_new[None], shift_new[None], wkv_new[None],
                  sconv_new[None], ssd_new[None], gla_new[None])
    return y.reshape(B, L, D), new_states


def kernel(x_prompt, x_sample, p_prompt, p_sample, state_lru_conv, state_lru_h, state_rwkv_shift,
           state_rwkv_wkv, state_ssd_conv, state_ssd, state_gla, norm_mix, norm_ffn, norm_ple,
           w_ple_gate, w_ple_proj, norm_final, w_in_ab, w_out_ab, lru_conv_w, lru_conv_b, lru_w_r,
           lru_b_r, lru_w_i, lru_b_i, lru_lambda, rwkv_mu, rwkv_w0, rwkv_w2, rwkv_a0, rwkv_a2,
           rwkv_g2, rwkv_k_k, rwkv_k_a, rwkv_r_k, rwkv_ln_w, rwkv_ln_b, ffn_w_gate, ffn_w_up,
           ffn_w_down, w_in_cd, w_out_cd, ssd_conv_w, ssd_conv_b, ssd_dt_bias, ssd_a_log, ssd_d,
           ssd_norm_w, gla_alpha_up, gla_alpha_b, gla_norm_w, moe_router, moe_w_gate, moe_w_up,
           moe_w_down):
    prm = dict(
        norm_mix=norm_mix, norm_ffn=norm_ffn, norm_ple=norm_ple, w_ple_gate=w_ple_gate,
        w_ple_proj=w_ple_proj, norm_final=norm_final, w_in_ab=w_in_ab, w_out_ab=w_out_ab,
        lru_conv_w=lru_conv_w, lru_conv_b=lru_conv_b, lru_w_r=lru_w_r, lru_b_r=lru_b_r,
        lru_w_i=lru_w_i, lru_b_i=lru_b_i, lru_lambda=lru_lambda, rwkv_mu=rwkv_mu, rwkv_w0=rwkv_w0,
        rwkv_w2=rwkv_w2, rwkv_a0=rwkv_a0, rwkv_a2=rwkv_a2, rwkv_g2=rwkv_g2, rwkv_k_k=rwkv_k_k,
        rwkv_k_a=rwkv_k_a, rwkv_r_k=rwkv_r_k, rwkv_ln_w=rwkv_ln_w, rwkv_ln_b=rwkv_ln_b,
        ffn_w_gate=ffn_w_gate, ffn_w_up=ffn_w_up, ffn_w_down=ffn_w_down, w_in_cd=w_in_cd,
        w_out_cd=w_out_cd, ssd_conv_w=ssd_conv_w, ssd_conv_b=ssd_conv_b, ssd_dt_bias=ssd_dt_bias,
        ssd_a_log=ssd_a_log, ssd_d=ssd_d, ssd_norm_w=ssd_norm_w, gla_alpha_up=gla_alpha_up,
        gla_alpha_b=gla_alpha_b, gla_norm_w=gla_norm_w, moe_router=moe_router,
        moe_w_gate=moe_w_gate, moe_w_up=moe_w_up, moe_w_down=moe_w_down)
    q = _prep_params(prm)
    states_s = (state_lru_conv, state_lru_h, state_rwkv_shift, state_rwkv_wkv,
                state_ssd_conv, state_ssd, state_gla)
    n_prompt = x_prompt.shape[0]
    states_p = tuple(jnp.zeros((s.shape[0], n_prompt) + s.shape[2:], s.dtype) for s in states_s)
    y_p, st_p = _trunk(x_prompt, p_prompt, states_p, q)
    y_s, st_s = _trunk(x_sample, p_sample, states_s, q)
    return (y_p, y_s) + tuple(st_p) + tuple(st_s)
```

```python
import functools
import math

import jax
import jax.numpy as jnp
from jax import lax
from jax.experimental import pallas as pl
from jax.experimental.pallas import tpu as pltpu

F32, BF16 = jnp.float32, jnp.bfloat16

D_MODEL = 1024
DEPTH = 2
D_PLE = 256
EPS = 1e-6
CONV_W = 4
D_A = 512
N_BLK_A = 8
BW_A = D_A // N_BLK_A
LRU_C = 8.0
H_B = 8
HD_B = 64
D_B = H_B * HD_B
R_W = 64
R_A = 64
R_G = 128
D_BPROJ = 3 * D_B + R_W + R_A + R_G
GN_EPS_B = 64e-5
IN_AB = 2 * D_A + D_BPROJ
H_C = 8
P_C = 64
D_C = H_C * P_C
G_C = 2
HG_C = H_C // G_C
N_C = 128
D_XBC = D_C + 2 * G_C * N_C
H_D = 4
DK_D = 64
DV_D = 128
K_D = H_D * DK_D
V_D = H_D * DV_D
R_ALPHA = 16
GATE_NORM = 16.0
D_FF = 2816
N_EXP = 8
D_FF_E = 1408
IN_CD_PAD = D_C + D_XBC + 2 * K_D + 2 * V_D + 128

LANES = 128
SUBLANES = 8
VMEM_LIMIT = 56 * 1024 * 1024

G_ROWS = SUBLANES
T_CHUNK = 16
TL_BLOCK = 64
TM_TOK = 1024
TM_FFN = 512
TM_MOE = 1024
MOE_RB = 128
MOE_SORT_ROWS = 256
MOE_UNSORT_COLS = 512
TOP_K = 2
HEAD_SUM_PIECES = 2
CONV_COLS = 256
LRU_TL = 256

NN = ((1,), (0,))
NT = ((1,), (1,))
TN = ((0,), (0,))


def _cparams(sem):
    return pltpu.CompilerParams(dimension_semantics=sem, vmem_limit_bytes=VMEM_LIMIT)


def _rms(x, g):
    return x * lax.rsqrt(jnp.mean(x * x, axis=-1, keepdims=True) + EPS) * g


def _sigmoid(x):
    return jax.nn.sigmoid(x)


def _silu(x):
    return x * jax.nn.sigmoid(x)


def _softplus(x):
    return jnp.maximum(x, 0.0) + jnp.log1p(jnp.exp(-jnp.abs(x)))


def _gelu_tanh(x):
    c = math.sqrt(2.0 / math.pi)
    return 0.5 * x * (1.0 + jnp.tanh(c * (x + 0.044715 * (x * x * x))))


def _d(a, b, dims=NN):
    return lax.dot_general(a, b, (dims, ((), ())), preferred_element_type=F32)


def _mm(a, b, dims=NN):
    return _d(a.astype(BF16), b.astype(BF16), dims)


def _split2(x):
    hi = x.astype(BF16)
    lo = (x - hi.astype(F32)).astype(BF16)
    return hi, lo


def _mm3(a, b, dims=NN):
    ah, al = _split2(a)
    bh, bl = _split2(b)
    return _d(ah, bh, dims) + (_d(ah, bl, dims) + _d(al, bh, dims))


def _mm_sel(sel, x, pieces, sel_left=True, dims=NN):
    out = None
    rest = x
    for i in range(pieces):
        p = rest.astype(BF16)
        if i + 1 < pieces:
            rest = rest - p.astype(F32)
        t = _d(sel, p, dims) if sel_left else _d(p, sel, dims)
        out = t if out is None else out + t
    return out


def _chunk_masks(G, T):
    R = G * T
    ri = lax.broadcasted_iota(jnp.int32, (R, R), 0)
    ci = lax.broadcasted_iota(jnp.int32, (R, R), 1)
    same = (ri // T) == (ci // T)
    incl = same & (ci <= ri)
    strict = same & (ci < ri)
    last = ci == (ri // T) * T + (T - 1)
    return incl, strict, last


def _to_bf16_mask(m):
    return jnp.where(m, 1.0, 0.0).astype(BF16)


def _tile_sel(W, G):
    r = lax.broadcasted_iota(jnp.int32, (W, G * W), 0)
    c = lax.broadcasted_iota(jnp.int32, (W, G * W), 1)
    return _to_bf16_mask((c % W) == r)


def _batch_lane_mask(n_stack, G, T, W):
    R = G * T
    r = lax.broadcasted_iota(jnp.int32, (n_stack * R, G * W), 0)
    c = lax.broadcasted_iota(jnp.int32, (n_stack * R, G * W), 1)
    return ((r % R) // T) == (c // W)


def _head_sum_sel(width, head):
    r = lax.broadcasted_iota(jnp.int32, (width, width), 0)
    c = lax.broadcasted_iota(jnp.int32, (width, width), 1)
    return _to_bf16_mask((r // head) == (c // head))


def _norm_mm_kernel(x_ref, g_ref, w_ref, o_ref, xn_ref):
    @pl.when(pl.program_id(1) == 0)
    def _():
        xn_ref[...] = _rms(x_ref[...], g_ref[...]).astype(BF16)

    o_ref[...] = _d(xn_ref[...], w_ref[...]).astype(o_ref.dtype)


def _norm_mm(x, g, w, tn, out_dtype):
    M, K = x.shape
    N = w.shape[1]
    tm = min(TM_TOK, M)
    return pl.pallas_call(
        _norm_mm_kernel,
        grid=(M // tm, N // tn),
        in_specs=[pl.BlockSpec((tm, K), lambda i, j: (i, 0)),
                  pl.BlockSpec((1, K), lambda i, j: (0, 0)),
                  pl.BlockSpec((K, tn), lambda i, j: (0, j))],
        out_specs=pl.BlockSpec((tm, tn), lambda i, j: (i, j)),
        out_shape=jax.ShapeDtypeStruct((M, N), out_dtype),
        scratch_shapes=[pltpu.VMEM((tm, K), BF16)],
        compiler_params=_cparams(("parallel", "arbitrary")),
    )(x, g, w)


def _out_proj_kernel(n_in, *refs):
    h_ref = refs[0]
    ys = refs[1:1 + n_in]
    ws = refs[1 + n_in:1 + 2 * n_in]
    o_ref = refs[1 + 2 * n_in]
    acc = h_ref[...]
    for y_ref, w_ref in zip(ys, ws):
        acc = acc + _mm(y_ref[...], w_ref[...])
    o_ref[...] = acc


def _out_proj(h, ys, ws):
    M, D = h.shape
    tm = min(TM_TOK, M)
    n_in = len(ys)
    in_specs = [pl.BlockSpec((tm, D), lambda i: (i, 0))]
    in_specs += [pl.BlockSpec((tm, y.shape[1]), lambda i: (i, 0)) for y in ys]
    in_specs += [pl.BlockSpec(w.shape, lambda i: (0, 0)) for w in ws]
    return pl.pallas_call(
        functools.partial(_out_proj_kernel, n_in),
        grid=(M // tm,),
        in_specs=in_specs,
        out_specs=pl.BlockSpec((tm, D), lambda i: (i, 0)),
        out_shape=jax.ShapeDtypeStruct((M, D), F32),
        compiler_params=_cparams(("parallel",)),
    )(h, *ys, *ws)


def _ffn_kernel(h_ref, g_ref, wg_ref, wu_ref, wd_ref, o_ref, xn_ref, acc_ref):
    j = pl.program_id(1)

    @pl.when(j == 0)
    def _():
        xn_ref[...] = _rms(h_ref[...], g_ref[...]).astype(BF16)
        acc_ref[...] = jnp.zeros_like(acc_ref)

    xn = xn_ref[...]
    act = _silu(_d(xn, wg_ref[...])) * _d(xn, wu_ref[...])
    acc_ref[...] += _mm(act, wd_ref[...])

    @pl.when(j == pl.num_programs(1) - 1)
    def _():
        o_ref[...] = h_ref[...] + acc_ref[...]


def _ffn(h, g, wg, wu, wd, tf):
    M, D = h.shape
    FF = wg.shape[1]
    tm = min(TM_FFN, M)
    return pl.pallas_call(
        _ffn_kernel,
        grid=(M // tm, FF // tf),
        in_specs=[pl.BlockSpec((tm, D), lambda i, j: (i, 0)),
                  pl.BlockSpec((1, D), lambda i, j: (0, 0)),
                  pl.BlockSpec((D, tf), lambda i, j: (0, j)),
                  pl.BlockSpec((D, tf), lambda i, j: (0, j)),
                  pl.BlockSpec((tf, D), lambda i, j: (j, 0))],
        out_specs=pl.BlockSpec((tm, D), lambda i, j: (i, 0)),
        out_shape=jax.ShapeDtypeStruct((M, D), F32),
        scratch_shapes=[pltpu.VMEM((tm, D), BF16), pltpu.VMEM((tm, D), F32)],
        compiler_params=_cparams(("parallel", "arbitrary")),
    )(h, g, wg, wu, wd)


def _router_kernel(h_ref, g_ref, wr_ref, route_ref, routeT_ref, meta_ref):
    xn = _rms(h_ref[...], g_ref[...])
    logits = _mm3(xn, wr_ref[...])
    lane = lax.broadcasted_iota(jnp.int32, logits.shape, 1)
    valid = lane < N_EXP
    logits = jnp.where(valid, logits, -1e30)
    m = jnp.max(logits, axis=-1, keepdims=True)
    e = jnp.where(valid, jnp.exp(logits - m), 0.0)
    p = e / jnp.sum(e, axis=-1, keepdims=True)
    lane_f = lane.astype(F32)
    m1 = jnp.max(p, axis=-1, keepdims=True)
    i1 = jnp.min(jnp.where(p == m1, lane_f, float(LANES)), axis=-1, keepdims=True)
    p2 = jnp.where((lane_f == i1) | (lane >= N_EXP), -1.0, p)
    m2 = jnp.max(p2, axis=-1, keepdims=True)
    i2 = jnp.min(jnp.where(p2 == m2, lane_f, float(LANES)), axis=-1, keepdims=True)
    tot = m1 + m2
    tm = p.shape[0]
    sel1 = lane_f == i1
    sel2 = lane_f == i2
    assign = jnp.where(sel1 | sel2, 1.0, 0.0)
    ri = lax.broadcasted_iota(jnp.int32, (tm, tm), 0)
    ci = lax.broadcasted_iota(jnp.int32, (tm, tm), 1)
    rank = _d(_to_bf16_mask(ci < ri), assign.astype(BF16))
    cnt = jnp.sum(assign, axis=0, keepdims=True)
    padded = jnp.floor((cnt + (MOE_RB - 1)) * (1.0 / MOE_RB)) * MOE_RB
    er = lax.broadcasted_iota(jnp.int32, (LANES, LANES), 0)
    ec = lax.broadcasted_iota(jnp.int32, (LANES, LANES), 1)
    off = _mm_sel(_to_bf16_mask(er < ec), jnp.broadcast_to(padded, (SUBLANES, LANES)), 2, sel_left=False)[0:1]
    pos = off + rank
    dest1 = jnp.sum(jnp.where(sel1, pos, 0.0), axis=-1, keepdims=True)
    dest2 = jnp.sum(jnp.where(sel2, pos, 0.0), axis=-1, keepdims=True)
    route = jnp.where(lane == 0, dest1, jnp.where(lane == 1, dest2,
                      jnp.where(lane == 2, m1 / tot, jnp.where(lane == 3, m2 / tot, 0.0))))
    route_ref[...] = route
    routeT_ref[...] = route.T[:SUBLANES]
    mrow = lax.broadcasted_iota(jnp.int32, (SUBLANES, LANES), 0)
    meta_ref[...] = jnp.where(mrow == 0, padded * (1.0 / MOE_RB), jnp.where(mrow == 1, off * (1.0 / MOE_RB), 0.0))


def _router(h, g, wr):
    M, D = h.shape
    tm = min(TM_MOE, M)
    nt = M // tm
    return pl.pallas_call(
        _router_kernel,
        grid=(nt,),
        in_specs=[pl.BlockSpec((tm, D), lambda i: (i, 0)),
                  pl.BlockSpec((1, D), lambda i: (0, 0)),
                  pl.BlockSpec((D, LANES), lambda i: (0, 0))],
        out_specs=[pl.BlockSpec((tm, LANES), lambda i: (i, 0)),
                   pl.BlockSpec((None, SUBLANES, tm), lambda i: (i, 0, 0)),
                   pl.BlockSpec((None, SUBLANES, LANES), lambda i: (i, 0, 0))],
        out_shape=[jax.ShapeDtypeStruct((M, LANES), F32),
                   jax.ShapeDtypeStruct((nt, SUBLANES, tm), F32),
                   jax.ShapeDtypeStruct((nt, SUBLANES, LANES), F32)],
        compiler_params=_cparams(("parallel",)),
    )(h, g, wr)


def _moe_kernel(nblk_ref, offb_ref, h_ref, g_ref, route_ref, routeT_ref, wg_ref, wu_ref, wd_ref, o_ref,
                xs_ref, ys_ref, gs_ref, xn_ref):
    i = pl.program_id(0)
    e = pl.program_id(1)
    tm = h_ref.shape[0]
    cap = xs_ref.shape[0]

    last_e = pl.num_programs(1) - 1
    used_rows = (offb_ref[i, last_e] + nblk_ref[i, last_e]) * MOE_RB

    @pl.when(e == 0)
    def _():
        xn_ref[...] = _rms(h_ref[...], g_ref[...]).astype(BF16)
        ys_ref[...] = jnp.zeros_like(ys_ref)
        row0 = lax.broadcasted_iota(jnp.int32, (MOE_SORT_ROWS, tm), 0).astype(F32)

        def sort_block(lo):
            d1, d2 = routeT_ref[0:1, :], routeT_ref[1:2, :]
            g1, g2 = routeT_ref[2:3, :], routeT_ref[3:4, :]
            rows = row0 + float(lo)
            m1 = rows == d1
            m2 = rows == d2
            xs_ref[lo:lo + MOE_SORT_ROWS, :] = _d(_to_bf16_mask(m1 | m2), xn_ref[...]).astype(BF16)
            gate = jnp.sum(jnp.where(m1, g1, 0.0) + jnp.where(m2, g2, 0.0), axis=-1, keepdims=True)
            gs_ref[lo:lo + MOE_SORT_ROWS, :] = jnp.broadcast_to(gate, (MOE_SORT_ROWS, LANES))

        for lo in range(0, cap, MOE_SORT_ROWS):
            if lo < TOP_K * tm:
                sort_block(lo)
            else:
                pl.when(lo < used_rows)(functools.partial(sort_block, lo))

    base = offb_ref[i, e]
    nb = nblk_ref[i, e]

    def expert_rows(blk, rows):
        r0 = pl.multiple_of(blk * MOE_RB, MOE_RB)
        x = xs_ref[pl.ds(r0, rows), :]
        act = _silu(_d(x, wg_ref[...])) * _d(x, wu_ref[...])
        act = act * gs_ref[pl.ds(r0, rows), 0:1]
        ys_ref[pl.ds(r0, rows), :] = _mm(act, wd_ref[...]).astype(BF16)

    def two_blocks(j, carry):
        expert_rows(base + 2 * j, 2 * MOE_RB)
        return carry

    lax.fori_loop(0, nb // 2, two_blocks, 0)

    @pl.when(nb % 2 == 1)
    def _():
        expert_rows(base + nb - 1, MOE_RB)

    @pl.when(e == last_e)
    def _():
        col0 = lax.broadcasted_iota(jnp.int32, (tm, MOE_UNSORT_COLS), 1).astype(F32)

        def unsort_block(lo):
            d1, d2 = route_ref[:, 0:1], route_ref[:, 1:2]
            cols = col0 + float(lo)
            return _d(_to_bf16_mask((cols == d1) | (cols == d2)), ys_ref[lo:lo + MOE_UNSORT_COLS, :])

        acc = h_ref[...]
        for lo in range(0, TOP_K * tm, MOE_UNSORT_COLS):
            acc = acc + unsort_block(lo)
        o_ref[...] = acc
        for lo in range(-(-TOP_K * tm // MOE_UNSORT_COLS) * MOE_UNSORT_COLS, cap, MOE_UNSORT_COLS):
            @pl.when(lo < used_rows)
            def _():
                o_ref[...] += unsort_block(lo)


def _moe(h, g, route, routeT, meta, wg, wu, wd):
    M, D = h.shape
    E, _, FF = wg.shape
    tm = min(TM_MOE, M)
    nt = M // tm
    cap = TOP_K * tm + E * MOE_RB
    cap = -(-cap // MOE_UNSORT_COLS) * MOE_UNSORT_COLS
    meta_i = meta.astype(jnp.int32)
    grid_spec = pltpu.PrefetchScalarGridSpec(
        num_scalar_prefetch=2,
        grid=(nt, E),
        in_specs=[pl.BlockSpec((tm, D), lambda i, e, nb, ob: (i, 0), pipeline_mode=pl.Buffered(1)),
                  pl.BlockSpec((1, D), lambda i, e, nb, ob: (0, 0)),
                  pl.BlockSpec((tm, LANES), lambda i, e, nb, ob: (i, 0)),
                  pl.BlockSpec((None, SUBLANES, tm), lambda i, e, nb, ob: (i, 0, 0)),
                  pl.BlockSpec((None, D, FF), lambda i, e, nb, ob: (e, 0, 0)),
                  pl.BlockSpec((None, D, FF), lambda i, e, nb, ob: (e, 0, 0)),
                  pl.BlockSpec((None, FF, D), lambda i, e, nb, ob: (e, 0, 0))],
        out_specs=pl.BlockSpec((tm, D), lambda i, e, nb, ob: (i, 0)),
        scratch_shapes=[pltpu.VMEM((cap, D), BF16), pltpu.VMEM((cap, D), BF16), pltpu.VMEM((cap, LANES), F32),
                        pltpu.VMEM((tm, D), BF16)])
    return pl.pallas_call(
        _moe_kernel,
        grid_spec=grid_spec,
        out_shape=jax.ShapeDtypeStruct((M, D), F32),
        compiler_params=_cparams(("parallel", "arbitrary")),
    )(meta_i[:, 0, :], meta_i[:, 1, :], h, g, route, routeT, wg, wu, wd)


def _ple_kernel(final, h_ref, g_ref, wg_ref, p_ref, wp_ref, gf_ref, o_ref):
    h = h_ref[...]
    gate = _sigmoid(_mm(_rms(h, g_ref[...]), wg_ref[...]))
    out = h + gate * _mm(p_ref[...], wp_ref[...])
    if final:
        out = _rms(out, gf_ref[...])
    o_ref[...] = out


def _ple(h, g, wg, p, wp, gf, final):
    M, D = h.shape
    tm = min(TM_TOK, M)
    return pl.pallas_call(
        functools.partial(_ple_kernel, final),
        grid=(M // tm,),
        in_specs=[pl.BlockSpec((tm, D), lambda i: (i, 0)),
                  pl.BlockSpec((1, D), lambda i: (0, 0)),
                  pl.BlockSpec((D, D), lambda i: (0, 0)),
                  pl.BlockSpec((tm, D_PLE), lambda i: (i, 0)),
                  pl.BlockSpec((D_PLE, D), lambda i: (0, 0)),
                  pl.BlockSpec((1, D), lambda i: (0, 0))],
        out_specs=pl.BlockSpec((tm, D), lambda i: (i, 0)),
        out_shape=jax.ShapeDtypeStruct((M, D), F32),
        compiler_params=_cparams(("parallel",)),
    )(h, g, wg, p, wp, gf)


def _lru_gates(xc, ga_unused, wr, br, wi, bi, lam):
    r = _sigmoid(_mm(xc, wr) + br)
    i = _sigmoid(_mm(xc, wi) + bi)
    log_a = -LRU_C * r * _softplus(-lam)
    a = jnp.exp(log_a)
    u = jnp.sqrt(jnp.tanh(-log_a) * (a * a + 1.0)) * (i * xc)
    return a, u


def _lru_seq_kernel(TL, proj_ref, cs_ref, h0_ref, cw_ref, cb_ref, wr_ref, br_ref, wi_ref, bi_ref, lam_ref,
                    y_ref, hT_ref, pad_ref, a_ref, u_ref, hs_ref, hc_ref):
    ti = pl.program_id(0)
    G = pad_ref.shape[0]

    @pl.when(ti == 0)
    def _():
        pad_ref[:, SUBLANES - (CONV_W - 1):SUBLANES, :] = cs_ref[...]
        hc_ref[...] = h0_ref[...]

    pad_ref[:, SUBLANES:, :] = proj_ref[:, :, :D_A].astype(F32)
    cw = cw_ref[...]
    xc = cb_ref[...][None]
    for k in range(CONV_W):
        off = SUBLANES - (CONV_W - 1) + k
        xc = xc + pad_ref[:, off:off + TL, :] * cw[k:k + 1, :][None]
    pad_ref[:, :SUBLANES, :] = pad_ref[:, TL:TL + SUBLANES, :]

    a, u = _lru_gates(xc.reshape(G * TL, D_A), None, wr_ref[...], br_ref[...], wi_ref[...], bi_ref[...],
                      lam_ref[...])
    a_ref[...] = a.reshape(G, TL, D_A)
    u_ref[...] = u.reshape(G, TL, D_A)

    def step(t, h):
        h = a_ref[:, pl.ds(t, 1), :] * h + u_ref[:, pl.ds(t, 1), :]
        hs_ref[:, pl.ds(t, 1), :] = h
        return h

    hc_ref[...] = lax.fori_loop(0, TL, step, hc_ref[...], unroll=8)
    y_ref[...] = (_gelu_tanh(proj_ref[:, :, D_A:].astype(F32)) * hs_ref[...]).astype(y_ref.dtype)

    @pl.when(ti == pl.num_programs(0) - 1)
    def _():
        hT_ref[...] = hc_ref[...]


def _lru_step_kernel(xa_ref, ga_ref, cs_ref, h0_ref, cw_ref, cb_ref, wr_ref, br_ref, wi_ref, bi_ref, lam_ref,
                     y_ref, hT_ref):
    cw = cw_ref[...]
    x = xa_ref[...]
    xc = cb_ref[...] + x * cw[CONV_W - 1:CONV_W, :]
    for k in range(CONV_W - 1):
        xc = xc + cs_ref[k] * cw[k:k + 1, :]
    a, u = _lru_gates(xc, None, wr_ref[...], br_ref[...], wi_ref[...], bi_ref[...], lam_ref[...])
    h = a * h0_ref[...] + u
    hT_ref[...] = h
    y_ref[...] = _gelu_tanh(ga_ref[...]) * h


def _lru(proj, conv_state, h0, cw, cb, wr, br, wi, bi, lam):
    B, L, _ = proj.shape
    small = [cw, cb, wr, br, wi, bi, lam]
    if L == 1:
        proj2 = proj.reshape(B, -1)
        cs = jnp.transpose(conv_state, (1, 0, 2))
        tb = min(B, 128)
        full = lambda s: pl.BlockSpec(s.shape, lambda i: (0,) * s.ndim)
        y, hT = pl.pallas_call(
            _lru_step_kernel,
            grid=(B // tb,),
            in_specs=[pl.BlockSpec((tb, D_A), lambda i: (i, 0)),
                      pl.BlockSpec((tb, D_A), lambda i: (i, 1)),
                      pl.BlockSpec((CONV_W - 1, tb, D_A), lambda i: (0, i, 0)),
                      pl.BlockSpec((tb, D_A), lambda i: (i, 0))] + [full(s) for s in small],
            out_specs=[pl.BlockSpec((tb, D_A), lambda i: (i, 0)), pl.BlockSpec((tb, D_A), lambda i: (i, 0))],
            out_shape=[jax.ShapeDtypeStruct((B, D_A), F32), jax.ShapeDtypeStruct((B, D_A), F32)],
            compiler_params=_cparams(("parallel",)),
        )(proj2, proj2, cs, h0, *small)
        return y.reshape(B, 1, D_A), hT
    TL = min(LRU_TL, L)
    G = B
    full = lambda s: pl.BlockSpec(s.shape, lambda i: (0,) * s.ndim)
    y, hT = pl.pallas_call(
        functools.partial(_lru_seq_kernel, TL),
        grid=(L // TL,),
        in_specs=[pl.BlockSpec((G, TL, 2 * D_A), lambda i: (0, i, 0)),
                  pl.BlockSpec((G, CONV_W - 1, D_A), lambda i: (0, 0, 0)),
                  pl.BlockSpec((G, 1, D_A), lambda i: (0, 0, 0))] + [full(s) for s in small],
        out_specs=[pl.BlockSpec((G, TL, D_A), lambda i: (0, i, 0)),
                   pl.BlockSpec((G, 1, D_A), lambda i: (0, 0, 0))],
        out_shape=[jax.ShapeDtypeStruct((B, L, D_A), BF16), jax.ShapeDtypeStruct((B, 1, D_A), F32)],
        scratch_shapes=[pltpu.VMEM((G, SUBLANES + TL, D_A), F32),
                        pltpu.VMEM((G, TL, D_A), F32), pltpu.VMEM((G, TL, D_A), F32),
                        pltpu.VMEM((G, TL, D_A), F32), pltpu.VMEM((G, 1, D_A), F32)],
        compiler_params=_cparams(("arbitrary",)),
    )(proj, conv_state, h0.reshape(B, 1, D_A), *small)
    return y, hT.reshape(B, D_A)


def _rwkv_prep(xs, w0, w2p, a0, a2p, g2, k_k, k_a, r_k, hsel):
    r = xs[:, 0:D_B]
    k = xs[:, D_B:2 * D_B]
    v = xs[:, 2 * D_B:3 * D_B]
    wa = xs[:, 3 * D_B:3 * D_B + R_W + R_A]
    gl = xs[:, 3 * D_B + R_W + R_A:]
    w_log = -_softplus(-(w0 + _mm(jnp.tanh(wa), w2p))) - 0.5
    logd = -jnp.exp(w_log)
    a = _sigmoid(a0 + _mm(wa, a2p))
    g = _mm(_sigmoid(gl), g2)
    kk = k * k_k
    kk = kk * lax.rsqrt(jnp.maximum(_mm_sel(hsel, kk * kk, HEAD_SUM_PIECES, sel_left=False), 1e-12))
    k2 = k * (1.0 + (a - 1.0) * k_a)
    beta = kk * a
    bonus = _mm_sel(hsel, r * k2 * r_k, HEAD_SUM_PIECES, sel_left=False) * v
    return r, k2, v, logd, kk, beta, g, bonus


def _rwkv_post(y, bonus, g, ln_w, ln_b, hsel):
    mean = _mm_sel(hsel, y, HEAD_SUM_PIECES, sel_left=False) * (1.0 / HD_B)
    d = y - mean
    var = _mm_sel(hsel, d * d, HEAD_SUM_PIECES, sel_left=False) * (1.0 / HD_B)
    yn = d * lax.rsqrt(var + GN_EPS_B) * ln_w + ln_b
    return (yn + bonus) * g


def _mmp(a, b, dims, passes):
    return _mm3(a, b, dims) if passes == 3 else _mm(a, b, dims)


RWKV_PASSES = dict(am=1, state=1, inv=1, apply=1, dw=1)


def _rwkv_chunk(G, T, ops, st_ref, masks, emask, store_y):
    r, k2, v, logd, kk, beta = ops
    incl, strict, lmask_bf, lastmask_bf = masks
    R = G * T
    ps = RWKV_PASSES
    cum = _mm_sel(lmask_bf, logd, 3)
    cend = _mm_sel(lastmask_bf, cum, 3)
    ecn = jnp.exp(-cum)
    P = jnp.exp(cum - logd) * kk
    Q = ecn * beta
    Kt = ecn * k2
    Rt = jnp.exp(cum) * r
    eend = jnp.exp(cend)
    lo = lax.broadcasted_iota(jnp.int32, (R, LANES), 1) < HD_B
    row_lo = lax.broadcasted_iota(jnp.int32, (LANES, G * HD_B), 0) < HD_B
    eye = None
    if T > 1:
        ri = lax.broadcasted_iota(jnp.int32, (R, R), 0)
        ci = lax.broadcasted_iota(jnp.int32, (R, R), 1)
        eye = jnp.where(ri == ci, 1.0, 0.0)
    n_rep = G * HD_B // LANES
    NP = H_B // 2
    heads = [(c, hh) for c in range(NP) for hh in range(2)]
    own = {0: lo, 1: ~lo}

    def tiled(x, xr, hh):
        t = jnp.where(lo, x, xr) if hh == 0 else jnp.where(lo, xr, x)
        return jnp.where(emask, jnp.concatenate([t] * n_rep, axis=1), 0.0)

    pair = []
    for c in range(NP):
        sl = slice(c * LANES, (c + 1) * LANES)
        vals = (P[:, sl], Rt[:, sl], Q[:, sl], Kt[:, sl], eend[:, sl])
        pair.append(dict(x=vals, xr=[pltpu.roll(x, HD_B, 1) for x in vals], V=v[:, sl]))
    AM = [_mmp(jnp.concatenate([jnp.where(own[hh], pair[c]['x'][0], 0.0),
                                jnp.where(own[hh], pair[c]['x'][1], 0.0)], axis=0),
               jnp.concatenate([pair[c]['x'][2], pair[c]['x'][3]], axis=0), NT, ps['am']) for c, hh in heads]
    Mqr = [jnp.where(incl, a[R:, :R], 0.0) for a in AM]
    Mkr = [jnp.where(incl, a[R:, R:], 0.0) for a in AM]
    MV = [_mmp(m, pair[c]['V'], NN, ps['apply']) for m, (c, hh) in zip(Mkr, heads)]
    if T > 1:
        Npow = [-jnp.where(strict, a[:R, :R], 0.0) for a in AM]
        Akp = [jnp.where(strict, a[:R, R:], 0.0) for a in AM]
        AV = [_mmp(m, pair[c]['V'], NN, ps['apply']) for m, (c, hh) in zip(Akp, heads)]
        inv = [eye + n for n in Npow]
        for _ in range(int(math.log2(T)) - 1):
            Npow = [_mmp(n, n, NN, ps['inv']) for n in Npow]
            inv = [i + _mmp(i, n, NN, ps['inv']) for i, n in zip(inv, Npow)]
    ex = [[tiled(x, xr, hh) for x, xr in zip(pair[c]['x'], pair[c]['xr'])] for c, hh in heads]
    WT = [st_ref[c] for c in range(NP)]
    PWRW = [_mmp(jnp.concatenate([e[0], e[1]], axis=0), WT[c], NT, ps['state']) for e, (c, hh) in zip(ex, heads)]
    if T > 1:
        E = [_mmp(i, pw[:R] + av, NN, ps['apply']) for i, pw, av in zip(inv, PWRW, AV)]
    else:
        E = [pw[:R] for pw in PWRW]
    Y = [pw[R:] + mv - _mmp(m, e, NN, ps['apply']) for pw, mv, m, e in zip(PWRW, MV, Mqr, E)]
    for c in range(NP):
        lhs = jnp.concatenate([jnp.where(own[hh], x, 0.0) for hh in range(2)
                               for x in (-E[2 * c + hh], pair[c]['V'])], axis=0)
        rhs = jnp.concatenate([ex[2 * c + hh][j] for hh in range(2) for j in (2, 3)], axis=0)
        dW = _mmp(lhs, rhs, TN, ps['dw'])
        erow = [jnp.sum(ex[2 * c + hh][4], axis=0, keepdims=True) * (1.0 / T) for hh in range(2)]
        st_ref[c] = (WT[c] + dW) * jnp.where(row_lo, erow[0], erow[1])
        store_y(c, jnp.where(lo, Y[2 * c], Y[2 * c + 1]))


def _rwkv_kernel(G, T, NC, seq, proj_ref, sh_ref, s0_ref, mu_ref, w0_ref, w2_ref, a0_ref, a2_ref, g2_ref,
                 kk_ref, ka_ref, rk_ref, lnw_ref, lnb_ref, y_ref, sT_ref, *scratch):
    ti = pl.program_id(1)
    TL = T * NC
    R = G * T
    st_ref = scratch[0]
    hsel = _head_sum_sel(D_B, HD_B)

    @pl.when(ti == 0)
    def _():
        for h in range(H_B):
            r0 = (h % 2) * HD_B
            for j in range(G // 2):
                st_ref[h // 2, r0:r0 + HD_B, j * LANES:(j + 1) * LANES] = jnp.concatenate(
                    [s0_ref[2 * j, h], s0_ref[2 * j + 1, h]], axis=1)

    if seq:
        pad_ref, ops_ref, gb_ref, ys_ref = scratch[1:]

        @pl.when(ti == 0)
        def _():
            pad_ref[:, SUBLANES - 1:SUBLANES, :] = sh_ref[...]

        pad_ref[:, SUBLANES:, :] = proj_ref[:, :, 2 * D_A:].astype(F32)
        pb = pad_ref[:, SUBLANES:, :]
        prev = pad_ref[:, SUBLANES - 1:SUBLANES - 1 + TL, :]
        xs = (pb + (prev - pb) * mu_ref[...][None]).reshape(G * TL, D_BPROJ)
        pad_ref[:, :SUBLANES, :] = pad_ref[:, TL:TL + SUBLANES, :]
    else:
        pb = proj_ref[:, 2 * D_A:]
        xs = pb + (sh_ref[...] - pb) * mu_ref[...]

    r, k2, v, logd, kk, beta, g, bonus = _rwkv_prep(
        xs, w0_ref[...], w2_ref[...], a0_ref[...], a2_ref[...], g2_ref[...], kk_ref[...], ka_ref[...],
        rk_ref[...], hsel)

    incl, strict, last = _chunk_masks(G, T)
    masks = (incl, strict, _to_bf16_mask(incl), _to_bf16_mask(last))
    emask = _batch_lane_mask(1, G, T, HD_B)

    if seq:
        for n, val in enumerate((r, k2, v, logd, kk, beta)):
            ops_ref[n] = val.reshape(G, TL, D_B)
        gb_ref[0] = g.reshape(G, TL, D_B)
        gb_ref[1] = bonus.reshape(G, TL, D_B)

        def chunk(c, carry):
            t0 = pl.multiple_of(c * T, T)
            ops = tuple(ops_ref[n, :, pl.ds(t0, T), :].reshape(R, D_B) for n in range(6))

            def store_y(hp, Y):
                ys_ref[:, pl.ds(t0, T), hp * LANES:(hp + 1) * LANES] = Y.reshape(G, T, LANES)

            _rwkv_chunk(G, T, ops, st_ref, masks, emask, store_y)
            return carry

        lax.fori_loop(0, NC, chunk, 0)
        y = ys_ref[...].reshape(G * TL, D_B)
        out = _rwkv_post(y, gb_ref[1].reshape(G * TL, D_B), gb_ref[0].reshape(G * TL, D_B),
                         lnw_ref[...], lnb_ref[...], hsel)
        y_ref[...] = out.reshape(G, TL, D_B).astype(y_ref.dtype)
    else:
        ys_ref = scratch[1]

        def store_y(hp, Y):
            ys_ref[:, hp * LANES:(hp + 1) * LANES] = Y

        _rwkv_chunk(G, T, (r, k2, v, logd, kk, beta), st_ref, masks, emask, store_y)
        y_ref[...] = _rwkv_post(ys_ref[...], bonus, g, lnw_ref[...], lnb_ref[...], hsel)

    @pl.when(ti == pl.num_programs(1) - 1)
    def _():
        for h in range(H_B):
            r0 = (h % 2) * HD_B
            for b in range(G):
                sT_ref[b, h] = st_ref[h // 2, r0:r0 + HD_B, b * HD_B:(b + 1) * HD_B]


def _state_to_lanes(s, G):
    B, H, X, Y = s.shape
    return s.reshape(B // G, G, H, X, Y).transpose(0, 2, 3, 1, 4).reshape(B // G, H, X, G * Y)


def _state_from_lanes(s, G):
    NB, H, X, GY = s.shape
    Y = GY // G
    return s.reshape(NB, H, X, G, Y).transpose(0, 3, 1, 2, 4).reshape(NB * G, H, X, Y)


def _rwkv(proj, shift0, s0, mu, w0, w2p, a0, a2p, g2, k_k, k_a, r_k, ln_w, ln_b):
    B, L, _ = proj.shape
    G = G_ROWS
    NB = B // G
    small = [mu, w0, w2p, a0, a2p, g2, k_k, k_a, r_k, ln_w, ln_b]
    full = lambda s: pl.BlockSpec(s.shape, lambda b, i: (0,) * s.ndim)
    st_dims = (H_B // 2, 2 * HD_B, G * HD_B)
    s0l = s0
    st_spec = pl.BlockSpec((G, H_B, HD_B, HD_B), lambda b, i: (b, 0, 0, 0))
    st_shape = jax.ShapeDtypeStruct(s0.shape, F32)
    unpair = lambda s: s
    if L == 1:
        T, NC = 1, 1
        kern = functools.partial(_rwkv_kernel, G, T, NC, False)
        y, sT = pl.pallas_call(
            kern,
            grid=(NB, 1),
            in_specs=[pl.BlockSpec((G, IN_AB), lambda b, i: (b, 0)),
                      pl.BlockSpec((G, D_BPROJ), lambda b, i: (b, 0)), st_spec] + [full(s) for s in small],
            out_specs=[pl.BlockSpec((G, D_B), lambda b, i: (b, 0)), st_spec],
            out_shape=[jax.ShapeDtypeStruct((B, D_B), F32), st_shape],
            scratch_shapes=[pltpu.VMEM(st_dims, F32), pltpu.VMEM((G, D_B), F32)],
            compiler_params=_cparams(("parallel", "arbitrary")),
        )(proj.reshape(B, IN_AB), shift0, s0l, *small)
        return y.reshape(B, 1, D_B), unpair(sT)
    T = min(T_CHUNK, L)
    TL = min(TL_BLOCK, L)
    NC = TL // T
    kern = functools.partial(_rwkv_kernel, G, T, NC, True)
    y, sT = pl.pallas_call(
        kern,
        grid=(NB, L // TL),
        in_specs=[pl.BlockSpec((G, TL, IN_AB), lambda b, i: (b, i, 0)),
                  pl.BlockSpec((G, 1, D_BPROJ), lambda b, i: (b, 0, 0)), st_spec] + [full(s) for s in small],
        out_specs=[pl.BlockSpec((G, TL, D_B), lambda b, i: (b, i, 0)), st_spec],
        out_shape=[jax.ShapeDtypeStruct((B, L, D_B), BF16), st_shape],
        scratch_shapes=[pltpu.VMEM(st_dims, F32),
                        pltpu.VMEM((G, SUBLANES + TL, D_BPROJ), F32),
                        pltpu.VMEM((6, G, TL, D_B), F32),
                        pltpu.VMEM((2, G, TL, D_B), F32),
                        pltpu.VMEM((G, TL, D_B), F32)],
        compiler_params=_cparams(("parallel", "arbitrary")),
    )(proj, shift0.reshape(B, 1, D_BPROJ), s0l, *small)
    return y, unpair(sT)


CD_Z, CD_XBC, CD_Q, CD_K, CD_V, CD_G, CD_S = 0, 512, 1536, 1792, 2048, 2560, 3072


def _ssd_chunk(G, T, xs, Bm, Cm, dt_all, la_all, st_ref, masks, emaskN, store_y):
    incl, lmask_bf, lastmask_bf = masks
    R = G * T
    cum = _mm_sel(lmask_bf, la_all, 3)
    cend = _mm_sel(lastmask_bf, cum, 3)
    cumT = cum.T
    wend = jnp.exp(cend - cum) * dt_all
    ecum = jnp.exp(cum)
    dend = jnp.exp(cend)
    lo = lax.broadcasted_iota(jnp.int32, (R, LANES), 1) < P_C

    def per_head(x):
        return jnp.concatenate([jnp.where(lo, x[:, 2 * c:2 * c + 1], x[:, 2 * c + 1:2 * c + 2])
                                for c in range(H_C // 2)], axis=1)

    tile = lambda x: jnp.where(emaskN, jnp.concatenate([x] * G, axis=1), 0.0)
    xdt = xs * per_head(dt_all)
    xw = xs * per_head(wend)
    ecf = per_head(ecum)
    Bg = [Bm[:, g * N_C:(g + 1) * N_C] for g in range(G_C)]
    Cg = [Cm[:, g * N_C:(g + 1) * N_C] for g in range(G_C)]
    CB = [_mm(c, b, NT) for c, b in zip(Cg, Bg)]
    ST = [st_ref[g] for g in range(G_C)]
    CS = [_mm(tile(c), s, NT) for c, s in zip(Cg, ST)]
    W = [CB[h // HG_C] * jnp.exp(jnp.where(incl, cum[:, h:h + 1] - cumT[h:h + 1, :], -1e30))
         for h in range(H_C)]
    intra = [_mm(W[h], xdt[:, (h // 2) * LANES:(h // 2 + 1) * LANES]) for h in range(H_C)]
    drow = [jnp.sum(jnp.where(emaskN, dend[:, h:h + 1], 0.0), axis=0, keepdims=True) * (1.0 / T)
            for h in range(H_C)]
    for c in range(H_C // 2):
        sl = slice(c * LANES, (c + 1) * LANES)
        gsl = slice((c % (HG_C // 2)) * LANES, (c % (HG_C // 2) + 1) * LANES)
        store_y(c, jnp.where(lo, intra[2 * c], intra[2 * c + 1]) + ecf[:, sl] * CS[c // (HG_C // 2)][:, gsl])
    for g in range(G_C):
        W_g = HG_C * P_C
        dS = _mm(xw[:, g * W_g:(g + 1) * W_g], tile(Bg[g]), TN)
        st_ref[g] = jnp.concatenate(
            [ST[g][hh * P_C:(hh + 1) * P_C] * drow[g * HG_C + hh] + dS[hh * P_C:(hh + 1) * P_C]
             for hh in range(HG_C)], axis=0)


def _gla_chunk(G, T, q, k, v, la, st_ref, masks, emask, store_o):
    incl, lmask_bf, lastmask_bf = masks
    R = G * T
    bc = _mm_sel(lmask_bf, la, 3)
    bend = _mm_sel(lastmask_bf, bc, 3)
    qe = q * jnp.exp(bc)
    kn = k * jnp.exp(-bc)
    kw = k * jnp.exp(bend - bc)
    dend = jnp.exp(bend)
    lo = lax.broadcasted_iota(jnp.int32, (R, LANES), 1) < DK_D
    own = {0: lo, 1: ~lo}
    n_rep = G * DK_D // LANES

    def tiled(x, xr, hh):
        t = jnp.where(lo, x, xr) if hh == 0 else jnp.where(lo, xr, x)
        return jnp.where(emask, jnp.concatenate([t] * n_rep, axis=1), 0.0)

    heads = [(c, hh) for c in range(H_D // 2) for hh in range(2)]
    pair = []
    for c in range(H_D // 2):
        sl = slice(c * LANES, (c + 1) * LANES)
        vals = (qe[:, sl], kw[:, sl], dend[:, sl])
        pair.append(dict(x=vals, xr=[pltpu.roll(x, DK_D, 1) for x in vals], kn=kn[:, sl]))
    vh = [v[:, h * DV_D:(h + 1) * DV_D] for h in range(H_D)]
    att = [jnp.where(incl, _mm(jnp.where(own[hh], pair[c]['x'][0], 0.0), pair[c]['kn'], NT), 0.0)
           for c, hh in heads]
    ex = [[tiled(x, xr, hh) for x, xr in zip(pair[c]['x'], pair[c]['xr'])] for c, hh in heads]
    ST = [st_ref[h] for h in range(H_D)]
    o = [_mm(a, vv) + _mm(e[0], s, NT) for a, vv, e, s in zip(att, vh, ex, ST)]
    for h in range(H_D):
        drow = jnp.sum(ex[h][2], axis=0, keepdims=True) * (1.0 / T)
        st_ref[h] = ST[h] * drow + _mm(vh[h], ex[h][1], TN)
        store_o(h, o[h])


def _cd_prep_small(small, dtb, a_neg, aup, ab):
    dt_all = _softplus(small + dtb)
    la_all = dt_all * a_neg
    la_gla = -_softplus(-(_mm(small, aup) + ab)) * (1.0 / GATE_NORM)
    return dt_all, la_all, la_gla


def _cd_post(y, xs, z, o, gg, dskip, ssd_norm, gla_norm):
    y_c = _rms((y + dskip * xs) * _silu(z), ssd_norm)
    outs = [y_c]
    for h in range(H_D):
        sl = slice(h * DV_D, (h + 1) * DV_D)
        outs.append(_rms(o[:, sl], gla_norm) * _silu(gg[:, sl]))
    return outs


def _cd_kernel(G, T, NC, seq, proj_ref, cs_ref, ssd0_ref, gla0_ref, cw_ref, cb_ref, dtb_ref, aneg_ref, dskip_ref,
               ssdn_ref, aup_ref, ab_ref, glan_ref, y_ref, ssdT_ref, glaT_ref, *scratch):
    ti = pl.program_id(1)
    TL = T * NC
    R = G * T
    sst_ref, gst_ref = scratch[0], scratch[1]

    @pl.when(ti == 0)
    def _():
        for h in range(H_C):
            r0 = (h % HG_C) * P_C
            for b in range(G):
                sst_ref[h // HG_C, r0:r0 + P_C, b * N_C:(b + 1) * N_C] = ssd0_ref[b, h]
        for h in range(H_D):
            for j in range(G // 2):
                two = jnp.concatenate([gla0_ref[2 * j, h], gla0_ref[2 * j + 1, h]], axis=0)
                gst_ref[h, :, j * LANES:(j + 1) * LANES] = two.T

    cw = cw_ref[...]
    if seq:
        pad_ref, xbc_ref, sm_ref, ys_ref, os_ref = scratch[2:]

        @pl.when(ti == 0)
        def _():
            pad_ref[:, SUBLANES - (CONV_W - 1):SUBLANES, :] = cs_ref[...]

        pad_ref[:, SUBLANES:, :] = proj_ref[:, :, CD_XBC:CD_XBC + D_XBC].astype(F32)

        def conv_b(b, carry):
            for c0 in range(0, D_XBC, CONV_COLS):
                cols = slice(c0, c0 + CONV_COLS)
                acc = cb_ref[:, cols]
                for kq in range(CONV_W):
                    off = SUBLANES - (CONV_W - 1) + kq
                    acc = acc + pad_ref[b, off:off + TL, cols] * cw_ref[kq:kq + 1, cols]
                xbc_ref[b, :, cols] = _silu(acc)
            return carry

        lax.fori_loop(0, G, conv_b, 0)
        pad_ref[:, :SUBLANES, :] = pad_ref[:, TL:TL + SUBLANES, :]
        small = proj_ref[:, :, CD_S:].astype(F32).reshape(G * TL, LANES)
    else:
        x = proj_ref[:, CD_XBC:CD_XBC + D_XBC]
        xc = cb_ref[...] + x * cw[CONV_W - 1:CONV_W, :]
        for kq in range(CONV_W - 1):
            xc = xc + cs_ref[kq] * cw[kq:kq + 1, :]
        xbc = _silu(xc)
        small = proj_ref[:, CD_S:]

    dt_all, la_all, la_gla = _cd_prep_small(small, dtb_ref[...], aneg_ref[...], aup_ref[...], ab_ref[...])

    incl, _, last = _chunk_masks(G, T)
    masks = (incl, _to_bf16_mask(incl), _to_bf16_mask(last))
    emaskN = _batch_lane_mask(1, G, T, N_C)
    emask1 = _batch_lane_mask(1, G, T, DK_D)
    scale = DK_D ** -0.5

    if seq:
        sm_ref[0] = dt_all.reshape(G, TL, LANES)
        sm_ref[1] = la_all.reshape(G, TL, LANES)
        sm_ref[2] = la_gla[:, :LANES].reshape(G, TL, LANES)
        sm_ref[3] = la_gla[:, LANES:].reshape(G, TL, LANES)

        def chunk(c, carry):
            t0 = pl.multiple_of(c * T, T)
            rows = lambda ref, lo, hi: ref[:, pl.ds(t0, T), lo:hi].astype(F32).reshape(R, hi - lo)
            xs = rows(xbc_ref, 0, D_C)
            Bm = rows(xbc_ref, D_C, D_C + G_C * N_C)
            Cm = rows(xbc_ref, D_C + G_C * N_C, D_XBC)
            dtc = sm_ref[0, :, pl.ds(t0, T), :].reshape(R, LANES)
            lac = sm_ref[1, :, pl.ds(t0, T), :].reshape(R, LANES)

            def store_y(hp, yv):
                ys_ref[:, pl.ds(t0, T), hp * LANES:(hp + 1) * LANES] = yv.reshape(G, T, LANES)

            _ssd_chunk(G, T, xs, Bm, Cm, dtc, lac, sst_ref, masks, emaskN, store_y)

            q = rows(proj_ref, CD_Q, CD_Q + K_D) * scale
            k = rows(proj_ref, CD_K, CD_K + K_D)
            v = rows(proj_ref, CD_V, CD_V + V_D)
            lag = jnp.concatenate([sm_ref[2, :, pl.ds(t0, T), :].reshape(R, LANES),
                                   sm_ref[3, :, pl.ds(t0, T), :].reshape(R, LANES)], axis=1)

            def store_o(h, ov):
                os_ref[:, pl.ds(t0, T), h * DV_D:(h + 1) * DV_D] = ov.reshape(G, T, DV_D)

            _gla_chunk(G, T, q, k, v, lag, gst_ref, masks, emask1, store_o)
            return carry

        lax.fori_loop(0, NC, chunk, 0)
        def post_b(b, carry):
            outs = _cd_post(ys_ref[b], xbc_ref[b, :, :D_C], proj_ref[b, :, CD_Z:CD_Z + D_C].astype(F32),
                            os_ref[b], proj_ref[b, :, CD_G:CD_G + V_D].astype(F32),
                            dskip_ref[...], ssdn_ref[...], glan_ref[...])
            y_ref[b, :, :D_C] = outs[0].astype(y_ref.dtype)
            for h in range(H_D):
                y_ref[b, :, D_C + h * DV_D:D_C + (h + 1) * DV_D] = outs[1 + h].astype(y_ref.dtype)
            return carry

        lax.fori_loop(0, G, post_b, 0)
    else:
        ys_ref, os_ref = scratch[2:]

        def store_y(hp, yv):
            ys_ref[:, hp * LANES:(hp + 1) * LANES] = yv

        def store_o(h, ov):
            os_ref[:, h * DV_D:(h + 1) * DV_D] = ov

        _ssd_chunk(G, T, xbc[:, :D_C], xbc[:, D_C:D_C + G_C * N_C], xbc[:, D_C + G_C * N_C:], dt_all, la_all,
                   sst_ref, masks, emaskN, store_y)
        _gla_chunk(G, T, proj_ref[:, CD_Q:CD_Q + K_D] * scale, proj_ref[:, CD_K:CD_K + K_D],
                   proj_ref[:, CD_V:CD_V + V_D], la_gla, gst_ref, masks, emask1, store_o)
        outs = _cd_post(ys_ref[...], xbc[:, :D_C], proj_ref[:, CD_Z:CD_Z + D_C], os_ref[...],
                        proj_ref[:, CD_G:CD_G + V_D], dskip_ref[...], ssdn_ref[...], glan_ref[...])
        y_ref[:, :D_C] = outs[0]
        for h in range(H_D):
            y_ref[:, D_C + h * DV_D:D_C + (h + 1) * DV_D] = outs[1 + h]

    @pl.when(ti == pl.num_programs(1) - 1)
    def _():
        for h in range(H_C):
            r0 = (h % HG_C) * P_C
            for b in range(G):
                ssdT_ref[b, h] = sst_ref[h // HG_C, r0:r0 + P_C, b * N_C:(b + 1) * N_C]
        for h in range(H_D):
            for j in range(G // 2):
                two = gst_ref[h, :, j * LANES:(j + 1) * LANES].T
                glaT_ref[2 * j, h] = two[:DK_D]
                glaT_ref[2 * j + 1, h] = two[DK_D:]


def _cd(proj, conv_state, ssd0, gla0, cw, cb, dtb, a_neg, dskip, ssd_norm, aup, ab, gla_norm):
    B, L, _ = proj.shape
    G = G_ROWS
    NB = B // G
    small = [cw, cb, dtb, a_neg, dskip, ssd_norm, aup, ab, gla_norm]
    full = lambda s: pl.BlockSpec(s.shape, lambda b, i: (0,) * s.ndim)
    ssd_dims = (G_C, HG_C * P_C, G * N_C)
    ssd0l, gla0l = ssd0, gla0
    ssd_spec = pl.BlockSpec((G, H_C, P_C, N_C), lambda b, i: (b, 0, 0, 0))
    gla_spec = pl.BlockSpec((G, H_D, DK_D, DV_D), lambda b, i: (b, 0, 0, 0))
    st_shapes = [jax.ShapeDtypeStruct(ssd0.shape, F32), jax.ShapeDtypeStruct(gla0.shape, F32)]
    st_scratch = [pltpu.VMEM(ssd_dims, F32), pltpu.VMEM((H_D, DV_D, G * DK_D), F32)]
    DO = D_C + V_D
    if L == 1:
        kern = functools.partial(_cd_kernel, G, 1, 1, False)
        y, ssdT, glaT = pl.pallas_call(
            kern,
            grid=(NB, 1),
            in_specs=[pl.BlockSpec((G, IN_CD_PAD), lambda b, i: (b, 0)),
                      pl.BlockSpec((CONV_W - 1, G, D_XBC), lambda b, i: (0, b, 0)),
                      ssd_spec, gla_spec] + [full(s) for s in small],
            out_specs=[pl.BlockSpec((G, DO), lambda b, i: (b, 0)), ssd_spec, gla_spec],
            out_shape=[jax.ShapeDtypeStruct((B, DO), F32)] + st_shapes,
            scratch_shapes=st_scratch + [pltpu.VMEM((G, D_C), F32), pltpu.VMEM((G, V_D), F32)],
            compiler_params=_cparams(("parallel", "arbitrary")),
        )(proj.reshape(B, IN_CD_PAD), jnp.transpose(conv_state, (1, 0, 2)), ssd0l, gla0l, *small)
        y = y.reshape(B, 1, DO)
    else:
        T = min(T_CHUNK, L)
        TL = min(TL_BLOCK, L)
        NC = TL // T
        kern = functools.partial(_cd_kernel, G, T, NC, True)
        y, ssdT, glaT = pl.pallas_call(
            kern,
            grid=(NB, L // TL),
            in_specs=[pl.BlockSpec((G, TL, IN_CD_PAD), lambda b, i: (b, i, 0)),
                      pl.BlockSpec((G, CONV_W - 1, D_XBC), lambda b, i: (b, 0, 0)),
                      ssd_spec, gla_spec] + [full(s) for s in small],
            out_specs=[pl.BlockSpec((G, TL, DO), lambda b, i: (b, i, 0)), ssd_spec, gla_spec],
            out_shape=[jax.ShapeDtypeStruct((B, L, DO), BF16)] + st_shapes,
            scratch_shapes=st_scratch + [pltpu.VMEM((G, SUBLANES + TL, D_XBC), F32),
                                         pltpu.VMEM((G, TL, D_XBC), F32),
                                         pltpu.VMEM((4, G, TL, LANES), F32),
                                         pltpu.VMEM((G, TL, D_C), F32),
                                         pltpu.VMEM((G, TL, V_D), F32)],
            compiler_params=_cparams(("parallel", "arbitrary")),
        )(proj, conv_state, ssd0l, gla0l, *small)
    return y, ssdT, glaT


def _block_diag(w):
    eye = jnp.eye(N_BLK_A, dtype=w.dtype)
    return jnp.einsum('nij,nm->nimj', w, eye).reshape(D_A, D_A)


def _row(v):
    return v.reshape(1, -1).astype(F32)


def _prep_params(p):
    q = {}
    j = 0
    q['w_in_ab'] = p['w_in_ab'][j].astype(BF16)
    q['w_out_ab'] = p['w_out_ab'][j].astype(BF16)
    q['lru'] = [p['lru_conv_w'][j], _row(p['lru_conv_b'][j]),
                _block_diag(p['lru_w_r'][j]).astype(BF16), _row(p['lru_b_r'][j]),
                _block_diag(p['lru_w_i'][j]).astype(BF16), _row(p['lru_b_i'][j]), _row(p['lru_lambda'][j])]
    zw = jnp.zeros((R_W, D_B), F32)
    w2p = jnp.concatenate([p['rwkv_w2'][j], zw], axis=0).astype(BF16)
    a2p = jnp.concatenate([zw, p['rwkv_a2'][j]], axis=0).astype(BF16)
    q['rwkv'] = [_row(p['rwkv_mu'][j]), _row(p['rwkv_w0'][j]), w2p, _row(p['rwkv_a0'][j]), a2p,
                 p['rwkv_g2'][j].astype(BF16), _row(p['rwkv_k_k'][j]), _row(p['rwkv_k_a'][j]),
                 _row(p['rwkv_r_k'][j]), _row(p['rwkv_ln_w'][j]), _row(p['rwkv_ln_b'][j])]
    q['ffn'] = [p['ffn_w_gate'][j].astype(BF16), p['ffn_w_up'][j].astype(BF16), p['ffn_w_down'][j].astype(BF16)]
    w = p['w_in_cd'][j]
    o_z, o_xbc, o_dt = 0, D_C, D_C + D_XBC
    o_q = o_dt + H_C
    o_k = o_q + K_D
    o_v = o_k + K_D
    o_g = o_v + V_D
    o_al = o_g + V_D
    pad = jnp.zeros((D_MODEL, LANES - H_C - R_ALPHA), F32)
    q['w_in_cd'] = jnp.concatenate([w[:, o_z:o_dt], w[:, o_q:o_al], w[:, o_dt:o_q], w[:, o_al:], pad],
                                   axis=1).astype(BF16)
    q['w_out_cd'] = p['w_out_cd'][j].astype(BF16)
    lane_pad = lambda v: jnp.concatenate([v.astype(F32), jnp.zeros((LANES - v.shape[0],), F32)]).reshape(1, LANES)
    aup = jnp.zeros((LANES, K_D), F32).at[H_C:H_C + R_ALPHA].set(p['gla_alpha_up'][j]).astype(BF16)
    q['cd'] = [p['ssd_conv_w'][j], _row(p['ssd_conv_b'][j]), lane_pad(p['ssd_dt_bias'][j]),
               lane_pad(-jnp.exp(p['ssd_a_log'][j].astype(F32))), _row(jnp.repeat(p['ssd_d'][j], P_C)),
               _row(p['ssd_norm_w'][j]), aup, _row(p['gla_alpha_b'][j]), _row(p['gla_norm_w'][j])]
    q['router'] = jnp.concatenate([p['moe_router'][j], jnp.zeros((D_MODEL, LANES - N_EXP), F32)], axis=1)
    q['moe'] = [p['moe_w_gate'][j].astype(BF16), p['moe_w_up'][j].astype(BF16), p['moe_w_down'][j].astype(BF16)]
    q['norm_mix'] = [_row(p['norm_mix'][i]) for i in range(DEPTH)]
    q['norm_ffn'] = [_row(p['norm_ffn'][i]) for i in range(DEPTH)]
    q['norm_ple'] = [_row(p['norm_ple'][i]) for i in range(DEPTH)]
    q['w_ple_gate'] = [p['w_ple_gate'][i].astype(BF16) for i in range(DEPTH)]
    q['w_ple_proj'] = [p['w_ple_proj'][i].astype(BF16) for i in range(DEPTH)]
    q['norm_final'] = _row(p['norm_final'])
    return q


def _trunk(x, p, states, q):
    lru_conv, lru_h, rwkv_shift, rwkv_wkv, ssd_conv, ssd_st, gla_st = states
    B, L, D = x.shape
    M = B * L
    h = x.reshape(M, D)
    p2 = p.reshape(DEPTH, M, D_PLE)

    proj_dtype = BF16 if L > 1 else F32
    proj = _norm_mm(h, q['norm_mix'][0], q['w_in_ab'], IN_AB // 2, proj_dtype).reshape(B, L, IN_AB)
    y_a, h_new = _lru(proj, lru_conv[0], lru_h[0], *q['lru'])
    y_b, wkv_new = _rwkv(proj, rwkv_shift[0], rwkv_wkv[0], *q['rwkv'])
    if L >= CONV_W - 1:
        conv_new = proj[:, L - (CONV_W - 1):, :D_A].astype(F32)
    else:
        conv_new = jnp.concatenate([lru_conv[0][:, L:], proj[:, :, :D_A]], axis=1)
    shift_new = proj[:, L - 1, 2 * D_A:].astype(F32)
    w_out = q['w_out_ab']
    h = _out_proj(h, [y_a.reshape(M, D_A), y_b.reshape(M, D_B)], [w_out[:D_A], w_out[D_A:]])
    h = _ffn(h, q['norm_ffn'][0], *q['ffn'], D_FF // 2)
    h = _ple(h, q['norm_ple'][0], q['w_ple_gate'][0], p2[0], q['w_ple_proj'][0], q['norm_final'], False)

    proj = _norm_mm(h, q['norm_mix'][1], q['w_in_cd'], IN_CD_PAD // 5, proj_dtype).reshape(B, L, IN_CD_PAD)
    y_cd, ssd_new, gla_new = _cd(proj, ssd_conv[0], ssd_st[0], gla_st[0], *q['cd'])
    if L >= CONV_W - 1:
        sconv_new = proj[:, L - (CONV_W - 1):, CD_XBC:CD_XBC + D_XBC].astype(F32)
    else:
        sconv_new = jnp.concatenate([ssd_conv[0][:, L:], proj[:, :, CD_XBC:CD_XBC + D_XBC]], axis=1)
    h = _out_proj(h, [y_cd.reshape(M, D_C + V_D)], [q['w_out_cd']])
    route, routeT, meta = _router(h, q['norm_ffn'][1], q['router'])
    h = _moe(h, q['norm_ffn'][1], route, routeT, meta, *q['moe'])
    y = _ple(h, q['norm_ple'][1], q['w_ple_gate'][1], p2[1], q['w_ple_proj'][1], q['norm_final'], True)

    new_states = (conv_new[None], h_new[None], shift_new[None], wkv_new[None],
                  sconv_new[None], ssd_new[None], gla_new[None])
    return y.reshape(B, L, D), new_states


def kernel(x_prompt, x_sample, p_prompt, p_sample, state_lru_conv, state_lru_h, state_rwkv_shift,
           state_rwkv_wkv, state_ssd_conv, state_ssd, state_gla, norm_mix, norm_ffn, norm_ple,
           w_ple_gate, w_ple_proj, norm_final, w_in_ab, w_out_ab, lru_conv_w, lru_conv_b, lru_w_r,
           lru_b_r, lru_w_i, lru_b_i, lru_lambda, rwkv_mu, rwkv_w0, rwkv_w2, rwkv_a0, rwkv_a2,
           rwkv_g2, rwkv_k_k, rwkv_k_a, rwkv_r_k, rwkv_ln_w, rwkv_ln_b, ffn_w_gate, ffn_w_up,
           ffn_w_down, w_in_cd, w_out_cd, ssd_conv_w, ssd_conv_b, ssd_dt_bias, ssd_a_log, ssd_d,
           ssd_norm_w, gla_alpha_up, gla_alpha_b, gla_norm_w, moe_router, moe_w_gate, moe_w_up,
           moe_w_down):
    prm = dict(
        norm_mix=norm_mix, norm_ffn=norm_ffn, norm_ple=norm_ple, w_ple_gate=w_ple_gate,
        w_ple_proj=w_ple_proj, norm_final=norm_final, w_in_ab=w_in_ab, w_out_ab=w_out_ab,
        lru_conv_w=lru_conv_w, lru_conv_b=lru_conv_b, lru_w_r=lru_w_r, lru_b_r=lru_b_r,
        lru_w_i=lru_w_i, lru_b_i=lru_b_i, lru_lambda=lru_lambda, rwkv_mu=rwkv_mu, rwkv_w0=rwkv_w0,
        rwkv_w2=rwkv_w2, rwkv_a0=rwkv_a0, rwkv_a2=rwkv_a2, rwkv_g2=rwkv_g2, rwkv_k_k=rwkv_k_k,
        rwkv_k_a=rwkv_k_a, rwkv_r_k=rwkv_r_k, rwkv_ln_w=rwkv_ln_w, rwkv_ln_b=rwkv_ln_b,
        ffn_w_gate=ffn_w_gate, ffn_w_up=ffn_w_up, ffn_w_down=ffn_w_down, w_in_cd=w_in_cd,
        w_out_cd=w_out_cd, ssd_conv_w=ssd_conv_w, ssd_conv_b=ssd_conv_b, ssd_dt_bias=ssd_dt_bias,
        ssd_a_log=ssd_a_log, ssd_d=ssd_d, ssd_norm_w=ssd_norm_w, gla_alpha_up=gla_alpha_up,
        gla_alpha_b=gla_alpha_b, gla_norm_w=gla_norm_w, moe_router=moe_router,
        moe_w_gate=moe_w_gate, moe_w_up=moe_w_up, moe_w_down=moe_w_down)
    q = _prep_params(prm)
    states_s = (state_lru_conv, state_lru_h, state_rwkv_shift, state_rwkv_wkv,
                state_ssd_conv, state_ssd, state_gla)
    n_prompt = x_prompt.shape[0]
    states_p = tuple(jnp.zeros((s.shape[0], n_prompt) + s.shape[2:], s.dtype) for s in states_s)
    y_p, st_p = _trunk(x_prompt, p_prompt, states_p, q)
    y_s, st_s = _trunk(x_sample, p_sample, states_s, q)
    return (y_p, y_s) + tuple(st_p) + tuple(st_s)
```

```python
import functools
import math

import jax
import jax.numpy as jnp
from jax import lax
from jax.experimental import pallas as pl
from jax.experimental.pallas import tpu as pltpu

F32, BF16 = jnp.float32, jnp.bfloat16

D_MODEL = 1024
DEPTH = 2
D_PLE = 256
EPS = 1e-6
CONV_W = 4
D_A = 512
N_BLK_A = 8
BW_A = D_A // N_BLK_A
LRU_C = 8.0
H_B = 8
HD_B = 64
D_B = H_B * HD_B
R_W = 64
R_A = 64
R_G = 128
D_BPROJ = 3 * D_B + R_W + R_A + R_G
GN_EPS_B = 64e-5
IN_AB = 2 * D_A + D_BPROJ
H_C = 8
P_C = 64
D_C = H_C * P_C
G_C = 2
HG_C = H_C // G_C
N_C = 128
D_XBC = D_C + 2 * G_C * N_C
H_D = 4
DK_D = 64
DV_D = 128
K_D = H_D * DK_D
V_D = H_D * DV_D
R_ALPHA = 16
GATE_NORM = 16.0
D_FF = 2816
N_EXP = 8
D_FF_E = 1408
IN_CD_PAD = D_C + D_XBC + 2 * K_D + 2 * V_D + 128

LANES = 128
SUBLANES = 8
VMEM_LIMIT = 56 * 1024 * 1024

G_ROWS = SUBLANES
T_CHUNK = 16
TL_BLOCK = 64
TM_TOK = 1024
TM_FFN = 512
TM_MOE = 1024
MOE_RB = 128
MOE_SORT_ROWS = 256
MOE_UNSORT_COLS = 512
TOP_K = 2
HEAD_SUM_PIECES = 2
CONV_COLS = 256
LRU_TL = 256

NN = ((1,), (0,))
NT = ((1,), (1,))
TN = ((0,), (0,))


def _cparams(sem):
    return pltpu.CompilerParams(dimension_semantics=sem, vmem_limit_bytes=VMEM_LIMIT)


def _rms(x, g):
    return x * lax.rsqrt(jnp.mean(x * x, axis=-1, keepdims=True) + EPS) * g


def _sigmoid(x):
    return jax.nn.sigmoid(x)


def _silu(x):
    return x * jax.nn.sigmoid(x)


def _softplus(x):
    return jnp.maximum(x, 0.0) + jnp.log1p(jnp.exp(-jnp.abs(x)))


def _gelu_tanh(x):
    c = math.sqrt(2.0 / math.pi)
    return 0.5 * x * (1.0 + jnp.tanh(c * (x + 0.044715 * (x * x * x))))


def _d(a, b, dims=NN):
    return lax.dot_general(a, b, (dims, ((), ())), preferred_element_type=F32)


def _mm(a, b, dims=NN):
    return _d(a.astype(BF16), b.astype(BF16), dims)


def _split2(x):
    hi = x.astype(BF16)
    lo = (x - hi.astype(F32)).astype(BF16)
    return hi, lo


def _mm3(a, b, dims=NN):
    ah, al = _split2(a)
    bh, bl = _split2(b)
    return _d(ah, bh, dims) + (_d(ah, bl, dims) + _d(al, bh, dims))


def _mm_sel(sel, x, pieces, sel_left=True, dims=NN):
    out = None
    rest = x
    for i in range(pieces):
        p = rest.astype(BF16)
        if i + 1 < pieces:
            rest = rest - p.astype(F32)
        t = _d(sel, p, dims) if sel_left else _d(p, sel, dims)
        out = t if out is None else out + t
    return out


def _chunk_masks(G, T):
    R = G * T
    ri = lax.broadcasted_iota(jnp.int32, (R, R), 0)
    ci = lax.broadcasted_iota(jnp.int32, (R, R), 1)
    same = (ri // T) == (ci // T)
    incl = same & (ci <= ri)
    strict = same & (ci < ri)
    last = ci == (ri // T) * T + (T - 1)
    return incl, strict, last


def _to_bf16_mask(m):
    return jnp.where(m, 1.0, 0.0).astype(BF16)


def _tile_sel(W, G):
    r = lax.broadcasted_iota(jnp.int32, (W, G * W), 0)
    c = lax.broadcasted_iota(jnp.int32, (W, G * W), 1)
    return _to_bf16_mask((c % W) == r)


def _batch_lane_mask(n_stack, G, T, W):
    R = G * T
    r = lax.broadcasted_iota(jnp.int32, (n_stack * R, G * W), 0)
    c = lax.broadcasted_iota(jnp.int32, (n_stack * R, G * W), 1)
    return ((r % R) // T) == (c // W)


def _head_sum_sel(width, head):
    r = lax.broadcasted_iota(jnp.int32, (width, width), 0)
    c = lax.broadcasted_iota(jnp.int32, (width, width), 1)
    return _to_bf16_mask((r // head) == (c // head))


def _norm_mm_kernel(x_ref, g_ref, w_ref, o_ref, xn_ref):
    @pl.when(pl.program_id(1) == 0)
    def _():
        xn_ref[...] = _rms(x_ref[...], g_ref[...]).astype(BF16)

    o_ref[...] = _d(xn_ref[...], w_ref[...]).astype(o_ref.dtype)


def _norm_mm(x, g, w, tn, out_dtype):
    M, K = x.shape
    N = w.shape[1]
    tm = min(TM_TOK, M)
    mode = dict(pipeline_mode=pl.Buffered(1)) if tn == N else {}
    return pl.pallas_call(
        _norm_mm_kernel,
        grid=(M // tm, N // tn),
        in_specs=[pl.BlockSpec((tm, K), lambda i, j: (i, 0)),
                  pl.BlockSpec((1, K), lambda i, j: (0, 0)),
                  pl.BlockSpec((K, tn), lambda i, j: (0, j), **mode)],
        out_specs=pl.BlockSpec((tm, tn), lambda i, j: (i, j)),
        out_shape=jax.ShapeDtypeStruct((M, N), out_dtype),
        scratch_shapes=[pltpu.VMEM((tm, K), BF16)],
        compiler_params=_cparams(("parallel", "arbitrary")),
    )(x, g, w)


def _out_proj_kernel(n_in, *refs):
    h_ref = refs[0]
    ys = refs[1:1 + n_in]
    ws = refs[1 + n_in:1 + 2 * n_in]
    o_ref = refs[1 + 2 * n_in]
    acc = h_ref[...]
    for y_ref, w_ref in zip(ys, ws):
        acc = acc + _mm(y_ref[...], w_ref[...])
    o_ref[...] = acc


def _out_proj(h, ys, ws):
    M, D = h.shape
    tm = min(TM_TOK, M)
    n_in = len(ys)
    in_specs = [pl.BlockSpec((tm, D), lambda i: (i, 0))]
    in_specs += [pl.BlockSpec((tm, y.shape[1]), lambda i: (i, 0)) for y in ys]
    in_specs += [pl.BlockSpec(w.shape, lambda i: (0, 0)) for w in ws]
    return pl.pallas_call(
        functools.partial(_out_proj_kernel, n_in),
        grid=(M // tm,),
        in_specs=in_specs,
        out_specs=pl.BlockSpec((tm, D), lambda i: (i, 0)),
        out_shape=jax.ShapeDtypeStruct((M, D), F32),
        compiler_params=_cparams(("parallel",)),
    )(h, *ys, *ws)


def _ffn_kernel(h_ref, g_ref, wg_ref, wu_ref, wd_ref, o_ref, xn_ref, acc_ref):
    j = pl.program_id(1)

    @pl.when(j == 0)
    def _():
        xn_ref[...] = _rms(h_ref[...], g_ref[...]).astype(BF16)
        acc_ref[...] = jnp.zeros_like(acc_ref)

    xn = xn_ref[...]
    act = _silu(_d(xn, wg_ref[...])) * _d(xn, wu_ref[...])
    acc_ref[...] += _mm(act, wd_ref[...])

    @pl.when(j == pl.num_programs(1) - 1)
    def _():
        o_ref[...] = h_ref[...] + acc_ref[...]


def _ffn(h, g, wg, wu, wd, tf):
    M, D = h.shape
    FF = wg.shape[1]
    tm = min(TM_FFN, M)
    mode = dict(pipeline_mode=pl.Buffered(1)) if tf == FF else {}
    return pl.pallas_call(
        _ffn_kernel,
        grid=(M // tm, FF // tf),
        in_specs=[pl.BlockSpec((tm, D), lambda i, j: (i, 0)),
                  pl.BlockSpec((1, D), lambda i, j: (0, 0)),
                  pl.BlockSpec((D, tf), lambda i, j: (0, j), **mode),
                  pl.BlockSpec((D, tf), lambda i, j: (0, j), **mode),
                  pl.BlockSpec((tf, D), lambda i, j: (j, 0), **mode)],
        out_specs=pl.BlockSpec((tm, D), lambda i, j: (i, 0)),
        out_shape=jax.ShapeDtypeStruct((M, D), F32),
        scratch_shapes=[pltpu.VMEM((tm, D), BF16), pltpu.VMEM((tm, D), F32)],
        compiler_params=_cparams(("parallel", "arbitrary")),
    )(h, g, wg, wu, wd)


def _router_kernel(h_ref, g_ref, wr_ref, route_ref, routeT_ref, meta_ref):
    xn = _rms(h_ref[...], g_ref[...])
    logits = _mm3(xn, wr_ref[...])
    lane = lax.broadcasted_iota(jnp.int32, logits.shape, 1)
    valid = lane < N_EXP
    logits = jnp.where(valid, logits, -1e30)
    m = jnp.max(logits, axis=-1, keepdims=True)
    e = jnp.where(valid, jnp.exp(logits - m), 0.0)
    p = e / jnp.sum(e, axis=-1, keepdims=True)
    lane_f = lane.astype(F32)
    m1 = jnp.max(p, axis=-1, keepdims=True)
    i1 = jnp.min(jnp.where(p == m1, lane_f, float(LANES)), axis=-1, keepdims=True)
    p2 = jnp.where((lane_f == i1) | (lane >= N_EXP), -1.0, p)
    m2 = jnp.max(p2, axis=-1, keepdims=True)
    i2 = jnp.min(jnp.where(p2 == m2, lane_f, float(LANES)), axis=-1, keepdims=True)
    tot = m1 + m2
    tm = p.shape[0]
    sel1 = lane_f == i1
    sel2 = lane_f == i2
    assign = jnp.where(sel1 | sel2, 1.0, 0.0)
    ri = lax.broadcasted_iota(jnp.int32, (tm, tm), 0)
    ci = lax.broadcasted_iota(jnp.int32, (tm, tm), 1)
    rank = _d(_to_bf16_mask(ci < ri), assign.astype(BF16))
    cnt = jnp.sum(assign, axis=0, keepdims=True)
    padded = jnp.floor((cnt + (MOE_RB - 1)) * (1.0 / MOE_RB)) * MOE_RB
    er = lax.broadcasted_iota(jnp.int32, (LANES, LANES), 0)
    ec = lax.broadcasted_iota(jnp.int32, (LANES, LANES), 1)
    off = _mm_sel(_to_bf16_mask(er < ec), jnp.broadcast_to(padded, (SUBLANES, LANES)), 2, sel_left=False)[0:1]
    pos = off + rank
    dest1 = jnp.sum(jnp.where(sel1, pos, 0.0), axis=-1, keepdims=True)
    dest2 = jnp.sum(jnp.where(sel2, pos, 0.0), axis=-1, keepdims=True)
    route = jnp.where(lane == 0, dest1, jnp.where(lane == 1, dest2,
                      jnp.where(lane == 2, m1 / tot, jnp.where(lane == 3, m2 / tot, 0.0))))
    route_ref[...] = route
    routeT_ref[...] = route.T[:SUBLANES]
    mrow = lax.broadcasted_iota(jnp.int32, (SUBLANES, LANES), 0)
    meta_ref[...] = jnp.where(mrow == 0, padded * (1.0 / MOE_RB), jnp.where(mrow == 1, off * (1.0 / MOE_RB), 0.0))


def _router(h, g, wr):
    M, D = h.shape
    tm = min(TM_MOE, M)
    nt = M // tm
    return pl.pallas_call(
        _router_kernel,
        grid=(nt,),
        in_specs=[pl.BlockSpec((tm, D), lambda i: (i, 0)),
                  pl.BlockSpec((1, D), lambda i: (0, 0)),
                  pl.BlockSpec((D, LANES), lambda i: (0, 0))],
        out_specs=[pl.BlockSpec((tm, LANES), lambda i: (i, 0)),
                   pl.BlockSpec((None, SUBLANES, tm), lambda i: (i, 0, 0)),
                   pl.BlockSpec((None, SUBLANES, LANES), lambda i: (i, 0, 0))],
        out_shape=[jax.ShapeDtypeStruct((M, LANES), F32),
                   jax.ShapeDtypeStruct((nt, SUBLANES, tm), F32),
                   jax.ShapeDtypeStruct((nt, SUBLANES, LANES), F32)],
        compiler_params=_cparams(("parallel",)),
    )(h, g, wr)


def _moe_kernel(nblk_ref, offb_ref, h_ref, g_ref, route_ref, routeT_ref, wg_ref, wu_ref, wd_ref, o_ref,
                xs_ref, gs_ref, xn_ref):
    i = pl.program_id(0)
    e = pl.program_id(1)
    tm = h_ref.shape[0]
    cap = xs_ref.shape[0]

    last_e = pl.num_programs(1) - 1
    used_rows = (offb_ref[i, last_e] + nblk_ref[i, last_e]) * MOE_RB

    @pl.when(e == 0)
    def _():
        xn_ref[...] = _rms(h_ref[...], g_ref[...]).astype(BF16)
        xs_ref[...] = jnp.zeros_like(xs_ref)
        row0 = lax.broadcasted_iota(jnp.int32, (MOE_SORT_ROWS, tm), 0).astype(F32)

        def sort_block(lo):
            d1, d2 = routeT_ref[0:1, :], routeT_ref[1:2, :]
            g1, g2 = routeT_ref[2:3, :], routeT_ref[3:4, :]
            rows = row0 + float(lo)
            m1 = rows == d1
            m2 = rows == d2
            xs_ref[lo:lo + MOE_SORT_ROWS, :] = _d(_to_bf16_mask(m1 | m2), xn_ref[...]).astype(BF16)
            gate = jnp.sum(jnp.where(m1, g1, 0.0) + jnp.where(m2, g2, 0.0), axis=-1, keepdims=True)
            gs_ref[lo:lo + MOE_SORT_ROWS, :] = jnp.broadcast_to(gate, (MOE_SORT_ROWS, LANES))

        for lo in range(0, cap, MOE_SORT_ROWS):
            if lo < TOP_K * tm:
                sort_block(lo)
            else:
                pl.when(lo < used_rows)(functools.partial(sort_block, lo))

    base = offb_ref[i, e]
    nb = nblk_ref[i, e]

    def expert_rows(blk, rows):
        r0 = pl.multiple_of(blk * MOE_RB, MOE_RB)
        x = xs_ref[pl.ds(r0, rows), :]
        act = _silu(_d(x, wg_ref[...])) * _d(x, wu_ref[...])
        act = act * gs_ref[pl.ds(r0, rows), 0:1]
        xs_ref[pl.ds(r0, rows), :] = _mm(act, wd_ref[...]).astype(BF16)

    def two_blocks(j, carry):
        expert_rows(base + 2 * j, 2 * MOE_RB)
        return carry

    lax.fori_loop(0, nb // 2, two_blocks, 0)

    @pl.when(nb % 2 == 1)
    def _():
        expert_rows(base + nb - 1, MOE_RB)

    @pl.when(e == last_e)
    def _():
        col0 = lax.broadcasted_iota(jnp.int32, (tm, MOE_UNSORT_COLS), 1).astype(F32)

        def unsort_block(lo):
            d1, d2 = route_ref[:, 0:1], route_ref[:, 1:2]
            cols = col0 + float(lo)
            return _d(_to_bf16_mask((cols == d1) | (cols == d2)), xs_ref[lo:lo + MOE_UNSORT_COLS, :])

        acc = h_ref[...]
        for lo in range(0, TOP_K * tm, MOE_UNSORT_COLS):
            acc = acc + unsort_block(lo)
        o_ref[...] = acc
        for lo in range(-(-TOP_K * tm // MOE_UNSORT_COLS) * MOE_UNSORT_COLS, cap, MOE_UNSORT_COLS):
            @pl.when(lo < used_rows)
            def _():
                o_ref[...] += unsort_block(lo)


def _moe(h, g, route, routeT, meta, wg, wu, wd):
    M, D = h.shape
    E, _, FF = wg.shape
    tm = min(TM_MOE, M)
    nt = M // tm
    cap = TOP_K * tm + E * MOE_RB
    cap = -(-cap // MOE_UNSORT_COLS) * MOE_UNSORT_COLS
    meta_i = meta.astype(jnp.int32)
    grid_spec = pltpu.PrefetchScalarGridSpec(
        num_scalar_prefetch=2,
        grid=(nt, E),
        in_specs=[pl.BlockSpec((tm, D), lambda i, e, nb, ob: (i, 0)),
                  pl.BlockSpec((1, D), lambda i, e, nb, ob: (0, 0)),
                  pl.BlockSpec((tm, LANES), lambda i, e, nb, ob: (i, 0)),
                  pl.BlockSpec((None, SUBLANES, tm), lambda i, e, nb, ob: (i, 0, 0)),
                  pl.BlockSpec((None, D, FF), lambda i, e, nb, ob: (e, 0, 0)),
                  pl.BlockSpec((None, D, FF), lambda i, e, nb, ob: (e, 0, 0)),
                  pl.BlockSpec((None, FF, D), lambda i, e, nb, ob: (e, 0, 0))],
        out_specs=pl.BlockSpec((tm, D), lambda i, e, nb, ob: (i, 0)),
        scratch_shapes=[pltpu.VMEM((cap, D), BF16), pltpu.VMEM((cap, LANES), F32), pltpu.VMEM((tm, D), BF16)])
    return pl.pallas_call(
        _moe_kernel,
        grid_spec=grid_spec,
        out_shape=jax.ShapeDtypeStruct((M, D), F32),
        compiler_params=_cparams(("parallel", "arbitrary")),
    )(meta_i[:, 0, :], meta_i[:, 1, :], h, g, route, routeT, wg, wu, wd)


def _ple_kernel(final, h_ref, g_ref, wg_ref, p_ref, wp_ref, gf_ref, o_ref):
    h = h_ref[...]
    gate = _sigmoid(_mm(_rms(h, g_ref[...]), wg_ref[...]))
    out = h + gate * _mm(p_ref[...], wp_ref[...])
    if final:
        out = _rms(out, gf_ref[...])
    o_ref[...] = out


def _ple(h, g, wg, p, wp, gf, final):
    M, D = h.shape
    tm = min(TM_TOK, M)
    return pl.pallas_call(
        functools.partial(_ple_kernel, final),
        grid=(M // tm,),
        in_specs=[pl.BlockSpec((tm, D), lambda i: (i, 0)),
                  pl.BlockSpec((1, D), lambda i: (0, 0)),
                  pl.BlockSpec((D, D), lambda i: (0, 0)),
                  pl.BlockSpec((tm, D_PLE), lambda i: (i, 0)),
                  pl.BlockSpec((D_PLE, D), lambda i: (0, 0)),
                  pl.BlockSpec((1, D), lambda i: (0, 0))],
        out_specs=pl.BlockSpec((tm, D), lambda i: (i, 0)),
        out_shape=jax.ShapeDtypeStruct((M, D), F32),
        compiler_params=_cparams(("parallel",)),
    )(h, g, wg, p, wp, gf)


def _lru_gates(xc, ga_unused, wr, br, wi, bi, lam):
    r = _sigmoid(_mm(xc, wr) + br)
    i = _sigmoid(_mm(xc, wi) + bi)
    log_a = -LRU_C * r * _softplus(-lam)
    a = jnp.exp(log_a)
    u = jnp.sqrt(jnp.tanh(-log_a) * (a * a + 1.0)) * (i * xc)
    return a, u


def _lru_seq_kernel(TL, proj_ref, cs_ref, h0_ref, cw_ref, cb_ref, wr_ref, br_ref, wi_ref, bi_ref, lam_ref,
                    y_ref, hT_ref, pad_ref, a_ref, u_ref, hs_ref, hc_ref):
    ti = pl.program_id(0)
    G = pad_ref.shape[0]

    @pl.when(ti == 0)
    def _():
        pad_ref[:, SUBLANES - (CONV_W - 1):SUBLANES, :] = cs_ref[...]
        hc_ref[...] = h0_ref[...]

    pad_ref[:, SUBLANES:, :] = proj_ref[:, :, :D_A].astype(F32)
    cw = cw_ref[...]
    xc = cb_ref[...][None]
    for k in range(CONV_W):
        off = SUBLANES - (CONV_W - 1) + k
        xc = xc + pad_ref[:, off:off + TL, :] * cw[k:k + 1, :][None]
    pad_ref[:, :SUBLANES, :] = pad_ref[:, TL:TL + SUBLANES, :]

    a, u = _lru_gates(xc.reshape(G * TL, D_A), None, wr_ref[...], br_ref[...], wi_ref[...], bi_ref[...],
                      lam_ref[...])
    a_ref[...] = a.reshape(G, TL, D_A)
    u_ref[...] = u.reshape(G, TL, D_A)

    def step(t, h):
        h = a_ref[:, pl.ds(t, 1), :] * h + u_ref[:, pl.ds(t, 1), :]
        hs_ref[:, pl.ds(t, 1), :] = h
        return h

    hc_ref[...] = lax.fori_loop(0, TL, step, hc_ref[...], unroll=8)
    y_ref[...] = (_gelu_tanh(proj_ref[:, :, D_A:].astype(F32)) * hs_ref[...]).astype(y_ref.dtype)

    @pl.when(ti == pl.num_programs(0) - 1)
    def _():
        hT_ref[...] = hc_ref[...]


def _lru_step_kernel(xa_ref, ga_ref, cs_ref, h0_ref, cw_ref, cb_ref, wr_ref, br_ref, wi_ref, bi_ref, lam_ref,
                     y_ref, hT_ref):
    cw = cw_ref[...]
    x = xa_ref[...]
    xc = cb_ref[...] + x * cw[CONV_W - 1:CONV_W, :]
    for k in range(CONV_W - 1):
        xc = xc + cs_ref[k] * cw[k:k + 1, :]
    a, u = _lru_gates(xc, None, wr_ref[...], br_ref[...], wi_ref[...], bi_ref[...], lam_ref[...])
    h = a * h0_ref[...] + u
    hT_ref[...] = h
    y_ref[...] = _gelu_tanh(ga_ref[...]) * h


def _lru(proj, conv_state, h0, cw, cb, wr, br, wi, bi, lam):
    B, L, _ = proj.shape
    small = [cw, cb, wr, br, wi, bi, lam]
    if L == 1:
        proj2 = proj.reshape(B, -1)
        cs = jnp.transpose(conv_state, (1, 0, 2))
        tb = min(B, 128)
        full = lambda s: pl.BlockSpec(s.shape, lambda i: (0,) * s.ndim)
        y, hT = pl.pallas_call(
            _lru_step_kernel,
            grid=(B // tb,),
            in_specs=[pl.BlockSpec((tb, D_A), lambda i: (i, 0)),
                      pl.BlockSpec((tb, D_A), lambda i: (i, 1)),
                      pl.BlockSpec((CONV_W - 1, tb, D_A), lambda i: (0, i, 0)),
                      pl.BlockSpec((tb, D_A), lambda i: (i, 0))] + [full(s) for s in small],
            out_specs=[pl.BlockSpec((tb, D_A), lambda i: (i, 0)), pl.BlockSpec((tb, D_A), lambda i: (i, 0))],
            out_shape=[jax.ShapeDtypeStruct((B, D_A), F32), jax.ShapeDtypeStruct((B, D_A), F32)],
            compiler_params=_cparams(("parallel",)),
        )(proj2, proj2, cs, h0, *small)
        return y.reshape(B, 1, D_A), hT
    TL = min(LRU_TL, L)
    G = B
    full = lambda s: pl.BlockSpec(s.shape, lambda i: (0,) * s.ndim)
    y, hT = pl.pallas_call(
        functools.partial(_lru_seq_kernel, TL),
        grid=(L // TL,),
        in_specs=[pl.BlockSpec((G, TL, 2 * D_A), lambda i: (0, i, 0)),
                  pl.BlockSpec((G, CONV_W - 1, D_A), lambda i: (0, 0, 0)),
                  pl.BlockSpec((G, 1, D_A), lambda i: (0, 0, 0))] + [full(s) for s in small],
        out_specs=[pl.BlockSpec((G, TL, D_A), lambda i: (0, i, 0)),
                   pl.BlockSpec((G, 1, D_A), lambda i: (0, 0, 0))],
        out_shape=[jax.ShapeDtypeStruct((B, L, D_A), BF16), jax.ShapeDtypeStruct((B, 1, D_A), F32)],
        scratch_shapes=[pltpu.VMEM((G, SUBLANES + TL, D_A), F32),
                        pltpu.VMEM((G, TL, D_A), F32), pltpu.VMEM((G, TL, D_A), F32),
                        pltpu.VMEM((G, TL, D_A), F32), pltpu.VMEM((G, 1, D_A), F32)],
        compiler_params=_cparams(("arbitrary",)),
    )(proj, conv_state, h0.reshape(B, 1, D_A), *small)
    return y, hT.reshape(B, D_A)


def _rwkv_prep(xs, w0, w2p, a0, a2p, g2, k_k, k_a, r_k, hsel):
    r = xs[:, 0:D_B]
    k = xs[:, D_B:2 * D_B]
    v = xs[:, 2 * D_B:3 * D_B]
    wa = xs[:, 3 * D_B:3 * D_B + R_W + R_A]
    gl = xs[:, 3 * D_B + R_W + R_A:]
    w_log = -_softplus(-(w0 + _mm(jnp.tanh(wa), w2p))) - 0.5
    logd = -jnp.exp(w_log)
    a = _sigmoid(a0 + _mm(wa, a2p))
    g = _mm(_sigmoid(gl), g2)
    kk = k * k_k
    kk = kk * lax.rsqrt(jnp.maximum(_mm_sel(hsel, kk * kk, HEAD_SUM_PIECES, sel_left=False), 1e-12))
    k2 = k * (1.0 + (a - 1.0) * k_a)
    beta = kk * a
    bonus = _mm_sel(hsel, r * k2 * r_k, HEAD_SUM_PIECES, sel_left=False) * v
    return r, k2, v, logd, kk, beta, g, bonus


def _rwkv_post(y, bonus, g, ln_w, ln_b, hsel):
    mean = _mm_sel(hsel, y, HEAD_SUM_PIECES, sel_left=False) * (1.0 / HD_B)
    d = y - mean
    var = _mm_sel(hsel, d * d, HEAD_SUM_PIECES, sel_left=False) * (1.0 / HD_B)
    yn = d * lax.rsqrt(var + GN_EPS_B) * ln_w + ln_b
    return (yn + bonus) * g


def _mmp(a, b, dims, passes):
    return _mm3(a, b, dims) if passes == 3 else _mm(a, b, dims)


RWKV_PASSES = dict(am=1, state=1, inv=1, apply=1, dw=1)


def _rwkv_chunk(G, T, ops, st_ref, masks, emask, store_y):
    r, k2, v, logd, kk, beta = ops
    incl, strict, lmask_bf, lastmask_bf = masks
    R = G * T
    ps = RWKV_PASSES
    cum = _mm_sel(lmask_bf, logd, 3)
    cend = _mm_sel(lastmask_bf, cum, 3)
    ecn = jnp.exp(-cum)
    P = jnp.exp(cum - logd) * kk
    Q = ecn * beta
    Kt = ecn * k2
    Rt = jnp.exp(cum) * r
    eend = jnp.exp(cend)
    lo = lax.broadcasted_iota(jnp.int32, (R, LANES), 1) < HD_B
    row_lo = lax.broadcasted_iota(jnp.int32, (LANES, G * HD_B), 0) < HD_B
    eye = None
    if T > 1:
        ri = lax.broadcasted_iota(jnp.int32, (R, R), 0)
        ci = lax.broadcasted_iota(jnp.int32, (R, R), 1)
        eye = jnp.where(ri == ci, 1.0, 0.0)
    n_rep = G * HD_B // LANES
    NP = H_B // 2
    heads = [(c, hh) for c in range(NP) for hh in range(2)]
    own = {0: lo, 1: ~lo}

    def tiled(x, xr, hh):
        t = jnp.where(lo, x, xr) if hh == 0 else jnp.where(lo, xr, x)
        return jnp.where(emask, jnp.concatenate([t] * n_rep, axis=1), 0.0)

    pair = []
    for c in range(NP):
        sl = slice(c * LANES, (c + 1) * LANES)
        vals = (P[:, sl], Rt[:, sl], Q[:, sl], Kt[:, sl], eend[:, sl])
        pair.append(dict(x=vals, xr=[pltpu.roll(x, HD_B, 1) for x in vals], V=v[:, sl]))
    AM = [_mmp(jnp.concatenate([jnp.where(own[hh], pair[c]['x'][0], 0.0),
                                jnp.where(own[hh], pair[c]['x'][1], 0.0)], axis=0),
               jnp.concatenate([pair[c]['x'][2], pair[c]['x'][3]], axis=0), NT, ps['am']) for c, hh in heads]
    Mqr = [jnp.where(incl, a[R:, :R], 0.0) for a in AM]
    Mkr = [jnp.where(incl, a[R:, R:], 0.0) for a in AM]
    MV = [_mmp(m, pair[c]['V'], NN, ps['apply']) for m, (c, hh) in zip(Mkr, heads)]
    if T > 1:
        Npow = [-jnp.where(strict, a[:R, :R], 0.0) for a in AM]
        Akp = [jnp.where(strict, a[:R, R:], 0.0) for a in AM]
        AV = [_mmp(m, pair[c]['V'], NN, ps['apply']) for m, (c, hh) in zip(Akp, heads)]
        inv = [eye + n for n in Npow]
        for _ in range(int(math.log2(T)) - 1):
            Npow = [_mmp(n, n, NN, ps['inv']) for n in Npow]
            inv = [i + _mmp(i, n, NN, ps['inv']) for i, n in zip(inv, Npow)]
    ex = [[tiled(x, xr, hh) for x, xr in zip(pair[c]['x'], pair[c]['xr'])] for c, hh in heads]
    WT = [st_ref[c] for c in range(NP)]
    PWRW = [_mmp(jnp.concatenate([e[0], e[1]], axis=0), WT[c], NT, ps['state']) for e, (c, hh) in zip(ex, heads)]
    if T > 1:
        E = [_mmp(i, pw[:R] + av, NN, ps['apply']) for i, pw, av in zip(inv, PWRW, AV)]
    else:
        E = [pw[:R] for pw in PWRW]
    Y = [pw[R:] + mv - _mmp(m, e, NN, ps['apply']) for pw, mv, m, e in zip(PWRW, MV, Mqr, E)]
    for c in range(NP):
        lhs = jnp.concatenate([jnp.where(own[hh], x, 0.0) for hh in range(2)
                               for x in (-E[2 * c + hh], pair[c]['V'])], axis=0)
        rhs = jnp.concatenate([ex[2 * c + hh][j] for hh in range(2) for j in (2, 3)], axis=0)
        dW = _mmp(lhs, rhs, TN, ps['dw'])
        erow = [jnp.sum(ex[2 * c + hh][4], axis=0, keepdims=True) * (1.0 / T) for hh in range(2)]
        st_ref[c] = (WT[c] + dW) * jnp.where(row_lo, erow[0], erow[1])
        store_y(c, jnp.where(lo, Y[2 * c], Y[2 * c + 1]))


def _rwkv_kernel(G, T, NC, seq, proj_ref, sh_ref, s0_ref, mu_ref, w0_ref, w2_ref, a0_ref, a2_ref, g2_ref,
                 kk_ref, ka_ref, rk_ref, lnw_ref, lnb_ref, y_ref, sT_ref, *scratch):
    ti = pl.program_id(1)
    TL = T * NC
    R = G * T
    st_ref = scratch[0]
    hsel = _head_sum_sel(D_B, HD_B)

    @pl.when(ti == 0)
    def _():
        for h in range(H_B):
            r0 = (h % 2) * HD_B
            for j in range(G // 2):
                st_ref[h // 2, r0:r0 + HD_B, j * LANES:(j + 1) * LANES] = jnp.concatenate(
                    [s0_ref[2 * j, h], s0_ref[2 * j + 1, h]], axis=1)

    if seq:
        pad_ref, ops_ref, gb_ref, ys_ref = scratch[1:]

        @pl.when(ti == 0)
        def _():
            pad_ref[:, SUBLANES - 1:SUBLANES, :] = sh_ref[...]

        pad_ref[:, SUBLANES:, :] = proj_ref[:, :, 2 * D_A:].astype(F32)
        pb = pad_ref[:, SUBLANES:, :]
        prev = pad_ref[:, SUBLANES - 1:SUBLANES - 1 + TL, :]
        xs = (pb + (prev - pb) * mu_ref[...][None]).reshape(G * TL, D_BPROJ)
        pad_ref[:, :SUBLANES, :] = pad_ref[:, TL:TL + SUBLANES, :]
    else:
        pb = proj_ref[:, 2 * D_A:]
        xs = pb + (sh_ref[...] - pb) * mu_ref[...]

    r, k2, v, logd, kk, beta, g, bonus = _rwkv_prep(
        xs, w0_ref[...], w2_ref[...], a0_ref[...], a2_ref[...], g2_ref[...], kk_ref[...], ka_ref[...],
        rk_ref[...], hsel)

    incl, strict, last = _chunk_masks(G, T)
    masks = (incl, strict, _to_bf16_mask(incl), _to_bf16_mask(last))
    emask = _batch_lane_mask(1, G, T, HD_B)

    if seq:
        for n, val in enumerate((r, k2, v, logd, kk, beta)):
            ops_ref[n] = val.reshape(G, TL, D_B)
        gb_ref[0] = g.reshape(G, TL, D_B)
        gb_ref[1] = bonus.reshape(G, TL, D_B)

        def chunk(c, carry):
            t0 = pl.multiple_of(c * T, T)
            ops = tuple(ops_ref[n, :, pl.ds(t0, T), :].reshape(R, D_B) for n in range(6))

            def store_y(hp, Y):
                ys_ref[:, pl.ds(t0, T), hp * LANES:(hp + 1) * LANES] = Y.reshape(G, T, LANES)

            _rwkv_chunk(G, T, ops, st_ref, masks, emask, store_y)
            return carry

        lax.fori_loop(0, NC, chunk, 0)
        y = ys_ref[...].reshape(G * TL, D_B)
        out = _rwkv_post(y, gb_ref[1].reshape(G * TL, D_B), gb_ref[0].reshape(G * TL, D_B),
                         lnw_ref[...], lnb_ref[...], hsel)
        y_ref[...] = out.reshape(G, TL, D_B).astype(y_ref.dtype)
    else:
        ys_ref = scratch[1]

        def store_y(hp, Y):
            ys_ref[:, hp * LANES:(hp + 1) * LANES] = Y

        _rwkv_chunk(G, T, (r, k2, v, logd, kk, beta), st_ref, masks, emask, store_y)
        y_ref[...] = _rwkv_post(ys_ref[...], bonus, g, lnw_ref[...], lnb_ref[...], hsel)

    @pl.when(ti == pl.num_programs(1) - 1)
    def _():
        for h in range(H_B):
            r0 = (h % 2) * HD_B
            for b in range(G):
                sT_ref[b, h] = st_ref[h // 2, r0:r0 + HD_B, b * HD_B:(b + 1) * HD_B]


def _state_to_lanes(s, G):
    B, H, X, Y = s.shape
    return s.reshape(B // G, G, H, X, Y).transpose(0, 2, 3, 1, 4).reshape(B // G, H, X, G * Y)


def _state_from_lanes(s, G):
    NB, H, X, GY = s.shape
    Y = GY // G
    return s.reshape(NB, H, X, G, Y).transpose(0, 3, 1, 2, 4).reshape(NB * G, H, X, Y)


def _rwkv(proj, shift0, s0, mu, w0, w2p, a0, a2p, g2, k_k, k_a, r_k, ln_w, ln_b):
    B, L, _ = proj.shape
    G = G_ROWS
    NB = B // G
    small = [mu, w0, w2p, a0, a2p, g2, k_k, k_a, r_k, ln_w, ln_b]
    full = lambda s: pl.BlockSpec(s.shape, lambda b, i: (0,) * s.ndim)
    st_dims = (H_B // 2, 2 * HD_B, G * HD_B)
    s0l = s0
    st_spec = pl.BlockSpec((G, H_B, HD_B, HD_B), lambda b, i: (b, 0, 0, 0))
    st_shape = jax.ShapeDtypeStruct(s0.shape, F32)
    unpair = lambda s: s
    if L == 1:
        T, NC = 1, 1
        kern = functools.partial(_rwkv_kernel, G, T, NC, False)
        y, sT = pl.pallas_call(
            kern,
            grid=(NB, 1),
            in_specs=[pl.BlockSpec((G, IN_AB), lambda b, i: (b, 0)),
                      pl.BlockSpec((G, D_BPROJ), lambda b, i: (b, 0)), st_spec] + [full(s) for s in small],
            out_specs=[pl.BlockSpec((G, D_B), lambda b, i: (b, 0)), st_spec],
            out_shape=[jax.ShapeDtypeStruct((B, D_B), F32), st_shape],
            scratch_shapes=[pltpu.VMEM(st_dims, F32), pltpu.VMEM((G, D_B), F32)],
            compiler_params=_cparams(("parallel", "arbitrary")),
        )(proj.reshape(B, IN_AB), shift0, s0l, *small)
        return y.reshape(B, 1, D_B), unpair(sT)
    T = min(T_CHUNK, L)
    TL = min(TL_BLOCK, L)
    NC = TL // T
    kern = functools.partial(_rwkv_kernel, G, T, NC, True)
    y, sT = pl.pallas_call(
        kern,
        grid=(NB, L // TL),
        in_specs=[pl.BlockSpec((G, TL, IN_AB), lambda b, i: (b, i, 0)),
                  pl.BlockSpec((G, 1, D_BPROJ), lambda b, i: (b, 0, 0)), st_spec] + [full(s) for s in small],
        out_specs=[pl.BlockSpec((G, TL, D_B), lambda b, i: (b, i, 0)), st_spec],
        out_shape=[jax.ShapeDtypeStruct((B, L, D_B), BF16), st_shape],
        scratch_shapes=[pltpu.VMEM(st_dims, F32),
                        pltpu.VMEM((G, SUBLANES + TL, D_BPROJ), F32),
                        pltpu.VMEM((6, G, TL, D_B), F32),
                        pltpu.VMEM((2, G, TL, D_B), F32),
                        pltpu.VMEM((G, TL, D_B), F32)],
        compiler_params=_cparams(("parallel", "arbitrary")),
    )(proj, shift0.reshape(B, 1, D_BPROJ), s0l, *small)
    return y, unpair(sT)


CD_Z, CD_XBC, CD_Q, CD_K, CD_V, CD_G, CD_S = 0, 512, 1536, 1792, 2048, 2560, 3072


def _ssd_chunk(G, T, xs, Bm, Cm, dt_all, la_all, st_ref, masks, emaskN, store_y):
    incl, lmask_bf, lastmask_bf = masks
    R = G * T
    cum = _mm_sel(lmask_bf, la_all, 3)
    cend = _mm_sel(lastmask_bf, cum, 3)
    cumT = cum.T
    wend = jnp.exp(cend - cum) * dt_all
    ecum = jnp.exp(cum)
    dend = jnp.exp(cend)
    lo = lax.broadcasted_iota(jnp.int32, (R, LANES), 1) < P_C

    def per_head(x):
        return jnp.concatenate([jnp.where(lo, x[:, 2 * c:2 * c + 1], x[:, 2 * c + 1:2 * c + 2])
                                for c in range(H_C // 2)], axis=1)

    tile = lambda x: jnp.where(emaskN, jnp.concatenate([x] * G, axis=1), 0.0)
    xdt = xs * per_head(dt_all)
    xw = xs * per_head(wend)
    ecf = per_head(ecum)
    Bg = [Bm[:, g * N_C:(g + 1) * N_C] for g in range(G_C)]
    Cg = [Cm[:, g * N_C:(g + 1) * N_C] for g in range(G_C)]
    CB = [_mm(c, b, NT) for c, b in zip(Cg, Bg)]
    ST = [st_ref[g] for g in range(G_C)]
    CS = [_mm(tile(c), s, NT) for c, s in zip(Cg, ST)]
    W = [CB[h // HG_C] * jnp.exp(jnp.where(incl, cum[:, h:h + 1] - cumT[h:h + 1, :], -1e30))
         for h in range(H_C)]
    intra = [_mm(W[h], xdt[:, (h // 2) * LANES:(h // 2 + 1) * LANES]) for h in range(H_C)]
    drow = [jnp.sum(jnp.where(emaskN, dend[:, h:h + 1], 0.0), axis=0, keepdims=True) * (1.0 / T)
            for h in range(H_C)]
    for c in range(H_C // 2):
        sl = slice(c * LANES, (c + 1) * LANES)
        gsl = slice((c % (HG_C // 2)) * LANES, (c % (HG_C // 2) + 1) * LANES)
        store_y(c, jnp.where(lo, intra[2 * c], intra[2 * c + 1]) + ecf[:, sl] * CS[c // (HG_C // 2)][:, gsl])
    for g in range(G_C):
        W_g = HG_C * P_C
        dS = _mm(xw[:, g * W_g:(g + 1) * W_g], tile(Bg[g]), TN)
        st_ref[g] = jnp.concatenate(
            [ST[g][hh * P_C:(hh + 1) * P_C] * drow[g * HG_C + hh] + dS[hh * P_C:(hh + 1) * P_C]
             for hh in range(HG_C)], axis=0)


def _gla_chunk(G, T, q, k, v, la, st_ref, masks, emask, store_o):
    incl, lmask_bf, lastmask_bf = masks
    R = G * T
    bc = _mm_sel(lmask_bf, la, 3)
    bend = _mm_sel(lastmask_bf, bc, 3)
    qe = q * jnp.exp(bc)
    kn = k * jnp.exp(-bc)
    kw = k * jnp.exp(bend - bc)
    dend = jnp.exp(bend)
    lo = lax.broadcasted_iota(jnp.int32, (R, LANES), 1) < DK_D
    own = {0: lo, 1: ~lo}
    n_rep = G * DK_D // LANES

    def tiled(x, xr, hh):
        t = jnp.where(lo, x, xr) if hh == 0 else jnp.where(lo, xr, x)
        return jnp.where(emask, jnp.concatenate([t] * n_rep, axis=1), 0.0)

    heads = [(c, hh) for c in range(H_D // 2) for hh in range(2)]
    pair = []
    for c in range(H_D // 2):
        sl = slice(c * LANES, (c + 1) * LANES)
        vals = (qe[:, sl], kw[:, sl], dend[:, sl])
        pair.append(dict(x=vals, xr=[pltpu.roll(x, DK_D, 1) for x in vals], kn=kn[:, sl]))
    vh = [v[:, h * DV_D:(h + 1) * DV_D] for h in range(H_D)]
    att = [jnp.where(incl, _mm(jnp.where(own[hh], pair[c]['x'][0], 0.0), pair[c]['kn'], NT), 0.0)
           for c, hh in heads]
    ex = [[tiled(x, xr, hh) for x, xr in zip(pair[c]['x'], pair[c]['xr'])] for c, hh in heads]
    ST = [st_ref[h] for h in range(H_D)]
    o = [_mm(a, vv) + _mm(e[0], s, NT) for a, vv, e, s in zip(att, vh, ex, ST)]
    for h in range(H_D):
        drow = jnp.sum(ex[h][2], axis=0, keepdims=True) * (1.0 / T)
        st_ref[h] = ST[h] * drow + _mm(vh[h], ex[h][1], TN)
        store_o(h, o[h])


def _cd_prep_small(small, dtb, a_neg, aup, ab):
    dt_all = _softplus(small + dtb)
    la_all = dt_all * a_neg
    la_gla = -_softplus(-(_mm(small, aup) + ab)) * (1.0 / GATE_NORM)
    return dt_all, la_all, la_gla


def _cd_post(y, xs, z, o, gg, dskip, ssd_norm, gla_norm):
    y_c = _rms((y + dskip * xs) * _silu(z), ssd_norm)
    outs = [y_c]
    for h in range(H_D):
        sl = slice(h * DV_D, (h + 1) * DV_D)
        outs.append(_rms(o[:, sl], gla_norm) * _silu(gg[:, sl]))
    return outs


def _cd_kernel(G, T, NC, seq, proj_ref, cs_ref, ssd0_ref, gla0_ref, cw_ref, cb_ref, dtb_ref, aneg_ref, dskip_ref,
               ssdn_ref, aup_ref, ab_ref, glan_ref, y_ref, ssdT_ref, glaT_ref, *scratch):
    ti = pl.program_id(1)
    TL = T * NC
    R = G * T
    sst_ref, gst_ref = scratch[0], scratch[1]

    @pl.when(ti == 0)
    def _():
        for h in range(H_C):
            r0 = (h % HG_C) * P_C
            for b in range(G):
                sst_ref[h // HG_C, r0:r0 + P_C, b * N_C:(b + 1) * N_C] = ssd0_ref[b, h]
        for h in range(H_D):
            for j in range(G // 2):
                two = jnp.concatenate([gla0_ref[2 * j, h], gla0_ref[2 * j + 1, h]], axis=0)
                gst_ref[h, :, j * LANES:(j + 1) * LANES] = two.T

    cw = cw_ref[...]
    if seq:
        pad_ref, xbc_ref, sm_ref, ys_ref, os_ref = scratch[2:]

        @pl.when(ti == 0)
        def _():
            pad_ref[:, SUBLANES - (CONV_W - 1):SUBLANES, :] = cs_ref[...]

        pad_ref[:, SUBLANES:, :] = proj_ref[:, :, CD_XBC:CD_XBC + D_XBC].astype(F32)

        def conv_b(b, carry):
            for c0 in range(0, D_XBC, CONV_COLS):
                cols = slice(c0, c0 + CONV_COLS)
                acc = cb_ref[:, cols]
                for kq in range(CONV_W):
                    off = SUBLANES - (CONV_W - 1) + kq
                    acc = acc + pad_ref[b, off:off + TL, cols] * cw_ref[kq:kq + 1, cols]
                xbc_ref[b, :, cols] = _silu(acc)
            return carry

        lax.fori_loop(0, G, conv_b, 0)
        pad_ref[:, :SUBLANES, :] = pad_ref[:, TL:TL + SUBLANES, :]
        small = proj_ref[:, :, CD_S:].astype(F32).reshape(G * TL, LANES)
    else:
        x = proj_ref[:, CD_XBC:CD_XBC + D_XBC]
        xc = cb_ref[...] + x * cw[CONV_W - 1:CONV_W, :]
        for kq in range(CONV_W - 1):
            xc = xc + cs_ref[kq] * cw[kq:kq + 1, :]
        xbc = _silu(xc)
        small = proj_ref[:, CD_S:]

    dt_all, la_all, la_gla = _cd_prep_small(small, dtb_ref[...], aneg_ref[...], aup_ref[...], ab_ref[...])

    incl, _, last = _chunk_masks(G, T)
    masks = (incl, _to_bf16_mask(incl), _to_bf16_mask(last))
    emaskN = _batch_lane_mask(1, G, T, N_C)
    emask1 = _batch_lane_mask(1, G, T, DK_D)
    scale = DK_D ** -0.5

    if seq:
        sm_ref[0] = dt_all.reshape(G, TL, LANES)
        sm_ref[1] = la_all.reshape(G, TL, LANES)
        sm_ref[2] = la_gla[:, :LANES].reshape(G, TL, LANES)
        sm_ref[3] = la_gla[:, LANES:].reshape(G, TL, LANES)

        def chunk(c, carry):
            t0 = pl.multiple_of(c * T, T)
            rows = lambda ref, lo, hi: ref[:, pl.ds(t0, T), lo:hi].astype(F32).reshape(R, hi - lo)
            xs = rows(xbc_ref, 0, D_C)
            Bm = rows(xbc_ref, D_C, D_C + G_C * N_C)
            Cm = rows(xbc_ref, D_C + G_C * N_C, D_XBC)
            dtc = sm_ref[0, :, pl.ds(t0, T), :].reshape(R, LANES)
            lac = sm_ref[1, :, pl.ds(t0, T), :].reshape(R, LANES)

            def store_y(hp, yv):
                ys_ref[:, pl.ds(t0, T), hp * LANES:(hp + 1) * LANES] = yv.reshape(G, T, LANES)

            _ssd_chunk(G, T, xs, Bm, Cm, dtc, lac, sst_ref, masks, emaskN, store_y)

            q = rows(proj_ref, CD_Q, CD_Q + K_D) * scale
            k = rows(proj_ref, CD_K, CD_K + K_D)
            v = rows(proj_ref, CD_V, CD_V + V_D)
            lag = jnp.concatenate([sm_ref[2, :, pl.ds(t0, T), :].reshape(R, LANES),
                                   sm_ref[3, :, pl.ds(t0, T), :].reshape(R, LANES)], axis=1)

            def store_o(h, ov):
                os_ref[:, pl.ds(t0, T), h * DV_D:(h + 1) * DV_D] = ov.reshape(G, T, DV_D)

            _gla_chunk(G, T, q, k, v, lag, gst_ref, masks, emask1, store_o)
            return carry

        lax.fori_loop(0, NC, chunk, 0)
        def post_b(b, carry):
            outs = _cd_post(ys_ref[b], xbc_ref[b, :, :D_C], proj_ref[b, :, CD_Z:CD_Z + D_C].astype(F32),
                            os_ref[b], proj_ref[b, :, CD_G:CD_G + V_D].astype(F32),
                            dskip_ref[...], ssdn_ref[...], glan_ref[...])
            y_ref[b, :, :D_C] = outs[0].astype(y_ref.dtype)
            for h in range(H_D):
                y_ref[b, :, D_C + h * DV_D:D_C + (h + 1) * DV_D] = outs[1 + h].astype(y_ref.dtype)
            return carry

        lax.fori_loop(0, G, post_b, 0)
    else:
        ys_ref, os_ref = scratch[2:]

        def store_y(hp, yv):
            ys_ref[:, hp * LANES:(hp + 1) * LANES] = yv

        def store_o(h, ov):
            os_ref[:, h * DV_D:(h + 1) * DV_D] = ov

        _ssd_chunk(G, T, xbc[:, :D_C], xbc[:, D_C:D_C + G_C * N_C], xbc[:, D_C + G_C * N_C:], dt_all, la_all,
                   sst_ref, masks, emaskN, store_y)
        _gla_chunk(G, T, proj_ref[:, CD_Q:CD_Q + K_D] * scale, proj_ref[:, CD_K:CD_K + K_D],
                   proj_ref[:, CD_V:CD_V + V_D], la_gla, gst_ref, masks, emask1, store_o)
        outs = _cd_post(ys_ref[...], xbc[:, :D_C], proj_ref[:, CD_Z:CD_Z + D_C], os_ref[...],
                        proj_ref[:, CD_G:CD_G + V_D], dskip_ref[...], ssdn_ref[...], glan_ref[...])
        y_ref[:, :D_C] = outs[0]
        for h in range(H_D):
            y_ref[:, D_C + h * DV_D:D_C + (h + 1) * DV_D] = outs[1 + h]

    @pl.when(ti == pl.num_programs(1) - 1)
    def _():
        for h in range(H_C):
            r0 = (h % HG_C) * P_C
            for b in range(G):
                ssdT_ref[b, h] = sst_ref[h // HG_C, r0:r0 + P_C, b * N_C:(b + 1) * N_C]
        for h in range(H_D):
            for j in range(G // 2):
                two = gst_ref[h, :, j * LANES:(j + 1) * LANES].T
                glaT_ref[2 * j, h] = two[:DK_D]
                glaT_ref[2 * j + 1, h] = two[DK_D:]


def _cd(proj, conv_state, ssd0, gla0, cw, cb, dtb, a_neg, dskip, ssd_norm, aup, ab, gla_norm):
    B, L, _ = proj.shape
    G = G_ROWS
    NB = B // G
    small = [cw, cb, dtb, a_neg, dskip, ssd_norm, aup, ab, gla_norm]
    full = lambda s: pl.BlockSpec(s.shape, lambda b, i: (0,) * s.ndim)
    ssd_dims = (G_C, HG_C * P_C, G * N_C)
    ssd0l, gla0l = ssd0, gla0
    ssd_spec = pl.BlockSpec((G, H_C, P_C, N_C), lambda b, i: (b, 0, 0, 0))
    gla_spec = pl.BlockSpec((G, H_D, DK_D, DV_D), lambda b, i: (b, 0, 0, 0))
    st_shapes = [jax.ShapeDtypeStruct(ssd0.shape, F32), jax.ShapeDtypeStruct(gla0.shape, F32)]
    st_scratch = [pltpu.VMEM(ssd_dims, F32), pltpu.VMEM((H_D, DV_D, G * DK_D), F32)]
    DO = D_C + V_D
    if L == 1:
        kern = functools.partial(_cd_kernel, G, 1, 1, False)
        y, ssdT, glaT = pl.pallas_call(
            kern,
            grid=(NB, 1),
            in_specs=[pl.BlockSpec((G, IN_CD_PAD), lambda b, i: (b, 0)),
                      pl.BlockSpec((CONV_W - 1, G, D_XBC), lambda b, i: (0, b, 0)),
                      ssd_spec, gla_spec] + [full(s) for s in small],
            out_specs=[pl.BlockSpec((G, DO), lambda b, i: (b, 0)), ssd_spec, gla_spec],
            out_shape=[jax.ShapeDtypeStruct((B, DO), F32)] + st_shapes,
            scratch_shapes=st_scratch + [pltpu.VMEM((G, D_C), F32), pltpu.VMEM((G, V_D), F32)],
            compiler_params=_cparams(("parallel", "arbitrary")),
        )(proj.reshape(B, IN_CD_PAD), jnp.transpose(conv_state, (1, 0, 2)), ssd0l, gla0l, *small)
        y = y.reshape(B, 1, DO)
    else:
        T = min(T_CHUNK, L)
        TL = min(TL_BLOCK, L)
        NC = TL // T
        kern = functools.partial(_cd_kernel, G, T, NC, True)
        y, ssdT, glaT = pl.pallas_call(
            kern,
            grid=(NB, L // TL),
            in_specs=[pl.BlockSpec((G, TL, IN_CD_PAD), lambda b, i: (b, i, 0)),
                      pl.BlockSpec((G, CONV_W - 1, D_XBC), lambda b, i: (b, 0, 0)),
                      ssd_spec, gla_spec] + [full(s) for s in small],
            out_specs=[pl.BlockSpec((G, TL, DO), lambda b, i: (b, i, 0)), ssd_spec, gla_spec],
            out_shape=[jax.ShapeDtypeStruct((B, L, DO), BF16)] + st_shapes,
            scratch_shapes=st_scratch + [pltpu.VMEM((G, SUBLANES + TL, D_XBC), F32),
                                         pltpu.VMEM((G, TL, D_XBC), F32),
                                         pltpu.VMEM((4, G, TL, LANES), F32),
                                         pltpu.VMEM((G, TL, D_C), F32),
                                         pltpu.VMEM((G, TL, V_D), F32)],
            compiler_params=_cparams(("parallel", "arbitrary")),
        )(proj, conv_state, ssd0l, gla0l, *small)
    return y, ssdT, glaT


def _block_diag(w):
    eye = jnp.eye(N_BLK_A, dtype=w.dtype)
    return jnp.einsum('nij,nm->nimj', w, eye).reshape(D_A, D_A)


def _row(v):
    return v.reshape(1, -1).astype(F32)


def _prep_params(p):
    q = {}
    j = 0
    q['w_in_ab'] = p['w_in_ab'][j].astype(BF16)
    q['w_out_ab'] = p['w_out_ab'][j].astype(BF16)
    q['lru'] = [p['lru_conv_w'][j], _row(p['lru_conv_b'][j]),
                _block_diag(p['lru_w_r'][j]).astype(BF16), _row(p['lru_b_r'][j]),
                _block_diag(p['lru_w_i'][j]).astype(BF16), _row(p['lru_b_i'][j]), _row(p['lru_lambda'][j])]
    zw = jnp.zeros((R_W, D_B), F32)
    w2p = jnp.concatenate([p['rwkv_w2'][j], zw], axis=0).astype(BF16)
    a2p = jnp.concatenate([zw, p['rwkv_a2'][j]], axis=0).astype(BF16)
    q['rwkv'] = [_row(p['rwkv_mu'][j]), _row(p['rwkv_w0'][j]), w2p, _row(p['rwkv_a0'][j]), a2p,
                 p['rwkv_g2'][j].astype(BF16), _row(p['rwkv_k_k'][j]), _row(p['rwkv_k_a'][j]),
                 _row(p['rwkv_r_k'][j]), _row(p['rwkv_ln_w'][j]), _row(p['rwkv_ln_b'][j])]
    q['ffn'] = [p['ffn_w_gate'][j].astype(BF16), p['ffn_w_up'][j].astype(BF16), p['ffn_w_down'][j].astype(BF16)]
    w = p['w_in_cd'][j]
    o_z, o_xbc, o_dt = 0, D_C, D_C + D_XBC
    o_q = o_dt + H_C
    o_k = o_q + K_D
    o_v = o_k + K_D
    o_g = o_v + V_D
    o_al = o_g + V_D
    pad = jnp.zeros((D_MODEL, LANES - H_C - R_ALPHA), F32)
    q['w_in_cd'] = jnp.concatenate([w[:, o_z:o_dt], w[:, o_q:o_al], w[:, o_dt:o_q], w[:, o_al:], pad],
                                   axis=1).astype(BF16)
    q['w_out_cd'] = p['w_out_cd'][j].astype(BF16)
    lane_pad = lambda v: jnp.concatenate([v.astype(F32), jnp.zeros((LANES - v.shape[0],), F32)]).reshape(1, LANES)
    aup = jnp.zeros((LANES, K_D), F32).at[H_C:H_C + R_ALPHA].set(p['gla_alpha_up'][j]).astype(BF16)
    q['cd'] = [p['ssd_conv_w'][j], _row(p['ssd_conv_b'][j]), lane_pad(p['ssd_dt_bias'][j]),
               lane_pad(-jnp.exp(p['ssd_a_log'][j].astype(F32))), _row(jnp.repeat(p['ssd_d'][j], P_C)),
               _row(p['ssd_norm_w'][j]), aup, _row(p['gla_alpha_b'][j]), _row(p['gla_norm_w'][j])]
    q['router'] = jnp.concatenate([p['moe_router'][j], jnp.zeros((D_MODEL, LANES - N_EXP), F32)], axis=1)
    q['moe'] = [p['moe_w_gate'][j].astype(BF16), p['moe_w_up'][j].astype(BF16), p['moe_w_down'][j].astype(BF16)]
    q['norm_mix'] = [_row(p['norm_mix'][i]) for i in range(DEPTH)]
    q['norm_ffn'] = [_row(p['norm_ffn'][i]) for i in range(DEPTH)]
    q['norm_ple'] = [_row(p['norm_ple'][i]) for i in range(DEPTH)]
    q['w_ple_gate'] = [p['w_ple_gate'][i].astype(BF16) for i in range(DEPTH)]
    q['w_ple_proj'] = [p['w_ple_proj'][i].astype(BF16) for i in range(DEPTH)]
    q['norm_final'] = _row(p['norm_final'])
    return q


def _trunk(x, p, states, q):
    lru_conv, lru_h, rwkv_shift, rwkv_wkv, ssd_conv, ssd_st, gla_st = states
    B, L, D = x.shape
    M = B * L
    h = x.reshape(M, D)
    p2 = p.reshape(DEPTH, M, D_PLE)

    proj_dtype = BF16 if L > 1 else F32
    proj = _norm_mm(h, q['norm_mix'][0], q['w_in_ab'], IN_AB, proj_dtype).reshape(B, L, IN_AB)
    y_a, h_new = _lru(proj, lru_conv[0], lru_h[0], *q['lru'])
    y_b, wkv_new = _rwkv(proj, rwkv_shift[0], rwkv_wkv[0], *q['rwkv'])
    if L >= CONV_W - 1:
        conv_new = proj[:, L - (CONV_W - 1):, :D_A].astype(F32)
    else:
        conv_new = jnp.concatenate([lru_conv[0][:, L:], proj[:, :, :D_A]], axis=1)
    shift_new = proj[:, L - 1, 2 * D_A:].astype(F32)
    w_out = q['w_out_ab']
    h = _out_proj(h, [y_a.reshape(M, D_A), y_b.reshape(M, D_B)], [w_out[:D_A], w_out[D_A:]])
    h = _ffn(h, q['norm_ffn'][0], *q['ffn'], D_FF)
    h = _ple(h, q['norm_ple'][0], q['w_ple_gate'][0], p2[0], q['w_ple_proj'][0], q['norm_final'], False)

    proj = _norm_mm(h, q['norm_mix'][1], q['w_in_cd'], IN_CD_PAD, proj_dtype).reshape(B, L, IN_CD_PAD)
    y_cd, ssd_new, gla_new = _cd(proj, ssd_conv[0], ssd_st[0], gla_st[0], *q['cd'])
    if L >= CONV_W - 1:
        sconv_new = proj[:, L - (CONV_W - 1):, CD_XBC:CD_XBC + D_XBC].astype(F32)
    else:
        sconv_new = jnp.concatenate([ssd_conv[0][:, L:], proj[:, :, CD_XBC:CD_XBC + D_XBC]], axis=1)
    h = _out_proj(h, [y_cd.reshape(M, D_C + V_D)], [q['w_out_cd']])
    route, routeT, meta = _router(h, q['norm_ffn'][1], q['router'])
    h = _moe(h, q['norm_ffn'][1], route, routeT, meta, *q['moe'])
    y = _ple(h, q['norm_ple'][1], q['w_ple_gate'][1], p2[1], q['w_ple_proj'][1], q['norm_final'], True)

    new_states = (conv_new[None], h_new[None], shift_new[None], wkv_new[None],
                  sconv_new[None], ssd_new[None], gla_new[None])
    return y.reshape(B, L, D), new_states


def kernel(x_prompt, x_sample, p_prompt, p_sample, state_lru_conv, state_lru_h, state_rwkv_shift,
           state_rwkv_wkv, state_ssd_conv, state_ssd, state_gla, norm_mix, norm_ffn, norm_ple,
           w_ple_gate, w_ple_proj, norm_final, w_in_ab, w_out_ab, lru_conv_w, lru_conv_b, lru_w_r,
           lru_b_r, lru_w_i, lru_b_i, lru_lambda, rwkv_mu, rwkv_w0, rwkv_w2, rwkv_a0, rwkv_a2,
           rwkv_g2, rwkv_k_k, rwkv_k_a, rwkv_r_k, rwkv_ln_w, rwkv_ln_b, ffn_w_gate, ffn_w_up,
           ffn_w_down, w_in_cd, w_out_cd, ssd_conv_w, ssd_conv_b, ssd_dt_bias, ssd_a_log, ssd_d,
           ssd_norm_w, gla_alpha_up, gla_alpha_b, gla_norm_w, moe_router, moe_w_gate, moe_w_up,
           moe_w_down):
    prm = dict(
        norm_mix=norm_mix, norm_ffn=norm_ffn, norm_ple=norm_ple, w_ple_gate=w_ple_gate,
        w_ple_proj=w_ple_proj, norm_final=norm_final, w_in_ab=w_in_ab, w_out_ab=w_out_ab,
        lru_conv_w=lru_conv_w, lru_conv_b=lru_conv_b, lru_w_r=lru_w_r, lru_b_r=lru_b_r,
        lru_w_i=lru_w_i, lru_b_i=lru_b_i, lru_lambda=lru_lambda, rwkv_mu=rwkv_mu, rwkv_w0=rwkv_w0,
        rwkv_w2=rwkv_w2, rwkv_a0=rwkv_a0, rwkv_a2=rwkv_a2, rwkv_g2=rwkv_g2, rwkv_k_k=rwkv_k_k,
        rwkv_k_a=rwkv_k_a, rwkv_r_k=rwkv_r_k, rwkv_ln_w=rwkv_ln_w, rwkv_ln_b=rwkv_ln_b,
        ffn_w_gate=ffn_w_gate, ffn_w_up=ffn_w_up, ffn_w_down=ffn_w_down, w_in_cd=w_in_cd,
        w_out_cd=w_out_cd, ssd_conv_w=ssd_conv_w, ssd_conv_b=ssd_conv_b, ssd_dt_bias=ssd_dt_bias,
        ssd_a_log=ssd_a_log, ssd_d=ssd_d, ssd_norm_w=ssd_norm_w, gla_alpha_up=gla_alpha_up,
        gla_alpha_b=gla_alpha_b, gla_norm_w=gla_norm_w, moe_router=moe_router,
        moe_w_gate=moe_w_gate, moe_w_up=moe_w_up, moe_w_down=moe_w_down)
    q = _prep_params(prm)
    states_s = (state_lru_conv, state_lru_h, state_rwkv_shift, state_rwkv_wkv,
                state_ssd_conv, state_ssd, state_gla)
    n_prompt = x_prompt.shape[0]
    states_p = tuple(jnp.zeros((s.shape[0], n_prompt) + s.shape[2:], s.dtype) for s in states_s)
    y_p, st_p = _trunk(x_prompt, p_prompt, states_p, q)
    y_s, st_s = _trunk(x_sample, p_sample, states_s, q)
    return (y_p, y_s) + tuple(st_p) + tuple(st_s)
```

```python
import functools
import math

import jax
import jax.numpy as jnp
from jax import lax
from jax.experimental import pallas as pl
from jax.experimental.pallas import tpu as pltpu

F32, BF16 = jnp.float32, jnp.bfloat16

D_MODEL = 1024
DEPTH = 2
D_PLE = 256
EPS = 1e-6
CONV_W = 4
D_A = 512
N_BLK_A = 8
BW_A = D_A // N_BLK_A
LRU_C = 8.0
H_B = 8
HD_B = 64
D_B = H_B * HD_B
R_W = 64
R_A = 64
R_G = 128
D_BPROJ = 3 * D_B + R_W + R_A + R_G
GN_EPS_B = 64e-5
IN_AB = 2 * D_A + D_BPROJ
H_C = 8
P_C = 64
D_C = H_C * P_C
G_C = 2
HG_C = H_C // G_C
N_C = 128
D_XBC = D_C + 2 * G_C * N_C
H_D = 4
DK_D = 64
DV_D = 128
K_D = H_D * DK_D
V_D = H_D * DV_D
R_ALPHA = 16
GATE_NORM = 16.0
D_FF = 2816
N_EXP = 8
D_FF_E = 1408
IN_CD_PAD = D_C + D_XBC + 2 * K_D + 2 * V_D + 128

LANES = 128
SUBLANES = 8
VMEM_LIMIT = 56 * 1024 * 1024

G_ROWS = SUBLANES
T_CHUNK = 16
TL_BLOCK = 64
TM_TOK = 1024
TM_FFN = 512
TM_MOE = 1024
MOE_RB = 128
MOE_SORT_ROWS = 256
MOE_UNSORT_COLS = 512
TOP_K = 2
HEAD_SUM_PIECES = 2
CONV_COLS = 256
LRU_TL = 256

NN = ((1,), (0,))
NT = ((1,), (1,))
TN = ((0,), (0,))


def _cparams(sem):
    return pltpu.CompilerParams(dimension_semantics=sem, vmem_limit_bytes=VMEM_LIMIT)


def _rms(x, g):
    return x * lax.rsqrt(jnp.mean(x * x, axis=-1, keepdims=True) + EPS) * g


def _sigmoid(x):
    return jax.nn.sigmoid(x)


def _silu(x):
    return x * jax.nn.sigmoid(x)


def _softplus(x):
    return jnp.maximum(x, 0.0) + jnp.log1p(jnp.exp(-jnp.abs(x)))


def _gelu_tanh(x):
    c = math.sqrt(2.0 / math.pi)
    return 0.5 * x * (1.0 + jnp.tanh(c * (x + 0.044715 * (x * x * x))))


def _d(a, b, dims=NN):
    return lax.dot_general(a, b, (dims, ((), ())), preferred_element_type=F32)


def _mm(a, b, dims=NN):
    return _d(a.astype(BF16), b.astype(BF16), dims)


def _split2(x):
    hi = x.astype(BF16)
    lo = (x - hi.astype(F32)).astype(BF16)
    return hi, lo


def _mm3(a, b, dims=NN):
    ah, al = _split2(a)
    bh, bl = _split2(b)
    return _d(ah, bh, dims) + (_d(ah, bl, dims) + _d(al, bh, dims))


def _mm_sel(sel, x, pieces, sel_left=True, dims=NN):
    out = None
    rest = x
    for i in range(pieces):
        p = rest.astype(BF16)
        if i + 1 < pieces:
            rest = rest - p.astype(F32)
        t = _d(sel, p, dims) if sel_left else _d(p, sel, dims)
        out = t if out is None else out + t
    return out


def _chunk_masks(G, T):
    R = G * T
    ri = lax.broadcasted_iota(jnp.int32, (R, R), 0)
    ci = lax.broadcasted_iota(jnp.int32, (R, R), 1)
    same = (ri // T) == (ci // T)
    incl = same & (ci <= ri)
    strict = same & (ci < ri)
    last = ci == (ri // T) * T + (T - 1)
    return incl, strict, last


def _to_bf16_mask(m):
    return jnp.where(m, 1.0, 0.0).astype(BF16)


def _batch_lane_mask(n_stack, G, T, W):
    R = G * T
    r = lax.broadcasted_iota(jnp.int32, (n_stack * R, G * W), 0)
    c = lax.broadcasted_iota(jnp.int32, (n_stack * R, G * W), 1)
    return ((r % R) // T) == (c // W)


def _head_sum_sel(width, head):
    r = lax.broadcasted_iota(jnp.int32, (width, width), 0)
    c = lax.broadcasted_iota(jnp.int32, (width, width), 1)
    return _to_bf16_mask((r // head) == (c // head))


def _norm_mm_kernel(x_ref, g_ref, w_ref, o_ref, xn_ref):
    @pl.when(pl.program_id(1) == 0)
    def _():
        xn_ref[...] = _rms(x_ref[...], g_ref[...]).astype(BF16)

    o_ref[...] = _d(xn_ref[...], w_ref[...]).astype(o_ref.dtype)


def _norm_mm(x, g, w, tn, out_dtype):
    M, K = x.shape
    N = w.shape[1]
    tm = min(TM_TOK, M)
    mode = dict(pipeline_mode=pl.Buffered(1)) if tn == N else {}
    return pl.pallas_call(
        _norm_mm_kernel,
        grid=(M // tm, N // tn),
        in_specs=[pl.BlockSpec((tm, K), lambda i, j: (i, 0)),
                  pl.BlockSpec((1, K), lambda i, j: (0, 0)),
                  pl.BlockSpec((K, tn), lambda i, j: (0, j), **mode)],
        out_specs=pl.BlockSpec((tm, tn), lambda i, j: (i, j)),
        out_shape=jax.ShapeDtypeStruct((M, N), out_dtype),
        scratch_shapes=[pltpu.VMEM((tm, K), BF16)],
        compiler_params=_cparams(("parallel", "arbitrary")),
    )(x, g, w)


def _ple_update(h, g, wg, p, wp):
    return h + _sigmoid(_mm(_rms(h, g), wg)) * _mm(p, wp)


def _channel_mix_kernel(h_ref, ya_ref, yb_ref, woa_ref, wob_ref, g_ref, wg_ref, wu_ref, wd_ref,
                        gp_ref, wpg_ref, p_ref, wpp_ref, o_ref):
    h = h_ref[...] + _mm(ya_ref[...], woa_ref[...]) + _mm(yb_ref[...], wob_ref[...])
    xn = _rms(h, g_ref[...]).astype(BF16)
    act = _silu(_d(xn, wg_ref[...])) * _d(xn, wu_ref[...])
    h = h + _mm(act, wd_ref[...])
    o_ref[...] = _ple_update(h, gp_ref[...], wpg_ref[...], p_ref[...], wpp_ref[...])


def _channel_mix(h, ya, yb, woa, wob, g, wg, wu, wd, gp, wpg, p, wpp):
    M, D = h.shape
    tm = min(TM_FFN, M)
    tok = lambda a: pl.BlockSpec((tm, a.shape[1]), lambda i: (i, 0))
    res = lambda a: pl.BlockSpec(a.shape, lambda i: (0, 0), pipeline_mode=pl.Buffered(1))
    return pl.pallas_call(
        _channel_mix_kernel,
        grid=(M // tm,),
        in_specs=[tok(h), tok(ya), tok(yb), res(woa), res(wob), res(g), res(wg), res(wu), res(wd),
                  res(gp), res(wpg), tok(p), res(wpp)],
        out_specs=pl.BlockSpec((tm, D), lambda i: (i, 0)),
        out_shape=jax.ShapeDtypeStruct((M, D), F32),
        compiler_params=_cparams(("parallel",)),
    )(h, ya, yb, woa, wob, g, wg, wu, wd, gp, wpg, p, wpp)


def _router_kernel(h_ref, y_ref, wo_ref, g_ref, wr_ref, hout_ref, route_ref, routeT_ref, meta_ref):
    h = h_ref[...] + _mm(y_ref[...], wo_ref[...])
    hout_ref[...] = h
    xn = _rms(h, g_ref[...])
    logits = _mm3(xn, wr_ref[...])
    lane = lax.broadcasted_iota(jnp.int32, logits.shape, 1)
    valid = lane < N_EXP
    logits = jnp.where(valid, logits, -1e30)
    m = jnp.max(logits, axis=-1, keepdims=True)
    e = jnp.where(valid, jnp.exp(logits - m), 0.0)
    p = e / jnp.sum(e, axis=-1, keepdims=True)
    lane_f = lane.astype(F32)
    m1 = jnp.max(p, axis=-1, keepdims=True)
    i1 = jnp.min(jnp.where(p == m1, lane_f, float(LANES)), axis=-1, keepdims=True)
    p2 = jnp.where((lane_f == i1) | (lane >= N_EXP), -1.0, p)
    m2 = jnp.max(p2, axis=-1, keepdims=True)
    i2 = jnp.min(jnp.where(p2 == m2, lane_f, float(LANES)), axis=-1, keepdims=True)
    tot = m1 + m2
    tm = p.shape[0]
    sel1 = lane_f == i1
    sel2 = lane_f == i2
    assign = jnp.where(sel1 | sel2, 1.0, 0.0)
    ri = lax.broadcasted_iota(jnp.int32, (tm, tm), 0)
    ci = lax.broadcasted_iota(jnp.int32, (tm, tm), 1)
    rank = _d(_to_bf16_mask(ci < ri), assign.astype(BF16))
    cnt = jnp.sum(assign, axis=0, keepdims=True)
    padded = jnp.floor((cnt + (MOE_RB - 1)) * (1.0 / MOE_RB)) * MOE_RB
    er = lax.broadcasted_iota(jnp.int32, (LANES, LANES), 0)
    ec = lax.broadcasted_iota(jnp.int32, (LANES, LANES), 1)
    off = _mm_sel(_to_bf16_mask(er < ec), jnp.broadcast_to(padded, (SUBLANES, LANES)), 2, sel_left=False)[0:1]
    pos = off + rank
    dest1 = jnp.sum(jnp.where(sel1, pos, 0.0), axis=-1, keepdims=True)
    dest2 = jnp.sum(jnp.where(sel2, pos, 0.0), axis=-1, keepdims=True)
    route = jnp.where(lane == 0, dest1, jnp.where(lane == 1, dest2,
                      jnp.where(lane == 2, m1 / tot, jnp.where(lane == 3, m2 / tot, 0.0))))
    route_ref[...] = route
    routeT_ref[...] = route.T[:SUBLANES]
    mrow = lax.broadcasted_iota(jnp.int32, (SUBLANES, LANES), 0)
    meta_ref[...] = jnp.where(mrow == 0, padded * (1.0 / MOE_RB), jnp.where(mrow == 1, off * (1.0 / MOE_RB), 0.0))


def _router(h, y, wo, g, wr):
    M, D = h.shape
    tm = min(TM_MOE, M)
    nt = M // tm
    return pl.pallas_call(
        _router_kernel,
        grid=(nt,),
        in_specs=[pl.BlockSpec((tm, D), lambda i: (i, 0)),
                  pl.BlockSpec((tm, y.shape[1]), lambda i: (i, 0)),
                  pl.BlockSpec(wo.shape, lambda i: (0, 0)),
                  pl.BlockSpec((1, D), lambda i: (0, 0)),
                  pl.BlockSpec((D, LANES), lambda i: (0, 0))],
        out_specs=[pl.BlockSpec((tm, D), lambda i: (i, 0)),
                   pl.BlockSpec((tm, LANES), lambda i: (i, 0)),
                   pl.BlockSpec((None, SUBLANES, tm), lambda i: (i, 0, 0)),
                   pl.BlockSpec((None, SUBLANES, LANES), lambda i: (i, 0, 0))],
        out_shape=[jax.ShapeDtypeStruct((M, D), F32),
                   jax.ShapeDtypeStruct((M, LANES), F32),
                   jax.ShapeDtypeStruct((nt, SUBLANES, tm), F32),
                   jax.ShapeDtypeStruct((nt, SUBLANES, LANES), F32)],
        compiler_params=_cparams(("parallel",)),
    )(h, y, wo, g, wr)


def _moe_kernel(nblk_ref, offb_ref, h_ref, g_ref, route_ref, routeT_ref, wg_ref, wu_ref, wd_ref, o_ref,
                xs_ref, gs_ref, xn_ref):
    i = pl.program_id(0)
    e = pl.program_id(1)
    tm = h_ref.shape[0]
    cap = xs_ref.shape[0]

    last_e = pl.num_programs(1) - 1
    used_rows = (offb_ref[i, last_e] + nblk_ref[i, last_e]) * MOE_RB

    @pl.when(e == 0)
    def _():
        xn_ref[...] = _rms(h_ref[...], g_ref[...]).astype(BF16)
        xs_ref[...] = jnp.zeros_like(xs_ref)
        row0 = lax.broadcasted_iota(jnp.int32, (MOE_SORT_ROWS, tm), 0).astype(F32)

        def sort_block(lo):
            d1, d2 = routeT_ref[0:1, :], routeT_ref[1:2, :]
            g1, g2 = routeT_ref[2:3, :], routeT_ref[3:4, :]
            rows = row0 + float(lo)
            m1 = rows == d1
            m2 = rows == d2
            xs_ref[lo:lo + MOE_SORT_ROWS, :] = _d(_to_bf16_mask(m1 | m2), xn_ref[...]).astype(BF16)
            gate = jnp.sum(jnp.where(m1, g1, 0.0) + jnp.where(m2, g2, 0.0), axis=-1, keepdims=True)
            gs_ref[lo:lo + MOE_SORT_ROWS, :] = jnp.broadcast_to(gate, (MOE_SORT_ROWS, LANES))

        for lo in range(0, cap, MOE_SORT_ROWS):
            if lo < TOP_K * tm:
                sort_block(lo)
            else:
                pl.when(lo < used_rows)(functools.partial(sort_block, lo))

    base = offb_ref[i, e]
    nb = nblk_ref[i, e]

    def expert_rows(blk, rows):
        r0 = pl.multiple_of(blk * MOE_RB, MOE_RB)
        x = xs_ref[pl.ds(r0, rows), :]
        act = _silu(_d(x, wg_ref[...])) * _d(x, wu_ref[...])
        act = act * gs_ref[pl.ds(r0, rows), 0:1]
        xs_ref[pl.ds(r0, rows), :] = _mm(act, wd_ref[...]).astype(BF16)

    def two_blocks(j, carry):
        expert_rows(base + 2 * j, 2 * MOE_RB)
        return carry

    lax.fori_loop(0, nb // 2, two_blocks, 0)

    @pl.when(nb % 2 == 1)
    def _():
        expert_rows(base + nb - 1, MOE_RB)

    @pl.when(e == last_e)
    def _():
        col0 = lax.broadcasted_iota(jnp.int32, (tm, MOE_UNSORT_COLS), 1).astype(F32)

        def unsort_block(lo):
            d1, d2 = route_ref[:, 0:1], route_ref[:, 1:2]
            cols = col0 + float(lo)
            return _d(_to_bf16_mask((cols == d1) | (cols == d2)), xs_ref[lo:lo + MOE_UNSORT_COLS, :])

        acc = h_ref[...]
        for lo in range(0, TOP_K * tm, MOE_UNSORT_COLS):
            acc = acc + unsort_block(lo)
        o_ref[...] = acc
        for lo in range(-(-TOP_K * tm // MOE_UNSORT_COLS) * MOE_UNSORT_COLS, cap, MOE_UNSORT_COLS):
            @pl.when(lo < used_rows)
            def _():
                o_ref[...] += unsort_block(lo)


def _moe(h, g, route, routeT, meta, wg, wu, wd):
    M, D = h.shape
    E, _, FF = wg.shape
    tm = min(TM_MOE, M)
    nt = M // tm
    cap = TOP_K * tm + E * MOE_RB
    cap = -(-cap // MOE_UNSORT_COLS) * MOE_UNSORT_COLS
    meta_i = meta.astype(jnp.int32)
    grid_spec = pltpu.PrefetchScalarGridSpec(
        num_scalar_prefetch=2,
        grid=(nt, E),
        in_specs=[pl.BlockSpec((tm, D), lambda i, e, nb, ob: (i, 0)),
                  pl.BlockSpec((1, D), lambda i, e, nb, ob: (0, 0)),
                  pl.BlockSpec((tm, LANES), lambda i, e, nb, ob: (i, 0)),
                  pl.BlockSpec((None, SUBLANES, tm), lambda i, e, nb, ob: (i, 0, 0)),
                  pl.BlockSpec((None, D, FF), lambda i, e, nb, ob: (e, 0, 0)),
                  pl.BlockSpec((None, D, FF), lambda i, e, nb, ob: (e, 0, 0)),
                  pl.BlockSpec((None, FF, D), lambda i, e, nb, ob: (e, 0, 0))],
        out_specs=pl.BlockSpec((tm, D), lambda i, e, nb, ob: (i, 0)),
        scratch_shapes=[pltpu.VMEM((cap, D), BF16), pltpu.VMEM((cap, LANES), F32), pltpu.VMEM((tm, D), BF16)])
    return pl.pallas_call(
        _moe_kernel,
        grid_spec=grid_spec,
        out_shape=jax.ShapeDtypeStruct((M, D), F32),
        compiler_params=_cparams(("parallel", "arbitrary")),
    )(meta_i[:, 0, :], meta_i[:, 1, :], h, g, route, routeT, wg, wu, wd)


def _ple_kernel(final, h_ref, g_ref, wg_ref, p_ref, wp_ref, gf_ref, o_ref):
    out = _ple_update(h_ref[...], g_ref[...], wg_ref[...], p_ref[...], wp_ref[...])
    if final:
        out = _rms(out, gf_ref[...])
    o_ref[...] = out


def _ple(h, g, wg, p, wp, gf, final):
    M, D = h.shape
    tm = min(TM_TOK, M)
    return pl.pallas_call(
        functools.partial(_ple_kernel, final),
        grid=(M // tm,),
        in_specs=[pl.BlockSpec((tm, D), lambda i: (i, 0)),
                  pl.BlockSpec((1, D), lambda i: (0, 0)),
                  pl.BlockSpec((D, D), lambda i: (0, 0)),
                  pl.BlockSpec((tm, D_PLE), lambda i: (i, 0)),
                  pl.BlockSpec((D_PLE, D), lambda i: (0, 0)),
                  pl.BlockSpec((1, D), lambda i: (0, 0))],
        out_specs=pl.BlockSpec((tm, D), lambda i: (i, 0)),
        out_shape=jax.ShapeDtypeStruct((M, D), F32),
        compiler_params=_cparams(("parallel",)),
    )(h, g, wg, p, wp, gf)


def _lru_gates(xc, wr, br, wi, bi, lam):
    r = _sigmoid(_mm(xc, wr) + br)
    i = _sigmoid(_mm(xc, wi) + bi)
    log_a = -LRU_C * r * _softplus(-lam)
    a = jnp.exp(log_a)
    u = jnp.sqrt(jnp.tanh(-log_a) * (a * a + 1.0)) * (i * xc)
    return a, u


def _lru_seq_kernel(TL, proj_ref, cs_ref, h0_ref, cw_ref, cb_ref, wr_ref, br_ref, wi_ref, bi_ref, lam_ref,
                    y_ref, hT_ref, pad_ref, a_ref, u_ref, hs_ref, hc_ref):
    ti = pl.program_id(0)
    G = pad_ref.shape[0]

    @pl.when(ti == 0)
    def _():
        pad_ref[:, SUBLANES - (CONV_W - 1):SUBLANES, :] = cs_ref[...]
        hc_ref[...] = h0_ref[...]

    pad_ref[:, SUBLANES:, :] = proj_ref[:, :, :D_A].astype(F32)
    cw = cw_ref[...]
    xc = cb_ref[...][None]
    for k in range(CONV_W):
        off = SUBLANES - (CONV_W - 1) + k
        xc = xc + pad_ref[:, off:off + TL, :] * cw[k:k + 1, :][None]
    pad_ref[:, :SUBLANES, :] = pad_ref[:, TL:TL + SUBLANES, :]

    a, u = _lru_gates(xc.reshape(G * TL, D_A), wr_ref[...], br_ref[...], wi_ref[...], bi_ref[...],
                      lam_ref[...])
    a_ref[...] = a.reshape(G, TL, D_A)
    u_ref[...] = u.reshape(G, TL, D_A)

    def step(t, h):
        h = a_ref[:, pl.ds(t, 1), :] * h + u_ref[:, pl.ds(t, 1), :]
        hs_ref[:, pl.ds(t, 1), :] = h
        return h

    hc_ref[...] = lax.fori_loop(0, TL, step, hc_ref[...], unroll=8)
    y_ref[...] = (_gelu_tanh(proj_ref[:, :, D_A:].astype(F32)) * hs_ref[...]).astype(y_ref.dtype)

    @pl.when(ti == pl.num_programs(0) - 1)
    def _():
        hT_ref[...] = hc_ref[...]


def _lru_step_kernel(xa_ref, ga_ref, cs_ref, h0_ref, cw_ref, cb_ref, wr_ref, br_ref, wi_ref, bi_ref, lam_ref,
                     y_ref, hT_ref):
    cw = cw_ref[...]
    x = xa_ref[...]
    xc = cb_ref[...] + x * cw[CONV_W - 1:CONV_W, :]
    for k in range(CONV_W - 1):
        xc = xc + cs_ref[k] * cw[k:k + 1, :]
    a, u = _lru_gates(xc, wr_ref[...], br_ref[...], wi_ref[...], bi_ref[...], lam_ref[...])
    h = a * h0_ref[...] + u
    hT_ref[...] = h
    y_ref[...] = _gelu_tanh(ga_ref[...]) * h


def _lru(proj, conv_state, h0, cw, cb, wr, br, wi, bi, lam):
    B, L, _ = proj.shape
    small = [cw, cb, wr, br, wi, bi, lam]
    if L == 1:
        proj2 = proj.reshape(B, -1)
        cs = jnp.transpose(conv_state, (1, 0, 2))
        tb = min(B, 128)
        full = lambda s: pl.BlockSpec(s.shape, lambda i: (0,) * s.ndim)
        y, hT = pl.pallas_call(
            _lru_step_kernel,
            grid=(B // tb,),
            in_specs=[pl.BlockSpec((tb, D_A), lambda i: (i, 0)),
                      pl.BlockSpec((tb, D_A), lambda i: (i, 1)),
                      pl.BlockSpec((CONV_W - 1, tb, D_A), lambda i: (0, i, 0)),
                      pl.BlockSpec((tb, D_A), lambda i: (i, 0))] + [full(s) for s in small],
            out_specs=[pl.BlockSpec((tb, D_A), lambda i: (i, 0)), pl.BlockSpec((tb, D_A), lambda i: (i, 0))],
            out_shape=[jax.ShapeDtypeStruct((B, D_A), F32), jax.ShapeDtypeStruct((B, D_A), F32)],
            compiler_params=_cparams(("parallel",)),
        )(proj2, proj2, cs, h0, *small)
        return y.reshape(B, 1, D_A), hT
    TL = min(LRU_TL, L)
    G = B
    full = lambda s: pl.BlockSpec(s.shape, lambda i: (0,) * s.ndim)
    y, hT = pl.pallas_call(
        functools.partial(_lru_seq_kernel, TL),
        grid=(L // TL,),
        in_specs=[pl.BlockSpec((G, TL, 2 * D_A), lambda i: (0, i, 0)),
                  pl.BlockSpec((G, CONV_W - 1, D_A), lambda i: (0, 0, 0)),
                  pl.BlockSpec((G, 1, D_A), lambda i: (0, 0, 0))] + [full(s) for s in small],
        out_specs=[pl.BlockSpec((G, TL, D_A), lambda i: (0, i, 0)),
                   pl.BlockSpec((G, 1, D_A), lambda i: (0, 0, 0))],
        out_shape=[jax.ShapeDtypeStruct((B, L, D_A), BF16), jax.ShapeDtypeStruct((B, 1, D_A), F32)],
        scratch_shapes=[pltpu.VMEM((G, SUBLANES + TL, D_A), F32),
                        pltpu.VMEM((G, TL, D_A), F32), pltpu.VMEM((G, TL, D_A), F32),
                        pltpu.VMEM((G, TL, D_A), F32), pltpu.VMEM((G, 1, D_A), F32)],
        compiler_params=_cparams(("arbitrary",)),
    )(proj, conv_state, h0.reshape(B, 1, D_A), *small)
    return y, hT.reshape(B, D_A)


def _rwkv_prep(xs, w0, w2p, a0, a2p, g2, k_k, k_a, r_k, hsel):
    r = xs[:, 0:D_B]
    k = xs[:, D_B:2 * D_B]
    v = xs[:, 2 * D_B:3 * D_B]
    wa = xs[:, 3 * D_B:3 * D_B + R_W + R_A]
    gl = xs[:, 3 * D_B + R_W + R_A:]
    w_log = -_softplus(-(w0 + _mm(jnp.tanh(wa), w2p))) - 0.5
    logd = -jnp.exp(w_log)
    a = _sigmoid(a0 + _mm(wa, a2p))
    g = _mm(_sigmoid(gl), g2)
    kk = k * k_k
    kk = kk * lax.rsqrt(jnp.maximum(_mm_sel(hsel, kk * kk, HEAD_SUM_PIECES, sel_left=False), 1e-12))
    k2 = k * (1.0 + (a - 1.0) * k_a)
    beta = kk * a
    bonus = _mm_sel(hsel, r * k2 * r_k, HEAD_SUM_PIECES, sel_left=False) * v
    return r, k2, v, logd, kk, beta, g, bonus


def _rwkv_post(y, bonus, g, ln_w, ln_b, hsel):
    mean = _mm_sel(hsel, y, HEAD_SUM_PIECES, sel_left=False) * (1.0 / HD_B)
    d = y - mean
    var = _mm_sel(hsel, d * d, HEAD_SUM_PIECES, sel_left=False) * (1.0 / HD_B)
    yn = d * lax.rsqrt(var + GN_EPS_B) * ln_w + ln_b
    return (yn + bonus) * g


def _mmp(a, b, dims, passes):
    return _mm3(a, b, dims) if passes == 3 else _mm(a, b, dims)


RWKV_PASSES = dict(am=1, state=1, inv=1, apply=1, dw=1)


def _rwkv_chunk(G, T, ops, st_ref, masks, emask, store_y):
    r, k2, v, logd, kk, beta = ops
    incl, strict, lmask_bf, lastmask_bf = masks
    R = G * T
    ps = RWKV_PASSES
    cum = _mm_sel(lmask_bf, logd, 3)
    cend = _mm_sel(lastmask_bf, cum, 3)
    ecn = jnp.exp(-cum)
    P = jnp.exp(cum - logd) * kk
    Q = ecn * beta
    Kt = ecn * k2
    Rt = jnp.exp(cum) * r
    eend = jnp.exp(cend)
    lo = lax.broadcasted_iota(jnp.int32, (R, LANES), 1) < HD_B
    row_lo = lax.broadcasted_iota(jnp.int32, (LANES, G * HD_B), 0) < HD_B
    eye = None
    if T > 1:
        ri = lax.broadcasted_iota(jnp.int32, (R, R), 0)
        ci = lax.broadcasted_iota(jnp.int32, (R, R), 1)
        eye = jnp.where(ri == ci, 1.0, 0.0)
    n_rep = G * HD_B // LANES
    NP = H_B // 2
    heads = [(c, hh) for c in range(NP) for hh in range(2)]
    own = {0: lo, 1: ~lo}

    def tiled(x, xr, hh):
        t = jnp.where(lo, x, xr) if hh == 0 else jnp.where(lo, xr, x)
        return jnp.where(emask, jnp.concatenate([t] * n_rep, axis=1), 0.0)

    pair = []
    for c in range(NP):
        sl = slice(c * LANES, (c + 1) * LANES)
        vals = (P[:, sl], Rt[:, sl], Q[:, sl], Kt[:, sl], eend[:, sl])
        pair.append(dict(x=vals, xr=[pltpu.roll(x, HD_B, 1) for x in vals], V=v[:, sl]))
    AM = [_mmp(jnp.concatenate([jnp.where(own[hh], pair[c]['x'][0], 0.0),
                                jnp.where(own[hh], pair[c]['x'][1], 0.0)], axis=0),
               jnp.concatenate([pair[c]['x'][2], pair[c]['x'][3]], axis=0), NT, ps['am']) for c, hh in heads]
    Mqr = [jnp.where(incl, a[R:, :R], 0.0) for a in AM]
    Mkr = [jnp.where(incl, a[R:, R:], 0.0) for a in AM]
    MV = [_mmp(m, pair[c]['V'], NN, ps['apply']) for m, (c, hh) in zip(Mkr, heads)]
    if T > 1:
        Npow = [-jnp.where(strict, a[:R, :R], 0.0) for a in AM]
        Akp = [jnp.where(strict, a[:R, R:], 0.0) for a in AM]
        AV = [_mmp(m, pair[c]['V'], NN, ps['apply']) for m, (c, hh) in zip(Akp, heads)]
        inv = [eye + n for n in Npow]
        for _ in range(int(math.log2(T)) - 1):
            Npow = [_mmp(n, n, NN, ps['inv']) for n in Npow]
            inv = [i + _mmp(i, n, NN, ps['inv']) for i, n in zip(inv, Npow)]
    ex = [[tiled(x, xr, hh) for x, xr in zip(pair[c]['x'], pair[c]['xr'])] for c, hh in heads]
    WT = [st_ref[c] for c in range(NP)]
    PWRW = [_mmp(jnp.concatenate([e[0], e[1]], axis=0), WT[c], NT, ps['state']) for e, (c, hh) in zip(ex, heads)]
    if T > 1:
        E = [_mmp(i, pw[:R] + av, NN, ps['apply']) for i, pw, av in zip(inv, PWRW, AV)]
    else:
        E = [pw[:R] for pw in PWRW]
    Y = [pw[R:] + mv - _mmp(m, e, NN, ps['apply']) for pw, mv, m, e in zip(PWRW, MV, Mqr, E)]
    for c in range(NP):
        lhs = jnp.concatenate([jnp.where(own[hh], x, 0.0) for hh in range(2)
                               for x in (-E[2 * c + hh], pair[c]['V'])], axis=0)
        rhs = jnp.concatenate([ex[2 * c + hh][j] for hh in range(2) for j in (2, 3)], axis=0)
        dW = _mmp(lhs, rhs, TN, ps['dw'])
        erow = [jnp.sum(ex[2 * c + hh][4], axis=0, keepdims=True) * (1.0 / T) for hh in range(2)]
        st_ref[c] = (WT[c] + dW) * jnp.where(row_lo, erow[0], erow[1])
        store_y(c, jnp.where(lo, Y[2 * c], Y[2 * c + 1]))


def _rwkv_kernel(G, T, NC, seq, proj_ref, sh_ref, s0_ref, mu_ref, w0_ref, w2_ref, a0_ref, a2_ref, g2_ref,
                 kk_ref, ka_ref, rk_ref, lnw_ref, lnb_ref, y_ref, sT_ref, *scratch):
    ti = pl.program_id(1)
    TL = T * NC
    R = G * T
    st_ref = scratch[0]
    hsel = _head_sum_sel(D_B, HD_B)

    @pl.when(ti == 0)
    def _():
        for h in range(H_B):
            r0 = (h % 2) * HD_B
            for j in range(G // 2):
                st_ref[h // 2, r0:r0 + HD_B, j * LANES:(j + 1) * LANES] = jnp.concatenate(
                    [s0_ref[2 * j, h], s0_ref[2 * j + 1, h]], axis=1)

    if seq:
        pad_ref, ops_ref, gb_ref, ys_ref = scratch[1:]

        @pl.when(ti == 0)
        def _():
            pad_ref[:, SUBLANES - 1:SUBLANES, :] = sh_ref[...]

        pad_ref[:, SUBLANES:, :] = proj_ref[:, :, 2 * D_A:].astype(F32)
        pb = pad_ref[:, SUBLANES:, :]
        prev = pad_ref[:, SUBLANES - 1:SUBLANES - 1 + TL, :]
        xs = (pb + (prev - pb) * mu_ref[...][None]).reshape(G * TL, D_BPROJ)
        pad_ref[:, :SUBLANES, :] = pad_ref[:, TL:TL + SUBLANES, :]
    else:
        pb = proj_ref[:, 2 * D_A:]
        xs = pb + (sh_ref[...] - pb) * mu_ref[...]

    r, k2, v, logd, kk, beta, g, bonus = _rwkv_prep(
        xs, w0_ref[...], w2_ref[...], a0_ref[...], a2_ref[...], g2_ref[...], kk_ref[...], ka_ref[...],
        rk_ref[...], hsel)

    incl, strict, last = _chunk_masks(G, T)
    masks = (incl, strict, _to_bf16_mask(incl), _to_bf16_mask(last))
    emask = _batch_lane_mask(1, G, T, HD_B)

    if seq:
        for n, val in enumerate((r, k2, v, logd, kk, beta)):
            ops_ref[n] = val.reshape(G, TL, D_B)
        gb_ref[0] = g.reshape(G, TL, D_B)
        gb_ref[1] = bonus.reshape(G, TL, D_B)

        def chunk(c, carry):
            t0 = pl.multiple_of(c * T, T)
            ops = tuple(ops_ref[n, :, pl.ds(t0, T), :].reshape(R, D_B) for n in range(6))

            def store_y(hp, Y):
                ys_ref[:, pl.ds(t0, T), hp * LANES:(hp + 1) * LANES] = Y.reshape(G, T, LANES)

            _rwkv_chunk(G, T, ops, st_ref, masks, emask, store_y)
            return carry

        lax.fori_loop(0, NC, chunk, 0)
        y = ys_ref[...].reshape(G * TL, D_B)
        out = _rwkv_post(y, gb_ref[1].reshape(G * TL, D_B), gb_ref[0].reshape(G * TL, D_B),
                         lnw_ref[...], lnb_ref[...], hsel)
        y_ref[...] = out.reshape(G, TL, D_B).astype(y_ref.dtype)
    else:
        ys_ref = scratch[1]

        def store_y(hp, Y):
            ys_ref[:, hp * LANES:(hp + 1) * LANES] = Y

        _rwkv_chunk(G, T, (r, k2, v, logd, kk, beta), st_ref, masks, emask, store_y)
        y_ref[...] = _rwkv_post(ys_ref[...], bonus, g, lnw_ref[...], lnb_ref[...], hsel)

    @pl.when(ti == pl.num_programs(1) - 1)
    def _():
        for h in range(H_B):
            r0 = (h % 2) * HD_B
            for b in range(G):
                sT_ref[b, h] = st_ref[h // 2, r0:r0 + HD_B, b * HD_B:(b + 1) * HD_B]


def _rwkv(proj, shift0, s0, mu, w0, w2p, a0, a2p, g2, k_k, k_a, r_k, ln_w, ln_b):
    B, L, _ = proj.shape
    G = G_ROWS
    NB = B // G
    small = [mu, w0, w2p, a0, a2p, g2, k_k, k_a, r_k, ln_w, ln_b]
    full = lambda s: pl.BlockSpec(s.shape, lambda b, i: (0,) * s.ndim)
    st_dims = (H_B // 2, 2 * HD_B, G * HD_B)
    s0l = s0
    st_spec = pl.BlockSpec((G, H_B, HD_B, HD_B), lambda b, i: (b, 0, 0, 0))
    st_shape = jax.ShapeDtypeStruct(s0.shape, F32)
    unpair = lambda s: s
    if L == 1:
        T, NC = 1, 1
        kern = functools.partial(_rwkv_kernel, G, T, NC, False)
        y, sT = pl.pallas_call(
            kern,
            grid=(NB, 1),
            in_specs=[pl.BlockSpec((G, IN_AB), lambda b, i: (b, 0)),
                      pl.BlockSpec((G, D_BPROJ), lambda b, i: (b, 0)), st_spec] + [full(s) for s in small],
            out_specs=[pl.BlockSpec((G, D_B), lambda b, i: (b, 0)), st_spec],
            out_shape=[jax.ShapeDtypeStruct((B, D_B), F32), st_shape],
            scratch_shapes=[pltpu.VMEM(st_dims, F32), pltpu.VMEM((G, D_B), F32)],
            compiler_params=_cparams(("parallel", "arbitrary")),
        )(proj.reshape(B, IN_AB), shift0, s0l, *small)
        return y.reshape(B, 1, D_B), unpair(sT)
    T = min(T_CHUNK, L)
    TL = min(TL_BLOCK, L)
    NC = TL // T
    kern = functools.partial(_rwkv_kernel, G, T, NC, True)
    y, sT = pl.pallas_call(
        kern,
        grid=(NB, L // TL),
        in_specs=[pl.BlockSpec((G, TL, IN_AB), lambda b, i: (b, i, 0)),
                  pl.BlockSpec((G, 1, D_BPROJ), lambda b, i: (b, 0, 0)), st_spec] + [full(s) for s in small],
        out_specs=[pl.BlockSpec((G, TL, D_B), lambda b, i: (b, i, 0)), st_spec],
        out_shape=[jax.ShapeDtypeStruct((B, L, D_B), BF16), st_shape],
        scratch_shapes=[pltpu.VMEM(st_dims, F32),
                        pltpu.VMEM((G, SUBLANES + TL, D_BPROJ), F32),
                        pltpu.VMEM((6, G, TL, D_B), F32),
                        pltpu.VMEM((2, G, TL, D_B), F32),
                        pltpu.VMEM((G, TL, D_B), F32)],
        compiler_params=_cparams(("parallel", "arbitrary")),
    )(proj, shift0.reshape(B, 1, D_BPROJ), s0l, *small)
    return y, unpair(sT)


CD_Z, CD_XBC, CD_Q, CD_K, CD_V, CD_G, CD_S = 0, 512, 1536, 1792, 2048, 2560, 3072


def _ssd_chunk(G, T, xs, Bm, Cm, dt_all, la_all, st_ref, masks, emaskN, store_y):
    incl, lmask_bf, lastmask_bf = masks
    R = G * T
    cum = _mm_sel(lmask_bf, la_all, 3)
    cend = _mm_sel(lastmask_bf, cum, 3)
    cumT = cum.T
    wend = jnp.exp(cend - cum) * dt_all
    ecum = jnp.exp(cum)
    dend = jnp.exp(cend)
    lo = lax.broadcasted_iota(jnp.int32, (R, LANES), 1) < P_C

    def per_head(x):
        return jnp.concatenate([jnp.where(lo, x[:, 2 * c:2 * c + 1], x[:, 2 * c + 1:2 * c + 2])
                                for c in range(H_C // 2)], axis=1)

    tile = lambda x: jnp.where(emaskN, jnp.concatenate([x] * G, axis=1), 0.0)
    xdt = xs * per_head(dt_all)
    xw = xs * per_head(wend)
    ecf = per_head(ecum)
    Bg = [Bm[:, g * N_C:(g + 1) * N_C] for g in range(G_C)]
    Cg = [Cm[:, g * N_C:(g + 1) * N_C] for g in range(G_C)]
    CB = [_mm(c, b, NT) for c, b in zip(Cg, Bg)]
    ST = [st_ref[g] for g in range(G_C)]
    CS = [_mm(tile(c), s, NT) for c, s in zip(Cg, ST)]
    W = [CB[h // HG_C] * jnp.exp(jnp.where(incl, cum[:, h:h + 1] - cumT[h:h + 1, :], -1e30))
         for h in range(H_C)]
    intra = [_mm(W[h], xdt[:, (h // 2) * LANES:(h // 2 + 1) * LANES]) for h in range(H_C)]
    drow = [jnp.sum(jnp.where(emaskN, dend[:, h:h + 1], 0.0), axis=0, keepdims=True) * (1.0 / T)
            for h in range(H_C)]
    for c in range(H_C // 2):
        sl = slice(c * LANES, (c + 1) * LANES)
        gsl = slice((c % (HG_C // 2)) * LANES, (c % (HG_C // 2) + 1) * LANES)
        store_y(c, jnp.where(lo, intra[2 * c], intra[2 * c + 1]) + ecf[:, sl] * CS[c // (HG_C // 2)][:, gsl])
    for g in range(G_C):
        W_g = HG_C * P_C
        dS = _mm(xw[:, g * W_g:(g + 1) * W_g], tile(Bg[g]), TN)
        st_ref[g] = jnp.concatenate(
            [ST[g][hh * P_C:(hh + 1) * P_C] * drow[g * HG_C + hh] + dS[hh * P_C:(hh + 1) * P_C]
             for hh in range(HG_C)], axis=0)


def _gla_chunk(G, T, q, k, v, la, st_ref, masks, emask, store_o):
    incl, lmask_bf, lastmask_bf = masks
    R = G * T
    bc = _mm_sel(lmask_bf, la, 3)
    bend = _mm_sel(lastmask_bf, bc, 3)
    qe = q * jnp.exp(bc)
    kn = k * jnp.exp(-bc)
    kw = k * jnp.exp(bend - bc)
    dend = jnp.exp(bend)
    lo = lax.broadcasted_iota(jnp.int32, (R, LANES), 1) < DK_D
    own = {0: lo, 1: ~lo}
    n_rep = G * DK_D // LANES

    def tiled(x, xr, hh):
        t = jnp.where(lo, x, xr) if hh == 0 else jnp.where(lo, xr, x)
        return jnp.where(emask, jnp.concatenate([t] * n_rep, axis=1), 0.0)

    heads = [(c, hh) for c in range(H_D // 2) for hh in range(2)]
    pair = []
    for c in range(H_D // 2):
        sl = slice(c * LANES, (c + 1) * LANES)
        vals = (qe[:, sl], kw[:, sl], dend[:, sl])
        pair.append(dict(x=vals, xr=[pltpu.roll(x, DK_D, 1) for x in vals], kn=kn[:, sl]))
    vh = [v[:, h * DV_D:(h + 1) * DV_D] for h in range(H_D)]
    att = [jnp.where(incl, _mm(jnp.where(own[hh], pair[c]['x'][0], 0.0), pair[c]['kn'], NT), 0.0)
           for c, hh in heads]
    ex = [[tiled(x, xr, hh) for x, xr in zip(pair[c]['x'], pair[c]['xr'])] for c, hh in heads]
    ST = [st_ref[h] for h in range(H_D)]
    o = [_mm(a, vv) + _mm(e[0], s, NT) for a, vv, e, s in zip(att, vh, ex, ST)]
    for h in range(H_D):
        drow = jnp.sum(ex[h][2], axis=0, keepdims=True) * (1.0 / T)
        st_ref[h] = ST[h] * drow + _mm(vh[h], ex[h][1], TN)
        store_o(h, o[h])


def _cd_prep_small(small, dtb, a_neg, aup, ab):
    dt_all = _softplus(small + dtb)
    la_all = dt_all * a_neg
    la_gla = -_softplus(-(_mm(small, aup) + ab)) * (1.0 / GATE_NORM)
    return dt_all, la_all, la_gla


def _cd_post(y, xs, z, o, gg, dskip, ssd_norm, gla_norm):
    y_c = _rms((y + dskip * xs) * _silu(z), ssd_norm)
    outs = [y_c]
    for h in range(H_D):
        sl = slice(h * DV_D, (h + 1) * DV_D)
        outs.append(_rms(o[:, sl], gla_norm) * _silu(gg[:, sl]))
    return outs


def _cd_kernel(G, T, NC, seq, proj_ref, cs_ref, ssd0_ref, gla0_ref, cw_ref, cb_ref, dtb_ref, aneg_ref, dskip_ref,
               ssdn_ref, aup_ref, ab_ref, glan_ref, y_ref, ssdT_ref, glaT_ref, *scratch):
    ti = pl.program_id(1)
    TL = T * NC
    R = G * T
    sst_ref, gst_ref = scratch[0], scratch[1]

    @pl.when(ti == 0)
    def _():
        for h in range(H_C):
            r0 = (h % HG_C) * P_C
            for b in range(G):
                sst_ref[h // HG_C, r0:r0 + P_C, b * N_C:(b + 1) * N_C] = ssd0_ref[b, h]
        for h in range(H_D):
            for j in range(G // 2):
                two = jnp.concatenate([gla0_ref[2 * j, h], gla0_ref[2 * j + 1, h]], axis=0)
                gst_ref[h, :, j * LANES:(j + 1) * LANES] = two.T

    cw = cw_ref[...]
    if seq:
        pad_ref, xbc_ref, sm_ref, ys_ref, os_ref = scratch[2:]

        @pl.when(ti == 0)
        def _():
            pad_ref[:, SUBLANES - (CONV_W - 1):SUBLANES, :] = cs_ref[...]

        pad_ref[:, SUBLANES:, :] = proj_ref[:, :, CD_XBC:CD_XBC + D_XBC].astype(F32)

        def conv_b(b, carry):
            for c0 in range(0, D_XBC, CONV_COLS):
                cols = slice(c0, c0 + CONV_COLS)
                acc = cb_ref[:, cols]
                for kq in range(CONV_W):
                    off = SUBLANES - (CONV_W - 1) + kq
                    acc = acc + pad_ref[b, off:off + TL, cols] * cw_ref[kq:kq + 1, cols]
                xbc_ref[b, :, cols] = _silu(acc)
            return carry

        lax.fori_loop(0, G, conv_b, 0)
        pad_ref[:, :SUBLANES, :] = pad_ref[:, TL:TL + SUBLANES, :]
        small = proj_ref[:, :, CD_S:].astype(F32).reshape(G * TL, LANES)
    else:
        x = proj_ref[:, CD_XBC:CD_XBC + D_XBC]
        xc = cb_ref[...] + x * cw[CONV_W - 1:CONV_W, :]
        for kq in range(CONV_W - 1):
            xc = xc + cs_ref[kq] * cw[kq:kq + 1, :]
        xbc = _silu(xc)
        small = proj_ref[:, CD_S:]

    dt_all, la_all, la_gla = _cd_prep_small(small, dtb_ref[...], aneg_ref[...], aup_ref[...], ab_ref[...])

    incl, _, last = _chunk_masks(G, T)
    masks = (incl, _to_bf16_mask(incl), _to_bf16_mask(last))
    emaskN = _batch_lane_mask(1, G, T, N_C)
    emask1 = _batch_lane_mask(1, G, T, DK_D)
    scale = DK_D ** -0.5

    if seq:
        sm_ref[0] = dt_all.reshape(G, TL, LANES)
        sm_ref[1] = la_all.reshape(G, TL, LANES)
        sm_ref[2] = la_gla[:, :LANES].reshape(G, TL, LANES)
        sm_ref[3] = la_gla[:, LANES:].reshape(G, TL, LANES)

        def chunk(c, carry):
            t0 = pl.multiple_of(c * T, T)
            rows = lambda ref, lo, hi: ref[:, pl.ds(t0, T), lo:hi].astype(F32).reshape(R, hi - lo)
            xs = rows(xbc_ref, 0, D_C)
            Bm = rows(xbc_ref, D_C, D_C + G_C * N_C)
            Cm = rows(xbc_ref, D_C + G_C * N_C, D_XBC)
            dtc = sm_ref[0, :, pl.ds(t0, T), :].reshape(R, LANES)
            lac = sm_ref[1, :, pl.ds(t0, T), :].reshape(R, LANES)

            def store_y(hp, yv):
                ys_ref[:, pl.ds(t0, T), hp * LANES:(hp + 1) * LANES] = yv.reshape(G, T, LANES)

            _ssd_chunk(G, T, xs, Bm, Cm, dtc, lac, sst_ref, masks, emaskN, store_y)

            q = rows(proj_ref, CD_Q, CD_Q + K_D) * scale
            k = rows(proj_ref, CD_K, CD_K + K_D)
            v = rows(proj_ref, CD_V, CD_V + V_D)
            lag = jnp.concatenate([sm_ref[2, :, pl.ds(t0, T), :].reshape(R, LANES),
                                   sm_ref[3, :, pl.ds(t0, T), :].reshape(R, LANES)], axis=1)

            def store_o(h, ov):
                os_ref[:, pl.ds(t0, T), h * DV_D:(h + 1) * DV_D] = ov.reshape(G, T, DV_D)

            _gla_chunk(G, T, q, k, v, lag, gst_ref, masks, emask1, store_o)
            return carry

        lax.fori_loop(0, NC, chunk, 0)
        def post_b(b, carry):
            outs = _cd_post(ys_ref[b], xbc_ref[b, :, :D_C], proj_ref[b, :, CD_Z:CD_Z + D_C].astype(F32),
                            os_ref[b], proj_ref[b, :, CD_G:CD_G + V_D].astype(F32),
                            dskip_ref[...], ssdn_ref[...], glan_ref[...])
            y_ref[b, :, :D_C] = outs[0].astype(y_ref.dtype)
            for h in range(H_D):
                y_ref[b, :, D_C + h * DV_D:D_C + (h + 1) * DV_D] = outs[1 + h].astype(y_ref.dtype)
            return carry

        lax.fori_loop(0, G, post_b, 0)
    else:
        ys_ref, os_ref = scratch[2:]

        def store_y(hp, yv):
            ys_ref[:, hp * LANES:(hp + 1) * LANES] = yv

        def store_o(h, ov):
            os_ref[:, h * DV_D:(h + 1) * DV_D] = ov

        _ssd_chunk(G, T, xbc[:, :D_C], xbc[:, D_C:D_C + G_C * N_C], xbc[:, D_C + G_C * N_C:], dt_all, la_all,
                   sst_ref, masks, emaskN, store_y)
        _gla_chunk(G, T, proj_ref[:, CD_Q:CD_Q + K_D] * scale, proj_ref[:, CD_K:CD_K + K_D],
                   proj_ref[:, CD_V:CD_V + V_D], la_gla, gst_ref, masks, emask1, store_o)
        outs = _cd_post(ys_ref[...], xbc[:, :D_C], proj_ref[:, CD_Z:CD_Z + D_C], os_ref[...],
                        proj_ref[:, CD_G:CD_G + V_D], dskip_ref[...], ssdn_ref[...], glan_ref[...])
        y_ref[:, :D_C] = outs[0]
        for h in range(H_D):
            y_ref[:, D_C + h * DV_D:D_C + (h + 1) * DV_D] = outs[1 + h]

    @pl.when(ti == pl.num_programs(1) - 1)
    def _():
        for h in range(H_C):
            r0 = (h % HG_C) * P_C
            for b in range(G):
                ssdT_ref[b, h] = sst_ref[h // HG_C, r0:r0 + P_C, b * N_C:(b + 1) * N_C]
        for h in range(H_D):
            for j in range(G // 2):
                two = gst_ref[h, :, j * LANES:(j + 1) * LANES].T
                glaT_ref[2 * j, h] = two[:DK_D]
                glaT_ref[2 * j + 1, h] = two[DK_D:]


def _cd(proj, conv_state, ssd0, gla0, cw, cb, dtb, a_neg, dskip, ssd_norm, aup, ab, gla_norm):
    B, L, _ = proj.shape
    G = G_ROWS
    NB = B // G
    small = [cw, cb, dtb, a_neg, dskip, ssd_norm, aup, ab, gla_norm]
    full = lambda s: pl.BlockSpec(s.shape, lambda b, i: (0,) * s.ndim)
    ssd_dims = (G_C, HG_C * P_C, G * N_C)
    ssd0l, gla0l = ssd0, gla0
    ssd_spec = pl.BlockSpec((G, H_C, P_C, N_C), lambda b, i: (b, 0, 0, 0))
    gla_spec = pl.BlockSpec((G, H_D, DK_D, DV_D), lambda b, i: (b, 0, 0, 0))
    st_shapes = [jax.ShapeDtypeStruct(ssd0.shape, F32), jax.ShapeDtypeStruct(gla0.shape, F32)]
    st_scratch = [pltpu.VMEM(ssd_dims, F32), pltpu.VMEM((H_D, DV_D, G * DK_D), F32)]
    DO = D_C + V_D
    if L == 1:
        kern = functools.partial(_cd_kernel, G, 1, 1, False)
        y, ssdT, glaT = pl.pallas_call(
            kern,
            grid=(NB, 1),
            in_specs=[pl.BlockSpec((G, IN_CD_PAD), lambda b, i: (b, 0)),
                      pl.BlockSpec((CONV_W - 1, G, D_XBC), lambda b, i: (0, b, 0)),
                      ssd_spec, gla_spec] + [full(s) for s in small],
            out_specs=[pl.BlockSpec((G, DO), lambda b, i: (b, 0)), ssd_spec, gla_spec],
            out_shape=[jax.ShapeDtypeStruct((B, DO), F32)] + st_shapes,
            scratch_shapes=st_scratch + [pltpu.VMEM((G, D_C), F32), pltpu.VMEM((G, V_D), F32)],
            compiler_params=_cparams(("parallel", "arbitrary")),
        )(proj.reshape(B, IN_CD_PAD), jnp.transpose(conv_state, (1, 0, 2)), ssd0l, gla0l, *small)
        y = y.reshape(B, 1, DO)
    else:
        T = min(T_CHUNK, L)
        TL = min(TL_BLOCK, L)
        NC = TL // T
        kern = functools.partial(_cd_kernel, G, T, NC, True)
        y, ssdT, glaT = pl.pallas_call(
            kern,
            grid=(NB, L // TL),
            in_specs=[pl.BlockSpec((G, TL, IN_CD_PAD), lambda b, i: (b, i, 0)),
                      pl.BlockSpec((G, CONV_W - 1, D_XBC), lambda b, i: (b, 0, 0)),
                      ssd_spec, gla_spec] + [full(s) for s in small],
            out_specs=[pl.BlockSpec((G, TL, DO), lambda b, i: (b, i, 0)), ssd_spec, gla_spec],
            out_shape=[jax.ShapeDtypeStruct((B, L, DO), BF16)] + st_shapes,
            scratch_shapes=st_scratch + [pltpu.VMEM((G, SUBLANES + TL, D_XBC), F32),
                                         pltpu.VMEM((G, TL, D_XBC), F32),
                                         pltpu.VMEM((4, G, TL, LANES), F32),
                                         pltpu.VMEM((G, TL, D_C), F32),
                                         pltpu.VMEM((G, TL, V_D), F32)],
            compiler_params=_cparams(("parallel", "arbitrary")),
        )(proj, conv_state, ssd0l, gla0l, *small)
    return y, ssdT, glaT


def _regroup_kernel(groups, w_ref, o_ref):
    at = 0
    for lo, hi in groups:
        o_ref[:, at:at + hi - lo] = w_ref[:, lo:hi].astype(o_ref.dtype)
        at += hi - lo
    o_ref[:, at:] = jnp.zeros((o_ref.shape[0], o_ref.shape[1] - at), o_ref.dtype)


def _regroup_cols(w, groups, n_out):
    K = w.shape[0]
    return pl.pallas_call(
        functools.partial(_regroup_kernel, groups),
        out_shape=jax.ShapeDtypeStruct((K, n_out), BF16),
        compiler_params=pltpu.CompilerParams(vmem_limit_bytes=VMEM_LIMIT),
    )(w)


def _block_diag(w):
    eye = jnp.eye(N_BLK_A, dtype=w.dtype)
    return jnp.einsum('nij,nm->nimj', w, eye).reshape(D_A, D_A)


def _row(v):
    return v.reshape(1, -1).astype(F32)


def _prep_params(p):
    q = {}
    j = 0
    q['w_in_ab'] = p['w_in_ab'][j].astype(BF16)
    q['w_out_ab'] = p['w_out_ab'][j].astype(BF16)
    q['lru'] = [p['lru_conv_w'][j], _row(p['lru_conv_b'][j]),
                _block_diag(p['lru_w_r'][j]).astype(BF16), _row(p['lru_b_r'][j]),
                _block_diag(p['lru_w_i'][j]).astype(BF16), _row(p['lru_b_i'][j]), _row(p['lru_lambda'][j])]
    zw = jnp.zeros((R_W, D_B), F32)
    w2p = jnp.concatenate([p['rwkv_w2'][j], zw], axis=0).astype(BF16)
    a2p = jnp.concatenate([zw, p['rwkv_a2'][j]], axis=0).astype(BF16)
    q['rwkv'] = [_row(p['rwkv_mu'][j]), _row(p['rwkv_w0'][j]), w2p, _row(p['rwkv_a0'][j]), a2p,
                 p['rwkv_g2'][j].astype(BF16), _row(p['rwkv_k_k'][j]), _row(p['rwkv_k_a'][j]),
                 _row(p['rwkv_r_k'][j]), _row(p['rwkv_ln_w'][j]), _row(p['rwkv_ln_b'][j])]
    q['ffn'] = [p['ffn_w_gate'][j].astype(BF16), p['ffn_w_up'][j].astype(BF16), p['ffn_w_down'][j].astype(BF16)]
    w = p['w_in_cd'][j]
    o_z, o_xbc, o_dt = 0, D_C, D_C + D_XBC
    o_q = o_dt + H_C
    o_k = o_q + K_D
    o_v = o_k + K_D
    o_g = o_v + V_D
    o_al = o_g + V_D
    q['w_in_cd'] = _regroup_cols(w, [(o_z, o_dt), (o_q, o_al), (o_dt, o_q), (o_al, o_al + R_ALPHA)], IN_CD_PAD)
    q['w_out_cd'] = p['w_out_cd'][j].astype(BF16)
    lane_pad = lambda v: jnp.concatenate([v.astype(F32), jnp.zeros((LANES - v.shape[0],), F32)]).reshape(1, LANES)
    aup = jnp.zeros((LANES, K_D), F32).at[H_C:H_C + R_ALPHA].set(p['gla_alpha_up'][j]).astype(BF16)
    q['cd'] = [p['ssd_conv_w'][j], _row(p['ssd_conv_b'][j]), lane_pad(p['ssd_dt_bias'][j]),
               lane_pad(-jnp.exp(p['ssd_a_log'][j].astype(F32))), _row(jnp.repeat(p['ssd_d'][j], P_C)),
               _row(p['ssd_norm_w'][j]), aup, _row(p['gla_alpha_b'][j]), _row(p['gla_norm_w'][j])]
    q['router'] = jnp.concatenate([p['moe_router'][j], jnp.zeros((D_MODEL, LANES - N_EXP), F32)], axis=1)
    q['moe'] = [p['moe_w_gate'][j].astype(BF16), p['moe_w_up'][j].astype(BF16), p['moe_w_down'][j].astype(BF16)]
    q['norm_mix'] = [_row(p['norm_mix'][i]) for i in range(DEPTH)]
    q['norm_ffn'] = [_row(p['norm_ffn'][i]) for i in range(DEPTH)]
    q['norm_ple'] = [_row(p['norm_ple'][i]) for i in range(DEPTH)]
    q['w_ple_gate'] = [p['w_ple_gate'][i].astype(BF16) for i in range(DEPTH)]
    q['w_ple_proj'] = [p['w_ple_proj'][i].astype(BF16) for i in range(DEPTH)]
    q['norm_final'] = _row(p['norm_final'])
    return q


def _trunk(x, p, states, q):
    lru_conv, lru_h, rwkv_shift, rwkv_wkv, ssd_conv, ssd_st, gla_st = states
    B, L, D = x.shape
    M = B * L
    h = x.reshape(M, D)
    p2 = p.reshape(DEPTH, M, D_PLE)

    proj_dtype = BF16 if L > 1 else F32
    proj = _norm_mm(h, q['norm_mix'][0], q['w_in_ab'], IN_AB, proj_dtype).reshape(B, L, IN_AB)
    y_a, h_new = _lru(proj, lru_conv[0], lru_h[0], *q['lru'])
    y_b, wkv_new = _rwkv(proj, rwkv_shift[0], rwkv_wkv[0], *q['rwkv'])
    if L >= CONV_W - 1:
        conv_new = proj[:, L - (CONV_W - 1):, :D_A].astype(F32)
    else:
        conv_new = jnp.concatenate([lru_conv[0][:, L:], proj[:, :, :D_A]], axis=1)
    shift_new = proj[:, L - 1, 2 * D_A:].astype(F32)
    w_out = q['w_out_ab']
    h = _channel_mix(h, y_a.reshape(M, D_A), y_b.reshape(M, D_B), w_out[:D_A], w_out[D_A:],
                     q['norm_ffn'][0], *q['ffn'], q['norm_ple'][0], q['w_ple_gate'][0], p2[0], q['w_ple_proj'][0])

    proj = _norm_mm(h, q['norm_mix'][1], q['w_in_cd'], IN_CD_PAD, proj_dtype).reshape(B, L, IN_CD_PAD)
    y_cd, ssd_new, gla_new = _cd(proj, ssd_conv[0], ssd_st[0], gla_st[0], *q['cd'])
    if L >= CONV_W - 1:
        sconv_new = proj[:, L - (CONV_W - 1):, CD_XBC:CD_XBC + D_XBC].astype(F32)
    else:
        sconv_new = jnp.concatenate([ssd_conv[0][:, L:], proj[:, :, CD_XBC:CD_XBC + D_XBC]], axis=1)
    h, route, routeT, meta = _router(h, y_cd.reshape(M, D_C + V_D), q['w_out_cd'], q['norm_ffn'][1], q['router'])
    h = _moe(h, q['norm_ffn'][1], route, routeT, meta, *q['moe'])
    y = _ple(h, q['norm_ple'][1], q['w_ple_gate'][1], p2[1], q['w_ple_proj'][1], q['norm_final'], True)

    new_states = (conv_new[None], h_new[None], shift_new[None], wkv_new[None],
                  sconv_new[None], ssd_new[None], gla_new[None])
    return y.reshape(B, L, D), new_states


def kernel(x_prompt, x_sample, p_prompt, p_sample, state_lru_conv, state_lru_h, state_rwkv_shift,
           state_rwkv_wkv, state_ssd_conv, state_ssd, state_gla, norm_mix, norm_ffn, norm_ple,
           w_ple_gate, w_ple_proj, norm_final, w_in_ab, w_out_ab, lru_conv_w, lru_conv_b, lru_w_r,
           lru_b_r, lru_w_i, lru_b_i, lru_lambda, rwkv_mu, rwkv_w0, rwkv_w2, rwkv_a0, rwkv_a2,
           rwkv_g2, rwkv_k_k, rwkv_k_a, rwkv_r_k, rwkv_ln_w, rwkv_ln_b, ffn_w_gate, ffn_w_up,
           ffn_w_down, w_in_cd, w_out_cd, ssd_conv_w, ssd_conv_b, ssd_dt_bias, ssd_a_log, ssd_d,
           ssd_norm_w, gla_alpha_up, gla_alpha_b, gla_norm_w, moe_router, moe_w_gate, moe_w_up,
           moe_w_down):
    prm = dict(
        norm_mix=norm_mix, norm_ffn=norm_ffn, norm_ple=norm_ple, w_ple_gate=w_ple_gate,
        w_ple_proj=w_ple_proj, norm_final=norm_final, w_in_ab=w_in_ab, w_out_ab=w_out_ab,
        lru_conv_w=lru_conv_w, lru_conv_b=lru_conv_b, lru_w_r=lru_w_r, lru_b_r=lru_b_r,
        lru_w_i=lru_w_i, lru_b_i=lru_b_i, lru_lambda=lru_lambda, rwkv_mu=rwkv_mu, rwkv_w0=rwkv_w0,
        rwkv_w2=rwkv_w2, rwkv_a0=rwkv_a0, rwkv_a2=rwkv_a2, rwkv_g2=rwkv_g2, rwkv_k_k=rwkv_k_k,
        rwkv_k_a=rwkv_k_a, rwkv_r_k=rwkv_r_k, rwkv_ln_w=rwkv_ln_w, rwkv_ln_b=rwkv_ln_b,
        ffn_w_gate=ffn_w_gate, ffn_w_up=ffn_w_up, ffn_w_down=ffn_w_down, w_in_cd=w_in_cd,
        w_out_cd=w_out_cd, ssd_conv_w=ssd_conv_w, ssd_conv_b=ssd_conv_b, ssd_dt_bias=ssd_dt_bias,
        ssd_a_log=ssd_a_log, ssd_d=ssd_d, ssd_norm_w=ssd_norm_w, gla_alpha_up=gla_alpha_up,
        gla_alpha_b=gla_alpha_b, gla_norm_w=gla_norm_w, moe_router=moe_router,
        moe_w_gate=moe_w_gate, moe_w_up=moe_w_up, moe_w_down=moe_w_down)
    q = _prep_params(prm)
    states_s = (state_lru_conv, state_lru_h, state_rwkv_shift, state_rwkv_wkv,
                state_ssd_conv, state_ssd, state_gla)
    n_prompt = x_prompt.shape[0]
    states_p = tuple(jnp.zeros((s.shape[0], n_prompt) + s.shape[2:], s.dtype) for s in states_s)
    y_p, st_p = _trunk(x_prompt, p_prompt, states_p, q)
    y_s, st_s = _trunk(x_sample, p_sample, states_s, q)
    return (y_p, y_s) + tuple(st_p) + tuple(st_s)
```

```python
import functools
import math

import jax
import jax.numpy as jnp
from jax import lax
from jax.experimental import pallas as pl
from jax.experimental.pallas import tpu as pltpu

F32, BF16 = jnp.float32, jnp.bfloat16

D_MODEL = 1024
DEPTH = 2
D_PLE = 256
EPS = 1e-6
CONV_W = 4
D_A = 512
N_BLK_A = 8
BW_A = D_A // N_BLK_A
LRU_C = 8.0
H_B = 8
HD_B = 64
D_B = H_B * HD_B
R_W = 64
R_A = 64
R_G = 128
D_BPROJ = 3 * D_B + R_W + R_A + R_G
GN_EPS_B = 64e-5
IN_AB = 2 * D_A + D_BPROJ
H_C = 8
P_C = 64
D_C = H_C * P_C
G_C = 2
HG_C = H_C // G_C
N_C = 128
D_XBC = D_C + 2 * G_C * N_C
H_D = 4
DK_D = 64
DV_D = 128
K_D = H_D * DK_D
V_D = H_D * DV_D
R_ALPHA = 16
GATE_NORM = 16.0
D_FF = 2816
N_EXP = 8
D_FF_E = 1408
IN_CD_PAD = D_C + D_XBC + 2 * K_D + 2 * V_D + 128

LANES = 128
SUBLANES = 8
VMEM_LIMIT = 56 * 1024 * 1024

G_ROWS = SUBLANES
T_CHUNK = 16
TL_BLOCK = 64
TM_TOK = 1024
TM_FFN = 512
TM_MOE = 1024
MOE_RB = 128
MOE_SORT_ROWS = 256
MOE_UNSORT_COLS = 512
TOP_K = 2
HEAD_SUM_PIECES = 2
CONV_COLS = 256
LRU_TL = 256

NN = ((1,), (0,))
NT = ((1,), (1,))
TN = ((0,), (0,))


def _cparams(sem):
    return pltpu.CompilerParams(dimension_semantics=sem, vmem_limit_bytes=VMEM_LIMIT)


def _rms(x, g):
    return x * lax.rsqrt(jnp.mean(x * x, axis=-1, keepdims=True) + EPS) * g


def _sigmoid(x):
    return jax.nn.sigmoid(x)


def _silu(x):
    return x * jax.nn.sigmoid(x)


def _softplus(x):
    return jnp.maximum(x, 0.0) + jnp.log1p(jnp.exp(-jnp.abs(x)))


def _gelu_tanh(x):
    c = math.sqrt(2.0 / math.pi)
    return 0.5 * x * (1.0 + jnp.tanh(c * (x + 0.044715 * (x * x * x))))


def _d(a, b, dims=NN):
    return lax.dot_general(a, b, (dims, ((), ())), preferred_element_type=F32)


def _mm(a, b, dims=NN):
    return _d(a.astype(BF16), b.astype(BF16), dims)


def _split2(x):
    hi = x.astype(BF16)
    lo = (x - hi.astype(F32)).astype(BF16)
    return hi, lo


def _mm3(a, b, dims=NN):
    ah, al = _split2(a)
    bh, bl = _split2(b)
    return _d(ah, bh, dims) + (_d(ah, bl, dims) + _d(al, bh, dims))


def _mm_sel(sel, x, pieces, sel_left=True, dims=NN):
    out = None
    rest = x
    for i in range(pieces):
        p = rest.astype(BF16)
        if i + 1 < pieces:
            rest = rest - p.astype(F32)
        t = _d(sel, p, dims) if sel_left else _d(p, sel, dims)
        out = t if out is None else out + t
    return out


def _chunk_masks(G, T):
    R = G * T
    ri = lax.broadcasted_iota(jnp.int32, (R, R), 0)
    ci = lax.broadcasted_iota(jnp.int32, (R, R), 1)
    same = (ri // T) == (ci // T)
    incl = same & (ci <= ri)
    strict = same & (ci < ri)
    last = ci == (ri // T) * T + (T - 1)
    return incl, strict, last


def _to_bf16_mask(m):
    return jnp.where(m, 1.0, 0.0).astype(BF16)


def _batch_lane_mask(n_stack, G, T, W):
    R = G * T
    r = lax.broadcasted_iota(jnp.int32, (n_stack * R, G * W), 0)
    c = lax.broadcasted_iota(jnp.int32, (n_stack * R, G * W), 1)
    return ((r % R) // T) == (c // W)


def _head_sum_sel(width, head):
    r = lax.broadcasted_iota(jnp.int32, (width, width), 0)
    c = lax.broadcasted_iota(jnp.int32, (width, width), 1)
    return _to_bf16_mask((r // head) == (c // head))


def _norm_mm_kernel(x_ref, g_ref, w_ref, o_ref, xn_ref):
    @pl.when(pl.program_id(1) == 0)
    def _():
        xn_ref[...] = _rms(x_ref[...], g_ref[...]).astype(BF16)

    o_ref[...] = _d(xn_ref[...], w_ref[...]).astype(o_ref.dtype)


def _norm_mm(x, g, w, tn, out_dtype):
    M, K = x.shape
    N = w.shape[1]
    tm = min(TM_TOK, M)
    mode = dict(pipeline_mode=pl.Buffered(1)) if tn == N else {}
    return pl.pallas_call(
        _norm_mm_kernel,
        grid=(M // tm, N // tn),
        in_specs=[pl.BlockSpec((tm, K), lambda i, j: (i, 0)),
                  pl.BlockSpec((1, K), lambda i, j: (0, 0)),
                  pl.BlockSpec((K, tn), lambda i, j: (0, j), **mode)],
        out_specs=pl.BlockSpec((tm, tn), lambda i, j: (i, j)),
        out_shape=jax.ShapeDtypeStruct((M, N), out_dtype),
        scratch_shapes=[pltpu.VMEM((tm, K), BF16)],
        compiler_params=_cparams(("parallel", "arbitrary")),
    )(x, g, w)


def _ple_update(h, g, wg, p, wp):
    return h + _sigmoid(_mm(_rms(h, g), wg)) * _mm(p, wp)


def _channel_mix_kernel(h_ref, ya_ref, yb_ref, woa_ref, wob_ref, g_ref, wg_ref, wu_ref, wd_ref,
                        gp_ref, wpg_ref, p_ref, wpp_ref, o_ref):
    h = h_ref[...] + _mm(ya_ref[...], woa_ref[...]) + _mm(yb_ref[...], wob_ref[...])
    xn = _rms(h, g_ref[...]).astype(BF16)
    act = _silu(_d(xn, wg_ref[...])) * _d(xn, wu_ref[...])
    h = h + _mm(act, wd_ref[...])
    o_ref[...] = _ple_update(h, gp_ref[...], wpg_ref[...], p_ref[...], wpp_ref[...])


def _channel_mix(h, ya, yb, woa, wob, g, wg, wu, wd, gp, wpg, p, layer, wpp):
    M, D = h.shape
    tm = min(TM_FFN, M)
    tok = lambda a: pl.BlockSpec((tm, a.shape[1]), lambda i: (i, 0))
    res = lambda a: pl.BlockSpec(a.shape, lambda i: (0, 0), pipeline_mode=pl.Buffered(1))
    return pl.pallas_call(
        _channel_mix_kernel,
        grid=(M // tm,),
        in_specs=[tok(h), tok(ya), tok(yb), res(woa), res(wob), res(g), res(wg), res(wu), res(wd),
                  res(gp), res(wpg),
                  pl.BlockSpec((None, tm, D_PLE), lambda i: (layer, i, 0)), res(wpp)],
        out_specs=pl.BlockSpec((tm, D), lambda i: (i, 0)),
        out_shape=jax.ShapeDtypeStruct((M, D), F32),
        compiler_params=_cparams(("parallel",)),
    )(h, ya, yb, woa, wob, g, wg, wu, wd, gp, wpg, p, wpp)


def _router_kernel(h_ref, y_ref, wo_ref, g_ref, wr_ref, hout_ref, route_ref, routeT_ref, meta_ref):
    h = h_ref[...] + _mm(y_ref[...], wo_ref[...])
    hout_ref[...] = h
    xn = _rms(h, g_ref[...])
    tm = xn.shape[0]
    p = _mm3(wr_ref[...], xn, NT)[:N_EXP]
    p = jnp.exp(p - jnp.max(p, axis=0, keepdims=True))
    p = p / jnp.sum(p, axis=0, keepdims=True)
    ex = lax.broadcasted_iota(jnp.int32, (N_EXP, tm), 0).astype(F32)
    m1 = jnp.max(p, axis=0, keepdims=True)
    i1 = jnp.min(jnp.where(p == m1, ex, float(N_EXP)), axis=0, keepdims=True)
    p2 = jnp.where(ex == i1, -1.0, p)
    m2 = jnp.max(p2, axis=0, keepdims=True)
    i2 = jnp.min(jnp.where(p2 == m2, ex, float(N_EXP)), axis=0, keepdims=True)
    tot = m1 + m2
    sel1 = ex == i1
    sel2 = ex == i2
    assign = jnp.where(sel1 | sel2, 1.0, 0.0)
    ri = lax.broadcasted_iota(jnp.int32, (tm, tm), 0)
    ci = lax.broadcasted_iota(jnp.int32, (tm, tm), 1)
    rank = _d(assign.astype(BF16), _to_bf16_mask(ri < ci))
    cnt = jnp.sum(assign, axis=1, keepdims=True)
    padded = jnp.floor((cnt + (MOE_RB - 1)) * (1.0 / MOE_RB)) * MOE_RB
    padded_b = jnp.broadcast_to(padded, (N_EXP, LANES))
    esub = lax.broadcasted_iota(jnp.int32, (N_EXP, LANES), 0)
    off_b = jnp.zeros((N_EXP, LANES), F32)
    for e in range(N_EXP - 1):
        off_b = off_b + jnp.where(esub > e, padded_b[e:e + 1, :], 0.0)
    off = off_b[:, 0:1]
    pos = off + rank
    dest1 = jnp.sum(jnp.where(sel1, pos, 0.0), axis=0, keepdims=True)
    dest2 = jnp.sum(jnp.where(sel2, pos, 0.0), axis=0, keepdims=True)
    routeT = jnp.where(ex == 0.0, dest1, jnp.where(ex == 1.0, dest2,
                       jnp.where(ex == 2.0, m1 / tot, jnp.where(ex == 3.0, m2 / tot, 0.0))))
    routeT_ref[...] = routeT
    route_ref[...] = jnp.concatenate([routeT, jnp.zeros((LANES - N_EXP, tm), F32)], axis=0).T
    mlane = lax.broadcasted_iota(jnp.int32, (N_EXP, LANES), 1)
    meta_ref[...] = jnp.where(mlane == 0, padded * (1.0 / MOE_RB), jnp.where(mlane == 1, off * (1.0 / MOE_RB), 0.0))


def _router(h, y, wo, g, wr):
    M, D = h.shape
    tm = min(TM_MOE, M)
    nt = M // tm
    return pl.pallas_call(
        _router_kernel,
        grid=(nt,),
        in_specs=[pl.BlockSpec((tm, D), lambda i: (i, 0)),
                  pl.BlockSpec((tm, y.shape[1]), lambda i: (i, 0)),
                  pl.BlockSpec(wo.shape, lambda i: (0, 0)),
                  pl.BlockSpec((1, D), lambda i: (0, 0)),
                  pl.BlockSpec((LANES, D), lambda i: (0, 0))],
        out_specs=[pl.BlockSpec((tm, D), lambda i: (i, 0)),
                   pl.BlockSpec((tm, LANES), lambda i: (i, 0)),
                   pl.BlockSpec((None, SUBLANES, tm), lambda i: (i, 0, 0)),
                   pl.BlockSpec((None, SUBLANES, LANES), lambda i: (i, 0, 0))],
        out_shape=[jax.ShapeDtypeStruct((M, D), F32),
                   jax.ShapeDtypeStruct((M, LANES), F32),
                   jax.ShapeDtypeStruct((nt, SUBLANES, tm), F32),
                   jax.ShapeDtypeStruct((nt, SUBLANES, LANES), F32)],
        compiler_params=_cparams(("parallel",)),
    )(h, y, wo, g, wr)


def _moe_kernel(nblk_ref, offb_ref, h_ref, g_ref, route_ref, routeT_ref, wg_ref, wu_ref, wd_ref, o_ref,
                xs_ref, gs_ref, xn_ref):
    i = pl.program_id(0)
    e = pl.program_id(1)
    tm = h_ref.shape[0]
    cap = xs_ref.shape[0]

    last_e = pl.num_programs(1) - 1
    used_rows = (offb_ref[i, last_e] + nblk_ref[i, last_e]) * MOE_RB

    @pl.when(e == 0)
    def _():
        xn_ref[...] = _rms(h_ref[...], g_ref[...]).astype(BF16)
        xs_ref[...] = jnp.zeros_like(xs_ref)
        row0 = lax.broadcasted_iota(jnp.int32, (MOE_SORT_ROWS, tm), 0).astype(F32)

        def sort_block(lo):
            d1, d2 = routeT_ref[0:1, :], routeT_ref[1:2, :]
            g1, g2 = routeT_ref[2:3, :], routeT_ref[3:4, :]
            rows = row0 + float(lo)
            m1 = rows == d1
            m2 = rows == d2
            xs_ref[lo:lo + MOE_SORT_ROWS, :] = _d(_to_bf16_mask(m1 | m2), xn_ref[...]).astype(BF16)
            gate = jnp.sum(jnp.where(m1, g1, 0.0) + jnp.where(m2, g2, 0.0), axis=-1, keepdims=True)
            gs_ref[lo:lo + MOE_SORT_ROWS, :] = jnp.broadcast_to(gate, (MOE_SORT_ROWS, LANES))

        for lo in range(0, cap, MOE_SORT_ROWS):
            if lo < TOP_K * tm:
                sort_block(lo)
            else:
                pl.when(lo < used_rows)(functools.partial(sort_block, lo))

    base = offb_ref[i, e]
    nb = nblk_ref[i, e]

    def expert_rows(blk, rows):
        r0 = pl.multiple_of(blk * MOE_RB, MOE_RB)
        x = xs_ref[pl.ds(r0, rows), :]
        act = _silu(_d(x, wg_ref[...])) * _d(x, wu_ref[...])
        act = act * gs_ref[pl.ds(r0, rows), 0:1]
        xs_ref[pl.ds(r0, rows), :] = _mm(act, wd_ref[...]).astype(BF16)

    def two_blocks(j, carry):
        expert_rows(base + 2 * j, 2 * MOE_RB)
        return carry

    lax.fori_loop(0, nb // 2, two_blocks, 0)

    @pl.when(nb % 2 == 1)
    def _():
        expert_rows(base + nb - 1, MOE_RB)

    @pl.when(e == last_e)
    def _():
        col0 = lax.broadcasted_iota(jnp.int32, (tm, MOE_UNSORT_COLS), 1).astype(F32)

        def unsort_block(lo):
            d1, d2 = route_ref[:, 0:1], route_ref[:, 1:2]
            cols = col0 + float(lo)
            return _d(_to_bf16_mask((cols == d1) | (cols == d2)), xs_ref[lo:lo + MOE_UNSORT_COLS, :])

        acc = h_ref[...]
        for lo in range(0, TOP_K * tm, MOE_UNSORT_COLS):
            acc = acc + unsort_block(lo)
        o_ref[...] = acc
        for lo in range(-(-TOP_K * tm // MOE_UNSORT_COLS) * MOE_UNSORT_COLS, cap, MOE_UNSORT_COLS):
            @pl.when(lo < used_rows)
            def _():
                o_ref[...] += unsort_block(lo)


def _moe(h, g, route, routeT, meta, wg, wu, wd):
    M, D = h.shape
    E, _, FF = wg.shape
    tm = min(TM_MOE, M)
    nt = M // tm
    cap = TOP_K * tm + E * MOE_RB
    cap = -(-cap // MOE_UNSORT_COLS) * MOE_UNSORT_COLS
    meta_i = meta.astype(jnp.int32)
    grid_spec = pltpu.PrefetchScalarGridSpec(
        num_scalar_prefetch=2,
        grid=(nt, E),
        in_specs=[pl.BlockSpec((tm, D), lambda i, e, nb, ob: (i, 0)),
                  pl.BlockSpec((1, D), lambda i, e, nb, ob: (0, 0)),
                  pl.BlockSpec((tm, LANES), lambda i, e, nb, ob: (i, 0)),
                  pl.BlockSpec((None, SUBLANES, tm), lambda i, e, nb, ob: (i, 0, 0)),
                  pl.BlockSpec((None, D, FF), lambda i, e, nb, ob: (e, 0, 0)),
                  pl.BlockSpec((None, D, FF), lambda i, e, nb, ob: (e, 0, 0)),
                  pl.BlockSpec((None, FF, D), lambda i, e, nb, ob: (e, 0, 0))],
        out_specs=pl.BlockSpec((tm, D), lambda i, e, nb, ob: (i, 0)),
        scratch_shapes=[pltpu.VMEM((cap, D), BF16), pltpu.VMEM((cap, LANES), F32), pltpu.VMEM((tm, D), BF16)])
    return pl.pallas_call(
        _moe_kernel,
        grid_spec=grid_spec,
        out_shape=jax.ShapeDtypeStruct((M, D), F32),
        compiler_params=_cparams(("parallel", "arbitrary")),
    )(meta_i[:, :, 0], meta_i[:, :, 1], h, g, route, routeT, wg, wu, wd)


def _ple_kernel(final, h_ref, g_ref, wg_ref, p_ref, wp_ref, gf_ref, o_ref):
    out = _ple_update(h_ref[...], g_ref[...], wg_ref[...], p_ref[...], wp_ref[...])
    if final:
        out = _rms(out, gf_ref[...])
    o_ref[...] = out


def _ple(h, g, wg, p, layer, wp, gf, final):
    M, D = h.shape
    tm = min(TM_TOK, M)
    return pl.pallas_call(
        functools.partial(_ple_kernel, final),
        grid=(M // tm,),
        in_specs=[pl.BlockSpec((tm, D), lambda i: (i, 0)),
                  pl.BlockSpec((1, D), lambda i: (0, 0)),
                  pl.BlockSpec((D, D), lambda i: (0, 0)),
                  pl.BlockSpec((None, tm, D_PLE), lambda i: (layer, i, 0)),
                  pl.BlockSpec((D_PLE, D), lambda i: (0, 0)),
                  pl.BlockSpec((1, D), lambda i: (0, 0))],
        out_specs=pl.BlockSpec((tm, D), lambda i: (i, 0)),
        out_shape=jax.ShapeDtypeStruct((M, D), F32),
        compiler_params=_cparams(("parallel",)),
    )(h, g, wg, p, wp, gf)


def _lru_gates(xc, wr, br, wi, bi, lam):
    r = _sigmoid(_mm(xc, wr) + br)
    i = _sigmoid(_mm(xc, wi) + bi)
    log_a = -LRU_C * r * _softplus(-lam)
    a = jnp.exp(log_a)
    u = jnp.sqrt(jnp.tanh(-log_a) * (a * a + 1.0)) * (i * xc)
    return a, u


def _lru_seq_kernel(TL, proj_ref, cs_ref, h0_ref, cw_ref, cb_ref, wr_ref, br_ref, wi_ref, bi_ref, lam_ref,
                    y_ref, hT_ref, pad_ref, a_ref, u_ref, hs_ref, hc_ref):
    ti = pl.program_id(0)
    G = pad_ref.shape[0]

    @pl.when(ti == 0)
    def _():
        pad_ref[:, SUBLANES - (CONV_W - 1):SUBLANES, :] = cs_ref[...]
        hc_ref[...] = h0_ref[...]

    pad_ref[:, SUBLANES:, :] = proj_ref[:, :, :D_A].astype(F32)
    cw = cw_ref[...]
    xc = cb_ref[...][None]
    for k in range(CONV_W):
        off = SUBLANES - (CONV_W - 1) + k
        xc = xc + pad_ref[:, off:off + TL, :] * cw[k:k + 1, :][None]
    pad_ref[:, :SUBLANES, :] = pad_ref[:, TL:TL + SUBLANES, :]

    a, u = _lru_gates(xc.reshape(G * TL, D_A), wr_ref[...], br_ref[...], wi_ref[...], bi_ref[...],
                      lam_ref[...])
    a_ref[...] = a.reshape(G, TL, D_A)
    u_ref[...] = u.reshape(G, TL, D_A)

    def step(t, h):
        h = a_ref[:, pl.ds(t, 1), :] * h + u_ref[:, pl.ds(t, 1), :]
        hs_ref[:, pl.ds(t, 1), :] = h
        return h

    hc_ref[...] = lax.fori_loop(0, TL, step, hc_ref[...], unroll=8)
    y_ref[...] = (_gelu_tanh(proj_ref[:, :, D_A:].astype(F32)) * hs_ref[...]).astype(y_ref.dtype)

    @pl.when(ti == pl.num_programs(0) - 1)
    def _():
        hT_ref[...] = hc_ref[...]


def _lru_step_kernel(xa_ref, ga_ref, cs_ref, h0_ref, cw_ref, cb_ref, wr_ref, br_ref, wi_ref, bi_ref, lam_ref,
                     y_ref, hT_ref):
    cw = cw_ref[...]
    x = xa_ref[...]
    xc = cb_ref[...] + x * cw[CONV_W - 1:CONV_W, :]
    for k in range(CONV_W - 1):
        xc = xc + cs_ref[k] * cw[k:k + 1, :]
    a, u = _lru_gates(xc, wr_ref[...], br_ref[...], wi_ref[...], bi_ref[...], lam_ref[...])
    h = a * h0_ref[...] + u
    hT_ref[...] = h
    y_ref[...] = _gelu_tanh(ga_ref[...]) * h


def _lru(proj, conv_state, h0, cw, cb, wr, br, wi, bi, lam):
    B, L, _ = proj.shape
    small = [cw, cb, wr, br, wi, bi, lam]
    if L == 1:
        proj2 = proj.reshape(B, -1)
        cs = jnp.transpose(conv_state, (1, 0, 2))
        tb = min(B, 128)
        full = lambda s: pl.BlockSpec(s.shape, lambda i: (0,) * s.ndim)
        y, hT = pl.pallas_call(
            _lru_step_kernel,
            grid=(B // tb,),
            in_specs=[pl.BlockSpec((tb, D_A), lambda i: (i, 0)),
                      pl.BlockSpec((tb, D_A), lambda i: (i, 1)),
                      pl.BlockSpec((CONV_W - 1, tb, D_A), lambda i: (0, i, 0)),
                      pl.BlockSpec((tb, D_A), lambda i: (i, 0))] + [full(s) for s in small],
            out_specs=[pl.BlockSpec((tb, D_A), lambda i: (i, 0)), pl.BlockSpec((tb, D_A), lambda i: (i, 0))],
            out_shape=[jax.ShapeDtypeStruct((B, D_A), F32), jax.ShapeDtypeStruct((B, D_A), F32)],
            compiler_params=_cparams(("parallel",)),
        )(proj2, proj2, cs, h0, *small)
        return y.reshape(B, 1, D_A), hT
    TL = min(LRU_TL, L)
    G = B
    full = lambda s: pl.BlockSpec(s.shape, lambda i: (0,) * s.ndim)
    y, hT = pl.pallas_call(
        functools.partial(_lru_seq_kernel, TL),
        grid=(L // TL,),
        in_specs=[pl.BlockSpec((G, TL, 2 * D_A), lambda i: (0, i, 0)),
                  pl.BlockSpec((G, CONV_W - 1, D_A), lambda i: (0, 0, 0)),
                  pl.BlockSpec((G, 1, D_A), lambda i: (0, 0, 0))] + [full(s) for s in small],
        out_specs=[pl.BlockSpec((G, TL, D_A), lambda i: (0, i, 0)),
                   pl.BlockSpec((G, 1, D_A), lambda i: (0, 0, 0))],
        out_shape=[jax.ShapeDtypeStruct((B, L, D_A), BF16), jax.ShapeDtypeStruct((B, 1, D_A), F32)],
        scratch_shapes=[pltpu.VMEM((G, SUBLANES + TL, D_A), F32),
                        pltpu.VMEM((G, TL, D_A), F32), pltpu.VMEM((G, TL, D_A), F32),
                        pltpu.VMEM((G, TL, D_A), F32), pltpu.VMEM((G, 1, D_A), F32)],
        compiler_params=_cparams(("arbitrary",)),
    )(proj, conv_state, h0.reshape(B, 1, D_A), *small)
    return y, hT.reshape(B, D_A)


def _rwkv_prep(xs, w0, w2p, a0, a2p, g2, k_k, k_a, r_k, hsel):
    r = xs[:, 0:D_B]
    k = xs[:, D_B:2 * D_B]
    v = xs[:, 2 * D_B:3 * D_B]
    wa = xs[:, 3 * D_B:3 * D_B + R_W + R_A]
    gl = xs[:, 3 * D_B + R_W + R_A:]
    w_log = -_softplus(-(w0 + _mm(jnp.tanh(wa), w2p))) - 0.5
    logd = -jnp.exp(w_log)
    a = _sigmoid(a0 + _mm(wa, a2p))
    g = _mm(_sigmoid(gl), g2)
    kk = k * k_k
    kk = kk * lax.rsqrt(jnp.maximum(_mm_sel(hsel, kk * kk, HEAD_SUM_PIECES, sel_left=False), 1e-12))
    k2 = k * (1.0 + (a - 1.0) * k_a)
    beta = kk * a
    bonus = _mm_sel(hsel, r * k2 * r_k, HEAD_SUM_PIECES, sel_left=False) * v
    return r, k2, v, logd, kk, beta, g, bonus


def _rwkv_post(y, bonus, g, ln_w, ln_b, hsel):
    mean = _mm_sel(hsel, y, HEAD_SUM_PIECES, sel_left=False) * (1.0 / HD_B)
    d = y - mean
    var = _mm_sel(hsel, d * d, HEAD_SUM_PIECES, sel_left=False) * (1.0 / HD_B)
    yn = d * lax.rsqrt(var + GN_EPS_B) * ln_w + ln_b
    return (yn + bonus) * g


def _mmp(a, b, dims, passes):
    return _mm3(a, b, dims) if passes == 3 else _mm(a, b, dims)


RWKV_PASSES = dict(am=1, state=1, inv=1, apply=1, dw=1)


def _rwkv_chunk(G, T, ops, st_ref, masks, emask, store_y):
    r, k2, v, logd, kk, beta = ops
    incl, strict, lmask_bf, lastmask_bf = masks
    R = G * T
    ps = RWKV_PASSES
    cum = _mm_sel(lmask_bf, logd, 3)
    cend = _mm_sel(lastmask_bf, cum, 3)
    ecn = jnp.exp(-cum)
    P = jnp.exp(cum - logd) * kk
    Q = ecn * beta
    Kt = ecn * k2
    Rt = jnp.exp(cum) * r
    eend = jnp.exp(cend)
    lo = lax.broadcasted_iota(jnp.int32, (R, LANES), 1) < HD_B
    row_lo = lax.broadcasted_iota(jnp.int32, (LANES, G * HD_B), 0) < HD_B
    eye = None
    if T > 1:
        ri = lax.broadcasted_iota(jnp.int32, (R, R), 0)
        ci = lax.broadcasted_iota(jnp.int32, (R, R), 1)
        eye = jnp.where(ri == ci, 1.0, 0.0)
    n_rep = G * HD_B // LANES
    NP = H_B // 2
    heads = [(c, hh) for c in range(NP) for hh in range(2)]
    own = {0: lo, 1: ~lo}

    def tiled(x, xr, hh):
        t = jnp.where(lo, x, xr) if hh == 0 else jnp.where(lo, xr, x)
        return jnp.where(emask, jnp.concatenate([t] * n_rep, axis=1), 0.0)

    pair = []
    for c in range(NP):
        sl = slice(c * LANES, (c + 1) * LANES)
        vals = (P[:, sl], Rt[:, sl], Q[:, sl], Kt[:, sl], eend[:, sl])
        pair.append(dict(x=vals, xr=[pltpu.roll(x, HD_B, 1) for x in vals], V=v[:, sl]))
    AM = [_mmp(jnp.concatenate([jnp.where(own[hh], pair[c]['x'][0], 0.0),
                                jnp.where(own[hh], pair[c]['x'][1], 0.0)], axis=0),
               jnp.concatenate([pair[c]['x'][2], pair[c]['x'][3]], axis=0), NT, ps['am']) for c, hh in heads]
    Mqr = [jnp.where(incl, a[R:, :R], 0.0) for a in AM]
    Mkr = [jnp.where(incl, a[R:, R:], 0.0) for a in AM]
    MV = [_mmp(m, pair[c]['V'], NN, ps['apply']) for m, (c, hh) in zip(Mkr, heads)]
    if T > 1:
        Npow = [-jnp.where(strict, a[:R, :R], 0.0) for a in AM]
        Akp = [jnp.where(strict, a[:R, R:], 0.0) for a in AM]
        AV = [_mmp(m, pair[c]['V'], NN, ps['apply']) for m, (c, hh) in zip(Akp, heads)]
        inv = [eye + n for n in Npow]
        for _ in range(int(math.log2(T)) - 1):
            Npow = [_mmp(n, n, NN, ps['inv']) for n in Npow]
            inv = [i + _mmp(i, n, NN, ps['inv']) for i, n in zip(inv, Npow)]
    ex = [[tiled(x, xr, hh) for x, xr in zip(pair[c]['x'], pair[c]['xr'])] for c, hh in heads]
    WT = [st_ref[c] for c in range(NP)]
    PWRW = [_mmp(jnp.concatenate([e[0], e[1]], axis=0), WT[c], NT, ps['state']) for e, (c, hh) in zip(ex, heads)]
    if T > 1:
        E = [_mmp(i, pw[:R] + av, NN, ps['apply']) for i, pw, av in zip(inv, PWRW, AV)]
    else:
        E = [pw[:R] for pw in PWRW]
    Y = [pw[R:] + mv - _mmp(m, e, NN, ps['apply']) for pw, mv, m, e in zip(PWRW, MV, Mqr, E)]
    for c in range(NP):
        lhs = jnp.concatenate([jnp.where(own[hh], x, 0.0) for hh in range(2)
                               for x in (-E[2 * c + hh], pair[c]['V'])], axis=0)
        rhs = jnp.concatenate([ex[2 * c + hh][j] for hh in range(2) for j in (2, 3)], axis=0)
        dW = _mmp(lhs, rhs, TN, ps['dw'])
        erow = [jnp.sum(ex[2 * c + hh][4], axis=0, keepdims=True) * (1.0 / T) for hh in range(2)]
        st_ref[c] = (WT[c] + dW) * jnp.where(row_lo, erow[0], erow[1])
        store_y(c, jnp.where(lo, Y[2 * c], Y[2 * c + 1]))


def _rwkv_kernel(G, T, NC, seq, proj_ref, sh_ref, s0_ref, mu_ref, w0_ref, w2_ref, a0_ref, a2_ref, g2_ref,
                 kk_ref, ka_ref, rk_ref, lnw_ref, lnb_ref, y_ref, sT_ref, *scratch):
    ti = pl.program_id(1)
    TL = T * NC
    R = G * T
    st_ref = scratch[0]
    hsel = _head_sum_sel(D_B, HD_B)

    @pl.when(ti == 0)
    def _():
        for h in range(H_B):
            r0 = (h % 2) * HD_B
            for j in range(G // 2):
                st_ref[h // 2, r0:r0 + HD_B, j * LANES:(j + 1) * LANES] = jnp.concatenate(
                    [s0_ref[2 * j, h], s0_ref[2 * j + 1, h]], axis=1)

    if seq:
        pad_ref, ops_ref, gb_ref, ys_ref = scratch[1:]

        @pl.when(ti == 0)
        def _():
            pad_ref[:, SUBLANES - 1:SUBLANES, :] = sh_ref[...]

        pad_ref[:, SUBLANES:, :] = proj_ref[:, :, 2 * D_A:].astype(F32)
        pb = pad_ref[:, SUBLANES:, :]
        prev = pad_ref[:, SUBLANES - 1:SUBLANES - 1 + TL, :]
        xs = (pb + (prev - pb) * mu_ref[...][None]).reshape(G * TL, D_BPROJ)
        pad_ref[:, :SUBLANES, :] = pad_ref[:, TL:TL + SUBLANES, :]
    else:
        pb = proj_ref[:, 2 * D_A:]
        xs = pb + (sh_ref[...] - pb) * mu_ref[...]

    r, k2, v, logd, kk, beta, g, bonus = _rwkv_prep(
        xs, w0_ref[...], w2_ref[...], a0_ref[...], a2_ref[...], g2_ref[...], kk_ref[...], ka_ref[...],
        rk_ref[...], hsel)

    incl, strict, last = _chunk_masks(G, T)
    masks = (incl, strict, _to_bf16_mask(incl), _to_bf16_mask(last))
    emask = _batch_lane_mask(1, G, T, HD_B)

    if seq:
        for n, val in enumerate((r, k2, v, logd, kk, beta)):
            ops_ref[n] = val.reshape(G, TL, D_B)
        gb_ref[0] = g.reshape(G, TL, D_B)
        gb_ref[1] = bonus.reshape(G, TL, D_B)

        def chunk(c, carry):
            t0 = pl.multiple_of(c * T, T)
            ops = tuple(ops_ref[n, :, pl.ds(t0, T), :].reshape(R, D_B) for n in range(6))

            def store_y(hp, Y):
                ys_ref[:, pl.ds(t0, T), hp * LANES:(hp + 1) * LANES] = Y.reshape(G, T, LANES)

            _rwkv_chunk(G, T, ops, st_ref, masks, emask, store_y)
            return carry

        lax.fori_loop(0, NC, chunk, 0)
        y = ys_ref[...].reshape(G * TL, D_B)
        out = _rwkv_post(y, gb_ref[1].reshape(G * TL, D_B), gb_ref[0].reshape(G * TL, D_B),
                         lnw_ref[...], lnb_ref[...], hsel)
        y_ref[...] = out.reshape(G, TL, D_B).astype(y_ref.dtype)
    else:
        ys_ref = scratch[1]

        def store_y(hp, Y):
            ys_ref[:, hp * LANES:(hp + 1) * LANES] = Y

        _rwkv_chunk(G, T, (r, k2, v, logd, kk, beta), st_ref, masks, emask, store_y)
        y_ref[...] = _rwkv_post(ys_ref[...], bonus, g, lnw_ref[...], lnb_ref[...], hsel)

    @pl.when(ti == pl.num_programs(1) - 1)
    def _():
        for h in range(H_B):
            r0 = (h % 2) * HD_B
            for b in range(G):
                sT_ref[b, h] = st_ref[h // 2, r0:r0 + HD_B, b * HD_B:(b + 1) * HD_B]


def _rwkv(proj, shift0, s0, mu, w0, w2p, a0, a2p, g2, k_k, k_a, r_k, ln_w, ln_b):
    B, L, _ = proj.shape
    G = G_ROWS
    NB = B // G
    small = [mu, w0, w2p, a0, a2p, g2, k_k, k_a, r_k, ln_w, ln_b]
    full = lambda s: pl.BlockSpec(s.shape, lambda b, i: (0,) * s.ndim)
    st_dims = (H_B // 2, 2 * HD_B, G * HD_B)
    s0l = s0
    st_spec = pl.BlockSpec((G, H_B, HD_B, HD_B), lambda b, i: (b, 0, 0, 0))
    st_shape = jax.ShapeDtypeStruct(s0.shape, F32)
    unpair = lambda s: s
    if L == 1:
        T, NC = 1, 1
        kern = functools.partial(_rwkv_kernel, G, T, NC, False)
        y, sT = pl.pallas_call(
            kern,
            grid=(NB, 1),
            in_specs=[pl.BlockSpec((G, IN_AB), lambda b, i: (b, 0)),
                      pl.BlockSpec((G, D_BPROJ), lambda b, i: (b, 0)), st_spec] + [full(s) for s in small],
            out_specs=[pl.BlockSpec((G, D_B), lambda b, i: (b, 0)), st_spec],
            out_shape=[jax.ShapeDtypeStruct((B, D_B), F32), st_shape],
            scratch_shapes=[pltpu.VMEM(st_dims, F32), pltpu.VMEM((G, D_B), F32)],
            compiler_params=_cparams(("parallel", "arbitrary")),
        )(proj.reshape(B, IN_AB), shift0, s0l, *small)
        return y.reshape(B, 1, D_B), unpair(sT)
    T = min(T_CHUNK, L)
    TL = min(TL_BLOCK, L)
    NC = TL // T
    kern = functools.partial(_rwkv_kernel, G, T, NC, True)
    y, sT = pl.pallas_call(
        kern,
        grid=(NB, L // TL),
        in_specs=[pl.BlockSpec((G, TL, IN_AB), lambda b, i: (b, i, 0)),
                  pl.BlockSpec((G, 1, D_BPROJ), lambda b, i: (b, 0, 0)), st_spec] + [full(s) for s in small],
        out_specs=[pl.BlockSpec((G, TL, D_B), lambda b, i: (b, i, 0)), st_spec],
        out_shape=[jax.ShapeDtypeStruct((B, L, D_B), BF16), st_shape],
        scratch_shapes=[pltpu.VMEM(st_dims, F32),
                        pltpu.VMEM((G, SUBLANES + TL, D_BPROJ), F32),
                        pltpu.VMEM((6, G, TL, D_B), F32),
                        pltpu.VMEM((2, G, TL, D_B), F32),
                        pltpu.VMEM((G, TL, D_B), F32)],
        compiler_params=_cparams(("parallel", "arbitrary")),
    )(proj, shift0.reshape(B, 1, D_BPROJ), s0l, *small)
    return y, unpair(sT)


CD_Z, CD_XBC, CD_Q, CD_K, CD_V, CD_G, CD_S = 0, 512, 1536, 1792, 2048, 2560, 3072


def _ssd_chunk(G, T, xs, Bm, Cm, dt_all, la_all, st_ref, masks, emaskN, store_y):
    incl, lmask_bf, lastmask_bf = masks
    R = G * T
    cum = _mm_sel(lmask_bf, la_all, 3)
    cend = _mm_sel(lastmask_bf, cum, 3)
    cumT = cum.T
    wend = jnp.exp(cend - cum) * dt_all
    ecum = jnp.exp(cum)
    dend = jnp.exp(cend)
    lo = lax.broadcasted_iota(jnp.int32, (R, LANES), 1) < P_C

    def per_head(x):
        return jnp.concatenate([jnp.where(lo, x[:, 2 * c:2 * c + 1], x[:, 2 * c + 1:2 * c + 2])
                                for c in range(H_C // 2)], axis=1)

    tile = lambda x: jnp.where(emaskN, jnp.concatenate([x] * G, axis=1), 0.0)
    xdt = xs * per_head(dt_all)
    xw = xs * per_head(wend)
    ecf = per_head(ecum)
    Bg = [Bm[:, g * N_C:(g + 1) * N_C] for g in range(G_C)]
    Cg = [Cm[:, g * N_C:(g + 1) * N_C] for g in range(G_C)]
    CB = [_mm(c, b, NT) for c, b in zip(Cg, Bg)]
    ST = [st_ref[g] for g in range(G_C)]
    CS = [_mm(tile(c), s, NT) for c, s in zip(Cg, ST)]
    W = [CB[h // HG_C] * jnp.exp(jnp.where(incl, cum[:, h:h + 1] - cumT[h:h + 1, :], -1e30))
         for h in range(H_C)]
    intra = [_mm(W[h], xdt[:, (h // 2) * LANES:(h // 2 + 1) * LANES]) for h in range(H_C)]
    drow = [jnp.sum(jnp.where(emaskN, dend[:, h:h + 1], 0.0), axis=0, keepdims=True) * (1.0 / T)
            for h in range(H_C)]
    for c in range(H_C // 2):
        sl = slice(c * LANES, (c + 1) * LANES)
        gsl = slice((c % (HG_C // 2)) * LANES, (c % (HG_C // 2) + 1) * LANES)
        store_y(c, jnp.where(lo, intra[2 * c], intra[2 * c + 1]) + ecf[:, sl] * CS[c // (HG_C // 2)][:, gsl])
    for g in range(G_C):
        W_g = HG_C * P_C
        dS = _mm(xw[:, g * W_g:(g + 1) * W_g], tile(Bg[g]), TN)
        st_ref[g] = jnp.concatenate(
            [ST[g][hh * P_C:(hh + 1) * P_C] * drow[g * HG_C + hh] + dS[hh * P_C:(hh + 1) * P_C]
             for hh in range(HG_C)], axis=0)


def _gla_chunk(G, T, q, k, v, la, st_ref, masks, emask, store_o):
    incl, lmask_bf, lastmask_bf = masks
    R = G * T
    bc = _mm_sel(lmask_bf, la, 3)
    bend = _mm_sel(lastmask_bf, bc, 3)
    qe = q * jnp.exp(bc)
    kn = k * jnp.exp(-bc)
    kw = k * jnp.exp(bend - bc)
    dend = jnp.exp(bend)
    lo = lax.broadcasted_iota(jnp.int32, (R, LANES), 1) < DK_D
    own = {0: lo, 1: ~lo}
    n_rep = G * DK_D // LANES

    def tiled(x, xr, hh):
        t = jnp.where(lo, x, xr) if hh == 0 else jnp.where(lo, xr, x)
        return jnp.where(emask, jnp.concatenate([t] * n_rep, axis=1), 0.0)

    heads = [(c, hh) for c in range(H_D // 2) for hh in range(2)]
    pair = []
    for c in range(H_D // 2):
        sl = slice(c * LANES, (c + 1) * LANES)
        vals = (qe[:, sl], kw[:, sl], dend[:, sl])
        pair.append(dict(x=vals, xr=[pltpu.roll(x, DK_D, 1) for x in vals], kn=kn[:, sl]))
    vh = [v[:, h * DV_D:(h + 1) * DV_D] for h in range(H_D)]
    att = [jnp.where(incl, _mm(jnp.where(own[hh], pair[c]['x'][0], 0.0), pair[c]['kn'], NT), 0.0)
           for c, hh in heads]
    ex = [[tiled(x, xr, hh) for x, xr in zip(pair[c]['x'], pair[c]['xr'])] for c, hh in heads]
    ST = [st_ref[h] for h in range(H_D)]
    o = [_mm(a, vv) + _mm(e[0], s, NT) for a, vv, e, s in zip(att, vh, ex, ST)]
    for h in range(H_D):
        drow = jnp.sum(ex[h][2], axis=0, keepdims=True) * (1.0 / T)
        st_ref[h] = ST[h] * drow + _mm(vh[h], ex[h][1], TN)
        store_o(h, o[h])


def _cd_prep_small(small, dtb, a_neg, aup, ab):
    dt_all = _softplus(small + dtb)
    la_all = dt_all * a_neg
    la_gla = -_softplus(-(_mm(small, aup) + ab)) * (1.0 / GATE_NORM)
    return dt_all, la_all, la_gla


def _cd_post(y, xs, z, o, gg, dskip, ssd_norm, gla_norm):
    y_c = _rms((y + dskip * xs) * _silu(z), ssd_norm)
    outs = [y_c]
    for h in range(H_D):
        sl = slice(h * DV_D, (h + 1) * DV_D)
        outs.append(_rms(o[:, sl], gla_norm) * _silu(gg[:, sl]))
    return outs


def _cd_kernel(G, T, NC, seq, proj_ref, cs_ref, ssd0_ref, gla0_ref, cw_ref, cb_ref, dtb_ref, aneg_ref, dskip_ref,
               ssdn_ref, aup_ref, ab_ref, glan_ref, y_ref, ssdT_ref, glaT_ref, *scratch):
    ti = pl.program_id(1)
    TL = T * NC
    R = G * T
    sst_ref, gst_ref = scratch[0], scratch[1]

    @pl.when(ti == 0)
    def _():
        for h in range(H_C):
            r0 = (h % HG_C) * P_C
            for b in range(G):
                sst_ref[h // HG_C, r0:r0 + P_C, b * N_C:(b + 1) * N_C] = ssd0_ref[b, h]
        for h in range(H_D):
            for j in range(G // 2):
                two = jnp.concatenate([gla0_ref[2 * j, h], gla0_ref[2 * j + 1, h]], axis=0)
                gst_ref[h, :, j * LANES:(j + 1) * LANES] = two.T

    cw = cw_ref[...]
    if seq:
        pad_ref, xbc_ref, sm_ref, ys_ref, os_ref = scratch[2:]

        @pl.when(ti == 0)
        def _():
            pad_ref[:, SUBLANES - (CONV_W - 1):SUBLANES, :] = cs_ref[...]

        pad_ref[:, SUBLANES:, :] = proj_ref[:, :, CD_XBC:CD_XBC + D_XBC].astype(F32)

        def conv_b(b, carry):
            for c0 in range(0, D_XBC, CONV_COLS):
                cols = slice(c0, c0 + CONV_COLS)
                acc = cb_ref[:, cols]
                for kq in range(CONV_W):
                    off = SUBLANES - (CONV_W - 1) + kq
                    acc = acc + pad_ref[b, off:off + TL, cols] * cw_ref[kq:kq + 1, cols]
                xbc_ref[b, :, cols] = _silu(acc)
            return carry

        lax.fori_loop(0, G, conv_b, 0)
        pad_ref[:, :SUBLANES, :] = pad_ref[:, TL:TL + SUBLANES, :]
        small = proj_ref[:, :, CD_S:].astype(F32).reshape(G * TL, LANES)
    else:
        x = proj_ref[:, CD_XBC:CD_XBC + D_XBC]
        xc = cb_ref[...] + x * cw[CONV_W - 1:CONV_W, :]
        for kq in range(CONV_W - 1):
            xc = xc + cs_ref[kq] * cw[kq:kq + 1, :]
        xbc = _silu(xc)
        small = proj_ref[:, CD_S:]

    dt_all, la_all, la_gla = _cd_prep_small(small, dtb_ref[...], aneg_ref[...], aup_ref[...], ab_ref[...])

    incl, _, last = _chunk_masks(G, T)
    masks = (incl, _to_bf16_mask(incl), _to_bf16_mask(last))
    emaskN = _batch_lane_mask(1, G, T, N_C)
    emask1 = _batch_lane_mask(1, G, T, DK_D)
    scale = DK_D ** -0.5

    if seq:
        sm_ref[0] = dt_all.reshape(G, TL, LANES)
        sm_ref[1] = la_all.reshape(G, TL, LANES)
        sm_ref[2] = la_gla[:, :LANES].reshape(G, TL, LANES)
        sm_ref[3] = la_gla[:, LANES:].reshape(G, TL, LANES)

        def chunk(c, carry):
            t0 = pl.multiple_of(c * T, T)
            rows = lambda ref, lo, hi: ref[:, pl.ds(t0, T), lo:hi].astype(F32).reshape(R, hi - lo)
            xs = rows(xbc_ref, 0, D_C)
            Bm = rows(xbc_ref, D_C, D_C + G_C * N_C)
            Cm = rows(xbc_ref, D_C + G_C * N_C, D_XBC)
            dtc = sm_ref[0, :, pl.ds(t0, T), :].reshape(R, LANES)
            lac = sm_ref[1, :, pl.ds(t0, T), :].reshape(R, LANES)

            def store_y(hp, yv):
                ys_ref[:, pl.ds(t0, T), hp * LANES:(hp + 1) * LANES] = yv.reshape(G, T, LANES)

            _ssd_chunk(G, T, xs, Bm, Cm, dtc, lac, sst_ref, masks, emaskN, store_y)

            q = rows(proj_ref, CD_Q, CD_Q + K_D) * scale
            k = rows(proj_ref, CD_K, CD_K + K_D)
            v = rows(proj_ref, CD_V, CD_V + V_D)
            lag = jnp.concatenate([sm_ref[2, :, pl.ds(t0, T), :].reshape(R, LANES),
                                   sm_ref[3, :, pl.ds(t0, T), :].reshape(R, LANES)], axis=1)

            def store_o(h, ov):
                os_ref[:, pl.ds(t0, T), h * DV_D:(h + 1) * DV_D] = ov.reshape(G, T, DV_D)

            _gla_chunk(G, T, q, k, v, lag, gst_ref, masks, emask1, store_o)
            return carry

        lax.fori_loop(0, NC, chunk, 0)
        def post_b(b, carry):
            outs = _cd_post(ys_ref[b], xbc_ref[b, :, :D_C], proj_ref[b, :, CD_Z:CD_Z + D_C].astype(F32),
                            os_ref[b], proj_ref[b, :, CD_G:CD_G + V_D].astype(F32),
                            dskip_ref[...], ssdn_ref[...], glan_ref[...])
            y_ref[b, :, :D_C] = outs[0].astype(y_ref.dtype)
            for h in range(H_D):
                y_ref[b, :, D_C + h * DV_D:D_C + (h + 1) * DV_D] = outs[1 + h].astype(y_ref.dtype)
            return carry

        lax.fori_loop(0, G, post_b, 0)
    else:
        ys_ref, os_ref = scratch[2:]

        def store_y(hp, yv):
            ys_ref[:, hp * LANES:(hp + 1) * LANES] = yv

        def store_o(h, ov):
            os_ref[:, h * DV_D:(h + 1) * DV_D] = ov

        _ssd_chunk(G, T, xbc[:, :D_C], xbc[:, D_C:D_C + G_C * N_C], xbc[:, D_C + G_C * N_C:], dt_all, la_all,
                   sst_ref, masks, emaskN, store_y)
        _gla_chunk(G, T, proj_ref[:, CD_Q:CD_Q + K_D] * scale, proj_ref[:, CD_K:CD_K + K_D],
                   proj_ref[:, CD_V:CD_V + V_D], la_gla, gst_ref, masks, emask1, store_o)
        outs = _cd_post(ys_ref[...], xbc[:, :D_C], proj_ref[:, CD_Z:CD_Z + D_C], os_ref[...],
                        proj_ref[:, CD_G:CD_G + V_D], dskip_ref[...], ssdn_ref[...], glan_ref[...])
        y_ref[:, :D_C] = outs[0]
        for h in range(H_D):
            y_ref[:, D_C + h * DV_D:D_C + (h + 1) * DV_D] = outs[1 + h]

    @pl.when(ti == pl.num_programs(1) - 1)
    def _():
        for h in range(H_C):
            r0 = (h % HG_C) * P_C
            for b in range(G):
                ssdT_ref[b, h] = sst_ref[h // HG_C, r0:r0 + P_C, b * N_C:(b + 1) * N_C]
        for h in range(H_D):
            for j in range(G // 2):
                two = gst_ref[h, :, j * LANES:(j + 1) * LANES].T
                glaT_ref[2 * j, h] = two[:DK_D]
                glaT_ref[2 * j + 1, h] = two[DK_D:]


def _cd(proj, conv_state, ssd0, gla0, cw, cb, dtb, a_neg, dskip, ssd_norm, aup, ab, gla_norm):
    B, L, _ = proj.shape
    G = G_ROWS
    NB = B // G
    small = [cw, cb, dtb, a_neg, dskip, ssd_norm, aup, ab, gla_norm]
    full = lambda s: pl.BlockSpec(s.shape, lambda b, i: (0,) * s.ndim)
    ssd_dims = (G_C, HG_C * P_C, G * N_C)
    ssd0l, gla0l = ssd0, gla0
    ssd_spec = pl.BlockSpec((G, H_C, P_C, N_C), lambda b, i: (b, 0, 0, 0))
    gla_spec = pl.BlockSpec((G, H_D, DK_D, DV_D), lambda b, i: (b, 0, 0, 0))
    st_shapes = [jax.ShapeDtypeStruct(ssd0.shape, F32), jax.ShapeDtypeStruct(gla0.shape, F32)]
    st_scratch = [pltpu.VMEM(ssd_dims, F32), pltpu.VMEM((H_D, DV_D, G * DK_D), F32)]
    DO = D_C + V_D
    if L == 1:
        kern = functools.partial(_cd_kernel, G, 1, 1, False)
        y, ssdT, glaT = pl.pallas_call(
            kern,
            grid=(NB, 1),
            in_specs=[pl.BlockSpec((G, IN_CD_PAD), lambda b, i: (b, 0)),
                      pl.BlockSpec((CONV_W - 1, G, D_XBC), lambda b, i: (0, b, 0)),
                      ssd_spec, gla_spec] + [full(s) for s in small],
            out_specs=[pl.BlockSpec((G, DO), lambda b, i: (b, 0)), ssd_spec, gla_spec],
            out_shape=[jax.ShapeDtypeStruct((B, DO), F32)] + st_shapes,
            scratch_shapes=st_scratch + [pltpu.VMEM((G, D_C), F32), pltpu.VMEM((G, V_D), F32)],
            compiler_params=_cparams(("parallel", "arbitrary")),
        )(proj.reshape(B, IN_CD_PAD), jnp.transpose(conv_state, (1, 0, 2)), ssd0l, gla0l, *small)
        y = y.reshape(B, 1, DO)
    else:
        T = min(T_CHUNK, L)
        TL = min(TL_BLOCK, L)
        NC = TL // T
        kern = functools.partial(_cd_kernel, G, T, NC, True)
        y, ssdT, glaT = pl.pallas_call(
            kern,
            grid=(NB, L // TL),
            in_specs=[pl.BlockSpec((G, TL, IN_CD_PAD), lambda b, i: (b, i, 0)),
                      pl.BlockSpec((G, CONV_W - 1, D_XBC), lambda b, i: (b, 0, 0)),
                      ssd_spec, gla_spec] + [full(s) for s in small],
            out_specs=[pl.BlockSpec((G, TL, DO), lambda b, i: (b, i, 0)), ssd_spec, gla_spec],
            out_shape=[jax.ShapeDtypeStruct((B, L, DO), BF16)] + st_shapes,
            scratch_shapes=st_scratch + [pltpu.VMEM((G, SUBLANES + TL, D_XBC), F32),
                                         pltpu.VMEM((G, TL, D_XBC), F32),
                                         pltpu.VMEM((4, G, TL, LANES), F32),
                                         pltpu.VMEM((G, TL, D_C), F32),
                                         pltpu.VMEM((G, TL, V_D), F32)],
            compiler_params=_cparams(("parallel", "arbitrary")),
        )(proj, conv_state, ssd0l, gla0l, *small)
    return y, ssdT, glaT


def _regroup_kernel(groups, w_ref, o_ref):
    at = 0
    for lo, hi in groups:
        o_ref[:, at:at + hi - lo] = w_ref[:, lo:hi].astype(o_ref.dtype)
        at += hi - lo
    o_ref[:, at:] = jnp.zeros((o_ref.shape[0], o_ref.shape[1] - at), o_ref.dtype)


def _regroup_cols(w, groups, n_out):
    K = w.shape[0]
    return pl.pallas_call(
        functools.partial(_regroup_kernel, groups),
        out_shape=jax.ShapeDtypeStruct((K, n_out), BF16),
        compiler_params=pltpu.CompilerParams(vmem_limit_bytes=VMEM_LIMIT),
    )(w)


def _block_diag(w):
    eye = jnp.eye(N_BLK_A, dtype=w.dtype)
    return jnp.einsum('nij,nm->nimj', w, eye).reshape(D_A, D_A)


def _row(v):
    return v.reshape(1, -1).astype(F32)


def _prep_params(p):
    q = {}
    j = 0
    q['w_in_ab'] = p['w_in_ab'][j].astype(BF16)
    q['w_out_ab'] = p['w_out_ab'][j].astype(BF16)
    q['lru'] = [p['lru_conv_w'][j], _row(p['lru_conv_b'][j]),
                _block_diag(p['lru_w_r'][j]).astype(BF16), _row(p['lru_b_r'][j]),
                _block_diag(p['lru_w_i'][j]).astype(BF16), _row(p['lru_b_i'][j]), _row(p['lru_lambda'][j])]
    zw = jnp.zeros((R_W, D_B), F32)
    w2p = jnp.concatenate([p['rwkv_w2'][j], zw], axis=0).astype(BF16)
    a2p = jnp.concatenate([zw, p['rwkv_a2'][j]], axis=0).astype(BF16)
    q['rwkv'] = [_row(p['rwkv_mu'][j]), _row(p['rwkv_w0'][j]), w2p, _row(p['rwkv_a0'][j]), a2p,
                 p['rwkv_g2'][j].astype(BF16), _row(p['rwkv_k_k'][j]), _row(p['rwkv_k_a'][j]),
                 _row(p['rwkv_r_k'][j]), _row(p['rwkv_ln_w'][j]), _row(p['rwkv_ln_b'][j])]
    q['ffn'] = [p['ffn_w_gate'][j].astype(BF16), p['ffn_w_up'][j].astype(BF16), p['ffn_w_down'][j].astype(BF16)]
    w = p['w_in_cd'][j]
    o_z, o_xbc, o_dt = 0, D_C, D_C + D_XBC
    o_q = o_dt + H_C
    o_k = o_q + K_D
    o_v = o_k + K_D
    o_g = o_v + V_D
    o_al = o_g + V_D
    q['w_in_cd'] = _regroup_cols(w, [(o_z, o_dt), (o_q, o_al), (o_dt, o_q), (o_al, o_al + R_ALPHA)], IN_CD_PAD)
    q['w_out_cd'] = p['w_out_cd'][j].astype(BF16)
    lane_pad = lambda v: jnp.concatenate([v.astype(F32), jnp.zeros((LANES - v.shape[0],), F32)]).reshape(1, LANES)
    aup = jnp.zeros((LANES, K_D), F32).at[H_C:H_C + R_ALPHA].set(p['gla_alpha_up'][j]).astype(BF16)
    q['cd'] = [p['ssd_conv_w'][j], _row(p['ssd_conv_b'][j]), lane_pad(p['ssd_dt_bias'][j]),
               lane_pad(-jnp.exp(p['ssd_a_log'][j].astype(F32))), _row(jnp.repeat(p['ssd_d'][j], P_C)),
               _row(p['ssd_norm_w'][j]), aup, _row(p['gla_alpha_b'][j]), _row(p['gla_norm_w'][j])]
    q['router'] = jnp.concatenate([p['moe_router'][j].T, jnp.zeros((LANES - N_EXP, D_MODEL), F32)], axis=0)
    q['moe'] = [p['moe_w_gate'][j].astype(BF16), p['moe_w_up'][j].astype(BF16), p['moe_w_down'][j].astype(BF16)]
    q['norm_mix'] = [_row(p['norm_mix'][i]) for i in range(DEPTH)]
    q['norm_ffn'] = [_row(p['norm_ffn'][i]) for i in range(DEPTH)]
    q['norm_ple'] = [_row(p['norm_ple'][i]) for i in range(DEPTH)]
    q['w_ple_gate'] = [p['w_ple_gate'][i].astype(BF16) for i in range(DEPTH)]
    q['w_ple_proj'] = [p['w_ple_proj'][i].astype(BF16) for i in range(DEPTH)]
    q['norm_final'] = _row(p['norm_final'])
    return q


def _trunk(x, p, states, q):
    lru_conv, lru_h, rwkv_shift, rwkv_wkv, ssd_conv, ssd_st, gla_st = states
    B, L, D = x.shape
    M = B * L
    h = x.reshape(M, D)
    p2 = p.reshape(DEPTH, M, D_PLE)

    proj_dtype = BF16 if L > 1 else F32
    proj = _norm_mm(h, q['norm_mix'][0], q['w_in_ab'], IN_AB, proj_dtype).reshape(B, L, IN_AB)
    y_a, h_new = _lru(proj, lru_conv[0], lru_h[0], *q['lru'])
    y_b, wkv_new = _rwkv(proj, rwkv_shift[0], rwkv_wkv[0], *q['rwkv'])
    if L >= CONV_W - 1:
        conv_new = proj[:, L - (CONV_W - 1):, :D_A].astype(F32)
    else:
        conv_new = jnp.concatenate([lru_conv[0][:, L:], proj[:, :, :D_A]], axis=1)
    shift_new = proj[:, L - 1, 2 * D_A:].astype(F32)
    w_out = q['w_out_ab']
    h = _channel_mix(h, y_a.reshape(M, D_A), y_b.reshape(M, D_B), w_out[:D_A], w_out[D_A:],
                     q['norm_ffn'][0], *q['ffn'], q['norm_ple'][0], q['w_ple_gate'][0], p2, 0, q['w_ple_proj'][0])

    proj = _norm_mm(h, q['norm_mix'][1], q['w_in_cd'], IN_CD_PAD, proj_dtype).reshape(B, L, IN_CD_PAD)
    y_cd, ssd_new, gla_new = _cd(proj, ssd_conv[0], ssd_st[0], gla_st[0], *q['cd'])
    if L >= CONV_W - 1:
        sconv_new = proj[:, L - (CONV_W - 1):, CD_XBC:CD_XBC + D_XBC].astype(F32)
    else:
        sconv_new = jnp.concatenate([ssd_conv[0][:, L:], proj[:, :, CD_XBC:CD_XBC + D_XBC]], axis=1)
    h, route, routeT, meta = _router(h, y_cd.reshape(M, D_C + V_D), q['w_out_cd'], q['norm_ffn'][1], q['router'])
    h = _moe(h, q['norm_ffn'][1], route, routeT, meta, *q['moe'])
    y = _ple(h, q['norm_ple'][1], q['w_ple_gate'][1], p2, 1, q['w_ple_proj'][1], q['norm_final'], True)

    new_states = (conv_new[None], h_new[None], shift_new[None], wkv_new[None],
                  sconv_new[None], ssd_new[None], gla_new[None])
    return y.reshape(B, L, D), new_states


def kernel(x_prompt, x_sample, p_prompt, p_sample, state_lru_conv, state_lru_h, state_rwkv_shift,
           state_rwkv_wkv, state_ssd_conv, state_ssd, state_gla, norm_mix, norm_ffn, norm_ple,
           w_ple_gate, w_ple_proj, norm_final, w_in_ab, w_out_ab, lru_conv_w, lru_conv_b, lru_w_r,
           lru_b_r, lru_w_i, lru_b_i, lru_lambda, rwkv_mu, rwkv_w0, rwkv_w2, rwkv_a0, rwkv_a2,
           rwkv_g2, rwkv_k_k, rwkv_k_a, rwkv_r_k, rwkv_ln_w, rwkv_ln_b, ffn_w_gate, ffn_w_up,
           ffn_w_down, w_in_cd, w_out_cd, ssd_conv_w, ssd_conv_b, ssd_dt_bias, ssd_a_log, ssd_d,
           ssd_norm_w, gla_alpha_up, gla_alpha_b, gla_norm_w, moe_router, moe_w_gate, moe_w_up,
           moe_w_down):
    prm = dict(
        norm_mix=norm_mix, norm_ffn=norm_ffn, norm_ple=norm_ple, w_ple_gate=w_ple_gate,
        w_ple_proj=w_ple_proj, norm_final=norm_final, w_in_ab=w_in_ab, w_out_ab=w_out_ab,
        lru_conv_w=lru_conv_w, lru_conv_b=lru_conv_b, lru_w_r=lru_w_r, lru_b_r=lru_b_r,
        lru_w_i=lru_w_i, lru_b_i=lru_b_i, lru_lambda=lru_lambda, rwkv_mu=rwkv_mu, rwkv_w0=rwkv_w0,
        rwkv_w2=rwkv_w2, rwkv_a0=rwkv_a0, rwkv_a2=rwkv_a2, rwkv_g2=rwkv_g2, rwkv_k_k=rwkv_k_k,
        rwkv_k_a=rwkv_k_a, rwkv_r_k=rwkv_r_k, rwkv_ln_w=rwkv_ln_w, rwkv_ln_b=rwkv_ln_b,
        ffn_w_gate=ffn_w_gate, ffn_w_up=ffn_w_up, ffn_w_down=ffn_w_down, w_in_cd=w_in_cd,
        w_out_cd=w_out_cd, ssd_conv_w=ssd_conv_w, ssd_conv_b=ssd_conv_b, ssd_dt_bias=ssd_dt_bias,
        ssd_a_log=ssd_a_log, ssd_d=ssd_d, ssd_norm_w=ssd_norm_w, gla_alpha_up=gla_alpha_up,
        gla_alpha_b=gla_alpha_b, gla_norm_w=gla_norm_w, moe_router=moe_router,
        moe_w_gate=moe_w_gate, moe_w_up=moe_w_up, moe_w_down=moe_w_down)
    q = _prep_params(prm)
    states_s = (state_lru_conv, state_lru_h, state_rwkv_shift, state_rwkv_wkv,
                state_ssd_conv, state_ssd, state_gla)
    n_prompt = x_prompt.shape[0]
    states_p = tuple(jnp.zeros((s.shape[0], n_prompt) + s.shape[2:], s.dtype) for s in states_s)
    y_p, st_p = _trunk(x_prompt, p_prompt, states_p, q)
    y_s, st_s = _trunk(x_sample, p_sample, states_s, q)
    return (y_p, y_s) + tuple(st_p) + tuple(st_s)
```

```python
import functools
import math

import jax
import jax.numpy as jnp
from jax import lax
from jax.experimental import pallas as pl
from jax.experimental.pallas import tpu as pltpu

F32, BF16 = jnp.float32, jnp.bfloat16

D_MODEL = 1024
DEPTH = 2
D_PLE = 256
EPS = 1e-6
CONV_W = 4
D_A = 512
N_BLK_A = 8
BW_A = D_A // N_BLK_A
LRU_C = 8.0
H_B = 8
HD_B = 64
D_B = H_B * HD_B
R_W = 64
R_A = 64
R_G = 128
D_BPROJ = 3 * D_B + R_W + R_A + R_G
GN_EPS_B = 64e-5
IN_AB = 2 * D_A + D_BPROJ
H_C = 8
P_C = 64
D_C = H_C * P_C
G_C = 2
HG_C = H_C // G_C
N_C = 128
D_XBC = D_C + 2 * G_C * N_C
H_D = 4
DK_D = 64
DV_D = 128
K_D = H_D * DK_D
V_D = H_D * DV_D
R_ALPHA = 16
GATE_NORM = 16.0
D_FF = 2816
N_EXP = 8
D_FF_E = 1408
IN_CD_PAD = D_C + D_XBC + 2 * K_D + 2 * V_D + 128

LANES = 128
SUBLANES = 8
VMEM_LIMIT = 56 * 1024 * 1024

G_ROWS = SUBLANES
T_CHUNK = 16
TL_BLOCK = 64
TM_TOK = 1024
TM_FFN = 512
TM_MOE = 1024
MOE_RB = 64
MOE_STEP_BLOCKS = 4
MOE_SORT_ROWS = 256
MOE_UNSORT_COLS = 512
TOP_K = 2
HEAD_SUM_PIECES = 2
CONV_COLS = 256
LRU_TL = 256

MASKED_LOG = -1e30

NN = ((1,), (0,))
NT = ((1,), (1,))
TN = ((0,), (0,))


def _cparams(sem):
    return pltpu.CompilerParams(dimension_semantics=sem, vmem_limit_bytes=VMEM_LIMIT)


def _rms(x, g):
    return x * lax.rsqrt(jnp.mean(x * x, axis=-1, keepdims=True) + EPS) * g


def _sigmoid(x):
    return jax.nn.sigmoid(x)


def _silu(x):
    return x * jax.nn.sigmoid(x)


def _softplus(x):
    return jnp.maximum(x, 0.0) + jnp.log1p(jnp.exp(-jnp.abs(x)))


def _gelu_tanh(x):
    c = math.sqrt(2.0 / math.pi)
    return 0.5 * x * (1.0 + jnp.tanh(c * (x + 0.044715 * (x * x * x))))


def _d(a, b, dims=NN):
    return lax.dot_general(a, b, (dims, ((), ())), preferred_element_type=F32)


def _mm(a, b, dims=NN):
    return _d(a.astype(BF16), b.astype(BF16), dims)


def _split2(x):
    hi = x.astype(BF16)
    lo = (x - hi.astype(F32)).astype(BF16)
    return hi, lo


def _mm3(a, b, dims=NN):
    ah, al = _split2(a)
    bh, bl = _split2(b)
    return _d(ah, bh, dims) + (_d(ah, bl, dims) + _d(al, bh, dims))


def _mm_sel(sel, x, pieces, sel_left=True, dims=NN):
    out = None
    rest = x
    for i in range(pieces):
        p = rest.astype(BF16)
        if i + 1 < pieces:
            rest = rest - p.astype(F32)
        t = _d(sel, p, dims) if sel_left else _d(p, sel, dims)
        out = t if out is None else out + t
    return out


def _chunk_masks(G, T):
    R = G * T
    ri = lax.broadcasted_iota(jnp.int32, (R, R), 0)
    ci = lax.broadcasted_iota(jnp.int32, (R, R), 1)
    same = (ri // T) == (ci // T)
    incl = same & (ci <= ri)
    strict = same & (ci < ri)
    last = ci == (ri // T) * T + (T - 1)
    return incl, strict, last


def _to_bf16_mask(m):
    return jnp.where(m, 1.0, 0.0).astype(BF16)


def _batch_lane_mask(n_stack, G, T, W):
    R = G * T
    r = lax.broadcasted_iota(jnp.int32, (n_stack * R, G * W), 0)
    c = lax.broadcasted_iota(jnp.int32, (n_stack * R, G * W), 1)
    return ((r % R) // T) == (c // W)


def _head_sum_sel(width, head):
    r = lax.broadcasted_iota(jnp.int32, (width, width), 0)
    c = lax.broadcasted_iota(jnp.int32, (width, width), 1)
    return _to_bf16_mask((r // head) == (c // head))


def _norm_mm_kernel(x_ref, g_ref, w_ref, o_ref, xn_ref):
    @pl.when(pl.program_id(1) == 0)
    def _():
        xn_ref[...] = _rms(x_ref[...], g_ref[...]).astype(BF16)

    o_ref[...] = _d(xn_ref[...], w_ref[...]).astype(o_ref.dtype)


def _norm_mm(x, g, w, tn, out_dtype):
    M, K = x.shape
    N = w.shape[1]
    tm = min(TM_TOK, M)
    mode = dict(pipeline_mode=pl.Buffered(1)) if tn == N else {}
    return pl.pallas_call(
        _norm_mm_kernel,
        grid=(M // tm, N // tn),
        in_specs=[pl.BlockSpec((tm, K), lambda i, j: (i, 0)),
                  pl.BlockSpec((1, K), lambda i, j: (0, 0)),
                  pl.BlockSpec((K, tn), lambda i, j: (0, j), **mode)],
        out_specs=pl.BlockSpec((tm, tn), lambda i, j: (i, j)),
        out_shape=jax.ShapeDtypeStruct((M, N), out_dtype),
        scratch_shapes=[pltpu.VMEM((tm, K), BF16)],
        compiler_params=_cparams(("parallel", "arbitrary")),
    )(x, g, w)


def _ple_update(h, g, wg, p, wp):
    return h + _sigmoid(_mm(_rms(h, g), wg)) * _mm(p, wp)


def _channel_mix_kernel(h_ref, ya_ref, yb_ref, woa_ref, wob_ref, g_ref, wg_ref, wu_ref, wd_ref,
                        gp_ref, wpg_ref, p_ref, wpp_ref, o_ref):
    h = h_ref[...] + _mm(ya_ref[...], woa_ref[...]) + _mm(yb_ref[...], wob_ref[...])
    xn = _rms(h, g_ref[...]).astype(BF16)
    act = _silu(_d(xn, wg_ref[...])) * _d(xn, wu_ref[...])
    h = h + _mm(act, wd_ref[...])
    o_ref[...] = _ple_update(h, gp_ref[...], wpg_ref[...], p_ref[...], wpp_ref[...])


def _channel_mix(h, ya, yb, woa, wob, g, wg, wu, wd, gp, wpg, p, layer, wpp):
    M, D = h.shape
    tm = min(TM_FFN, M)
    tok = lambda a: pl.BlockSpec((tm, a.shape[1]), lambda i: (i, 0))
    res = lambda a: pl.BlockSpec(a.shape, lambda i: (0, 0), pipeline_mode=pl.Buffered(1))
    return pl.pallas_call(
        _channel_mix_kernel,
        grid=(M // tm,),
        in_specs=[tok(h), tok(ya), tok(yb), res(woa), res(wob), res(g), res(wg), res(wu), res(wd),
                  res(gp), res(wpg),
                  pl.BlockSpec((None, tm, D_PLE), lambda i: (layer, i, 0)), res(wpp)],
        out_specs=pl.BlockSpec((tm, D), lambda i: (i, 0)),
        out_shape=jax.ShapeDtypeStruct((M, D), F32),
        compiler_params=_cparams(("parallel",)),
    )(h, ya, yb, woa, wob, g, wg, wu, wd, gp, wpg, p, wpp)


def _router_kernel(h_ref, y_ref, wo_ref, g_ref, wr_ref, hout_ref, route_ref, routeT_ref, meta_ref):
    h = h_ref[...] + _mm(y_ref[...], wo_ref[...])
    hout_ref[...] = h
    xn = _rms(h, g_ref[...])
    tm = xn.shape[0]
    p = _mm3(wr_ref[...], xn, NT)[:N_EXP]
    p = jnp.exp(p - jnp.max(p, axis=0, keepdims=True))
    p = p / jnp.sum(p, axis=0, keepdims=True)
    ex = lax.broadcasted_iota(jnp.int32, (N_EXP, tm), 0).astype(F32)
    m1 = jnp.max(p, axis=0, keepdims=True)
    i1 = jnp.min(jnp.where(p == m1, ex, float(N_EXP)), axis=0, keepdims=True)
    p2 = jnp.where(ex == i1, -1.0, p)
    m2 = jnp.max(p2, axis=0, keepdims=True)
    i2 = jnp.min(jnp.where(p2 == m2, ex, float(N_EXP)), axis=0, keepdims=True)
    tot = m1 + m2
    sel1 = ex == i1
    sel2 = ex == i2
    assign = jnp.where(sel1 | sel2, 1.0, 0.0)
    ri = lax.broadcasted_iota(jnp.int32, (tm, tm), 0)
    ci = lax.broadcasted_iota(jnp.int32, (tm, tm), 1)
    rank = _d(assign.astype(BF16), _to_bf16_mask(ri < ci))
    cnt = jnp.sum(assign, axis=1, keepdims=True)
    padded = jnp.floor((cnt + (MOE_RB - 1)) * (1.0 / MOE_RB)) * MOE_RB
    padded_b = jnp.broadcast_to(padded, (N_EXP, LANES))
    esub = lax.broadcasted_iota(jnp.int32, (N_EXP, LANES), 0)
    off_b = jnp.zeros((N_EXP, LANES), F32)
    for e in range(N_EXP - 1):
        off_b = off_b + jnp.where(esub > e, padded_b[e:e + 1, :], 0.0)
    off = off_b[:, 0:1]
    pos = off + rank
    dest1 = jnp.sum(jnp.where(sel1, pos, 0.0), axis=0, keepdims=True)
    dest2 = jnp.sum(jnp.where(sel2, pos, 0.0), axis=0, keepdims=True)
    routeT = jnp.where(ex == 0.0, dest1, jnp.where(ex == 1.0, dest2,
                       jnp.where(ex == 2.0, m1 / tot, jnp.where(ex == 3.0, m2 / tot, 0.0))))
    routeT_ref[...] = routeT
    route_ref[...] = jnp.concatenate([routeT, jnp.zeros((LANES - N_EXP, tm), F32)], axis=0).T
    mlane = lax.broadcasted_iota(jnp.int32, (N_EXP, LANES), 1)
    meta_ref[...] = jnp.where(mlane == 0, padded * (1.0 / MOE_RB), jnp.where(mlane == 1, off * (1.0 / MOE_RB), 0.0))


def _router(h, y, wo, g, wr):
    M, D = h.shape
    tm = min(TM_MOE, M)
    nt = M // tm
    return pl.pallas_call(
        _router_kernel,
        grid=(nt,),
        in_specs=[pl.BlockSpec((tm, D), lambda i: (i, 0)),
                  pl.BlockSpec((tm, y.shape[1]), lambda i: (i, 0)),
                  pl.BlockSpec(wo.shape, lambda i: (0, 0)),
                  pl.BlockSpec((1, D), lambda i: (0, 0)),
                  pl.BlockSpec((LANES, D), lambda i: (0, 0))],
        out_specs=[pl.BlockSpec((tm, D), lambda i: (i, 0)),
                   pl.BlockSpec((tm, LANES), lambda i: (i, 0)),
                   pl.BlockSpec((None, SUBLANES, tm), lambda i: (i, 0, 0)),
                   pl.BlockSpec((None, SUBLANES, LANES), lambda i: (i, 0, 0))],
        out_shape=[jax.ShapeDtypeStruct((M, D), F32),
                   jax.ShapeDtypeStruct((M, LANES), F32),
                   jax.ShapeDtypeStruct((nt, SUBLANES, tm), F32),
                   jax.ShapeDtypeStruct((nt, SUBLANES, LANES), F32)],
        compiler_params=_cparams(("parallel",)),
    )(h, y, wo, g, wr)


def _moe_kernel(nblk_ref, offb_ref, h_ref, g_ref, route_ref, routeT_ref, wg_ref, wu_ref, wd_ref, o_ref,
                xs_ref, gs_ref, xn_ref):
    i = pl.program_id(0)
    e = pl.program_id(1)
    tm = h_ref.shape[0]
    cap = xs_ref.shape[0]

    last_e = pl.num_programs(1) - 1
    used_rows = (offb_ref[i, last_e] + nblk_ref[i, last_e]) * MOE_RB

    @pl.when(e == 0)
    def _():
        xn_ref[...] = _rms(h_ref[...], g_ref[...]).astype(BF16)
        xs_ref[...] = jnp.zeros_like(xs_ref)
        row0 = lax.broadcasted_iota(jnp.int32, (MOE_SORT_ROWS, tm), 0).astype(F32)

        def sort_block(lo):
            d1, d2 = routeT_ref[0:1, :], routeT_ref[1:2, :]
            g1, g2 = routeT_ref[2:3, :], routeT_ref[3:4, :]
            rows = row0 + float(lo)
            m1 = rows == d1
            m2 = rows == d2
            xs_ref[lo:lo + MOE_SORT_ROWS, :] = _d(_to_bf16_mask(m1 | m2), xn_ref[...]).astype(BF16)
            gate = jnp.sum(jnp.where(m1, g1, 0.0) + jnp.where(m2, g2, 0.0), axis=-1, keepdims=True)
            gs_ref[lo:lo + MOE_SORT_ROWS, :] = jnp.broadcast_to(gate, (MOE_SORT_ROWS, LANES))

        for lo in range(0, cap, MOE_SORT_ROWS):
            if lo < TOP_K * tm:
                sort_block(lo)
            else:
                pl.when(lo < used_rows)(functools.partial(sort_block, lo))

    base = offb_ref[i, e]
    nb = nblk_ref[i, e]

    def expert_rows(blk, rows):
        r0 = pl.multiple_of(blk * MOE_RB, MOE_RB)
        x = xs_ref[pl.ds(r0, rows), :]
        act = _silu(_d(x, wg_ref[...])) * _d(x, wu_ref[...])
        act = act * gs_ref[pl.ds(r0, rows), 0:1]
        xs_ref[pl.ds(r0, rows), :] = _mm(act, wd_ref[...]).astype(BF16)

    def full_step(j, carry):
        expert_rows(base + MOE_STEP_BLOCKS * j, MOE_STEP_BLOCKS * MOE_RB)
        return carry

    lax.fori_loop(0, nb // MOE_STEP_BLOCKS, full_step, 0)
    done = (nb // MOE_STEP_BLOCKS) * MOE_STEP_BLOCKS
    piece = MOE_STEP_BLOCKS // 2
    while piece >= 1:
        take = ((nb - done) // piece) * piece

        @pl.when(take > 0)
        def _(done=done, piece=piece):
            expert_rows(base + done, piece * MOE_RB)

        done = done + take
        piece //= 2

    @pl.when(e == last_e)
    def _():
        col0 = lax.broadcasted_iota(jnp.int32, (tm, MOE_UNSORT_COLS), 1).astype(F32)

        def unsort_block(lo):
            d1, d2 = route_ref[:, 0:1], route_ref[:, 1:2]
            cols = col0 + float(lo)
            return _d(_to_bf16_mask((cols == d1) | (cols == d2)), xs_ref[lo:lo + MOE_UNSORT_COLS, :])

        acc = h_ref[...]
        for lo in range(0, TOP_K * tm, MOE_UNSORT_COLS):
            acc = acc + unsort_block(lo)
        o_ref[...] = acc
        for lo in range(-(-TOP_K * tm // MOE_UNSORT_COLS) * MOE_UNSORT_COLS, cap, MOE_UNSORT_COLS):
            @pl.when(lo < used_rows)
            def _():
                o_ref[...] += unsort_block(lo)


def _moe(h, g, route, routeT, meta, wg, wu, wd):
    M, D = h.shape
    E, _, FF = wg.shape
    tm = min(TM_MOE, M)
    nt = M // tm
    cap = TOP_K * tm + E * MOE_RB
    cap = -(-cap // MOE_UNSORT_COLS) * MOE_UNSORT_COLS
    meta_i = meta.astype(jnp.int32)
    grid_spec = pltpu.PrefetchScalarGridSpec(
        num_scalar_prefetch=2,
        grid=(nt, E),
        in_specs=[pl.BlockSpec((tm, D), lambda i, e, nb, ob: (i, 0)),
                  pl.BlockSpec((1, D), lambda i, e, nb, ob: (0, 0)),
                  pl.BlockSpec((tm, LANES), lambda i, e, nb, ob: (i, 0)),
                  pl.BlockSpec((None, SUBLANES, tm), lambda i, e, nb, ob: (i, 0, 0)),
                  pl.BlockSpec((None, D, FF), lambda i, e, nb, ob: (e, 0, 0)),
                  pl.BlockSpec((None, D, FF), lambda i, e, nb, ob: (e, 0, 0)),
                  pl.BlockSpec((None, FF, D), lambda i, e, nb, ob: (e, 0, 0))],
        out_specs=pl.BlockSpec((tm, D), lambda i, e, nb, ob: (i, 0)),
        scratch_shapes=[pltpu.VMEM((cap, D), BF16), pltpu.VMEM((cap, LANES), F32), pltpu.VMEM((tm, D), BF16)])
    return pl.pallas_call(
        _moe_kernel,
        grid_spec=grid_spec,
        out_shape=jax.ShapeDtypeStruct((M, D), F32),
        compiler_params=_cparams(("parallel", "arbitrary")),
    )(meta_i[:, :, 0], meta_i[:, :, 1], h, g, route, routeT, wg, wu, wd)


def _ple_kernel(final, h_ref, g_ref, wg_ref, p_ref, wp_ref, gf_ref, o_ref):
    out = _ple_update(h_ref[...], g_ref[...], wg_ref[...], p_ref[...], wp_ref[...])
    if final:
        out = _rms(out, gf_ref[...])
    o_ref[...] = out


def _ple(h, g, wg, p, layer, wp, gf, final):
    M, D = h.shape
    tm = min(TM_TOK, M)
    return pl.pallas_call(
        functools.partial(_ple_kernel, final),
        grid=(M // tm,),
        in_specs=[pl.BlockSpec((tm, D), lambda i: (i, 0)),
                  pl.BlockSpec((1, D), lambda i: (0, 0)),
                  pl.BlockSpec((D, D), lambda i: (0, 0)),
                  pl.BlockSpec((None, tm, D_PLE), lambda i: (layer, i, 0)),
                  pl.BlockSpec((D_PLE, D), lambda i: (0, 0)),
                  pl.BlockSpec((1, D), lambda i: (0, 0))],
        out_specs=pl.BlockSpec((tm, D), lambda i: (i, 0)),
        out_shape=jax.ShapeDtypeStruct((M, D), F32),
        compiler_params=_cparams(("parallel",)),
    )(h, g, wg, p, wp, gf)


def _lru_gates(xc, wr, br, wi, bi, lam):
    r = _sigmoid(_mm(xc, wr) + br)
    i = _sigmoid(_mm(xc, wi) + bi)
    log_a = -LRU_C * r * _softplus(-lam)
    a = jnp.exp(log_a)
    u = jnp.sqrt(jnp.tanh(-log_a) * (a * a + 1.0)) * (i * xc)
    return a, u


def _lru_seq_kernel(TL, proj_ref, cs_ref, h0_ref, cw_ref, cb_ref, wr_ref, br_ref, wi_ref, bi_ref, lam_ref,
                    y_ref, hT_ref, pad_ref, a_ref, u_ref, hs_ref, hc_ref):
    ti = pl.program_id(0)
    G = pad_ref.shape[0]

    @pl.when(ti == 0)
    def _():
        pad_ref[:, SUBLANES - (CONV_W - 1):SUBLANES, :] = cs_ref[...]
        hc_ref[...] = h0_ref[...]

    pad_ref[:, SUBLANES:, :] = proj_ref[:, :, :D_A].astype(F32)
    cw = cw_ref[...]
    xc = cb_ref[...][None]
    for k in range(CONV_W):
        off = SUBLANES - (CONV_W - 1) + k
        xc = xc + pad_ref[:, off:off + TL, :] * cw[k:k + 1, :][None]
    pad_ref[:, :SUBLANES, :] = pad_ref[:, TL:TL + SUBLANES, :]

    a, u = _lru_gates(xc.reshape(G * TL, D_A), wr_ref[...], br_ref[...], wi_ref[...], bi_ref[...],
                      lam_ref[...])
    a_ref[...] = a.reshape(G, TL, D_A)
    u_ref[...] = u.reshape(G, TL, D_A)

    def step(t, h):
        h = a_ref[:, pl.ds(t, 1), :] * h + u_ref[:, pl.ds(t, 1), :]
        hs_ref[:, pl.ds(t, 1), :] = h
        return h

    hc_ref[...] = lax.fori_loop(0, TL, step, hc_ref[...], unroll=8)
    y_ref[...] = (_gelu_tanh(proj_ref[:, :, D_A:].astype(F32)) * hs_ref[...]).astype(y_ref.dtype)

    @pl.when(ti == pl.num_programs(0) - 1)
    def _():
        hT_ref[...] = hc_ref[...]


def _lru_step_kernel(xa_ref, ga_ref, cs_ref, h0_ref, cw_ref, cb_ref, wr_ref, br_ref, wi_ref, bi_ref, lam_ref,
                     y_ref, hT_ref):
    cw = cw_ref[...]
    x = xa_ref[...]
    xc = cb_ref[...] + x * cw[CONV_W - 1:CONV_W, :]
    for k in range(CONV_W - 1):
        xc = xc + cs_ref[k] * cw[k:k + 1, :]
    a, u = _lru_gates(xc, wr_ref[...], br_ref[...], wi_ref[...], bi_ref[...], lam_ref[...])
    h = a * h0_ref[...] + u
    hT_ref[...] = h
    y_ref[...] = _gelu_tanh(ga_ref[...]) * h


def _lru(proj, conv_state, h0, cw, cb, wr, br, wi, bi, lam):
    B, L, _ = proj.shape
    small = [cw, cb, wr, br, wi, bi, lam]
    if L == 1:
        proj2 = proj.reshape(B, -1)
        cs = jnp.transpose(conv_state, (1, 0, 2))
        tb = min(B, 128)
        full = lambda s: pl.BlockSpec(s.shape, lambda i: (0,) * s.ndim)
        y, hT = pl.pallas_call(
            _lru_step_kernel,
            grid=(B // tb,),
            in_specs=[pl.BlockSpec((tb, D_A), lambda i: (i, 0)),
                      pl.BlockSpec((tb, D_A), lambda i: (i, 1)),
                      pl.BlockSpec((CONV_W - 1, tb, D_A), lambda i: (0, i, 0)),
                      pl.BlockSpec((tb, D_A), lambda i: (i, 0))] + [full(s) for s in small],
            out_specs=[pl.BlockSpec((tb, D_A), lambda i: (i, 0)), pl.BlockSpec((tb, D_A), lambda i: (i, 0))],
            out_shape=[jax.ShapeDtypeStruct((B, D_A), F32), jax.ShapeDtypeStruct((B, D_A), F32)],
            compiler_params=_cparams(("parallel",)),
        )(proj2, proj2, cs, h0, *small)
        return y.reshape(B, 1, D_A), hT
    TL = min(LRU_TL, L)
    G = B
    full = lambda s: pl.BlockSpec(s.shape, lambda i: (0,) * s.ndim)
    y, hT = pl.pallas_call(
        functools.partial(_lru_seq_kernel, TL),
        grid=(L // TL,),
        in_specs=[pl.BlockSpec((G, TL, 2 * D_A), lambda i: (0, i, 0)),
                  pl.BlockSpec((G, CONV_W - 1, D_A), lambda i: (0, 0, 0)),
                  pl.BlockSpec((G, 1, D_A), lambda i: (0, 0, 0))] + [full(s) for s in small],
        out_specs=[pl.BlockSpec((G, TL, D_A), lambda i: (0, i, 0)),
                   pl.BlockSpec((G, 1, D_A), lambda i: (0, 0, 0))],
        out_shape=[jax.ShapeDtypeStruct((B, L, D_A), BF16), jax.ShapeDtypeStruct((B, 1, D_A), F32)],
        scratch_shapes=[pltpu.VMEM((G, SUBLANES + TL, D_A), F32),
                        pltpu.VMEM((G, TL, D_A), F32), pltpu.VMEM((G, TL, D_A), F32),
                        pltpu.VMEM((G, TL, D_A), F32), pltpu.VMEM((G, 1, D_A), F32)],
        compiler_params=_cparams(("arbitrary",)),
    )(proj, conv_state, h0.reshape(B, 1, D_A), *small)
    return y, hT.reshape(B, D_A)


def _rwkv_prep(xs, w0, w2p, a0, a2p, g2, k_k, k_a, r_k, hsel):
    r = xs[:, 0:D_B]
    k = xs[:, D_B:2 * D_B]
    v = xs[:, 2 * D_B:3 * D_B]
    wa = xs[:, 3 * D_B:3 * D_B + R_W + R_A]
    gl = xs[:, 3 * D_B + R_W + R_A:]
    w_in = w0 + _mm(jnp.tanh(wa), w2p)
    w_log = -(jnp.maximum(-w_in, 0.0) + jnp.log(1.0 + jnp.exp(-jnp.abs(w_in)))) - 0.5
    logd = -jnp.exp(w_log)
    a = _sigmoid(a0 + _mm(wa, a2p))
    g = _mm(_sigmoid(gl), g2)
    kk = k * k_k
    kk = kk * lax.rsqrt(jnp.maximum(_mm_sel(hsel, kk * kk, HEAD_SUM_PIECES, sel_left=False), 1e-12))
    k2 = k * (1.0 + (a - 1.0) * k_a)
    beta = kk * a
    bonus = _mm_sel(hsel, r * k2 * r_k, HEAD_SUM_PIECES, sel_left=False) * v
    return r, k2, v, logd, kk, beta, g, bonus


def _rwkv_post(y, bonus, g, ln_w, ln_b, hsel):
    mean = _mm_sel(hsel, y, HEAD_SUM_PIECES, sel_left=False) * (1.0 / HD_B)
    d = y - mean
    var = _mm_sel(hsel, d * d, HEAD_SUM_PIECES, sel_left=False) * (1.0 / HD_B)
    yn = d * lax.rsqrt(var + GN_EPS_B) * ln_w + ln_b
    return (yn + bonus) * g


def _rwkv_chunk(G, T, ops, st_ref, masks, emask, store_y):
    r, k2, v, logd, kk, beta = ops
    incl, strict, lmask_bf, lastmask_bf = masks
    R = G * T
    cum = _mm_sel(lmask_bf, logd, 3)
    cend = _mm_sel(lastmask_bf, cum, 3)
    ecn = jnp.exp(-cum)
    P = jnp.exp(cum - logd) * kk
    Q = ecn * beta
    Kt = ecn * k2
    Rt = jnp.exp(cum) * r
    eend = jnp.exp(cend)
    lo = lax.broadcasted_iota(jnp.int32, (R, LANES), 1) < HD_B
    row_lo = lax.broadcasted_iota(jnp.int32, (LANES, G * HD_B), 0) < HD_B
    eye = None
    if T > 1:
        ri = lax.broadcasted_iota(jnp.int32, (R, R), 0)
        ci = lax.broadcasted_iota(jnp.int32, (R, R), 1)
        eye = jnp.where(ri == ci, 1.0, 0.0)
    n_rep = G * HD_B // LANES
    NP = H_B // 2
    heads = [(c, hh) for c in range(NP) for hh in range(2)]
    own = {0: lo, 1: ~lo}

    def tiled(x, xr, hh):
        t = jnp.where(lo, x, xr) if hh == 0 else jnp.where(lo, xr, x)
        return jnp.where(emask, jnp.concatenate([t] * n_rep, axis=1), 0.0)

    pair = []
    for c in range(NP):
        sl = slice(c * LANES, (c + 1) * LANES)
        vals = (P[:, sl], Rt[:, sl], Q[:, sl], Kt[:, sl], eend[:, sl])
        pair.append(dict(x=vals, xr=[pltpu.roll(x, HD_B, 1) for x in vals], V=v[:, sl]))
    AM = [_mm(jnp.concatenate([jnp.where(own[hh], pair[c]['x'][0], 0.0),
                               jnp.where(own[hh], pair[c]['x'][1], 0.0)], axis=0),
              jnp.concatenate([pair[c]['x'][2], pair[c]['x'][3]], axis=0), NT) for c, hh in heads]
    Mqr = [jnp.where(incl, a[R:, :R], 0.0) for a in AM]
    Mkr = [jnp.where(incl, a[R:, R:], 0.0) for a in AM]
    MV = [_mm(m, pair[c]['V']) for m, (c, hh) in zip(Mkr, heads)]
    if T > 1:
        Npow = [-jnp.where(strict, a[:R, :R], 0.0) for a in AM]
        Akp = [jnp.where(strict, a[:R, R:], 0.0) for a in AM]
        AV = [_mm(m, pair[c]['V']) for m, (c, hh) in zip(Akp, heads)]
        inv = [eye + n for n in Npow]
        for _ in range(int(math.log2(T)) - 1):
            Npow = [_mm(n, n) for n in Npow]
            inv = [i + _mm(i, n) for i, n in zip(inv, Npow)]
    ex = [[tiled(x, xr, hh) for x, xr in zip(pair[c]['x'], pair[c]['xr'])] for c, hh in heads]
    WT = [st_ref[c] for c in range(NP)]
    PWRW = [_mm(jnp.concatenate([e[0], e[1]], axis=0), WT[c], NT) for e, (c, hh) in zip(ex, heads)]
    if T > 1:
        E = [_mm(i, pw[:R] + av) for i, pw, av in zip(inv, PWRW, AV)]
    else:
        E = [pw[:R] for pw in PWRW]
    Y = [pw[R:] + mv - _mm(m, e) for pw, mv, m, e in zip(PWRW, MV, Mqr, E)]
    for c in range(NP):
        lhs = jnp.concatenate([jnp.where(own[hh], x, 0.0) for hh in range(2)
                               for x in (-E[2 * c + hh], pair[c]['V'])], axis=0)
        rhs = jnp.concatenate([ex[2 * c + hh][j] for hh in range(2) for j in (2, 3)], axis=0)
        dW = _mm(lhs, rhs, TN)
        erow = [jnp.sum(ex[2 * c + hh][4], axis=0, keepdims=True) * (1.0 / T) for hh in range(2)]
        st_ref[c] = (WT[c] + dW) * jnp.where(row_lo, erow[0], erow[1])
        store_y(c, jnp.where(lo, Y[2 * c], Y[2 * c + 1]))


def _rwkv_kernel(G, T, NC, seq, proj_ref, sh_ref, s0_ref, mu_ref, w0_ref, w2_ref, a0_ref, a2_ref, g2_ref,
                 kk_ref, ka_ref, rk_ref, lnw_ref, lnb_ref, y_ref, sT_ref, *scratch):
    ti = pl.program_id(1)
    TL = T * NC
    R = G * T
    st_ref = scratch[0]
    hsel = _head_sum_sel(D_B, HD_B)

    @pl.when(ti == 0)
    def _():
        for h in range(H_B):
            r0 = (h % 2) * HD_B
            for j in range(G // 2):
                st_ref[h // 2, r0:r0 + HD_B, j * LANES:(j + 1) * LANES] = jnp.concatenate(
                    [s0_ref[2 * j, h], s0_ref[2 * j + 1, h]], axis=1)

    if seq:
        pad_ref, ops_ref, gb_ref, ys_ref = scratch[1:]

        @pl.when(ti == 0)
        def _():
            pad_ref[:, SUBLANES - 1:SUBLANES, :] = sh_ref[...]

        pad_ref[:, SUBLANES:, :] = proj_ref[:, :, 2 * D_A:].astype(F32)
        pb = pad_ref[:, SUBLANES:, :]
        prev = pad_ref[:, SUBLANES - 1:SUBLANES - 1 + TL, :]
        xs = (pb + (prev - pb) * mu_ref[...][None]).reshape(G * TL, D_BPROJ)
        pad_ref[:, :SUBLANES, :] = pad_ref[:, TL:TL + SUBLANES, :]
    else:
        pb = proj_ref[:, 2 * D_A:]
        xs = pb + (sh_ref[...] - pb) * mu_ref[...]

    r, k2, v, logd, kk, beta, g, bonus = _rwkv_prep(
        xs, w0_ref[...], w2_ref[...], a0_ref[...], a2_ref[...], g2_ref[...], kk_ref[...], ka_ref[...],
        rk_ref[...], hsel)

    incl, strict, last = _chunk_masks(G, T)
    masks = (incl, strict, _to_bf16_mask(incl), _to_bf16_mask(last))
    emask = _batch_lane_mask(1, G, T, HD_B)

    if seq:
        for n, val in enumerate((r, k2, v, logd, kk, beta)):
            ops_ref[n] = val.reshape(G, TL, D_B)
        gb_ref[0] = g.reshape(G, TL, D_B)
        gb_ref[1] = bonus.reshape(G, TL, D_B)

        def chunk(c, carry):
            t0 = pl.multiple_of(c * T, T)
            ops = tuple(ops_ref[n, :, pl.ds(t0, T), :].reshape(R, D_B) for n in range(6))

            def store_y(hp, Y):
                ys_ref[:, pl.ds(t0, T), hp * LANES:(hp + 1) * LANES] = Y.reshape(G, T, LANES)

            _rwkv_chunk(G, T, ops, st_ref, masks, emask, store_y)
            return carry

        lax.fori_loop(0, NC, chunk, 0)
        y = ys_ref[...].reshape(G * TL, D_B)
        out = _rwkv_post(y, gb_ref[1].reshape(G * TL, D_B), gb_ref[0].reshape(G * TL, D_B),
                         lnw_ref[...], lnb_ref[...], hsel)
        y_ref[...] = out.reshape(G, TL, D_B).astype(y_ref.dtype)
    else:
        ys_ref = scratch[1]

        def store_y(hp, Y):
            ys_ref[:, hp * LANES:(hp + 1) * LANES] = Y

        _rwkv_chunk(G, T, (r, k2, v, logd, kk, beta), st_ref, masks, emask, store_y)
        y_ref[...] = _rwkv_post(ys_ref[...], bonus, g, lnw_ref[...], lnb_ref[...], hsel)

    @pl.when(ti == pl.num_programs(1) - 1)
    def _():
        for h in range(H_B):
            r0 = (h % 2) * HD_B
            for b in range(G):
                sT_ref[b, h] = st_ref[h // 2, r0:r0 + HD_B, b * HD_B:(b + 1) * HD_B]


def _rwkv(proj, shift0, s0, mu, w0, w2p, a0, a2p, g2, k_k, k_a, r_k, ln_w, ln_b):
    B, L, _ = proj.shape
    G = G_ROWS
    NB = B // G
    small = [mu, w0, w2p, a0, a2p, g2, k_k, k_a, r_k, ln_w, ln_b]
    full = lambda s: pl.BlockSpec(s.shape, lambda b, i: (0,) * s.ndim)
    st_dims = (H_B // 2, 2 * HD_B, G * HD_B)
    s0l = s0
    st_spec = pl.BlockSpec((G, H_B, HD_B, HD_B), lambda b, i: (b, 0, 0, 0))
    st_shape = jax.ShapeDtypeStruct(s0.shape, F32)
    unpair = lambda s: s
    if L == 1:
        T, NC = 1, 1
        kern = functools.partial(_rwkv_kernel, G, T, NC, False)
        y, sT = pl.pallas_call(
            kern,
            grid=(NB, 1),
            in_specs=[pl.BlockSpec((G, IN_AB), lambda b, i: (b, 0)),
                      pl.BlockSpec((G, D_BPROJ), lambda b, i: (b, 0)), st_spec] + [full(s) for s in small],
            out_specs=[pl.BlockSpec((G, D_B), lambda b, i: (b, 0)), st_spec],
            out_shape=[jax.ShapeDtypeStruct((B, D_B), F32), st_shape],
            scratch_shapes=[pltpu.VMEM(st_dims, F32), pltpu.VMEM((G, D_B), F32)],
            compiler_params=_cparams(("parallel", "arbitrary")),
        )(proj.reshape(B, IN_AB), shift0, s0l, *small)
        return y.reshape(B, 1, D_B), unpair(sT)
    T = min(T_CHUNK, L)
    TL = min(TL_BLOCK, L)
    NC = TL // T
    kern = functools.partial(_rwkv_kernel, G, T, NC, True)
    y, sT = pl.pallas_call(
        kern,
        grid=(NB, L // TL),
        in_specs=[pl.BlockSpec((G, TL, IN_AB), lambda b, i: (b, i, 0)),
                  pl.BlockSpec((G, 1, D_BPROJ), lambda b, i: (b, 0, 0)), st_spec] + [full(s) for s in small],
        out_specs=[pl.BlockSpec((G, TL, D_B), lambda b, i: (b, i, 0)), st_spec],
        out_shape=[jax.ShapeDtypeStruct((B, L, D_B), BF16), st_shape],
        scratch_shapes=[pltpu.VMEM(st_dims, F32),
                        pltpu.VMEM((G, SUBLANES + TL, D_BPROJ), F32),
                        pltpu.VMEM((6, G, TL, D_B), F32),
                        pltpu.VMEM((2, G, TL, D_B), F32),
                        pltpu.VMEM((G, TL, D_B), F32)],
        compiler_params=_cparams(("parallel", "arbitrary")),
    )(proj, shift0.reshape(B, 1, D_BPROJ), s0l, *small)
    return y, unpair(sT)


CD_Z, CD_XBC, CD_Q, CD_K, CD_V, CD_G, CD_S = 0, 512, 1536, 1792, 2048, 2560, 3072


def _ssd_chunk(G, T, xs, Bm, Cm, dt_all, la_all, st_ref, masks, emaskN, store_y):
    incl, lmask_bf, lastmask_bf = masks
    R = G * T
    cum = _mm_sel(lmask_bf, la_all, 3)
    cend = _mm_sel(lastmask_bf, cum, 3)
    cumT = cum.T
    wend = jnp.exp(cend - cum) * dt_all
    ecum = jnp.exp(cum)
    dend = jnp.exp(cend)
    lo = lax.broadcasted_iota(jnp.int32, (R, LANES), 1) < P_C

    def per_head(x):
        return jnp.concatenate([jnp.where(lo, x[:, 2 * c:2 * c + 1], x[:, 2 * c + 1:2 * c + 2])
                                for c in range(H_C // 2)], axis=1)

    tile = lambda x: jnp.where(emaskN, jnp.concatenate([x] * G, axis=1), 0.0)
    xdt = xs * per_head(dt_all)
    xw = xs * per_head(wend)
    ecf = per_head(ecum)
    Bg = [Bm[:, g * N_C:(g + 1) * N_C] for g in range(G_C)]
    Cg = [Cm[:, g * N_C:(g + 1) * N_C] for g in range(G_C)]
    CB = [_mm(c, b, NT) for c, b in zip(Cg, Bg)]
    ST = [st_ref[g] for g in range(G_C)]
    CS = [_mm(tile(c), s, NT) for c, s in zip(Cg, ST)]
    W = [CB[h // HG_C] * jnp.exp(jnp.where(incl, cum[:, h:h + 1] - cumT[h:h + 1, :], MASKED_LOG))
         for h in range(H_C)]
    intra = [_mm(W[h], xdt[:, (h // 2) * LANES:(h // 2 + 1) * LANES]) for h in range(H_C)]
    drow = [jnp.sum(jnp.where(emaskN, dend[:, h:h + 1], 0.0), axis=0, keepdims=True) * (1.0 / T)
            for h in range(H_C)]
    for c in range(H_C // 2):
        sl = slice(c * LANES, (c + 1) * LANES)
        gsl = slice((c % (HG_C // 2)) * LANES, (c % (HG_C // 2) + 1) * LANES)
        store_y(c, jnp.where(lo, intra[2 * c], intra[2 * c + 1]) + ecf[:, sl] * CS[c // (HG_C // 2)][:, gsl])
    for g in range(G_C):
        W_g = HG_C * P_C
        dS = _mm(xw[:, g * W_g:(g + 1) * W_g], tile(Bg[g]), TN)
        st_ref[g] = jnp.concatenate(
            [ST[g][hh * P_C:(hh + 1) * P_C] * drow[g * HG_C + hh] + dS[hh * P_C:(hh + 1) * P_C]
             for hh in range(HG_C)], axis=0)


def _gla_chunk(G, T, q, k, v, la, st_ref, masks, emask, store_o):
    incl, lmask_bf, lastmask_bf = masks
    R = G * T
    bc = _mm_sel(lmask_bf, la, 3)
    bend = _mm_sel(lastmask_bf, bc, 3)
    qe = q * jnp.exp(bc)
    kn = k * jnp.exp(-bc)
    kw = k * jnp.exp(bend - bc)
    dend = jnp.exp(bend)
    lo = lax.broadcasted_iota(jnp.int32, (R, LANES), 1) < DK_D
    own = {0: lo, 1: ~lo}
    n_rep = G * DK_D // LANES

    def tiled(x, xr, hh):
        t = jnp.where(lo, x, xr) if hh == 0 else jnp.where(lo, xr, x)
        return jnp.where(emask, jnp.concatenate([t] * n_rep, axis=1), 0.0)

    heads = [(c, hh) for c in range(H_D // 2) for hh in range(2)]
    pair = []
    for c in range(H_D // 2):
        sl = slice(c * LANES, (c + 1) * LANES)
        vals = (qe[:, sl], kw[:, sl], dend[:, sl])
        pair.append(dict(x=vals, xr=[pltpu.roll(x, DK_D, 1) for x in vals], kn=kn[:, sl]))
    vh = [v[:, h * DV_D:(h + 1) * DV_D] for h in range(H_D)]
    att = [jnp.where(incl, _mm(jnp.where(own[hh], pair[c]['x'][0], 0.0), pair[c]['kn'], NT), 0.0)
           for c, hh in heads]
    ex = [[tiled(x, xr, hh) for x, xr in zip(pair[c]['x'], pair[c]['xr'])] for c, hh in heads]
    ST = [st_ref[h] for h in range(H_D)]
    o = [_mm(a, vv) + _mm(e[0], s, NT) for a, vv, e, s in zip(att, vh, ex, ST)]
    for h in range(H_D):
        drow = jnp.sum(ex[h][2], axis=0, keepdims=True) * (1.0 / T)
        st_ref[h] = ST[h] * drow + _mm(vh[h], ex[h][1], TN)
        store_o(h, o[h])


def _cd_prep_small(small, dtb, a_neg, aup, ab):
    dt_all = _softplus(small + dtb)
    la_all = dt_all * a_neg
    la_gla = -_softplus(-(_mm(small, aup) + ab)) * (1.0 / GATE_NORM)
    return dt_all, la_all, la_gla


def _cd_post(y, xs, z, o, gg, dskip, ssd_norm, gla_norm):
    y_c = _rms((y + dskip * xs) * _silu(z), ssd_norm)
    outs = [y_c]
    for h in range(H_D):
        sl = slice(h * DV_D, (h + 1) * DV_D)
        outs.append(_rms(o[:, sl], gla_norm) * _silu(gg[:, sl]))
    return outs


def _cd_kernel(G, T, NC, seq, proj_ref, cs_ref, ssd0_ref, gla0_ref, cw_ref, cb_ref, dtb_ref, aneg_ref, dskip_ref,
               ssdn_ref, aup_ref, ab_ref, glan_ref, y_ref, ssdT_ref, glaT_ref, *scratch):
    ti = pl.program_id(1)
    TL = T * NC
    R = G * T
    sst_ref, gst_ref = scratch[0], scratch[1]

    @pl.when(ti == 0)
    def _():
        for h in range(H_C):
            r0 = (h % HG_C) * P_C
            for b in range(G):
                sst_ref[h // HG_C, r0:r0 + P_C, b * N_C:(b + 1) * N_C] = ssd0_ref[b, h]
        for h in range(H_D):
            for j in range(G // 2):
                two = jnp.concatenate([gla0_ref[2 * j, h], gla0_ref[2 * j + 1, h]], axis=0)
                gst_ref[h, :, j * LANES:(j + 1) * LANES] = two.T

    cw = cw_ref[...]
    if seq:
        pad_ref, xbc_ref, sm_ref, ys_ref, os_ref = scratch[2:]

        @pl.when(ti == 0)
        def _():
            pad_ref[:, SUBLANES - (CONV_W - 1):SUBLANES, :] = cs_ref[...]

        pad_ref[:, SUBLANES:, :] = proj_ref[:, :, CD_XBC:CD_XBC + D_XBC].astype(F32)

        def conv_b(b, carry):
            for c0 in range(0, D_XBC, CONV_COLS):
                cols = slice(c0, c0 + CONV_COLS)
                acc = cb_ref[:, cols]
                for kq in range(CONV_W):
                    off = SUBLANES - (CONV_W - 1) + kq
                    acc = acc + pad_ref[b, off:off + TL, cols] * cw_ref[kq:kq + 1, cols]
                xbc_ref[b, :, cols] = _silu(acc)
            return carry

        lax.fori_loop(0, G, conv_b, 0)
        pad_ref[:, :SUBLANES, :] = pad_ref[:, TL:TL + SUBLANES, :]
        small = proj_ref[:, :, CD_S:].astype(F32).reshape(G * TL, LANES)
    else:
        x = proj_ref[:, CD_XBC:CD_XBC + D_XBC]
        xc = cb_ref[...] + x * cw[CONV_W - 1:CONV_W, :]
        for kq in range(CONV_W - 1):
            xc = xc + cs_ref[kq] * cw[kq:kq + 1, :]
        xbc = _silu(xc)
        small = proj_ref[:, CD_S:]

    dt_all, la_all, la_gla = _cd_prep_small(small, dtb_ref[...], aneg_ref[...], aup_ref[...], ab_ref[...])

    incl, _, last = _chunk_masks(G, T)
    masks = (incl, _to_bf16_mask(incl), _to_bf16_mask(last))
    emaskN = _batch_lane_mask(1, G, T, N_C)
    emask1 = _batch_lane_mask(1, G, T, DK_D)
    scale = DK_D ** -0.5

    if seq:
        sm_ref[0] = dt_all.reshape(G, TL, LANES)
        sm_ref[1] = la_all.reshape(G, TL, LANES)
        sm_ref[2] = la_gla[:, :LANES].reshape(G, TL, LANES)
        sm_ref[3] = la_gla[:, LANES:].reshape(G, TL, LANES)

        def chunk(c, carry):
            t0 = pl.multiple_of(c * T, T)
            rows = lambda ref, lo, hi: ref[:, pl.ds(t0, T), lo:hi].astype(F32).reshape(R, hi - lo)
            xs = rows(xbc_ref, 0, D_C)
            Bm = rows(xbc_ref, D_C, D_C + G_C * N_C)
            Cm = rows(xbc_ref, D_C + G_C * N_C, D_XBC)
            dtc = sm_ref[0, :, pl.ds(t0, T), :].reshape(R, LANES)
            lac = sm_ref[1, :, pl.ds(t0, T), :].reshape(R, LANES)

            def store_y(hp, yv):
                ys_ref[:, pl.ds(t0, T), hp * LANES:(hp + 1) * LANES] = yv.reshape(G, T, LANES)

            _ssd_chunk(G, T, xs, Bm, Cm, dtc, lac, sst_ref, masks, emaskN, store_y)

            q = rows(proj_ref, CD_Q, CD_Q + K_D) * scale
            k = rows(proj_ref, CD_K, CD_K + K_D)
            v = rows(proj_ref, CD_V, CD_V + V_D)
            lag = jnp.concatenate([sm_ref[2, :, pl.ds(t0, T), :].reshape(R, LANES),
                                   sm_ref[3, :, pl.ds(t0, T), :].reshape(R, LANES)], axis=1)

            def store_o(h, ov):
                os_ref[:, pl.ds(t0, T), h * DV_D:(h + 1) * DV_D] = ov.reshape(G, T, DV_D)

            _gla_chunk(G, T, q, k, v, lag, gst_ref, masks, emask1, store_o)
            return carry

        lax.fori_loop(0, NC, chunk, 0)
        def post_b(b, carry):
            outs = _cd_post(ys_ref[b], xbc_ref[b, :, :D_C], proj_ref[b, :, CD_Z:CD_Z + D_C].astype(F32),
                            os_ref[b], proj_ref[b, :, CD_G:CD_G + V_D].astype(F32),
                            dskip_ref[...], ssdn_ref[...], glan_ref[...])
            y_ref[b, :, :D_C] = outs[0].astype(y_ref.dtype)
            for h in range(H_D):
                y_ref[b, :, D_C + h * DV_D:D_C + (h + 1) * DV_D] = outs[1 + h].astype(y_ref.dtype)
            return carry

        lax.fori_loop(0, G, post_b, 0)
    else:
        ys_ref, os_ref = scratch[2:]

        def store_y(hp, yv):
            ys_ref[:, hp * LANES:(hp + 1) * LANES] = yv

        def store_o(h, ov):
            os_ref[:, h * DV_D:(h + 1) * DV_D] = ov

        _ssd_chunk(G, T, xbc[:, :D_C], xbc[:, D_C:D_C + G_C * N_C], xbc[:, D_C + G_C * N_C:], dt_all, la_all,
                   sst_ref, masks, emaskN, store_y)
        _gla_chunk(G, T, proj_ref[:, CD_Q:CD_Q + K_D] * scale, proj_ref[:, CD_K:CD_K + K_D],
                   proj_ref[:, CD_V:CD_V + V_D], la_gla, gst_ref, masks, emask1, store_o)
        outs = _cd_post(ys_ref[...], xbc[:, :D_C], proj_ref[:, CD_Z:CD_Z + D_C], os_ref[...],
                        proj_ref[:, CD_G:CD_G + V_D], dskip_ref[...], ssdn_ref[...], glan_ref[...])
        y_ref[:, :D_C] = outs[0]
        for h in range(H_D):
            y_ref[:, D_C + h * DV_D:D_C + (h + 1) * DV_D] = outs[1 + h]

    @pl.when(ti == pl.num_programs(1) - 1)
    def _():
        for h in range(H_C):
            r0 = (h % HG_C) * P_C
            for b in range(G):
                ssdT_ref[b, h] = sst_ref[h // HG_C, r0:r0 + P_C, b * N_C:(b + 1) * N_C]
        for h in range(H_D):
            for j in range(G // 2):
                two = gst_ref[h, :, j * LANES:(j + 1) * LANES].T
                glaT_ref[2 * j, h] = two[:DK_D]
                glaT_ref[2 * j + 1, h] = two[DK_D:]


def _cd(proj, conv_state, ssd0, gla0, cw, cb, dtb, a_neg, dskip, ssd_norm, aup, ab, gla_norm):
    B, L, _ = proj.shape
    G = G_ROWS
    NB = B // G
    small = [cw, cb, dtb, a_neg, dskip, ssd_norm, aup, ab, gla_norm]
    full = lambda s: pl.BlockSpec(s.shape, lambda b, i: (0,) * s.ndim)
    ssd_dims = (G_C, HG_C * P_C, G * N_C)
    ssd0l, gla0l = ssd0, gla0
    ssd_spec = pl.BlockSpec((G, H_C, P_C, N_C), lambda b, i: (b, 0, 0, 0))
    gla_spec = pl.BlockSpec((G, H_D, DK_D, DV_D), lambda b, i: (b, 0, 0, 0))
    st_shapes = [jax.ShapeDtypeStruct(ssd0.shape, F32), jax.ShapeDtypeStruct(gla0.shape, F32)]
    st_scratch = [pltpu.VMEM(ssd_dims, F32), pltpu.VMEM((H_D, DV_D, G * DK_D), F32)]
    DO = D_C + V_D
    if L == 1:
        kern = functools.partial(_cd_kernel, G, 1, 1, False)
        y, ssdT, glaT = pl.pallas_call(
            kern,
            grid=(NB, 1),
            in_specs=[pl.BlockSpec((G, IN_CD_PAD), lambda b, i: (b, 0)),
                      pl.BlockSpec((CONV_W - 1, G, D_XBC), lambda b, i: (0, b, 0)),
                      ssd_spec, gla_spec] + [full(s) for s in small],
            out_specs=[pl.BlockSpec((G, DO), lambda b, i: (b, 0)), ssd_spec, gla_spec],
            out_shape=[jax.ShapeDtypeStruct((B, DO), F32)] + st_shapes,
            scratch_shapes=st_scratch + [pltpu.VMEM((G, D_C), F32), pltpu.VMEM((G, V_D), F32)],
            compiler_params=_cparams(("parallel", "arbitrary")),
        )(proj.reshape(B, IN_CD_PAD), jnp.transpose(conv_state, (1, 0, 2)), ssd0l, gla0l, *small)
        y = y.reshape(B, 1, DO)
    else:
        T = min(T_CHUNK, L)
        TL = min(TL_BLOCK, L)
        NC = TL // T
        kern = functools.partial(_cd_kernel, G, T, NC, True)
        y, ssdT, glaT = pl.pallas_call(
            kern,
            grid=(NB, L // TL),
            in_specs=[pl.BlockSpec((G, TL, IN_CD_PAD), lambda b, i: (b, i, 0)),
                      pl.BlockSpec((G, CONV_W - 1, D_XBC), lambda b, i: (b, 0, 0)),
                      ssd_spec, gla_spec] + [full(s) for s in small],
            out_specs=[pl.BlockSpec((G, TL, DO), lambda b, i: (b, i, 0)), ssd_spec, gla_spec],
            out_shape=[jax.ShapeDtypeStruct((B, L, DO), BF16)] + st_shapes,
            scratch_shapes=st_scratch + [pltpu.VMEM((G, SUBLANES + TL, D_XBC), F32),
                                         pltpu.VMEM((G, TL, D_XBC), F32),
                                         pltpu.VMEM((4, G, TL, LANES), F32),
                                         pltpu.VMEM((G, TL, D_C), F32),
                                         pltpu.VMEM((G, TL, V_D), F32)],
            compiler_params=_cparams(("parallel", "arbitrary")),
        )(proj, conv_state, ssd0l, gla0l, *small)
    return y, ssdT, glaT


def _regroup_kernel(groups, w_ref, o_ref):
    at = 0
    for lo, hi in groups:
        o_ref[:, at:at + hi - lo] = w_ref[:, lo:hi].astype(o_ref.dtype)
        at += hi - lo
    o_ref[:, at:] = jnp.zeros((o_ref.shape[0], o_ref.shape[1] - at), o_ref.dtype)


def _regroup_cols(w, groups, n_out):
    K = w.shape[0]
    return pl.pallas_call(
        functools.partial(_regroup_kernel, groups),
        out_shape=jax.ShapeDtypeStruct((K, n_out), BF16),
        compiler_params=pltpu.CompilerParams(vmem_limit_bytes=VMEM_LIMIT),
    )(w)


def _block_diag(w):
    eye = jnp.eye(N_BLK_A, dtype=w.dtype)
    return jnp.einsum('nij,nm->nimj', w, eye).reshape(D_A, D_A)


def _row(v):
    return v.reshape(1, -1).astype(F32)


def _prep_params(p):
    q = {}
    j = 0
    q['w_in_ab'] = p['w_in_ab'][j].astype(BF16)
    q['w_out_ab'] = p['w_out_ab'][j].astype(BF16)
    q['lru'] = [p['lru_conv_w'][j], _row(p['lru_conv_b'][j]),
                _block_diag(p['lru_w_r'][j]).astype(BF16), _row(p['lru_b_r'][j]),
                _block_diag(p['lru_w_i'][j]).astype(BF16), _row(p['lru_b_i'][j]), _row(p['lru_lambda'][j])]
    zw = jnp.zeros((R_W, D_B), F32)
    w2p = jnp.concatenate([p['rwkv_w2'][j], zw], axis=0).astype(BF16)
    a2p = jnp.concatenate([zw, p['rwkv_a2'][j]], axis=0).astype(BF16)
    q['rwkv'] = [_row(p['rwkv_mu'][j]), _row(p['rwkv_w0'][j]), w2p, _row(p['rwkv_a0'][j]), a2p,
                 p['rwkv_g2'][j].astype(BF16), _row(p['rwkv_k_k'][j]), _row(p['rwkv_k_a'][j]),
                 _row(p['rwkv_r_k'][j]), _row(p['rwkv_ln_w'][j]), _row(p['rwkv_ln_b'][j])]
    q['ffn'] = [p['ffn_w_gate'][j].astype(BF16), p['ffn_w_up'][j].astype(BF16), p['ffn_w_down'][j].astype(BF16)]
    w = p['w_in_cd'][j]
    o_z, o_xbc, o_dt = 0, D_C, D_C + D_XBC
    o_q = o_dt + H_C
    o_k = o_q + K_D
    o_v = o_k + K_D
    o_g = o_v + V_D
    o_al = o_g + V_D
    q['w_in_cd'] = _regroup_cols(w, [(o_z, o_dt), (o_q, o_al), (o_dt, o_q), (o_al, o_al + R_ALPHA)], IN_CD_PAD)
    q['w_out_cd'] = p['w_out_cd'][j].astype(BF16)
    lane_pad = lambda v: jnp.concatenate([v.astype(F32), jnp.zeros((LANES - v.shape[0],), F32)]).reshape(1, LANES)
    aup = jnp.zeros((LANES, K_D), F32).at[H_C:H_C + R_ALPHA].set(p['gla_alpha_up'][j]).astype(BF16)
    q['cd'] = [p['ssd_conv_w'][j], _row(p['ssd_conv_b'][j]), lane_pad(p['ssd_dt_bias'][j]),
               lane_pad(-jnp.exp(p['ssd_a_log'][j].astype(F32))), _row(jnp.repeat(p['ssd_d'][j], P_C)),
               _row(p['ssd_norm_w'][j]), aup, _row(p['gla_alpha_b'][j]), _row(p['gla_norm_w'][j])]
    q['router'] = jnp.concatenate([p['moe_router'][j].T, jnp.zeros((LANES - N_EXP, D_MODEL), F32)], axis=0)
    q['moe'] = [p['moe_w_gate'][j].astype(BF16), p['moe_w_up'][j].astype(BF16), p['moe_w_down'][j].astype(BF16)]
    q['norm_mix'] = [_row(p['norm_mix'][i]) for i in range(DEPTH)]
    q['norm_ffn'] = [_row(p['norm_ffn'][i]) for i in range(DEPTH)]
    q['norm_ple'] = [_row(p['norm_ple'][i]) for i in range(DEPTH)]
    q['w_ple_gate'] = [p['w_ple_gate'][i].astype(BF16) for i in range(DEPTH)]
    q['w_ple_proj'] = [p['w_ple_proj'][i].astype(BF16) for i in range(DEPTH)]
    q['norm_final'] = _row(p['norm_final'])
    return q


def _trunk(x, p, states, q):
    lru_conv, lru_h, rwkv_shift, rwkv_wkv, ssd_conv, ssd_st, gla_st = states
    B, L, D = x.shape
    M = B * L
    h = x.reshape(M, D)
    p2 = p.reshape(DEPTH, M, D_PLE)

    proj_dtype = BF16 if L > 1 else F32
    proj = _norm_mm(h, q['norm_mix'][0], q['w_in_ab'], IN_AB, proj_dtype).reshape(B, L, IN_AB)
    y_a, h_new = _lru(proj, lru_conv[0], lru_h[0], *q['lru'])
    y_b, wkv_new = _rwkv(proj, rwkv_shift[0], rwkv_wkv[0], *q['rwkv'])
    if L >= CONV_W - 1:
        conv_new = proj[:, L - (CONV_W - 1):, :D_A].astype(F32)
    else:
        conv_new = jnp.concatenate([lru_conv[0][:, L:], proj[:, :, :D_A]], axis=1)
    shift_new = proj[:, L - 1, 2 * D_A:].astype(F32)
    w_out = q['w_out_ab']
    h = _channel_mix(h, y_a.reshape(M, D_A), y_b.reshape(M, D_B), w_out[:D_A], w_out[D_A:],
                     q['norm_ffn'][0], *q['ffn'], q['norm_ple'][0], q['w_ple_gate'][0], p2, 0, q['w_ple_proj'][0])

    proj = _norm_mm(h, q['norm_mix'][1], q['w_in_cd'], IN_CD_PAD, proj_dtype).reshape(B, L, IN_CD_PAD)
    y_cd, ssd_new, gla_new = _cd(proj, ssd_conv[0], ssd_st[0], gla_st[0], *q['cd'])
    if L >= CONV_W - 1:
        sconv_new = proj[:, L - (CONV_W - 1):, CD_XBC:CD_XBC + D_XBC].astype(F32)
    else:
        sconv_new = jnp.concatenate([ssd_conv[0][:, L:], proj[:, :, CD_XBC:CD_XBC + D_XBC]], axis=1)
    h, route, routeT, meta = _router(h, y_cd.reshape(M, D_C + V_D), q['w_out_cd'], q['norm_ffn'][1], q['router'])
    h = _moe(h, q['norm_ffn'][1], route, routeT, meta, *q['moe'])
    y = _ple(h, q['norm_ple'][1], q['w_ple_gate'][1], p2, 1, q['w_ple_proj'][1], q['norm_final'], True)

    new_states = (conv_new[None], h_new[None], shift_new[None], wkv_new[None],
                  sconv_new[None], ssd_new[None], gla_new[None])
    return y.reshape(B, L, D), new_states


def kernel(x_prompt, x_sample, p_prompt, p_sample, state_lru_conv, state_lru_h, state_rwkv_shift,
           state_rwkv_wkv, state_ssd_conv, state_ssd, state_gla, norm_mix, norm_ffn, norm_ple,
           w_ple_gate, w_ple_proj, norm_final, w_in_ab, w_out_ab, lru_conv_w, lru_conv_b, lru_w_r,
           lru_b_r, lru_w_i, lru_b_i, lru_lambda, rwkv_mu, rwkv_w0, rwkv_w2, rwkv_a0, rwkv_a2,
           rwkv_g2, rwkv_k_k, rwkv_k_a, rwkv_r_k, rwkv_ln_w, rwkv_ln_b, ffn_w_gate, ffn_w_up,
           ffn_w_down, w_in_cd, w_out_cd, ssd_conv_w, ssd_conv_b, ssd_dt_bias, ssd_a_log, ssd_d,
           ssd_norm_w, gla_alpha_up, gla_alpha_b, gla_norm_w, moe_router, moe_w_gate, moe_w_up,
           moe_w_down):
    prm = dict(
        norm_mix=norm_mix, norm_ffn=norm_ffn, norm_ple=norm_ple, w_ple_gate=w_ple_gate,
        w_ple_proj=w_ple_proj, norm_final=norm_final, w_in_ab=w_in_ab, w_out_ab=w_out_ab,
        lru_conv_w=lru_conv_w, lru_conv_b=lru_conv_b, lru_w_r=lru_w_r, lru_b_r=lru_b_r,
        lru_w_i=lru_w_i, lru_b_i=lru_b_i, lru_lambda=lru_lambda, rwkv_mu=rwkv_mu, rwkv_w0=rwkv_w0,
        rwkv_w2=rwkv_w2, rwkv_a0=rwkv_a0, rwkv_a2=rwkv_a2, rwkv_g2=rwkv_g2, rwkv_k_k=rwkv_k_k,
        rwkv_k_a=rwkv_k_a, rwkv_r_k=rwkv_r_k, rwkv_ln_w=rwkv_ln_w, rwkv_ln_b=rwkv_ln_b,
        ffn_w_gate=ffn_w_gate, ffn_w_up=ffn_w_up, ffn_w_down=ffn_w_down, w_in_cd=w_in_cd,
        w_out_cd=w_out_cd, ssd_conv_w=ssd_conv_w, ssd_conv_b=ssd_conv_b, ssd_dt_bias=ssd_dt_bias,
        ssd_a_log=ssd_a_log, ssd_d=ssd_d, ssd_norm_w=ssd_norm_w, gla_alpha_up=gla_alpha_up,
        gla_alpha_b=gla_alpha_b, gla_norm_w=gla_norm_w, moe_router=moe_router,
        moe_w_gate=moe_w_gate, moe_w_up=moe_w_up, moe_w_down=moe_w_down)
    q = _prep_params(prm)
    states_s = (state_lru_conv, state_lru_h, state_rwkv_shift, state_rwkv_wkv,
                state_ssd_conv, state_ssd, state_gla)
    n_prompt = x_prompt.shape[0]
    states_p = tuple(jnp.zeros((s.shape[0], n_prompt) + s.shape[2:], s.dtype) for s in states_s)
    y_p, st_p = _trunk(x_prompt, p_prompt, states_p, q)
    y_s, st_s = _trunk(x_sample, p_sample, states_s, q)
    return (y_p, y_s) + tuple(st_p) + tuple(st_s)
```

```python
import functools
import math

import jax
import jax.numpy as jnp
from jax import lax
from jax.experimental import pallas as pl
from jax.experimental.pallas import tpu as pltpu

F32, BF16 = jnp.float32, jnp.bfloat16

D_MODEL = 1024
DEPTH = 2
D_PLE = 256
EPS = 1e-6
CONV_W = 4
D_A = 512
N_BLK_A = 8
BW_A = D_A // N_BLK_A
LRU_C = 8.0
H_B = 8
HD_B = 64
D_B = H_B * HD_B
R_W = 64
R_A = 64
R_G = 128
D_BPROJ = 3 * D_B + R_W + R_A + R_G
GN_EPS_B = 64e-5
IN_AB = 2 * D_A + D_BPROJ
H_C = 8
P_C = 64
D_C = H_C * P_C
G_C = 2
HG_C = H_C // G_C
N_C = 128
D_XBC = D_C + 2 * G_C * N_C
H_D = 4
DK_D = 64
DV_D = 128
K_D = H_D * DK_D
V_D = H_D * DV_D
R_ALPHA = 16
GATE_NORM = 16.0
D_FF = 2816
N_EXP = 8
D_FF_E = 1408
IN_CD_PAD = D_C + D_XBC + 2 * K_D + 2 * V_D + 128

LANES = 128
SUBLANES = 8
VMEM_LIMIT = 56 * 1024 * 1024

G_ROWS = SUBLANES
G_ROWS_STEP = 2 * SUBLANES
T_CHUNK = 16
TL_BLOCK = 64
TM_TOK = 1024
TM_FFN = 512
TM_MOE = 1024
MOE_RB = 64
MOE_STEP_BLOCKS = 4
MOE_SORT_ROWS = 512
MOE_UNSORT_COLS = 512
TOP_K = 2
HEAD_SUM_PIECES = 2
CONV_COLS = 256
LRU_TL = 256

MASKED_LOG = -1e30

NN = ((1,), (0,))
NT = ((1,), (1,))
TN = ((0,), (0,))


def _cparams(sem):
    return pltpu.CompilerParams(dimension_semantics=sem, vmem_limit_bytes=VMEM_LIMIT)


def _rms(x, g):
    return x * lax.rsqrt(jnp.mean(x * x, axis=-1, keepdims=True) + EPS) * g


def _sigmoid(x):
    return jax.nn.sigmoid(x)


def _silu(x):
    return x * jax.nn.sigmoid(x)


def _softplus(x):
    return jnp.maximum(x, 0.0) + jnp.log1p(jnp.exp(-jnp.abs(x)))


def _gelu_tanh(x):
    c = math.sqrt(2.0 / math.pi)
    return 0.5 * x * (1.0 + jnp.tanh(c * (x + 0.044715 * (x * x * x))))


def _d(a, b, dims=NN):
    return lax.dot_general(a, b, (dims, ((), ())), preferred_element_type=F32)


def _mm(a, b, dims=NN):
    return _d(a.astype(BF16), b.astype(BF16), dims)


def _split2(x):
    hi = x.astype(BF16)
    lo = (x - hi.astype(F32)).astype(BF16)
    return hi, lo


def _mm3(a, b, dims=NN):
    ah, al = _split2(a)
    bh, bl = _split2(b)
    return _d(ah, bh, dims) + (_d(ah, bl, dims) + _d(al, bh, dims))


def _mm_sel(sel, x, pieces, sel_left=True, dims=NN):
    out = None
    rest = x
    for i in range(pieces):
        p = rest.astype(BF16)
        if i + 1 < pieces:
            rest = rest - p.astype(F32)
        t = _d(sel, p, dims) if sel_left else _d(p, sel, dims)
        out = t if out is None else out + t
    return out


def _chunk_masks(G, T):
    R = G * T
    ri = lax.broadcasted_iota(jnp.int32, (R, R), 0)
    ci = lax.broadcasted_iota(jnp.int32, (R, R), 1)
    same = (ri // T) == (ci // T)
    incl = same & (ci <= ri)
    strict = same & (ci < ri)
    last = ci == (ri // T) * T + (T - 1)
    return incl, strict, last


def _to_bf16_mask(m):
    return jnp.where(m, 1.0, 0.0).astype(BF16)


def _batch_lane_mask(n_stack, G, T, W):
    R = G * T
    r = lax.broadcasted_iota(jnp.int32, (n_stack * R, G * W), 0)
    c = lax.broadcasted_iota(jnp.int32, (n_stack * R, G * W), 1)
    return ((r % R) // T) == (c // W)


def _head_sum_sel(width, head):
    r = lax.broadcasted_iota(jnp.int32, (width, width), 0)
    c = lax.broadcasted_iota(jnp.int32, (width, width), 1)
    return _to_bf16_mask((r // head) == (c // head))


def _norm_mm_kernel(x_ref, g_ref, w_ref, o_ref, xn_ref):
    @pl.when(pl.program_id(1) == 0)
    def _():
        xn_ref[...] = _rms(x_ref[...], g_ref[...]).astype(BF16)

    o_ref[...] = _d(xn_ref[...], w_ref[...]).astype(o_ref.dtype)


def _norm_mm(x, g, w, tn, out_dtype):
    M, K = x.shape
    N = w.shape[1]
    tm = min(TM_TOK, M)
    mode = dict(pipeline_mode=pl.Buffered(1)) if tn == N else {}
    return pl.pallas_call(
        _norm_mm_kernel,
        grid=(M // tm, N // tn),
        in_specs=[pl.BlockSpec((tm, K), lambda i, j: (i, 0)),
                  pl.BlockSpec((1, K), lambda i, j: (0, 0)),
                  pl.BlockSpec((K, tn), lambda i, j: (0, j), **mode)],
        out_specs=pl.BlockSpec((tm, tn), lambda i, j: (i, j)),
        out_shape=jax.ShapeDtypeStruct((M, N), out_dtype),
        scratch_shapes=[pltpu.VMEM((tm, K), BF16)],
        compiler_params=_cparams(("parallel", "arbitrary")),
    )(x, g, w)


def _ple_update(h, g, wg, p, wp):
    return h + _sigmoid(_mm(_rms(h, g), wg)) * _mm(p, wp)


def _channel_mix_kernel(h_ref, ya_ref, yb_ref, woa_ref, wob_ref, g_ref, wg_ref, wu_ref, wd_ref,
                        gp_ref, wpg_ref, p_ref, wpp_ref, o_ref):
    h = h_ref[...] + _mm(ya_ref[...], woa_ref[...]) + _mm(yb_ref[...], wob_ref[...])
    xn = _rms(h, g_ref[...]).astype(BF16)
    act = _silu(_d(xn, wg_ref[...])) * _d(xn, wu_ref[...])
    h = h + _mm(act, wd_ref[...])
    o_ref[...] = _ple_update(h, gp_ref[...], wpg_ref[...], p_ref[...], wpp_ref[...])


def _channel_mix(h, ya, yb, woa, wob, g, wg, wu, wd, gp, wpg, p, layer, wpp):
    M, D = h.shape
    tm = min(TM_FFN, M)
    tok = lambda a: pl.BlockSpec((tm, a.shape[1]), lambda i: (i, 0))
    res = lambda a: pl.BlockSpec(a.shape, lambda i: (0, 0), pipeline_mode=pl.Buffered(1))
    return pl.pallas_call(
        _channel_mix_kernel,
        grid=(M // tm,),
        in_specs=[tok(h), tok(ya), tok(yb), res(woa), res(wob), res(g), res(wg), res(wu), res(wd),
                  res(gp), res(wpg),
                  pl.BlockSpec((None, tm, D_PLE), lambda i: (layer, i, 0)), res(wpp)],
        out_specs=pl.BlockSpec((tm, D), lambda i: (i, 0)),
        out_shape=jax.ShapeDtypeStruct((M, D), F32),
        compiler_params=_cparams(("parallel",)),
    )(h, ya, yb, woa, wob, g, wg, wu, wd, gp, wpg, p, wpp)


def _router_kernel(h_ref, y_ref, wo_ref, g_ref, wr_ref, hout_ref, route_ref, routeT_ref, meta_ref):
    h = h_ref[...] + _mm(y_ref[...], wo_ref[...])
    hout_ref[...] = h
    xn = _rms(h, g_ref[...])
    tm = xn.shape[0]
    p = _mm3(wr_ref[...], xn, NT)[:N_EXP]
    p = jnp.exp(p - jnp.max(p, axis=0, keepdims=True))
    p = p / jnp.sum(p, axis=0, keepdims=True)
    ex = lax.broadcasted_iota(jnp.int32, (N_EXP, tm), 0).astype(F32)
    m1 = jnp.max(p, axis=0, keepdims=True)
    i1 = jnp.min(jnp.where(p == m1, ex, float(N_EXP)), axis=0, keepdims=True)
    p2 = jnp.where(ex == i1, -1.0, p)
    m2 = jnp.max(p2, axis=0, keepdims=True)
    i2 = jnp.min(jnp.where(p2 == m2, ex, float(N_EXP)), axis=0, keepdims=True)
    tot = m1 + m2
    sel1 = ex == i1
    sel2 = ex == i2
    assign = jnp.where(sel1 | sel2, 1.0, 0.0)
    ri = lax.broadcasted_iota(jnp.int32, (tm, tm), 0)
    ci = lax.broadcasted_iota(jnp.int32, (tm, tm), 1)
    rank = _d(assign.astype(BF16), _to_bf16_mask(ri < ci))
    cnt = jnp.sum(assign, axis=1, keepdims=True)
    padded = jnp.floor((cnt + (MOE_RB - 1)) * (1.0 / MOE_RB)) * MOE_RB
    padded_b = jnp.broadcast_to(padded, (N_EXP, LANES))
    esub = lax.broadcasted_iota(jnp.int32, (N_EXP, LANES), 0)
    off_b = jnp.zeros((N_EXP, LANES), F32)
    for e in range(N_EXP - 1):
        off_b = off_b + jnp.where(esub > e, padded_b[e:e + 1, :], 0.0)
    off = off_b[:, 0:1]
    pos = off + rank
    dest1 = jnp.sum(jnp.where(sel1, pos, 0.0), axis=0, keepdims=True)
    dest2 = jnp.sum(jnp.where(sel2, pos, 0.0), axis=0, keepdims=True)
    routeT = jnp.where(ex == 0.0, dest1, jnp.where(ex == 1.0, dest2,
                       jnp.where(ex == 2.0, m1 / tot, jnp.where(ex == 3.0, m2 / tot, 0.0))))
    routeT_ref[...] = routeT
    route_ref[...] = jnp.concatenate([routeT, jnp.zeros((LANES - N_EXP, tm), F32)], axis=0).T
    mlane = lax.broadcasted_iota(jnp.int32, (N_EXP, LANES), 1)
    meta_ref[...] = jnp.where(mlane == 0, padded * (1.0 / MOE_RB), jnp.where(mlane == 1, off * (1.0 / MOE_RB), 0.0))


def _router(h, y, wo, g, wr):
    M, D = h.shape
    tm = min(TM_MOE, M)
    nt = M // tm
    return pl.pallas_call(
        _router_kernel,
        grid=(nt,),
        in_specs=[pl.BlockSpec((tm, D), lambda i: (i, 0)),
                  pl.BlockSpec((tm, y.shape[1]), lambda i: (i, 0)),
                  pl.BlockSpec(wo.shape, lambda i: (0, 0)),
                  pl.BlockSpec((1, D), lambda i: (0, 0)),
                  pl.BlockSpec((LANES, D), lambda i: (0, 0))],
        out_specs=[pl.BlockSpec((tm, D), lambda i: (i, 0)),
                   pl.BlockSpec((tm, LANES), lambda i: (i, 0)),
                   pl.BlockSpec((None, SUBLANES, tm), lambda i: (i, 0, 0)),
                   pl.BlockSpec((None, SUBLANES, LANES), lambda i: (i, 0, 0))],
        out_shape=[jax.ShapeDtypeStruct((M, D), F32),
                   jax.ShapeDtypeStruct((M, LANES), F32),
                   jax.ShapeDtypeStruct((nt, SUBLANES, tm), F32),
                   jax.ShapeDtypeStruct((nt, SUBLANES, LANES), F32)],
        compiler_params=_cparams(("parallel",)),
    )(h, y, wo, g, wr)


def _moe_kernel(nblk_ref, offb_ref, h_ref, g_ref, route_ref, routeT_ref, wg_ref, wu_ref, wd_ref, o_ref,
                xs_ref, gs_ref, xn_ref):
    i = pl.program_id(0)
    e = pl.program_id(1)
    tm = h_ref.shape[0]
    cap = xs_ref.shape[0]

    last_e = pl.num_programs(1) - 1
    used_rows = (offb_ref[i, last_e] + nblk_ref[i, last_e]) * MOE_RB

    @pl.when(e == 0)
    def _():
        xn_ref[...] = _rms(h_ref[...], g_ref[...]).astype(BF16)
        xs_ref[...] = jnp.zeros_like(xs_ref)
        row0 = lax.broadcasted_iota(jnp.int32, (MOE_SORT_ROWS, tm), 0).astype(F32)

        def sort_block(lo):
            d1, d2 = routeT_ref[0:1, :], routeT_ref[1:2, :]
            g1, g2 = routeT_ref[2:3, :], routeT_ref[3:4, :]
            rows = row0 + float(lo)
            m1 = rows == d1
            m2 = rows == d2
            xs_ref[lo:lo + MOE_SORT_ROWS, :] = _d(_to_bf16_mask(m1 | m2), xn_ref[...]).astype(BF16)
            gate = jnp.sum(jnp.where(m1, g1, 0.0) + jnp.where(m2, g2, 0.0), axis=-1, keepdims=True)
            gs_ref[lo:lo + MOE_SORT_ROWS, :] = jnp.broadcast_to(gate, (MOE_SORT_ROWS, LANES))

        for lo in range(0, cap, MOE_SORT_ROWS):
            if lo < TOP_K * tm:
                sort_block(lo)
            else:
                pl.when(lo < used_rows)(functools.partial(sort_block, lo))

    base = offb_ref[i, e]
    nb = nblk_ref[i, e]

    def expert_rows(blk, rows):
        r0 = pl.multiple_of(blk * MOE_RB, MOE_RB)
        x = xs_ref[pl.ds(r0, rows), :]
        act = _silu(_d(x, wg_ref[...])) * _d(x, wu_ref[...])
        act = act * gs_ref[pl.ds(r0, rows), 0:1]
        xs_ref[pl.ds(r0, rows), :] = _mm(act, wd_ref[...]).astype(BF16)

    def full_step(j, carry):
        expert_rows(base + MOE_STEP_BLOCKS * j, MOE_STEP_BLOCKS * MOE_RB)
        return carry

    lax.fori_loop(0, nb // MOE_STEP_BLOCKS, full_step, 0)
    done = (nb // MOE_STEP_BLOCKS) * MOE_STEP_BLOCKS
    piece = MOE_STEP_BLOCKS // 2
    while piece >= 1:
        take = ((nb - done) // piece) * piece

        @pl.when(take > 0)
        def _(done=done, piece=piece):
            expert_rows(base + done, piece * MOE_RB)

        done = done + take
        piece //= 2

    @pl.when(e == last_e)
    def _():
        col0 = lax.broadcasted_iota(jnp.int32, (tm, MOE_UNSORT_COLS), 1).astype(F32)

        def unsort_block(lo):
            d1, d2 = route_ref[:, 0:1], route_ref[:, 1:2]
            cols = col0 + float(lo)
            return _d(_to_bf16_mask((cols == d1) | (cols == d2)), xs_ref[lo:lo + MOE_UNSORT_COLS, :])

        acc = h_ref[...]
        for lo in range(0, TOP_K * tm, MOE_UNSORT_COLS):
            acc = acc + unsort_block(lo)
        o_ref[...] = acc
        for lo in range(-(-TOP_K * tm // MOE_UNSORT_COLS) * MOE_UNSORT_COLS, cap, MOE_UNSORT_COLS):
            @pl.when(lo < used_rows)
            def _():
                o_ref[...] += unsort_block(lo)


def _moe(h, g, route, routeT, meta, wg, wu, wd):
    M, D = h.shape
    E, _, FF = wg.shape
    tm = min(TM_MOE, M)
    nt = M // tm
    cap = TOP_K * tm + E * MOE_RB
    cap = -(-cap // MOE_UNSORT_COLS) * MOE_UNSORT_COLS
    meta_i = meta.astype(jnp.int32)
    grid_spec = pltpu.PrefetchScalarGridSpec(
        num_scalar_prefetch=2,
        grid=(nt, E),
        in_specs=[pl.BlockSpec((tm, D), lambda i, e, nb, ob: (i, 0)),
                  pl.BlockSpec((1, D), lambda i, e, nb, ob: (0, 0)),
                  pl.BlockSpec((tm, LANES), lambda i, e, nb, ob: (i, 0)),
                  pl.BlockSpec((None, SUBLANES, tm), lambda i, e, nb, ob: (i, 0, 0)),
                  pl.BlockSpec((None, D, FF), lambda i, e, nb, ob: (e, 0, 0)),
                  pl.BlockSpec((None, D, FF), lambda i, e, nb, ob: (e, 0, 0)),
                  pl.BlockSpec((None, FF, D), lambda i, e, nb, ob: (e, 0, 0))],
        out_specs=pl.BlockSpec((tm, D), lambda i, e, nb, ob: (i, 0)),
        scratch_shapes=[pltpu.VMEM((cap, D), BF16), pltpu.VMEM((cap, LANES), F32), pltpu.VMEM((tm, D), BF16)])
    return pl.pallas_call(
        _moe_kernel,
        grid_spec=grid_spec,
        out_shape=jax.ShapeDtypeStruct((M, D), F32),
        compiler_params=_cparams(("parallel", "arbitrary")),
    )(meta_i[:, :, 0], meta_i[:, :, 1], h, g, route, routeT, wg, wu, wd)


def _ple_kernel(final, h_ref, g_ref, wg_ref, p_ref, wp_ref, gf_ref, o_ref):
    out = _ple_update(h_ref[...], g_ref[...], wg_ref[...], p_ref[...], wp_ref[...])
    if final:
        out = _rms(out, gf_ref[...])
    o_ref[...] = out


def _ple(h, g, wg, p, layer, wp, gf, final):
    M, D = h.shape
    tm = min(TM_TOK, M)
    return pl.pallas_call(
        functools.partial(_ple_kernel, final),
        grid=(M // tm,),
        in_specs=[pl.BlockSpec((tm, D), lambda i: (i, 0)),
                  pl.BlockSpec((1, D), lambda i: (0, 0)),
                  pl.BlockSpec((D, D), lambda i: (0, 0)),
                  pl.BlockSpec((None, tm, D_PLE), lambda i: (layer, i, 0)),
                  pl.BlockSpec((D_PLE, D), lambda i: (0, 0)),
                  pl.BlockSpec((1, D), lambda i: (0, 0))],
        out_specs=pl.BlockSpec((tm, D), lambda i: (i, 0)),
        out_shape=jax.ShapeDtypeStruct((M, D), F32),
        compiler_params=_cparams(("parallel",)),
    )(h, g, wg, p, wp, gf)


def _lru_gates(xc, wr, br, wi, bi, lam):
    r = _sigmoid(_mm(xc, wr) + br)
    i = _sigmoid(_mm(xc, wi) + bi)
    log_a = -LRU_C * r * _softplus(-lam)
    a = jnp.exp(log_a)
    u = jnp.sqrt(jnp.tanh(-log_a) * (a * a + 1.0)) * (i * xc)
    return a, u


def _lru_seq_kernel(TL, proj_ref, cs_ref, h0_ref, cw_ref, cb_ref, wr_ref, br_ref, wi_ref, bi_ref, lam_ref,
                    y_ref, hT_ref, pad_ref, a_ref, u_ref, hs_ref, hc_ref):
    ti = pl.program_id(0)
    G = pad_ref.shape[0]

    @pl.when(ti == 0)
    def _():
        pad_ref[:, SUBLANES - (CONV_W - 1):SUBLANES, :] = cs_ref[...]
        hc_ref[...] = h0_ref[...]

    pad_ref[:, SUBLANES:, :] = proj_ref[:, :, :D_A].astype(F32)
    cw = cw_ref[...]
    xc = cb_ref[...][None]
    for k in range(CONV_W):
        off = SUBLANES - (CONV_W - 1) + k
        xc = xc + pad_ref[:, off:off + TL, :] * cw[k:k + 1, :][None]
    pad_ref[:, :SUBLANES, :] = pad_ref[:, TL:TL + SUBLANES, :]

    a, u = _lru_gates(xc.reshape(G * TL, D_A), wr_ref[...], br_ref[...], wi_ref[...], bi_ref[...],
                      lam_ref[...])
    a_ref[...] = a.reshape(G, TL, D_A)
    u_ref[...] = u.reshape(G, TL, D_A)

    def step(t, h):
        h = a_ref[:, pl.ds(t, 1), :] * h + u_ref[:, pl.ds(t, 1), :]
        hs_ref[:, pl.ds(t, 1), :] = h
        return h

    hc_ref[...] = lax.fori_loop(0, TL, step, hc_ref[...], unroll=8)
    y_ref[...] = (_gelu_tanh(proj_ref[:, :, D_A:].astype(F32)) * hs_ref[...]).astype(y_ref.dtype)

    @pl.when(ti == pl.num_programs(0) - 1)
    def _():
        hT_ref[...] = hc_ref[...]


def _lru_step_kernel(xa_ref, ga_ref, cs_ref, h0_ref, cw_ref, cb_ref, wr_ref, br_ref, wi_ref, bi_ref, lam_ref,
                     y_ref, hT_ref):
    cw = cw_ref[...]
    x = xa_ref[...]
    xc = cb_ref[...] + x * cw[CONV_W - 1:CONV_W, :]
    for k in range(CONV_W - 1):
        xc = xc + cs_ref[k] * cw[k:k + 1, :]
    a, u = _lru_gates(xc, wr_ref[...], br_ref[...], wi_ref[...], bi_ref[...], lam_ref[...])
    h = a * h0_ref[...] + u
    hT_ref[...] = h
    y_ref[...] = _gelu_tanh(ga_ref[...]) * h


def _lru(proj, conv_state, h0, cw, cb, wr, br, wi, bi, lam):
    B, L, _ = proj.shape
    small = [cw, cb, wr, br, wi, bi, lam]
    if L == 1:
        proj2 = proj.reshape(B, -1)
        cs = jnp.transpose(conv_state, (1, 0, 2))
        tb = min(B, 128)
        full = lambda s: pl.BlockSpec(s.shape, lambda i: (0,) * s.ndim)
        y, hT = pl.pallas_call(
            _lru_step_kernel,
            grid=(B // tb,),
            in_specs=[pl.BlockSpec((tb, D_A), lambda i: (i, 0)),
                      pl.BlockSpec((tb, D_A), lambda i: (i, 1)),
                      pl.BlockSpec((CONV_W - 1, tb, D_A), lambda i: (0, i, 0)),
                      pl.BlockSpec((tb, D_A), lambda i: (i, 0))] + [full(s) for s in small],
            out_specs=[pl.BlockSpec((tb, D_A), lambda i: (i, 0)), pl.BlockSpec((tb, D_A), lambda i: (i, 0))],
            out_shape=[jax.ShapeDtypeStruct((B, D_A), F32), jax.ShapeDtypeStruct((B, D_A), F32)],
            compiler_params=_cparams(("parallel",)),
        )(proj2, proj2, cs, h0, *small)
        return y.reshape(B, 1, D_A), hT
    TL = min(LRU_TL, L)
    G = B
    full = lambda s: pl.BlockSpec(s.shape, lambda i: (0,) * s.ndim)
    y, hT = pl.pallas_call(
        functools.partial(_lru_seq_kernel, TL),
        grid=(L // TL,),
        in_specs=[pl.BlockSpec((G, TL, 2 * D_A), lambda i: (0, i, 0)),
                  pl.BlockSpec((G, CONV_W - 1, D_A), lambda i: (0, 0, 0)),
                  pl.BlockSpec((G, 1, D_A), lambda i: (0, 0, 0))] + [full(s) for s in small],
        out_specs=[pl.BlockSpec((G, TL, D_A), lambda i: (0, i, 0)),
                   pl.BlockSpec((G, 1, D_A), lambda i: (0, 0, 0))],
        out_shape=[jax.ShapeDtypeStruct((B, L, D_A), BF16), jax.ShapeDtypeStruct((B, 1, D_A), F32)],
        scratch_shapes=[pltpu.VMEM((G, SUBLANES + TL, D_A), F32),
                        pltpu.VMEM((G, TL, D_A), F32), pltpu.VMEM((G, TL, D_A), F32),
                        pltpu.VMEM((G, TL, D_A), F32), pltpu.VMEM((G, 1, D_A), F32)],
        compiler_params=_cparams(("arbitrary",)),
    )(proj, conv_state, h0.reshape(B, 1, D_A), *small)
    return y, hT.reshape(B, D_A)


def _rwkv_prep(xs, w0, w2p, a0, a2p, g2, k_k, k_a, r_k, hsel):
    r = xs[:, 0:D_B]
    k = xs[:, D_B:2 * D_B]
    v = xs[:, 2 * D_B:3 * D_B]
    wa = xs[:, 3 * D_B:3 * D_B + R_W + R_A]
    gl = xs[:, 3 * D_B + R_W + R_A:]
    w_in = w0 + _mm(jnp.tanh(wa), w2p)
    w_log = -(jnp.maximum(-w_in, 0.0) + jnp.log(1.0 + jnp.exp(-jnp.abs(w_in)))) - 0.5
    logd = -jnp.exp(w_log)
    a = _sigmoid(a0 + _mm(wa, a2p))
    g = _mm(_sigmoid(gl), g2)
    kk = k * k_k
    kk = kk * lax.rsqrt(jnp.maximum(_mm_sel(hsel, kk * kk, HEAD_SUM_PIECES, sel_left=False), 1e-12))
    k2 = k * (1.0 + (a - 1.0) * k_a)
    beta = kk * a
    bonus = _mm_sel(hsel, r * k2 * r_k, HEAD_SUM_PIECES, sel_left=False) * v
    return r, k2, v, logd, kk, beta, g, bonus


def _rwkv_post(y, bonus, g, ln_w, ln_b, hsel):
    mean = _mm_sel(hsel, y, HEAD_SUM_PIECES, sel_left=False) * (1.0 / HD_B)
    d = y - mean
    var = _mm_sel(hsel, d * d, HEAD_SUM_PIECES, sel_left=False) * (1.0 / HD_B)
    yn = d * lax.rsqrt(var + GN_EPS_B) * ln_w + ln_b
    return (yn + bonus) * g


def _rwkv_chunk(G, T, ops, st_ref, masks, emask, store_y):
    r, k2, v, logd, kk, beta = ops
    incl, strict, lmask_bf, lastmask_bf = masks
    R = G * T
    cum = _mm_sel(lmask_bf, logd, 2)
    cend = _mm_sel(lastmask_bf, cum, 2)
    ecn = jnp.exp(-cum)
    P = jnp.exp(cum - logd) * kk
    Q = ecn * beta
    Kt = ecn * k2
    Rt = jnp.exp(cum) * r
    eend = jnp.exp(cend)
    lo = lax.broadcasted_iota(jnp.int32, (R, LANES), 1) < HD_B
    row_lo = lax.broadcasted_iota(jnp.int32, (LANES, G * HD_B), 0) < HD_B
    eye = None
    if T > 1:
        ri = lax.broadcasted_iota(jnp.int32, (R, R), 0)
        ci = lax.broadcasted_iota(jnp.int32, (R, R), 1)
        eye = jnp.where(ri == ci, 1.0, 0.0)
    n_rep = G * HD_B // LANES
    NP = H_B // 2
    heads = [(c, hh) for c in range(NP) for hh in range(2)]
    own = {0: lo, 1: ~lo}

    def tiled(x, xr, hh):
        t = jnp.where(lo, x, xr) if hh == 0 else jnp.where(lo, xr, x)
        return jnp.where(emask, jnp.concatenate([t] * n_rep, axis=1), 0.0)

    pair = []
    for c in range(NP):
        sl = slice(c * LANES, (c + 1) * LANES)
        vals = (P[:, sl], Rt[:, sl], Q[:, sl], Kt[:, sl], eend[:, sl])
        pair.append(dict(x=vals, xr=[pltpu.roll(x, HD_B, 1) for x in vals], V=v[:, sl]))
    AM = [_mm(jnp.concatenate([jnp.where(own[hh], pair[c]['x'][0], 0.0),
                               jnp.where(own[hh], pair[c]['x'][1], 0.0)], axis=0),
              jnp.concatenate([pair[c]['x'][2], pair[c]['x'][3]], axis=0), NT) for c, hh in heads]
    Mqr = [jnp.where(incl, a[R:, :R], 0.0) for a in AM]
    Mkr = [jnp.where(incl, a[R:, R:], 0.0) for a in AM]
    MV = [_mm(m, pair[c]['V']) for m, (c, hh) in zip(Mkr, heads)]
    if T > 1:
        Npow = [-jnp.where(strict, a[:R, :R], 0.0) for a in AM]
        Akp = [jnp.where(strict, a[:R, R:], 0.0) for a in AM]
        AV = [_mm(m, pair[c]['V']) for m, (c, hh) in zip(Akp, heads)]
        inv = [eye + n for n in Npow]
        for _ in range(int(math.log2(T)) - 1):
            Npow = [_mm(n, n) for n in Npow]
            inv = [i + _mm(i, n) for i, n in zip(inv, Npow)]
    ex = [[tiled(x, xr, hh) for x, xr in zip(pair[c]['x'], pair[c]['xr'])] for c, hh in heads]
    WT = [st_ref[c] for c in range(NP)]
    PWRW = [_mm(jnp.concatenate([e[0], e[1]], axis=0), WT[c], NT) for e, (c, hh) in zip(ex, heads)]
    if T > 1:
        E = [_mm(i, pw[:R] + av) for i, pw, av in zip(inv, PWRW, AV)]
    else:
        E = [pw[:R] for pw in PWRW]
    Y = [pw[R:] + mv - _mm(m, e) for pw, mv, m, e in zip(PWRW, MV, Mqr, E)]
    for c in range(NP):
        lhs = jnp.concatenate([jnp.where(own[hh], x, 0.0) for hh in range(2)
                               for x in (-E[2 * c + hh], pair[c]['V'])], axis=0)
        rhs = jnp.concatenate([ex[2 * c + hh][j] for hh in range(2) for j in (2, 3)], axis=0)
        dW = _mm(lhs, rhs, TN)
        erow = [jnp.sum(ex[2 * c + hh][4], axis=0, keepdims=True) * (1.0 / T) for hh in range(2)]
        st_ref[c] = (WT[c] + dW) * jnp.where(row_lo, erow[0], erow[1])
        store_y(c, jnp.where(lo, Y[2 * c], Y[2 * c + 1]))


def _rwkv_kernel(G, T, NC, seq, proj_ref, sh_ref, s0_ref, mu_ref, w0_ref, w2_ref, a0_ref, a2_ref, g2_ref,
                 kk_ref, ka_ref, rk_ref, lnw_ref, lnb_ref, y_ref, sT_ref, *scratch):
    ti = pl.program_id(1)
    TL = T * NC
    R = G * T
    st_ref = scratch[0]
    hsel = _head_sum_sel(D_B, HD_B)

    @pl.when(ti == 0)
    def _():
        for h in range(H_B):
            r0 = (h % 2) * HD_B
            for j in range(G // 2):
                st_ref[h // 2, r0:r0 + HD_B, j * LANES:(j + 1) * LANES] = jnp.concatenate(
                    [s0_ref[2 * j, h], s0_ref[2 * j + 1, h]], axis=1)

    if seq:
        pad_ref, ops_ref, gb_ref, ys_ref = scratch[1:]

        @pl.when(ti == 0)
        def _():
            pad_ref[:, SUBLANES - 1:SUBLANES, :] = sh_ref[...]

        pad_ref[:, SUBLANES:, :] = proj_ref[:, :, 2 * D_A:].astype(F32)
        pb = pad_ref[:, SUBLANES:, :]
        prev = pad_ref[:, SUBLANES - 1:SUBLANES - 1 + TL, :]
        xs = (pb + (prev - pb) * mu_ref[...][None]).reshape(G * TL, D_BPROJ)
        pad_ref[:, :SUBLANES, :] = pad_ref[:, TL:TL + SUBLANES, :]
    else:
        pb = proj_ref[:, 2 * D_A:]
        xs = pb + (sh_ref[...] - pb) * mu_ref[...]

    r, k2, v, logd, kk, beta, g, bonus = _rwkv_prep(
        xs, w0_ref[...], w2_ref[...], a0_ref[...], a2_ref[...], g2_ref[...], kk_ref[...], ka_ref[...],
        rk_ref[...], hsel)

    incl, strict, last = _chunk_masks(G, T)
    masks = (incl, strict, _to_bf16_mask(incl), _to_bf16_mask(last))
    emask = _batch_lane_mask(1, G, T, HD_B)

    if seq:
        for n, val in enumerate((r, k2, v, logd, kk, beta)):
            ops_ref[n] = val.reshape(G, TL, D_B)
        gb_ref[0] = g.reshape(G, TL, D_B)
        gb_ref[1] = bonus.reshape(G, TL, D_B)

        def chunk(c, carry):
            t0 = pl.multiple_of(c * T, T)
            ops = tuple(ops_ref[n, :, pl.ds(t0, T), :].reshape(R, D_B) for n in range(6))

            def store_y(hp, Y):
                ys_ref[:, pl.ds(t0, T), hp * LANES:(hp + 1) * LANES] = Y.reshape(G, T, LANES)

            _rwkv_chunk(G, T, ops, st_ref, masks, emask, store_y)
            return carry

        lax.fori_loop(0, NC, chunk, 0)
        y = ys_ref[...].reshape(G * TL, D_B)
        out = _rwkv_post(y, gb_ref[1].reshape(G * TL, D_B), gb_ref[0].reshape(G * TL, D_B),
                         lnw_ref[...], lnb_ref[...], hsel)
        y_ref[...] = out.reshape(G, TL, D_B).astype(y_ref.dtype)
    else:
        ys_ref = scratch[1]

        def store_y(hp, Y):
            ys_ref[:, hp * LANES:(hp + 1) * LANES] = Y

        _rwkv_chunk(G, T, (r, k2, v, logd, kk, beta), st_ref, masks, emask, store_y)
        y_ref[...] = _rwkv_post(ys_ref[...], bonus, g, lnw_ref[...], lnb_ref[...], hsel)

    @pl.when(ti == pl.num_programs(1) - 1)
    def _():
        for h in range(H_B):
            r0 = (h % 2) * HD_B
            for b in range(G):
                sT_ref[b, h] = st_ref[h // 2, r0:r0 + HD_B, b * HD_B:(b + 1) * HD_B]


def _rwkv(proj, shift0, s0, mu, w0, w2p, a0, a2p, g2, k_k, k_a, r_k, ln_w, ln_b):
    B, L, _ = proj.shape
    G = G_ROWS if L > 1 else min(G_ROWS_STEP, B)
    NB = B // G
    small = [mu, w0, w2p, a0, a2p, g2, k_k, k_a, r_k, ln_w, ln_b]
    full = lambda s: pl.BlockSpec(s.shape, lambda b, i: (0,) * s.ndim)
    st_dims = (H_B // 2, 2 * HD_B, G * HD_B)
    s0l = s0
    st_spec = pl.BlockSpec((G, H_B, HD_B, HD_B), lambda b, i: (b, 0, 0, 0))
    st_shape = jax.ShapeDtypeStruct(s0.shape, F32)
    unpair = lambda s: s
    if L == 1:
        T, NC = 1, 1
        kern = functools.partial(_rwkv_kernel, G, T, NC, False)
        y, sT = pl.pallas_call(
            kern,
            grid=(NB, 1),
            in_specs=[pl.BlockSpec((G, IN_AB), lambda b, i: (b, 0)),
                      pl.BlockSpec((G, D_BPROJ), lambda b, i: (b, 0)), st_spec] + [full(s) for s in small],
            out_specs=[pl.BlockSpec((G, D_B), lambda b, i: (b, 0)), st_spec],
            out_shape=[jax.ShapeDtypeStruct((B, D_B), F32), st_shape],
            scratch_shapes=[pltpu.VMEM(st_dims, F32), pltpu.VMEM((G, D_B), F32)],
            compiler_params=_cparams(("parallel", "arbitrary")),
        )(proj.reshape(B, IN_AB), shift0, s0l, *small)
        return y.reshape(B, 1, D_B), unpair(sT)
    T = min(T_CHUNK, L)
    TL = min(TL_BLOCK, L)
    NC = TL // T
    kern = functools.partial(_rwkv_kernel, G, T, NC, True)
    y, sT = pl.pallas_call(
        kern,
        grid=(NB, L // TL),
        in_specs=[pl.BlockSpec((G, TL, IN_AB), lambda b, i: (b, i, 0)),
                  pl.BlockSpec((G, 1, D_BPROJ), lambda b, i: (b, 0, 0)), st_spec] + [full(s) for s in small],
        out_specs=[pl.BlockSpec((G, TL, D_B), lambda b, i: (b, i, 0)), st_spec],
        out_shape=[jax.ShapeDtypeStruct((B, L, D_B), BF16), st_shape],
        scratch_shapes=[pltpu.VMEM(st_dims, F32),
                        pltpu.VMEM((G, SUBLANES + TL, D_BPROJ), F32),
                        pltpu.VMEM((6, G, TL, D_B), F32),
                        pltpu.VMEM((2, G, TL, D_B), F32),
                        pltpu.VMEM((G, TL, D_B), F32)],
        compiler_params=_cparams(("parallel", "arbitrary")),
    )(proj, shift0.reshape(B, 1, D_BPROJ), s0l, *small)
    return y, unpair(sT)


CD_Z, CD_XBC, CD_Q, CD_K, CD_V, CD_G, CD_S = 0, 512, 1536, 1792, 2048, 2560, 3072


def _ssd_chunk(G, T, xs, Bm, Cm, dt_all, la_all, st_ref, masks, emaskN, store_y):
    incl, lmask_bf, lastmask_bf = masks
    R = G * T
    cum = _mm_sel(lmask_bf, la_all, 3)
    cend = _mm_sel(lastmask_bf, cum, 3)
    cumT = cum.T
    wend = jnp.exp(cend - cum) * dt_all
    ecum = jnp.exp(cum)
    dend = jnp.exp(cend)
    lo = lax.broadcasted_iota(jnp.int32, (R, LANES), 1) < P_C

    def per_head(x):
        return jnp.concatenate([jnp.where(lo, x[:, 2 * c:2 * c + 1], x[:, 2 * c + 1:2 * c + 2])
                                for c in range(H_C // 2)], axis=1)

    tile = lambda x: jnp.where(emaskN, jnp.concatenate([x] * G, axis=1), 0.0)
    xdt = xs * per_head(dt_all)
    xw = xs * per_head(wend)
    ecf = per_head(ecum)
    Bg = [Bm[:, g * N_C:(g + 1) * N_C] for g in range(G_C)]
    Cg = [Cm[:, g * N_C:(g + 1) * N_C] for g in range(G_C)]
    CB = [_mm(c, b, NT) for c, b in zip(Cg, Bg)]
    ST = [st_ref[g] for g in range(G_C)]
    CS = [_mm(tile(c), s, NT) for c, s in zip(Cg, ST)]
    W = [CB[h // HG_C] * jnp.exp(jnp.where(incl, cum[:, h:h + 1] - cumT[h:h + 1, :], MASKED_LOG))
         for h in range(H_C)]
    intra = [_mm(W[h], xdt[:, (h // 2) * LANES:(h // 2 + 1) * LANES]) for h in range(H_C)]
    drow = [jnp.sum(jnp.where(emaskN, dend[:, h:h + 1], 0.0), axis=0, keepdims=True) * (1.0 / T)
            for h in range(H_C)]
    for c in range(H_C // 2):
        sl = slice(c * LANES, (c + 1) * LANES)
        gsl = slice((c % (HG_C // 2)) * LANES, (c % (HG_C // 2) + 1) * LANES)
        store_y(c, jnp.where(lo, intra[2 * c], intra[2 * c + 1]) + ecf[:, sl] * CS[c // (HG_C // 2)][:, gsl])
    for g in range(G_C):
        W_g = HG_C * P_C
        dS = _mm(xw[:, g * W_g:(g + 1) * W_g], tile(Bg[g]), TN)
        st_ref[g] = jnp.concatenate(
            [ST[g][hh * P_C:(hh + 1) * P_C] * drow[g * HG_C + hh] + dS[hh * P_C:(hh + 1) * P_C]
             for hh in range(HG_C)], axis=0)


def _gla_chunk(G, T, q, k, v, la, st_ref, masks, emask, store_o):
    incl, lmask_bf, lastmask_bf = masks
    R = G * T
    bc = _mm_sel(lmask_bf, la, 3)
    bend = _mm_sel(lastmask_bf, bc, 3)
    qe = q * jnp.exp(bc)
    kn = k * jnp.exp(-bc)
    kw = k * jnp.exp(bend - bc)
    dend = jnp.exp(bend)
    lo = lax.broadcasted_iota(jnp.int32, (R, LANES), 1) < DK_D
    own = {0: lo, 1: ~lo}
    n_rep = G * DK_D // LANES

    def tiled(x, xr, hh):
        t = jnp.where(lo, x, xr) if hh == 0 else jnp.where(lo, xr, x)
        return jnp.where(emask, jnp.concatenate([t] * n_rep, axis=1), 0.0)

    heads = [(c, hh) for c in range(H_D // 2) for hh in range(2)]
    pair = []
    for c in range(H_D // 2):
        sl = slice(c * LANES, (c + 1) * LANES)
        vals = (qe[:, sl], kw[:, sl], dend[:, sl])
        pair.append(dict(x=vals, xr=[pltpu.roll(x, DK_D, 1) for x in vals], kn=kn[:, sl]))
    vh = [v[:, h * DV_D:(h + 1) * DV_D] for h in range(H_D)]
    att = [jnp.where(incl, _mm(jnp.where(own[hh], pair[c]['x'][0], 0.0), pair[c]['kn'], NT), 0.0)
           for c, hh in heads]
    ex = [[tiled(x, xr, hh) for x, xr in zip(pair[c]['x'], pair[c]['xr'])] for c, hh in heads]
    ST = [st_ref[h] for h in range(H_D)]
    o = [_mm(a, vv) + _mm(e[0], s, NT) for a, vv, e, s in zip(att, vh, ex, ST)]
    for h in range(H_D):
        drow = jnp.sum(ex[h][2], axis=0, keepdims=True) * (1.0 / T)
        st_ref[h] = ST[h] * drow + _mm(vh[h], ex[h][1], TN)
        store_o(h, o[h])


def _cd_prep_small(small, dtb, a_neg, aup, ab):
    dt_all = _softplus(small + dtb)
    la_all = dt_all * a_neg
    la_gla = -_softplus(-(_mm(small, aup) + ab)) * (1.0 / GATE_NORM)
    return dt_all, la_all, la_gla


def _cd_post(y, xs, z, o, gg, dskip, ssd_norm, gla_norm):
    y_c = _rms((y + dskip * xs) * _silu(z), ssd_norm)
    outs = [y_c]
    for h in range(H_D):
        sl = slice(h * DV_D, (h + 1) * DV_D)
        outs.append(_rms(o[:, sl], gla_norm) * _silu(gg[:, sl]))
    return outs


def _cd_kernel(G, T, NC, seq, proj_ref, cs_ref, ssd0_ref, gla0_ref, cw_ref, cb_ref, dtb_ref, aneg_ref, dskip_ref,
               ssdn_ref, aup_ref, ab_ref, glan_ref, y_ref, ssdT_ref, glaT_ref, *scratch):
    ti = pl.program_id(1)
    TL = T * NC
    R = G * T
    sst_ref, gst_ref = scratch[0], scratch[1]

    @pl.when(ti == 0)
    def _():
        for h in range(H_C):
            r0 = (h % HG_C) * P_C
            for b in range(G):
                sst_ref[h // HG_C, r0:r0 + P_C, b * N_C:(b + 1) * N_C] = ssd0_ref[b, h]
        for h in range(H_D):
            for j in range(G // 2):
                two = jnp.concatenate([gla0_ref[2 * j, h], gla0_ref[2 * j + 1, h]], axis=0)
                gst_ref[h, :, j * LANES:(j + 1) * LANES] = two.T

    cw = cw_ref[...]
    if seq:
        pad_ref, xbc_ref, sm_ref, ys_ref, os_ref = scratch[2:]

        @pl.when(ti == 0)
        def _():
            pad_ref[:, SUBLANES - (CONV_W - 1):SUBLANES, :] = cs_ref[...]

        pad_ref[:, SUBLANES:, :] = proj_ref[:, :, CD_XBC:CD_XBC + D_XBC].astype(F32)

        def conv_b(b, carry):
            for c0 in range(0, D_XBC, CONV_COLS):
                cols = slice(c0, c0 + CONV_COLS)
                acc = cb_ref[:, cols]
                for kq in range(CONV_W):
                    off = SUBLANES - (CONV_W - 1) + kq
                    acc = acc + pad_ref[b, off:off + TL, cols] * cw_ref[kq:kq + 1, cols]
                xbc_ref[b, :, cols] = _silu(acc)
            return carry

        lax.fori_loop(0, G, conv_b, 0)
        pad_ref[:, :SUBLANES, :] = pad_ref[:, TL:TL + SUBLANES, :]
        small = proj_ref[:, :, CD_S:].astype(F32).reshape(G * TL, LANES)
    else:
        x = proj_ref[:, CD_XBC:CD_XBC + D_XBC]
        xc = cb_ref[...] + x * cw[CONV_W - 1:CONV_W, :]
        for kq in range(CONV_W - 1):
            xc = xc + cs_ref[kq] * cw[kq:kq + 1, :]
        xbc = _silu(xc)
        small = proj_ref[:, CD_S:]

    dt_all, la_all, la_gla = _cd_prep_small(small, dtb_ref[...], aneg_ref[...], aup_ref[...], ab_ref[...])

    incl, _, last = _chunk_masks(G, T)
    masks = (incl, _to_bf16_mask(incl), _to_bf16_mask(last))
    emaskN = _batch_lane_mask(1, G, T, N_C)
    emask1 = _batch_lane_mask(1, G, T, DK_D)
    scale = DK_D ** -0.5

    if seq:
        sm_ref[0] = dt_all.reshape(G, TL, LANES)
        sm_ref[1] = la_all.reshape(G, TL, LANES)
        sm_ref[2] = la_gla[:, :LANES].reshape(G, TL, LANES)
        sm_ref[3] = la_gla[:, LANES:].reshape(G, TL, LANES)

        def chunk(c, carry):
            t0 = pl.multiple_of(c * T, T)
            rows = lambda ref, lo, hi: ref[:, pl.ds(t0, T), lo:hi].astype(F32).reshape(R, hi - lo)
            xs = rows(xbc_ref, 0, D_C)
            Bm = rows(xbc_ref, D_C, D_C + G_C * N_C)
            Cm = rows(xbc_ref, D_C + G_C * N_C, D_XBC)
            dtc = sm_ref[0, :, pl.ds(t0, T), :].reshape(R, LANES)
            lac = sm_ref[1, :, pl.ds(t0, T), :].reshape(R, LANES)

            def store_y(hp, yv):
                ys_ref[:, pl.ds(t0, T), hp * LANES:(hp + 1) * LANES] = yv.reshape(G, T, LANES)

            _ssd_chunk(G, T, xs, Bm, Cm, dtc, lac, sst_ref, masks, emaskN, store_y)

            q = rows(proj_ref, CD_Q, CD_Q + K_D) * scale
            k = rows(proj_ref, CD_K, CD_K + K_D)
            v = rows(proj_ref, CD_V, CD_V + V_D)
            lag = jnp.concatenate([sm_ref[2, :, pl.ds(t0, T), :].reshape(R, LANES),
                                   sm_ref[3, :, pl.ds(t0, T), :].reshape(R, LANES)], axis=1)

            def store_o(h, ov):
                os_ref[:, pl.ds(t0, T), h * DV_D:(h + 1) * DV_D] = ov.reshape(G, T, DV_D)

            _gla_chunk(G, T, q, k, v, lag, gst_ref, masks, emask1, store_o)
            return carry

        lax.fori_loop(0, NC, chunk, 0)
        def post_b(b, carry):
            outs = _cd_post(ys_ref[b], xbc_ref[b, :, :D_C], proj_ref[b, :, CD_Z:CD_Z + D_C].astype(F32),
                            os_ref[b], proj_ref[b, :, CD_G:CD_G + V_D].astype(F32),
                            dskip_ref[...], ssdn_ref[...], glan_ref[...])
            y_ref[b, :, :D_C] = outs[0].astype(y_ref.dtype)
            for h in range(H_D):
                y_ref[b, :, D_C + h * DV_D:D_C + (h + 1) * DV_D] = outs[1 + h].astype(y_ref.dtype)
            return carry

        lax.fori_loop(0, G, post_b, 0)
    else:
        ys_ref, os_ref = scratch[2:]

        def store_y(hp, yv):
            ys_ref[:, hp * LANES:(hp + 1) * LANES] = yv

        def store_o(h, ov):
            os_ref[:, h * DV_D:(h + 1) * DV_D] = ov

        _ssd_chunk(G, T, xbc[:, :D_C], xbc[:, D_C:D_C + G_C * N_C], xbc[:, D_C + G_C * N_C:], dt_all, la_all,
                   sst_ref, masks, emaskN, store_y)
        _gla_chunk(G, T, proj_ref[:, CD_Q:CD_Q + K_D] * scale, proj_ref[:, CD_K:CD_K + K_D],
                   proj_ref[:, CD_V:CD_V + V_D], la_gla, gst_ref, masks, emask1, store_o)
        outs = _cd_post(ys_ref[...], xbc[:, :D_C], proj_ref[:, CD_Z:CD_Z + D_C], os_ref[...],
                        proj_ref[:, CD_G:CD_G + V_D], dskip_ref[...], ssdn_ref[...], glan_ref[...])
        y_ref[:, :D_C] = outs[0]
        for h in range(H_D):
            y_ref[:, D_C + h * DV_D:D_C + (h + 1) * DV_D] = outs[1 + h]

    @pl.when(ti == pl.num_programs(1) - 1)
    def _():
        for h in range(H_C):
            r0 = (h % HG_C) * P_C
            for b in range(G):
                ssdT_ref[b, h] = sst_ref[h // HG_C, r0:r0 + P_C, b * N_C:(b + 1) * N_C]
        for h in range(H_D):
            for j in range(G // 2):
                two = gst_ref[h, :, j * LANES:(j + 1) * LANES].T
                glaT_ref[2 * j, h] = two[:DK_D]
                glaT_ref[2 * j + 1, h] = two[DK_D:]


def _cd(proj, conv_state, ssd0, gla0, cw, cb, dtb, a_neg, dskip, ssd_norm, aup, ab, gla_norm):
    B, L, _ = proj.shape
    G = G_ROWS if L > 1 else min(G_ROWS_STEP, B)
    NB = B // G
    small = [cw, cb, dtb, a_neg, dskip, ssd_norm, aup, ab, gla_norm]
    full = lambda s: pl.BlockSpec(s.shape, lambda b, i: (0,) * s.ndim)
    ssd_dims = (G_C, HG_C * P_C, G * N_C)
    ssd0l, gla0l = ssd0, gla0
    ssd_spec = pl.BlockSpec((G, H_C, P_C, N_C), lambda b, i: (b, 0, 0, 0))
    gla_spec = pl.BlockSpec((G, H_D, DK_D, DV_D), lambda b, i: (b, 0, 0, 0))
    st_shapes = [jax.ShapeDtypeStruct(ssd0.shape, F32), jax.ShapeDtypeStruct(gla0.shape, F32)]
    st_scratch = [pltpu.VMEM(ssd_dims, F32), pltpu.VMEM((H_D, DV_D, G * DK_D), F32)]
    DO = D_C + V_D
    if L == 1:
        kern = functools.partial(_cd_kernel, G, 1, 1, False)
        y, ssdT, glaT = pl.pallas_call(
            kern,
            grid=(NB, 1),
            in_specs=[pl.BlockSpec((G, IN_CD_PAD), lambda b, i: (b, 0)),
                      pl.BlockSpec((CONV_W - 1, G, D_XBC), lambda b, i: (0, b, 0)),
                      ssd_spec, gla_spec] + [full(s) for s in small],
            out_specs=[pl.BlockSpec((G, DO), lambda b, i: (b, 0)), ssd_spec, gla_spec],
            out_shape=[jax.ShapeDtypeStruct((B, DO), F32)] + st_shapes,
            scratch_shapes=st_scratch + [pltpu.VMEM((G, D_C), F32), pltpu.VMEM((G, V_D), F32)],
            compiler_params=_cparams(("parallel", "arbitrary")),
        )(proj.reshape(B, IN_CD_PAD), jnp.transpose(conv_state, (1, 0, 2)), ssd0l, gla0l, *small)
        y = y.reshape(B, 1, DO)
    else:
        T = min(T_CHUNK, L)
        TL = min(TL_BLOCK, L)
        NC = TL // T
        kern = functools.partial(_cd_kernel, G, T, NC, True)
        y, ssdT, glaT = pl.pallas_call(
            kern,
            grid=(NB, L // TL),
            in_specs=[pl.BlockSpec((G, TL, IN_CD_PAD), lambda b, i: (b, i, 0)),
                      pl.BlockSpec((G, CONV_W - 1, D_XBC), lambda b, i: (b, 0, 0)),
                      ssd_spec, gla_spec] + [full(s) for s in small],
            out_specs=[pl.BlockSpec((G, TL, DO), lambda b, i: (b, i, 0)), ssd_spec, gla_spec],
            out_shape=[jax.ShapeDtypeStruct((B, L, DO), BF16)] + st_shapes,
            scratch_shapes=st_scratch + [pltpu.VMEM((G, SUBLANES + TL, D_XBC), F32),
                                         pltpu.VMEM((G, TL, D_XBC), F32),
                                         pltpu.VMEM((4, G, TL, LANES), F32),
                                         pltpu.VMEM((G, TL, D_C), F32),
                                         pltpu.VMEM((G, TL, V_D), F32)],
            compiler_params=_cparams(("parallel", "arbitrary")),
        )(proj, conv_state, ssd0l, gla0l, *small)
    return y, ssdT, glaT


def _regroup_kernel(groups, w_ref, o_ref):
    at = 0
    for lo, hi in groups:
        o_ref[:, at:at + hi - lo] = w_ref[:, lo:hi].astype(o_ref.dtype)
        at += hi - lo
    o_ref[:, at:] = jnp.zeros((o_ref.shape[0], o_ref.shape[1] - at), o_ref.dtype)


def _regroup_cols(w, groups, n_out):
    K = w.shape[0]
    return pl.pallas_call(
        functools.partial(_regroup_kernel, groups),
        out_shape=jax.ShapeDtypeStruct((K, n_out), BF16),
        compiler_params=pltpu.CompilerParams(vmem_limit_bytes=VMEM_LIMIT),
    )(w)


def _block_diag(w):
    eye = jnp.eye(N_BLK_A, dtype=w.dtype)
    return jnp.einsum('nij,nm->nimj', w, eye).reshape(D_A, D_A)


def _row(v):
    return v.reshape(1, -1).astype(F32)


def _prep_params(p):
    q = {}
    j = 0
    q['w_in_ab'] = p['w_in_ab'][j].astype(BF16)
    q['w_out_ab'] = p['w_out_ab'][j].astype(BF16)
    q['lru'] = [p['lru_conv_w'][j], _row(p['lru_conv_b'][j]),
                _block_diag(p['lru_w_r'][j]).astype(BF16), _row(p['lru_b_r'][j]),
                _block_diag(p['lru_w_i'][j]).astype(BF16), _row(p['lru_b_i'][j]), _row(p['lru_lambda'][j])]
    zw = jnp.zeros((R_W, D_B), F32)
    w2p = jnp.concatenate([p['rwkv_w2'][j], zw], axis=0).astype(BF16)
    a2p = jnp.concatenate([zw, p['rwkv_a2'][j]], axis=0).astype(BF16)
    q['rwkv'] = [_row(p['rwkv_mu'][j]), _row(p['rwkv_w0'][j]), w2p, _row(p['rwkv_a0'][j]), a2p,
                 p['rwkv_g2'][j].astype(BF16), _row(p['rwkv_k_k'][j]), _row(p['rwkv_k_a'][j]),
                 _row(p['rwkv_r_k'][j]), _row(p['rwkv_ln_w'][j]), _row(p['rwkv_ln_b'][j])]
    q['ffn'] = [p['ffn_w_gate'][j].astype(BF16), p['ffn_w_up'][j].astype(BF16), p['ffn_w_down'][j].astype(BF16)]
    w = p['w_in_cd'][j]
    o_z, o_xbc, o_dt = 0, D_C, D_C + D_XBC
    o_q = o_dt + H_C
    o_k = o_q + K_D
    o_v = o_k + K_D
    o_g = o_v + V_D
    o_al = o_g + V_D
    q['w_in_cd'] = _regroup_cols(w, [(o_z, o_dt), (o_q, o_al), (o_dt, o_q), (o_al, o_al + R_ALPHA)], IN_CD_PAD)
    q['w_out_cd'] = p['w_out_cd'][j].astype(BF16)
    lane_pad = lambda v: jnp.concatenate([v.astype(F32), jnp.zeros((LANES - v.shape[0],), F32)]).reshape(1, LANES)
    aup = jnp.zeros((LANES, K_D), F32).at[H_C:H_C + R_ALPHA].set(p['gla_alpha_up'][j]).astype(BF16)
    q['cd'] = [p['ssd_conv_w'][j], _row(p['ssd_conv_b'][j]), lane_pad(p['ssd_dt_bias'][j]),
               lane_pad(-jnp.exp(p['ssd_a_log'][j].astype(F32))), _row(jnp.repeat(p['ssd_d'][j], P_C)),
               _row(p['ssd_norm_w'][j]), aup, _row(p['gla_alpha_b'][j]), _row(p['gla_norm_w'][j])]
    q['router'] = jnp.concatenate([p['moe_router'][j].T, jnp.zeros((LANES - N_EXP, D_MODEL), F32)], axis=0)
    q['moe'] = [p['moe_w_gate'][j].astype(BF16), p['moe_w_up'][j].astype(BF16), p['moe_w_down'][j].astype(BF16)]
    q['norm_mix'] = [_row(p['norm_mix'][i]) for i in range(DEPTH)]
    q['norm_ffn'] = [_row(p['norm_ffn'][i]) for i in range(DEPTH)]
    q['norm_ple'] = [_row(p['norm_ple'][i]) for i in range(DEPTH)]
    q['w_ple_gate'] = [p['w_ple_gate'][i].astype(BF16) for i in range(DEPTH)]
    q['w_ple_proj'] = [p['w_ple_proj'][i].astype(BF16) for i in range(DEPTH)]
    q['norm_final'] = _row(p['norm_final'])
    return q


def _trunk(x, p, states, q):
    lru_conv, lru_h, rwkv_shift, rwkv_wkv, ssd_conv, ssd_st, gla_st = states
    B, L, D = x.shape
    M = B * L
    h = x.reshape(M, D)
    p2 = p.reshape(DEPTH, M, D_PLE)

    proj_dtype = BF16 if L > 1 else F32
    proj = _norm_mm(h, q['norm_mix'][0], q['w_in_ab'], IN_AB, proj_dtype).reshape(B, L, IN_AB)
    y_a, h_new = _lru(proj, lru_conv[0], lru_h[0], *q['lru'])
    y_b, wkv_new = _rwkv(proj, rwkv_shift[0], rwkv_wkv[0], *q['rwkv'])
    if L >= CONV_W - 1:
        conv_new = proj[:, L - (CONV_W - 1):, :D_A].astype(F32)
    else:
        conv_new = jnp.concatenate([lru_conv[0][:, L:], proj[:, :, :D_A]], axis=1)
    shift_new = proj[:, L - 1, 2 * D_A:].astype(F32)
    w_out = q['w_out_ab']
    h = _channel_mix(h, y_a.reshape(M, D_A), y_b.reshape(M, D_B), w_out[:D_A], w_out[D_A:],
                     q['norm_ffn'][0], *q['ffn'], q['norm_ple'][0], q['w_ple_gate'][0], p2, 0, q['w_ple_proj'][0])

    proj = _norm_mm(h, q['norm_mix'][1], q['w_in_cd'], IN_CD_PAD, proj_dtype).reshape(B, L, IN_CD_PAD)
    y_cd, ssd_new, gla_new = _cd(proj, ssd_conv[0], ssd_st[0], gla_st[0], *q['cd'])
    if L >= CONV_W - 1:
        sconv_new = proj[:, L - (CONV_W - 1):, CD_XBC:CD_XBC + D_XBC].astype(F32)
    else:
        sconv_new = jnp.concatenate([ssd_conv[0][:, L:], proj[:, :, CD_XBC:CD_XBC + D_XBC]], axis=1)
    h, route, routeT, meta = _router(h, y_cd.reshape(M, D_C + V_D), q['w_out_cd'], q['norm_ffn'][1], q['router'])
    h = _moe(h, q['norm_ffn'][1], route, routeT, meta, *q['moe'])
    y = _ple(h, q['norm_ple'][1], q['w_ple_gate'][1], p2, 1, q['w_ple_proj'][1], q['norm_final'], True)

    new_states = (conv_new[None], h_new[None], shift_new[None], wkv_new[None],
                  sconv_new[None], ssd_new[None], gla_new[None])
    return y.reshape(B, L, D), new_states


def kernel(x_prompt, x_sample, p_prompt, p_sample, state_lru_conv, state_lru_h, state_rwkv_shift,
           state_rwkv_wkv, state_ssd_conv, state_ssd, state_gla, norm_mix, norm_ffn, norm_ple,
           w_ple_gate, w_ple_proj, norm_final, w_in_ab, w_out_ab, lru_conv_w, lru_conv_b, lru_w_r,
           lru_b_r, lru_w_i, lru_b_i, lru_lambda, rwkv_mu, rwkv_w0, rwkv_w2, rwkv_a0, rwkv_a2,
           rwkv_g2, rwkv_k_k, rwkv_k_a, rwkv_r_k, rwkv_ln_w, rwkv_ln_b, ffn_w_gate, ffn_w_up,
           ffn_w_down, w_in_cd, w_out_cd, ssd_conv_w, ssd_conv_b, ssd_dt_bias, ssd_a_log, ssd_d,
           ssd_norm_w, gla_alpha_up, gla_alpha_b, gla_norm_w, moe_router, moe_w_gate, moe_w_up,
           moe_w_down):
    prm = dict(
        norm_mix=norm_mix, norm_ffn=norm_ffn, norm_ple=norm_ple, w_ple_gate=w_ple_gate,
        w_ple_proj=w_ple_proj, norm_final=norm_final, w_in_ab=w_in_ab, w_out_ab=w_out_ab,
        lru_conv_w=lru_conv_w, lru_conv_b=lru_conv_b, lru_w_r=lru_w_r, lru_b_r=lru_b_r,
        lru_w_i=lru_w_i, lru_b_i=lru_b_i, lru_lambda=lru_lambda, rwkv_mu=rwkv_mu, rwkv_w0=rwkv_w0,
        rwkv_w2=rwkv_w2, rwkv_a0=rwkv_a0, rwkv_a2=rwkv_a2, rwkv_g2=rwkv_g2, rwkv_k_k=rwkv_k_k,
        rwkv_k_a=rwkv_k_a, rwkv_r_k=rwkv_r_k, rwkv_ln_w=rwkv_ln_w, rwkv_ln_b=rwkv_ln_b,
        ffn_w_gate=ffn_w_gate, ffn_w_up=ffn_w_up, ffn_w_down=ffn_w_down, w_in_cd=w_in_cd,
        w_out_cd=w_out_cd, ssd_conv_w=ssd_conv_w, ssd_conv_b=ssd_conv_b, ssd_dt_bias=ssd_dt_bias,
        ssd_a_log=ssd_a_log, ssd_d=ssd_d, ssd_norm_w=ssd_norm_w, gla_alpha_up=gla_alpha_up,
        gla_alpha_b=gla_alpha_b, gla_norm_w=gla_norm_w, moe_router=moe_router,
        moe_w_gate=moe_w_gate, moe_w_up=moe_w_up, moe_w_down=moe_w_down)
    q = _prep_params(prm)
    states_s = (state_lru_conv, state_lru_h, state_rwkv_shift, state_rwkv_wkv,
                state_ssd_conv, state_ssd, state_gla)
    n_prompt = x_prompt.shape[0]
    states_p = tuple(jnp.zeros((s.shape[0], n_prompt) + s.shape[2:], s.dtype) for s in states_s)
    y_p, st_p = _trunk(x_prompt, p_prompt, states_p, q)
    y_s, st_s = _trunk(x_sample, p_sample, states_s, q)
    return (y_p, y_s) + tuple(st_p) + tuple(st_s)
```

```python
import functools
import math

import jax
import jax.numpy as jnp
from jax import lax
from jax.experimental import pallas as pl
from jax.experimental.pallas import tpu as pltpu

F32, BF16 = jnp.float32, jnp.bfloat16

D_MODEL = 1024
DEPTH = 2
D_PLE = 256
EPS = 1e-6
CONV_W = 4
D_A = 512
N_BLK_A = 8
BW_A = D_A // N_BLK_A
LRU_C = 8.0
H_B = 8
HD_B = 64
D_B = H_B * HD_B
R_W = 64
R_A = 64
R_G = 128
D_BPROJ = 3 * D_B + R_W + R_A + R_G
GN_EPS_B = 64e-5
IN_AB = 2 * D_A + D_BPROJ
H_C = 8
P_C = 64
D_C = H_C * P_C
G_C = 2
HG_C = H_C // G_C
N_C = 128
D_XBC = D_C + 2 * G_C * N_C
H_D = 4
DK_D = 64
DV_D = 128
K_D = H_D * DK_D
V_D = H_D * DV_D
R_ALPHA = 16
GATE_NORM = 16.0
D_FF = 2816
N_EXP = 8
D_FF_E = 1408
IN_CD_PAD = D_C + D_XBC + 2 * K_D + 2 * V_D + 128

LANES = 128
SUBLANES = 8
VMEM_LIMIT = 56 * 1024 * 1024

G_ROWS = SUBLANES
G_ROWS_STEP = 2 * SUBLANES
T_CHUNK = 16
TL_BLOCK = 64
TM_TOK = 1024
TM_FFN = 512
TM_MOE = 1024
MOE_RB = 64
MOE_STEP_BLOCKS = 4
MOE_SORT_ROWS = 512
MOE_UNSORT_COLS = 512
TOP_K = 2
HEAD_SUM_PIECES = 2
CONV_COLS = 256
LRU_TL = 256

MASKED_LOG = -1e30

NN = ((1,), (0,))
NT = ((1,), (1,))
TN = ((0,), (0,))


def _cparams(sem):
    return pltpu.CompilerParams(dimension_semantics=sem, vmem_limit_bytes=VMEM_LIMIT)


def _rms(x, g):
    return x * lax.rsqrt(jnp.mean(x * x, axis=-1, keepdims=True) + EPS) * g


def _sigmoid(x):
    return jax.nn.sigmoid(x)


def _silu(x):
    return x * jax.nn.sigmoid(x)


def _softplus(x):
    return jnp.maximum(x, 0.0) + jnp.log1p(jnp.exp(-jnp.abs(x)))


def _gelu_tanh(x):
    c = math.sqrt(2.0 / math.pi)
    return 0.5 * x * (1.0 + jnp.tanh(c * (x + 0.044715 * (x * x * x))))


def _d(a, b, dims=NN):
    return lax.dot_general(a, b, (dims, ((), ())), preferred_element_type=F32)


def _mm(a, b, dims=NN):
    return _d(a.astype(BF16), b.astype(BF16), dims)


def _split2(x):
    hi = x.astype(BF16)
    lo = (x - hi.astype(F32)).astype(BF16)
    return hi, lo


def _mm3(a, b, dims=NN):
    ah, al = _split2(a)
    bh, bl = _split2(b)
    return _d(ah, bh, dims) + (_d(ah, bl, dims) + _d(al, bh, dims))


def _mm_sel(sel, x, pieces, sel_left=True, dims=NN):
    out = None
    rest = x
    for i in range(pieces):
        p = rest.astype(BF16)
        if i + 1 < pieces:
            rest = rest - p.astype(F32)
        t = _d(sel, p, dims) if sel_left else _d(p, sel, dims)
        out = t if out is None else out + t
    return out


def _chunk_masks(G, T):
    R = G * T
    ri = lax.broadcasted_iota(jnp.int32, (R, R), 0)
    ci = lax.broadcasted_iota(jnp.int32, (R, R), 1)
    same = (ri // T) == (ci // T)
    incl = same & (ci <= ri)
    strict = same & (ci < ri)
    last = ci == (ri // T) * T + (T - 1)
    return incl, strict, last


def _to_bf16_mask(m):
    return jnp.where(m, 1.0, 0.0).astype(BF16)


def _batch_lane_mask(n_stack, G, T, W):
    R = G * T
    r = lax.broadcasted_iota(jnp.int32, (n_stack * R, G * W), 0)
    c = lax.broadcasted_iota(jnp.int32, (n_stack * R, G * W), 1)
    return ((r % R) // T) == (c // W)


def _head_sum_sel(width, head):
    r = lax.broadcasted_iota(jnp.int32, (width, width), 0)
    c = lax.broadcasted_iota(jnp.int32, (width, width), 1)
    return _to_bf16_mask((r // head) == (c // head))


def _norm_mm_kernel(x_ref, g_ref, w_ref, o_ref, xn_ref):
    @pl.when(pl.program_id(1) == 0)
    def _():
        xn_ref[...] = _rms(x_ref[...], g_ref[...]).astype(BF16)

    o_ref[...] = _d(xn_ref[...], w_ref[...]).astype(o_ref.dtype)


def _norm_mm(x, g, w, tn, out_dtype):
    M, K = x.shape
    N = w.shape[1]
    tm = min(TM_TOK, M)
    mode = dict(pipeline_mode=pl.Buffered(1)) if tn == N else {}
    return pl.pallas_call(
        _norm_mm_kernel,
        grid=(M // tm, N // tn),
        in_specs=[pl.BlockSpec((tm, K), lambda i, j: (i, 0)),
                  pl.BlockSpec((1, K), lambda i, j: (0, 0)),
                  pl.BlockSpec((K, tn), lambda i, j: (0, j), **mode)],
        out_specs=pl.BlockSpec((tm, tn), lambda i, j: (i, j)),
        out_shape=jax.ShapeDtypeStruct((M, N), out_dtype),
        scratch_shapes=[pltpu.VMEM((tm, K), BF16)],
        compiler_params=_cparams(("parallel", "arbitrary")),
    )(x, g, w)


def _ple_update(h, g, wg, p, wp):
    return h + _sigmoid(_mm(_rms(h, g), wg)) * _mm(p, wp)


def _channel_mix_kernel(h_ref, ya_ref, yb_ref, woa_ref, wob_ref, g_ref, wg_ref, wu_ref, wd_ref,
                        gp_ref, wpg_ref, p_ref, wpp_ref, o_ref):
    h = h_ref[...] + _mm(ya_ref[...], woa_ref[...]) + _mm(yb_ref[...], wob_ref[...])
    xn = _rms(h, g_ref[...]).astype(BF16)
    act = _silu(_d(xn, wg_ref[...])) * _d(xn, wu_ref[...])
    h = h + _mm(act, wd_ref[...])
    o_ref[...] = _ple_update(h, gp_ref[...], wpg_ref[...], p_ref[...], wpp_ref[...])


def _channel_mix(h, ya, yb, woa, wob, g, wg, wu, wd, gp, wpg, p, layer, wpp):
    M, D = h.shape
    tm = min(TM_FFN, M)
    tok = lambda a: pl.BlockSpec((tm, a.shape[1]), lambda i: (i, 0))
    res = lambda a: pl.BlockSpec(a.shape, lambda i: (0, 0), pipeline_mode=pl.Buffered(1))
    return pl.pallas_call(
        _channel_mix_kernel,
        grid=(M // tm,),
        in_specs=[tok(h), tok(ya), tok(yb), res(woa), res(wob), res(g), res(wg), res(wu), res(wd),
                  res(gp), res(wpg),
                  pl.BlockSpec((None, tm, D_PLE), lambda i: (layer, i, 0)), res(wpp)],
        out_specs=pl.BlockSpec((tm, D), lambda i: (i, 0)),
        out_shape=jax.ShapeDtypeStruct((M, D), F32),
        compiler_params=_cparams(("parallel",)),
    )(h, ya, yb, woa, wob, g, wg, wu, wd, gp, wpg, p, wpp)


def _router_kernel(h_ref, y_ref, wo_ref, g_ref, wr_ref, hout_ref, route_ref, routeT_ref, meta_ref):
    h = h_ref[...] + _mm(y_ref[...], wo_ref[...])
    hout_ref[...] = h
    xn = _rms(h, g_ref[...])
    tm = xn.shape[0]
    p = _mm3(wr_ref[...], xn, NT)[:N_EXP]
    p = jnp.exp(p - jnp.max(p, axis=0, keepdims=True))
    p = p / jnp.sum(p, axis=0, keepdims=True)
    ex = lax.broadcasted_iota(jnp.int32, (N_EXP, tm), 0).astype(F32)
    m1 = jnp.max(p, axis=0, keepdims=True)
    i1 = jnp.min(jnp.where(p == m1, ex, float(N_EXP)), axis=0, keepdims=True)
    p2 = jnp.where(ex == i1, -1.0, p)
    m2 = jnp.max(p2, axis=0, keepdims=True)
    i2 = jnp.min(jnp.where(p2 == m2, ex, float(N_EXP)), axis=0, keepdims=True)
    tot = m1 + m2
    sel1 = ex == i1
    sel2 = ex == i2
    assign = jnp.where(sel1 | sel2, 1.0, 0.0)
    ri = lax.broadcasted_iota(jnp.int32, (tm, tm), 0)
    ci = lax.broadcasted_iota(jnp.int32, (tm, tm), 1)
    rank = _d(assign.astype(BF16), _to_bf16_mask(ri < ci))
    cnt = jnp.sum(assign, axis=1, keepdims=True)
    padded = jnp.floor((cnt + (MOE_RB - 1)) * (1.0 / MOE_RB)) * MOE_RB
    padded_b = jnp.broadcast_to(padded, (N_EXP, LANES))
    esub = lax.broadcasted_iota(jnp.int32, (N_EXP, LANES), 0)
    off_b = jnp.zeros((N_EXP, LANES), F32)
    for e in range(N_EXP - 1):
        off_b = off_b + jnp.where(esub > e, padded_b[e:e + 1, :], 0.0)
    off = off_b[:, 0:1]
    pos = off + rank
    dest1 = jnp.sum(jnp.where(sel1, pos, 0.0), axis=0, keepdims=True)
    dest2 = jnp.sum(jnp.where(sel2, pos, 0.0), axis=0, keepdims=True)
    routeT = jnp.where(ex == 0.0, dest1, jnp.where(ex == 1.0, dest2,
                       jnp.where(ex == 2.0, m1 / tot, jnp.where(ex == 3.0, m2 / tot, 0.0))))
    routeT_ref[...] = routeT
    route_ref[...] = jnp.concatenate([routeT, jnp.zeros((LANES - N_EXP, tm), F32)], axis=0).T
    mlane = lax.broadcasted_iota(jnp.int32, (N_EXP, LANES), 1)
    meta_ref[...] = jnp.where(mlane == 0, padded * (1.0 / MOE_RB), jnp.where(mlane == 1, off * (1.0 / MOE_RB), 0.0))


def _router(h, y, wo, g, wr):
    M, D = h.shape
    tm = min(TM_MOE, M)
    nt = M // tm
    return pl.pallas_call(
        _router_kernel,
        grid=(nt,),
        in_specs=[pl.BlockSpec((tm, D), lambda i: (i, 0)),
                  pl.BlockSpec((tm, y.shape[1]), lambda i: (i, 0)),
                  pl.BlockSpec(wo.shape, lambda i: (0, 0)),
                  pl.BlockSpec((1, D), lambda i: (0, 0)),
                  pl.BlockSpec((LANES, D), lambda i: (0, 0))],
        out_specs=[pl.BlockSpec((tm, D), lambda i: (i, 0)),
                   pl.BlockSpec((tm, LANES), lambda i: (i, 0)),
                   pl.BlockSpec((None, SUBLANES, tm), lambda i: (i, 0, 0)),
                   pl.BlockSpec((None, SUBLANES, LANES), lambda i: (i, 0, 0))],
        out_shape=[jax.ShapeDtypeStruct((M, D), F32),
                   jax.ShapeDtypeStruct((M, LANES), F32),
                   jax.ShapeDtypeStruct((nt, SUBLANES, tm), F32),
                   jax.ShapeDtypeStruct((nt, SUBLANES, LANES), F32)],
        compiler_params=_cparams(("parallel",)),
    )(h, y, wo, g, wr)


def _moe_kernel(nblk_ref, offb_ref, h_ref, g_ref, route_ref, routeT_ref, wgu_ref, wd_ref, o_ref,
                xs_ref, gs_ref, xn_ref):
    i = pl.program_id(0)
    e = pl.program_id(1)
    tm = h_ref.shape[0]
    cap = xs_ref.shape[0]

    last_e = pl.num_programs(1) - 1
    used_rows = (offb_ref[i, last_e] + nblk_ref[i, last_e]) * MOE_RB

    @pl.when(e == 0)
    def _():
        xn_ref[...] = _rms(h_ref[...], g_ref[...]).astype(BF16)
        xs_ref[...] = jnp.zeros_like(xs_ref)
        row0 = lax.broadcasted_iota(jnp.int32, (MOE_SORT_ROWS, tm), 0).astype(F32)

        def sort_block(lo):
            d1, d2 = routeT_ref[0:1, :], routeT_ref[1:2, :]
            g1, g2 = routeT_ref[2:3, :], routeT_ref[3:4, :]
            rows = row0 + float(lo)
            m1 = rows == d1
            m2 = rows == d2
            xs_ref[lo:lo + MOE_SORT_ROWS, :] = _d(_to_bf16_mask(m1 | m2), xn_ref[...]).astype(BF16)
            gate = jnp.sum(jnp.where(m1, g1, 0.0) + jnp.where(m2, g2, 0.0), axis=-1, keepdims=True)
            gs_ref[lo:lo + MOE_SORT_ROWS, :] = jnp.broadcast_to(gate, (MOE_SORT_ROWS, LANES))

        for lo in range(0, cap, MOE_SORT_ROWS):
            if lo < TOP_K * tm:
                sort_block(lo)
            else:
                pl.when(lo < used_rows)(functools.partial(sort_block, lo))

    base = offb_ref[i, e]
    nb = nblk_ref[i, e]

    def expert_rows(blk, rows):
        r0 = pl.multiple_of(blk * MOE_RB, MOE_RB)
        x = xs_ref[pl.ds(r0, rows), :]
        gu = _d(x, wgu_ref[...])
        ff = gu.shape[1] // 2
        act = _silu(gu[:, :ff]) * gu[:, ff:]
        act = act * gs_ref[pl.ds(r0, rows), 0:1]
        xs_ref[pl.ds(r0, rows), :] = _mm(act, wd_ref[...]).astype(BF16)

    def full_step(j, carry):
        expert_rows(base + MOE_STEP_BLOCKS * j, MOE_STEP_BLOCKS * MOE_RB)
        return carry

    lax.fori_loop(0, nb // MOE_STEP_BLOCKS, full_step, 0)
    done = (nb // MOE_STEP_BLOCKS) * MOE_STEP_BLOCKS
    piece = MOE_STEP_BLOCKS // 2
    while piece >= 1:
        take = ((nb - done) // piece) * piece

        @pl.when(take > 0)
        def _(done=done, piece=piece):
            expert_rows(base + done, piece * MOE_RB)

        done = done + take
        piece //= 2

    @pl.when(e == last_e)
    def _():
        col0 = lax.broadcasted_iota(jnp.int32, (tm, MOE_UNSORT_COLS), 1).astype(F32)

        def unsort_block(lo):
            d1, d2 = route_ref[:, 0:1], route_ref[:, 1:2]
            cols = col0 + float(lo)
            return _d(_to_bf16_mask((cols == d1) | (cols == d2)), xs_ref[lo:lo + MOE_UNSORT_COLS, :])

        acc = h_ref[...]
        for lo in range(0, TOP_K * tm, MOE_UNSORT_COLS):
            acc = acc + unsort_block(lo)
        o_ref[...] = acc
        for lo in range(-(-TOP_K * tm // MOE_UNSORT_COLS) * MOE_UNSORT_COLS, cap, MOE_UNSORT_COLS):
            @pl.when(lo < used_rows)
            def _():
                o_ref[...] += unsort_block(lo)


def _moe(h, g, route, routeT, meta, wgu, wd):
    M, D = h.shape
    E, FF, _ = wd.shape
    tm = min(TM_MOE, M)
    nt = M // tm
    cap = TOP_K * tm + E * MOE_RB
    cap = -(-cap // MOE_UNSORT_COLS) * MOE_UNSORT_COLS
    meta_i = meta.astype(jnp.int32)
    grid_spec = pltpu.PrefetchScalarGridSpec(
        num_scalar_prefetch=2,
        grid=(nt, E),
        in_specs=[pl.BlockSpec((tm, D), lambda i, e, nb, ob: (i, 0)),
                  pl.BlockSpec((1, D), lambda i, e, nb, ob: (0, 0)),
                  pl.BlockSpec((tm, LANES), lambda i, e, nb, ob: (i, 0)),
                  pl.BlockSpec((None, SUBLANES, tm), lambda i, e, nb, ob: (i, 0, 0)),
                  pl.BlockSpec((None, D, 2 * FF), lambda i, e, nb, ob: (e, 0, 0)),
                  pl.BlockSpec((None, FF, D), lambda i, e, nb, ob: (e, 0, 0))],
        out_specs=pl.BlockSpec((tm, D), lambda i, e, nb, ob: (i, 0)),
        scratch_shapes=[pltpu.VMEM((cap, D), BF16), pltpu.VMEM((cap, LANES), F32), pltpu.VMEM((tm, D), BF16)])
    return pl.pallas_call(
        _moe_kernel,
        grid_spec=grid_spec,
        out_shape=jax.ShapeDtypeStruct((M, D), F32),
        compiler_params=_cparams(("parallel", "arbitrary")),
    )(meta_i[:, :, 0], meta_i[:, :, 1], h, g, route, routeT, wgu, wd)


def _ple_kernel(final, h_ref, g_ref, wg_ref, p_ref, wp_ref, gf_ref, o_ref):
    out = _ple_update(h_ref[...], g_ref[...], wg_ref[...], p_ref[...], wp_ref[...])
    if final:
        out = _rms(out, gf_ref[...])
    o_ref[...] = out


def _ple(h, g, wg, p, layer, wp, gf, final):
    M, D = h.shape
    tm = min(TM_TOK, M)
    return pl.pallas_call(
        functools.partial(_ple_kernel, final),
        grid=(M // tm,),
        in_specs=[pl.BlockSpec((tm, D), lambda i: (i, 0)),
                  pl.BlockSpec((1, D), lambda i: (0, 0)),
                  pl.BlockSpec((D, D), lambda i: (0, 0)),
                  pl.BlockSpec((None, tm, D_PLE), lambda i: (layer, i, 0)),
                  pl.BlockSpec((D_PLE, D), lambda i: (0, 0)),
                  pl.BlockSpec((1, D), lambda i: (0, 0))],
        out_specs=pl.BlockSpec((tm, D), lambda i: (i, 0)),
        out_shape=jax.ShapeDtypeStruct((M, D), F32),
        compiler_params=_cparams(("parallel",)),
    )(h, g, wg, p, wp, gf)


def _lru_gates(xc, wr, br, wi, bi, lam):
    r = _sigmoid(_mm(xc, wr) + br)
    i = _sigmoid(_mm(xc, wi) + bi)
    log_a = -LRU_C * r * _softplus(-lam)
    a = jnp.exp(log_a)
    u = jnp.sqrt(jnp.tanh(-log_a) * (a * a + 1.0)) * (i * xc)
    return a, u


def _lru_seq_kernel(TL, proj_ref, cs_ref, h0_ref, cw_ref, cb_ref, wr_ref, br_ref, wi_ref, bi_ref, lam_ref,
                    y_ref, hT_ref, pad_ref, a_ref, u_ref, hs_ref, hc_ref):
    ti = pl.program_id(0)
    G = pad_ref.shape[0]

    @pl.when(ti == 0)
    def _():
        pad_ref[:, SUBLANES - (CONV_W - 1):SUBLANES, :] = cs_ref[...]
        hc_ref[...] = h0_ref[...]

    pad_ref[:, SUBLANES:, :] = proj_ref[:, :, :D_A].astype(F32)
    cw = cw_ref[...]
    xc = cb_ref[...][None]
    for k in range(CONV_W):
        off = SUBLANES - (CONV_W - 1) + k
        xc = xc + pad_ref[:, off:off + TL, :] * cw[k:k + 1, :][None]
    pad_ref[:, :SUBLANES, :] = pad_ref[:, TL:TL + SUBLANES, :]

    a, u = _lru_gates(xc.reshape(G * TL, D_A), wr_ref[...], br_ref[...], wi_ref[...], bi_ref[...],
                      lam_ref[...])
    a_ref[...] = a.reshape(G, TL, D_A)
    u_ref[...] = u.reshape(G, TL, D_A)

    def step(t, h):
        h = a_ref[:, pl.ds(t, 1), :] * h + u_ref[:, pl.ds(t, 1), :]
        hs_ref[:, pl.ds(t, 1), :] = h
        return h

    hc_ref[...] = lax.fori_loop(0, TL, step, hc_ref[...], unroll=8)
    y_ref[...] = (_gelu_tanh(proj_ref[:, :, D_A:].astype(F32)) * hs_ref[...]).astype(y_ref.dtype)

    @pl.when(ti == pl.num_programs(0) - 1)
    def _():
        hT_ref[...] = hc_ref[...]


def _lru_step_kernel(xa_ref, ga_ref, cs_ref, h0_ref, cw_ref, cb_ref, wr_ref, br_ref, wi_ref, bi_ref, lam_ref,
                     y_ref, hT_ref):
    cw = cw_ref[...]
    x = xa_ref[...]
    xc = cb_ref[...] + x * cw[CONV_W - 1:CONV_W, :]
    for k in range(CONV_W - 1):
        xc = xc + cs_ref[k] * cw[k:k + 1, :]
    a, u = _lru_gates(xc, wr_ref[...], br_ref[...], wi_ref[...], bi_ref[...], lam_ref[...])
    h = a * h0_ref[...] + u
    hT_ref[...] = h
    y_ref[...] = _gelu_tanh(ga_ref[...]) * h


def _lru(proj, conv_state, h0, cw, cb, wr, br, wi, bi, lam):
    B, L, _ = proj.shape
    small = [cw, cb, wr, br, wi, bi, lam]
    if L == 1:
        proj2 = proj.reshape(B, -1)
        cs = jnp.transpose(conv_state, (1, 0, 2))
        tb = min(B, 128)
        full = lambda s: pl.BlockSpec(s.shape, lambda i: (0,) * s.ndim)
        y, hT = pl.pallas_call(
            _lru_step_kernel,
            grid=(B // tb,),
            in_specs=[pl.BlockSpec((tb, D_A), lambda i: (i, 0)),
                      pl.BlockSpec((tb, D_A), lambda i: (i, 1)),
                      pl.BlockSpec((CONV_W - 1, tb, D_A), lambda i: (0, i, 0)),
                      pl.BlockSpec((tb, D_A), lambda i: (i, 0))] + [full(s) for s in small],
            out_specs=[pl.BlockSpec((tb, D_A), lambda i: (i, 0)), pl.BlockSpec((tb, D_A), lambda i: (i, 0))],
            out_shape=[jax.ShapeDtypeStruct((B, D_A), F32), jax.ShapeDtypeStruct((B, D_A), F32)],
            compiler_params=_cparams(("parallel",)),
        )(proj2, proj2, cs, h0, *small)
        return y.reshape(B, 1, D_A), hT
    TL = min(LRU_TL, L)
    G = B
    full = lambda s: pl.BlockSpec(s.shape, lambda i: (0,) * s.ndim)
    y, hT = pl.pallas_call(
        functools.partial(_lru_seq_kernel, TL),
        grid=(L // TL,),
        in_specs=[pl.BlockSpec((G, TL, 2 * D_A), lambda i: (0, i, 0)),
                  pl.BlockSpec((G, CONV_W - 1, D_A), lambda i: (0, 0, 0)),
                  pl.BlockSpec((G, 1, D_A), lambda i: (0, 0, 0))] + [full(s) for s in small],
        out_specs=[pl.BlockSpec((G, TL, D_A), lambda i: (0, i, 0)),
                   pl.BlockSpec((G, 1, D_A), lambda i: (0, 0, 0))],
        out_shape=[jax.ShapeDtypeStruct((B, L, D_A), BF16), jax.ShapeDtypeStruct((B, 1, D_A), F32)],
        scratch_shapes=[pltpu.VMEM((G, SUBLANES + TL, D_A), F32),
                        pltpu.VMEM((G, TL, D_A), F32), pltpu.VMEM((G, TL, D_A), F32),
                        pltpu.VMEM((G, TL, D_A), F32), pltpu.VMEM((G, 1, D_A), F32)],
        compiler_params=_cparams(("arbitrary",)),
    )(proj, conv_state, h0.reshape(B, 1, D_A), *small)
    return y, hT.reshape(B, D_A)


def _rwkv_prep(xs, w0, w2p, a0, a2p, g2, k_k, k_a, r_k, hsel):
    r = xs[:, 0:D_B]
    k = xs[:, D_B:2 * D_B]
    v = xs[:, 2 * D_B:3 * D_B]
    wa = xs[:, 3 * D_B:3 * D_B + R_W + R_A]
    gl = xs[:, 3 * D_B + R_W + R_A:]
    w_in = w0 + _mm(jnp.tanh(wa), w2p)
    w_log = -(jnp.maximum(-w_in, 0.0) + jnp.log(1.0 + jnp.exp(-jnp.abs(w_in)))) - 0.5
    logd = -jnp.exp(w_log)
    a = _sigmoid(a0 + _mm(wa, a2p))
    g = _mm(_sigmoid(gl), g2)
    kk = k * k_k
    kk = kk * lax.rsqrt(jnp.maximum(_mm_sel(hsel, kk * kk, HEAD_SUM_PIECES, sel_left=False), 1e-12))
    k2 = k * (1.0 + (a - 1.0) * k_a)
    beta = kk * a
    bonus = _mm_sel(hsel, r * k2 * r_k, HEAD_SUM_PIECES, sel_left=False) * v
    return r, k2, v, logd, kk, beta, g, bonus


def _rwkv_post(y, bonus, g, ln_w, ln_b, hsel):
    mean = _mm_sel(hsel, y, HEAD_SUM_PIECES, sel_left=False) * (1.0 / HD_B)
    d = y - mean
    var = _mm_sel(hsel, d * d, HEAD_SUM_PIECES, sel_left=False) * (1.0 / HD_B)
    yn = d * lax.rsqrt(var + GN_EPS_B) * ln_w + ln_b
    return (yn + bonus) * g


def _rwkv_chunk(G, T, ops, st_ref, masks, emask, store_y):
    r, k2, v, logd, kk, beta = ops
    incl, strict, lmask_bf, lastmask_bf = masks
    R = G * T
    cum = _mm_sel(lmask_bf, logd, 2)
    cend = _mm_sel(lastmask_bf, cum, 2)
    ecn = jnp.exp(-cum)
    P = jnp.exp(cum - logd) * kk
    Q = ecn * beta
    Kt = ecn * k2
    Rt = jnp.exp(cum) * r
    eend = jnp.exp(cend)
    lo = lax.broadcasted_iota(jnp.int32, (R, LANES), 1) < HD_B
    row_lo = lax.broadcasted_iota(jnp.int32, (LANES, G * HD_B), 0) < HD_B
    eye = None
    if T > 1:
        ri = lax.broadcasted_iota(jnp.int32, (R, R), 0)
        ci = lax.broadcasted_iota(jnp.int32, (R, R), 1)
        eye = jnp.where(ri == ci, 1.0, 0.0)
    n_rep = G * HD_B // LANES
    NP = H_B // 2
    heads = [(c, hh) for c in range(NP) for hh in range(2)]
    own = {0: lo, 1: ~lo}

    def tiled(x, xr, hh):
        t = jnp.where(lo, x, xr) if hh == 0 else jnp.where(lo, xr, x)
        return jnp.where(emask, jnp.concatenate([t] * n_rep, axis=1), 0.0)

    pair = []
    for c in range(NP):
        sl = slice(c * LANES, (c + 1) * LANES)
        vals = (P[:, sl], Rt[:, sl], Q[:, sl], Kt[:, sl], eend[:, sl])
        pair.append(dict(x=vals, xr=[pltpu.roll(x, HD_B, 1) for x in vals], V=v[:, sl]))
    AM = [_mm(jnp.concatenate([jnp.where(own[hh], pair[c]['x'][0], 0.0),
                               jnp.where(own[hh], pair[c]['x'][1], 0.0)], axis=0),
              jnp.concatenate([pair[c]['x'][2], pair[c]['x'][3]], axis=0), NT) for c, hh in heads]
    Mqr = [jnp.where(incl, a[R:, :R], 0.0) for a in AM]
    Mkr = [jnp.where(incl, a[R:, R:], 0.0) for a in AM]
    MV = [_mm(m, pair[c]['V']) for m, (c, hh) in zip(Mkr, heads)]
    if T > 1:
        Npow = [-jnp.where(strict, a[:R, :R], 0.0) for a in AM]
        Akp = [jnp.where(strict, a[:R, R:], 0.0) for a in AM]
        AV = [_mm(m, pair[c]['V']) for m, (c, hh) in zip(Akp, heads)]
        inv = [eye + n for n in Npow]
        for _ in range(int(math.log2(T)) - 1):
            Npow = [_mm(n, n) for n in Npow]
            inv = [i + _mm(i, n) for i, n in zip(inv, Npow)]
    ex = [[tiled(x, xr, hh) for x, xr in zip(pair[c]['x'], pair[c]['xr'])] for c, hh in heads]
    WT = [st_ref[c] for c in range(NP)]
    PWRW = [_mm(jnp.concatenate([e[0], e[1]], axis=0), WT[c], NT) for e, (c, hh) in zip(ex, heads)]
    if T > 1:
        E = [_mm(i, pw[:R] + av) for i, pw, av in zip(inv, PWRW, AV)]
    else:
        E = [pw[:R] for pw in PWRW]
    Y = [pw[R:] + mv - _mm(m, e) for pw, mv, m, e in zip(PWRW, MV, Mqr, E)]
    for c in range(NP):
        lhs = jnp.concatenate([jnp.where(own[hh], x, 0.0) for hh in range(2)
                               for x in (-E[2 * c + hh], pair[c]['V'])], axis=0)
        rhs = jnp.concatenate([ex[2 * c + hh][j] for hh in range(2) for j in (2, 3)], axis=0)
        dW = _mm(lhs, rhs, TN)
        erow = [jnp.sum(ex[2 * c + hh][4], axis=0, keepdims=True) * (1.0 / T) for hh in range(2)]
        st_ref[c] = (WT[c] + dW) * jnp.where(row_lo, erow[0], erow[1])
        store_y(c, jnp.where(lo, Y[2 * c], Y[2 * c + 1]))


def _rwkv_kernel(G, T, NC, seq, proj_ref, sh_ref, s0_ref, mu_ref, w0_ref, w2_ref, a0_ref, a2_ref, g2_ref,
                 kk_ref, ka_ref, rk_ref, lnw_ref, lnb_ref, y_ref, sT_ref, *scratch):
    ti = pl.program_id(1)
    TL = T * NC
    R = G * T
    st_ref = scratch[0]
    hsel = _head_sum_sel(D_B, HD_B)

    @pl.when(ti == 0)
    def _():
        for h in range(H_B):
            r0 = (h % 2) * HD_B
            for j in range(G // 2):
                st_ref[h // 2, r0:r0 + HD_B, j * LANES:(j + 1) * LANES] = jnp.concatenate(
                    [s0_ref[2 * j, h], s0_ref[2 * j + 1, h]], axis=1)

    if seq:
        pad_ref, ops_ref, gb_ref, ys_ref = scratch[1:]

        @pl.when(ti == 0)
        def _():
            pad_ref[:, SUBLANES - 1:SUBLANES, :] = sh_ref[...]

        pad_ref[:, SUBLANES:, :] = proj_ref[:, :, 2 * D_A:].astype(F32)
        pb = pad_ref[:, SUBLANES:, :]
        prev = pad_ref[:, SUBLANES - 1:SUBLANES - 1 + TL, :]
        xs = (pb + (prev - pb) * mu_ref[...][None]).reshape(G * TL, D_BPROJ)
        pad_ref[:, :SUBLANES, :] = pad_ref[:, TL:TL + SUBLANES, :]
    else:
        pb = proj_ref[:, 2 * D_A:]
        xs = pb + (sh_ref[...] - pb) * mu_ref[...]

    r, k2, v, logd, kk, beta, g, bonus = _rwkv_prep(
        xs, w0_ref[...], w2_ref[...], a0_ref[...], a2_ref[...], g2_ref[...], kk_ref[...], ka_ref[...],
        rk_ref[...], hsel)

    incl, strict, last = _chunk_masks(G, T)
    masks = (incl, strict, _to_bf16_mask(incl), _to_bf16_mask(last))
    emask = _batch_lane_mask(1, G, T, HD_B)

    if seq:
        for n, val in enumerate((r, k2, v, logd, kk, beta)):
            ops_ref[n] = val.reshape(G, TL, D_B)
        gb_ref[0] = g.reshape(G, TL, D_B)
        gb_ref[1] = bonus.reshape(G, TL, D_B)

        def chunk(c, carry):
            t0 = pl.multiple_of(c * T, T)
            ops = tuple(ops_ref[n, :, pl.ds(t0, T), :].reshape(R, D_B) for n in range(6))

            def store_y(hp, Y):
                ys_ref[:, pl.ds(t0, T), hp * LANES:(hp + 1) * LANES] = Y.reshape(G, T, LANES)

            _rwkv_chunk(G, T, ops, st_ref, masks, emask, store_y)
            return carry

        lax.fori_loop(0, NC, chunk, 0)
        y = ys_ref[...].reshape(G * TL, D_B)
        out = _rwkv_post(y, gb_ref[1].reshape(G * TL, D_B), gb_ref[0].reshape(G * TL, D_B),
                         lnw_ref[...], lnb_ref[...], hsel)
        y_ref[...] = out.reshape(G, TL, D_B).astype(y_ref.dtype)
    else:
        ys_ref = scratch[1]

        def store_y(hp, Y):
            ys_ref[:, hp * LANES:(hp + 1) * LANES] = Y

        _rwkv_chunk(G, T, (r, k2, v, logd, kk, beta), st_ref, masks, emask, store_y)
        y_ref[...] = _rwkv_post(ys_ref[...], bonus, g, lnw_ref[...], lnb_ref[...], hsel)

    @pl.when(ti == pl.num_programs(1) - 1)
    def _():
        for h in range(H_B):
            r0 = (h % 2) * HD_B
            for b in range(G):
                sT_ref[b, h] = st_ref[h // 2, r0:r0 + HD_B, b * HD_B:(b + 1) * HD_B]


def _rwkv(proj, shift0, s0, mu, w0, w2p, a0, a2p, g2, k_k, k_a, r_k, ln_w, ln_b):
    B, L, _ = proj.shape
    G = G_ROWS if L > 1 else min(G_ROWS_STEP, B)
    NB = B // G
    small = [mu, w0, w2p, a0, a2p, g2, k_k, k_a, r_k, ln_w, ln_b]
    full = lambda s: pl.BlockSpec(s.shape, lambda b, i: (0,) * s.ndim)
    st_dims = (H_B // 2, 2 * HD_B, G * HD_B)
    s0l = s0
    st_spec = pl.BlockSpec((G, H_B, HD_B, HD_B), lambda b, i: (b, 0, 0, 0))
    st_shape = jax.ShapeDtypeStruct(s0.shape, F32)
    unpair = lambda s: s
    if L == 1:
        T, NC = 1, 1
        kern = functools.partial(_rwkv_kernel, G, T, NC, False)
        y, sT = pl.pallas_call(
            kern,
            grid=(NB, 1),
            in_specs=[pl.BlockSpec((G, IN_AB), lambda b, i: (b, 0)),
                      pl.BlockSpec((G, D_BPROJ), lambda b, i: (b, 0)), st_spec] + [full(s) for s in small],
            out_specs=[pl.BlockSpec((G, D_B), lambda b, i: (b, 0)), st_spec],
            out_shape=[jax.ShapeDtypeStruct((B, D_B), F32), st_shape],
            scratch_shapes=[pltpu.VMEM(st_dims, F32), pltpu.VMEM((G, D_B), F32)],
            compiler_params=_cparams(("parallel", "arbitrary")),
        )(proj.reshape(B, IN_AB), shift0, s0l, *small)
        return y.reshape(B, 1, D_B), unpair(sT)
    T = min(T_CHUNK, L)
    TL = min(TL_BLOCK, L)
    NC = TL // T
    kern = functools.partial(_rwkv_kernel, G, T, NC, True)
    y, sT = pl.pallas_call(
        kern,
        grid=(NB, L // TL),
        in_specs=[pl.BlockSpec((G, TL, IN_AB), lambda b, i: (b, i, 0)),
                  pl.BlockSpec((G, 1, D_BPROJ), lambda b, i: (b, 0, 0)), st_spec] + [full(s) for s in small],
        out_specs=[pl.BlockSpec((G, TL, D_B), lambda b, i: (b, i, 0)), st_spec],
        out_shape=[jax.ShapeDtypeStruct((B, L, D_B), BF16), st_shape],
        scratch_shapes=[pltpu.VMEM(st_dims, F32),
                        pltpu.VMEM((G, SUBLANES + TL, D_BPROJ), F32),
                        pltpu.VMEM((6, G, TL, D_B), F32),
                        pltpu.VMEM((2, G, TL, D_B), F32),
                        pltpu.VMEM((G, TL, D_B), F32)],
        compiler_params=_cparams(("parallel", "arbitrary")),
    )(proj, shift0.reshape(B, 1, D_BPROJ), s0l, *small)
    return y, unpair(sT)


CD_Z, CD_XBC, CD_Q, CD_K, CD_V, CD_G, CD_S = 0, 512, 1536, 1792, 2048, 2560, 3072


def _ssd_chunk(G, T, xs, Bm, Cm, dt_all, la_all, st_ref, masks, emaskN, store_y):
    incl, lmask_bf, lastmask_bf = masks
    R = G * T
    cum = _mm_sel(lmask_bf, la_all, 3)
    cend = _mm_sel(lastmask_bf, cum, 3)
    cumT = cum.T
    wend = jnp.exp(cend - cum) * dt_all
    ecum = jnp.exp(cum)
    dend = jnp.exp(cend)
    lo = lax.broadcasted_iota(jnp.int32, (R, LANES), 1) < P_C

    def per_head(x):
        return jnp.concatenate([jnp.where(lo, x[:, 2 * c:2 * c + 1], x[:, 2 * c + 1:2 * c + 2])
                                for c in range(H_C // 2)], axis=1)

    tile = lambda x: jnp.where(emaskN, jnp.concatenate([x] * G, axis=1), 0.0)
    xdt = xs * per_head(dt_all)
    xw = xs * per_head(wend)
    ecf = per_head(ecum)
    Bg = [Bm[:, g * N_C:(g + 1) * N_C] for g in range(G_C)]
    Cg = [Cm[:, g * N_C:(g + 1) * N_C] for g in range(G_C)]
    CB = [_mm(c, b, NT) for c, b in zip(Cg, Bg)]
    ST = [st_ref[g] for g in range(G_C)]
    CS = [_mm(tile(c), s, NT) for c, s in zip(Cg, ST)]
    W = [CB[h // HG_C] * jnp.exp(jnp.where(incl, cum[:, h:h + 1] - cumT[h:h + 1, :], MASKED_LOG))
         for h in range(H_C)]
    intra = [_mm(W[h], xdt[:, (h // 2) * LANES:(h // 2 + 1) * LANES]) for h in range(H_C)]
    drow = [jnp.sum(jnp.where(emaskN, dend[:, h:h + 1], 0.0), axis=0, keepdims=True) * (1.0 / T)
            for h in range(H_C)]
    for c in range(H_C // 2):
        sl = slice(c * LANES, (c + 1) * LANES)
        gsl = slice((c % (HG_C // 2)) * LANES, (c % (HG_C // 2) + 1) * LANES)
        store_y(c, jnp.where(lo, intra[2 * c], intra[2 * c + 1]) + ecf[:, sl] * CS[c // (HG_C // 2)][:, gsl])
    for g in range(G_C):
        W_g = HG_C * P_C
        dS = _mm(xw[:, g * W_g:(g + 1) * W_g], tile(Bg[g]), TN)
        st_ref[g] = jnp.concatenate(
            [ST[g][hh * P_C:(hh + 1) * P_C] * drow[g * HG_C + hh] + dS[hh * P_C:(hh + 1) * P_C]
             for hh in range(HG_C)], axis=0)


def _gla_chunk(G, T, q, k, v, la, st_ref, masks, emask, store_o):
    incl, lmask_bf, lastmask_bf = masks
    R = G * T
    bc = _mm_sel(lmask_bf, la, 3)
    bend = _mm_sel(lastmask_bf, bc, 3)
    qe = q * jnp.exp(bc)
    kn = k * jnp.exp(-bc)
    kw = k * jnp.exp(bend - bc)
    dend = jnp.exp(bend)
    lo = lax.broadcasted_iota(jnp.int32, (R, LANES), 1) < DK_D
    own = {0: lo, 1: ~lo}
    n_rep = G * DK_D // LANES

    def tiled(x, xr, hh):
        t = jnp.where(lo, x, xr) if hh == 0 else jnp.where(lo, xr, x)
        return jnp.where(emask, jnp.concatenate([t] * n_rep, axis=1), 0.0)

    heads = [(c, hh) for c in range(H_D // 2) for hh in range(2)]
    pair = []
    for c in range(H_D // 2):
        sl = slice(c * LANES, (c + 1) * LANES)
        vals = (qe[:, sl], kw[:, sl], dend[:, sl])
        pair.append(dict(x=vals, xr=[pltpu.roll(x, DK_D, 1) for x in vals], kn=kn[:, sl]))
    vh = [v[:, h * DV_D:(h + 1) * DV_D] for h in range(H_D)]
    att = [jnp.where(incl, _mm(jnp.where(own[hh], pair[c]['x'][0], 0.0), pair[c]['kn'], NT), 0.0)
           for c, hh in heads]
    ex = [[tiled(x, xr, hh) for x, xr in zip(pair[c]['x'], pair[c]['xr'])] for c, hh in heads]
    ST = [st_ref[h] for h in range(H_D)]
    o = [_mm(a, vv) + _mm(e[0], s, NT) for a, vv, e, s in zip(att, vh, ex, ST)]
    for h in range(H_D):
        drow = jnp.sum(ex[h][2], axis=0, keepdims=True) * (1.0 / T)
        st_ref[h] = ST[h] * drow + _mm(vh[h], ex[h][1], TN)
        store_o(h, o[h])


def _cd_prep_small(small, dtb, a_neg, aup, ab):
    dt_all = _softplus(small + dtb)
    la_all = dt_all * a_neg
    la_gla = -_softplus(-(_mm(small, aup) + ab)) * (1.0 / GATE_NORM)
    return dt_all, la_all, la_gla


def _cd_post(y, xs, z, o, gg, dskip, ssd_norm, gla_norm):
    y_c = _rms((y + dskip * xs) * _silu(z), ssd_norm)
    outs = [y_c]
    for h in range(H_D):
        sl = slice(h * DV_D, (h + 1) * DV_D)
        outs.append(_rms(o[:, sl], gla_norm) * _silu(gg[:, sl]))
    return outs


def _cd_kernel(G, T, NC, seq, proj_ref, cs_ref, ssd0_ref, gla0_ref, cw_ref, cb_ref, dtb_ref, aneg_ref, dskip_ref,
               ssdn_ref, aup_ref, ab_ref, glan_ref, y_ref, ssdT_ref, glaT_ref, *scratch):
    ti = pl.program_id(1)
    TL = T * NC
    R = G * T
    sst_ref, gst_ref = scratch[0], scratch[1]

    @pl.when(ti == 0)
    def _():
        for h in range(H_C):
            r0 = (h % HG_C) * P_C
            for b in range(G):
                sst_ref[h // HG_C, r0:r0 + P_C, b * N_C:(b + 1) * N_C] = ssd0_ref[b, h]
        for h in range(H_D):
            for j in range(G // 2):
                two = jnp.concatenate([gla0_ref[2 * j, h], gla0_ref[2 * j + 1, h]], axis=0)
                gst_ref[h, :, j * LANES:(j + 1) * LANES] = two.T

    cw = cw_ref[...]
    if seq:
        pad_ref, xbc_ref, sm_ref, ys_ref, os_ref = scratch[2:]

        @pl.when(ti == 0)
        def _():
            pad_ref[:, SUBLANES - (CONV_W - 1):SUBLANES, :] = cs_ref[...]

        pad_ref[:, SUBLANES:, :] = proj_ref[:, :, CD_XBC:CD_XBC + D_XBC].astype(F32)

        def conv_b(b, carry):
            for c0 in range(0, D_XBC, CONV_COLS):
                cols = slice(c0, c0 + CONV_COLS)
                acc = cb_ref[:, cols]
                for kq in range(CONV_W):
                    off = SUBLANES - (CONV_W - 1) + kq
                    acc = acc + pad_ref[b, off:off + TL, cols] * cw_ref[kq:kq + 1, cols]
                xbc_ref[b, :, cols] = _silu(acc)
            return carry

        lax.fori_loop(0, G, conv_b, 0)
        pad_ref[:, :SUBLANES, :] = pad_ref[:, TL:TL + SUBLANES, :]
        small = proj_ref[:, :, CD_S:].astype(F32).reshape(G * TL, LANES)
    else:
        x = proj_ref[:, CD_XBC:CD_XBC + D_XBC]
        xc = cb_ref[...] + x * cw[CONV_W - 1:CONV_W, :]
        for kq in range(CONV_W - 1):
            xc = xc + cs_ref[kq] * cw[kq:kq + 1, :]
        xbc = _silu(xc)
        small = proj_ref[:, CD_S:]

    dt_all, la_all, la_gla = _cd_prep_small(small, dtb_ref[...], aneg_ref[...], aup_ref[...], ab_ref[...])

    incl, _, last = _chunk_masks(G, T)
    masks = (incl, _to_bf16_mask(incl), _to_bf16_mask(last))
    emaskN = _batch_lane_mask(1, G, T, N_C)
    emask1 = _batch_lane_mask(1, G, T, DK_D)
    scale = DK_D ** -0.5

    if seq:
        sm_ref[0] = dt_all.reshape(G, TL, LANES)
        sm_ref[1] = la_all.reshape(G, TL, LANES)
        sm_ref[2] = la_gla[:, :LANES].reshape(G, TL, LANES)
        sm_ref[3] = la_gla[:, LANES:].reshape(G, TL, LANES)

        def chunk(c, carry):
            t0 = pl.multiple_of(c * T, T)
            rows = lambda ref, lo, hi: ref[:, pl.ds(t0, T), lo:hi].astype(F32).reshape(R, hi - lo)
            xs = rows(xbc_ref, 0, D_C)
            Bm = rows(xbc_ref, D_C, D_C + G_C * N_C)
            Cm = rows(xbc_ref, D_C + G_C * N_C, D_XBC)
            dtc = sm_ref[0, :, pl.ds(t0, T), :].reshape(R, LANES)
            lac = sm_ref[1, :, pl.ds(t0, T), :].reshape(R, LANES)

            def store_y(hp, yv):
                ys_ref[:, pl.ds(t0, T), hp * LANES:(hp + 1) * LANES] = yv.reshape(G, T, LANES)

            _ssd_chunk(G, T, xs, Bm, Cm, dtc, lac, sst_ref, masks, emaskN, store_y)

            q = rows(proj_ref, CD_Q, CD_Q + K_D) * scale
            k = rows(proj_ref, CD_K, CD_K + K_D)
            v = rows(proj_ref, CD_V, CD_V + V_D)
            lag = jnp.concatenate([sm_ref[2, :, pl.ds(t0, T), :].reshape(R, LANES),
                                   sm_ref[3, :, pl.ds(t0, T), :].reshape(R, LANES)], axis=1)

            def store_o(h, ov):
                os_ref[:, pl.ds(t0, T), h * DV_D:(h + 1) * DV_D] = ov.reshape(G, T, DV_D)

            _gla_chunk(G, T, q, k, v, lag, gst_ref, masks, emask1, store_o)
            return carry

        lax.fori_loop(0, NC, chunk, 0)
        def post_b(b, carry):
            outs = _cd_post(ys_ref[b], xbc_ref[b, :, :D_C], proj_ref[b, :, CD_Z:CD_Z + D_C].astype(F32),
                            os_ref[b], proj_ref[b, :, CD_G:CD_G + V_D].astype(F32),
                            dskip_ref[...], ssdn_ref[...], glan_ref[...])
            y_ref[b, :, :D_C] = outs[0].astype(y_ref.dtype)
            for h in range(H_D):
                y_ref[b, :, D_C + h * DV_D:D_C + (h + 1) * DV_D] = outs[1 + h].astype(y_ref.dtype)
            return carry

        lax.fori_loop(0, G, post_b, 0)
    else:
        ys_ref, os_ref = scratch[2:]

        def store_y(hp, yv):
            ys_ref[:, hp * LANES:(hp + 1) * LANES] = yv

        def store_o(h, ov):
            os_ref[:, h * DV_D:(h + 1) * DV_D] = ov

        _ssd_chunk(G, T, xbc[:, :D_C], xbc[:, D_C:D_C + G_C * N_C], xbc[:, D_C + G_C * N_C:], dt_all, la_all,
                   sst_ref, masks, emaskN, store_y)
        _gla_chunk(G, T, proj_ref[:, CD_Q:CD_Q + K_D] * scale, proj_ref[:, CD_K:CD_K + K_D],
                   proj_ref[:, CD_V:CD_V + V_D], la_gla, gst_ref, masks, emask1, store_o)
        outs = _cd_post(ys_ref[...], xbc[:, :D_C], proj_ref[:, CD_Z:CD_Z + D_C], os_ref[...],
                        proj_ref[:, CD_G:CD_G + V_D], dskip_ref[...], ssdn_ref[...], glan_ref[...])
        y_ref[:, :D_C] = outs[0]
        for h in range(H_D):
            y_ref[:, D_C + h * DV_D:D_C + (h + 1) * DV_D] = outs[1 + h]

    @pl.when(ti == pl.num_programs(1) - 1)
    def _():
        for h in range(H_C):
            r0 = (h % HG_C) * P_C
            for b in range(G):
                ssdT_ref[b, h] = sst_ref[h // HG_C, r0:r0 + P_C, b * N_C:(b + 1) * N_C]
        for h in range(H_D):
            for j in range(G // 2):
                two = gst_ref[h, :, j * LANES:(j + 1) * LANES].T
                glaT_ref[2 * j, h] = two[:DK_D]
                glaT_ref[2 * j + 1, h] = two[DK_D:]


def _cd(proj, conv_state, ssd0, gla0, cw, cb, dtb, a_neg, dskip, ssd_norm, aup, ab, gla_norm):
    B, L, _ = proj.shape
    G = G_ROWS if L > 1 else min(G_ROWS_STEP, B)
    NB = B // G
    small = [cw, cb, dtb, a_neg, dskip, ssd_norm, aup, ab, gla_norm]
    full = lambda s: pl.BlockSpec(s.shape, lambda b, i: (0,) * s.ndim)
    ssd_dims = (G_C, HG_C * P_C, G * N_C)
    ssd0l, gla0l = ssd0, gla0
    ssd_spec = pl.BlockSpec((G, H_C, P_C, N_C), lambda b, i: (b, 0, 0, 0))
    gla_spec = pl.BlockSpec((G, H_D, DK_D, DV_D), lambda b, i: (b, 0, 0, 0))
    st_shapes = [jax.ShapeDtypeStruct(ssd0.shape, F32), jax.ShapeDtypeStruct(gla0.shape, F32)]
    st_scratch = [pltpu.VMEM(ssd_dims, F32), pltpu.VMEM((H_D, DV_D, G * DK_D), F32)]
    DO = D_C + V_D
    if L == 1:
        kern = functools.partial(_cd_kernel, G, 1, 1, False)
        y, ssdT, glaT = pl.pallas_call(
            kern,
            grid=(NB, 1),
            in_specs=[pl.BlockSpec((G, IN_CD_PAD), lambda b, i: (b, 0)),
                      pl.BlockSpec((CONV_W - 1, G, D_XBC), lambda b, i: (0, b, 0)),
                      ssd_spec, gla_spec] + [full(s) for s in small],
            out_specs=[pl.BlockSpec((G, DO), lambda b, i: (b, 0)), ssd_spec, gla_spec],
            out_shape=[jax.ShapeDtypeStruct((B, DO), F32)] + st_shapes,
            scratch_shapes=st_scratch + [pltpu.VMEM((G, D_C), F32), pltpu.VMEM((G, V_D), F32)],
            compiler_params=_cparams(("parallel", "arbitrary")),
        )(proj.reshape(B, IN_CD_PAD), jnp.transpose(conv_state, (1, 0, 2)), ssd0l, gla0l, *small)
        y = y.reshape(B, 1, DO)
    else:
        T = min(T_CHUNK, L)
        TL = min(TL_BLOCK, L)
        NC = TL // T
        kern = functools.partial(_cd_kernel, G, T, NC, True)
        y, ssdT, glaT = pl.pallas_call(
            kern,
            grid=(NB, L // TL),
            in_specs=[pl.BlockSpec((G, TL, IN_CD_PAD), lambda b, i: (b, i, 0)),
                      pl.BlockSpec((G, CONV_W - 1, D_XBC), lambda b, i: (b, 0, 0)),
                      ssd_spec, gla_spec] + [full(s) for s in small],
            out_specs=[pl.BlockSpec((G, TL, DO), lambda b, i: (b, i, 0)), ssd_spec, gla_spec],
            out_shape=[jax.ShapeDtypeStruct((B, L, DO), BF16)] + st_shapes,
            scratch_shapes=st_scratch + [pltpu.VMEM((G, SUBLANES + TL, D_XBC), F32),
                                         pltpu.VMEM((G, TL, D_XBC), F32),
                                         pltpu.VMEM((4, G, TL, LANES), F32),
                                         pltpu.VMEM((G, TL, D_C), F32),
                                         pltpu.VMEM((G, TL, V_D), F32)],
            compiler_params=_cparams(("parallel", "arbitrary")),
        )(proj, conv_state, ssd0l, gla0l, *small)
    return y, ssdT, glaT


def _regroup_kernel(groups, w_ref, o_ref):
    at = 0
    for lo, hi in groups:
        o_ref[:, at:at + hi - lo] = w_ref[:, lo:hi].astype(o_ref.dtype)
        at += hi - lo
    o_ref[:, at:] = jnp.zeros((o_ref.shape[0], o_ref.shape[1] - at), o_ref.dtype)


def _regroup_cols(w, groups, n_out):
    K = w.shape[0]
    return pl.pallas_call(
        functools.partial(_regroup_kernel, groups),
        out_shape=jax.ShapeDtypeStruct((K, n_out), BF16),
        compiler_params=pltpu.CompilerParams(vmem_limit_bytes=VMEM_LIMIT),
    )(w)


def _block_diag(w):
    eye = jnp.eye(N_BLK_A, dtype=w.dtype)
    return jnp.einsum('nij,nm->nimj', w, eye).reshape(D_A, D_A)


def _row(v):
    return v.reshape(1, -1).astype(F32)


def _prep_params(p):
    q = {}
    j = 0
    q['w_in_ab'] = p['w_in_ab'][j].astype(BF16)
    q['w_out_ab'] = p['w_out_ab'][j].astype(BF16)
    q['lru'] = [p['lru_conv_w'][j], _row(p['lru_conv_b'][j]),
                _block_diag(p['lru_w_r'][j]).astype(BF16), _row(p['lru_b_r'][j]),
                _block_diag(p['lru_w_i'][j]).astype(BF16), _row(p['lru_b_i'][j]), _row(p['lru_lambda'][j])]
    zw = jnp.zeros((R_W, D_B), F32)
    w2p = jnp.concatenate([p['rwkv_w2'][j], zw], axis=0).astype(BF16)
    a2p = jnp.concatenate([zw, p['rwkv_a2'][j]], axis=0).astype(BF16)
    q['rwkv'] = [_row(p['rwkv_mu'][j]), _row(p['rwkv_w0'][j]), w2p, _row(p['rwkv_a0'][j]), a2p,
                 p['rwkv_g2'][j].astype(BF16), _row(p['rwkv_k_k'][j]), _row(p['rwkv_k_a'][j]),
                 _row(p['rwkv_r_k'][j]), _row(p['rwkv_ln_w'][j]), _row(p['rwkv_ln_b'][j])]
    q['ffn'] = [p['ffn_w_gate'][j].astype(BF16), p['ffn_w_up'][j].astype(BF16), p['ffn_w_down'][j].astype(BF16)]
    w = p['w_in_cd'][j]
    o_z, o_xbc, o_dt = 0, D_C, D_C + D_XBC
    o_q = o_dt + H_C
    o_k = o_q + K_D
    o_v = o_k + K_D
    o_g = o_v + V_D
    o_al = o_g + V_D
    q['w_in_cd'] = _regroup_cols(w, [(o_z, o_dt), (o_q, o_al), (o_dt, o_q), (o_al, o_al + R_ALPHA)], IN_CD_PAD)
    q['w_out_cd'] = p['w_out_cd'][j].astype(BF16)
    lane_pad = lambda v: jnp.concatenate([v.astype(F32), jnp.zeros((LANES - v.shape[0],), F32)]).reshape(1, LANES)
    aup = jnp.zeros((LANES, K_D), F32).at[H_C:H_C + R_ALPHA].set(p['gla_alpha_up'][j]).astype(BF16)
    q['cd'] = [p['ssd_conv_w'][j], _row(p['ssd_conv_b'][j]), lane_pad(p['ssd_dt_bias'][j]),
               lane_pad(-jnp.exp(p['ssd_a_log'][j].astype(F32))), _row(jnp.repeat(p['ssd_d'][j], P_C)),
               _row(p['ssd_norm_w'][j]), aup, _row(p['gla_alpha_b'][j]), _row(p['gla_norm_w'][j])]
    q['router'] = jnp.concatenate([p['moe_router'][j].T, jnp.zeros((LANES - N_EXP, D_MODEL), F32)], axis=0)
    q['moe'] = [jnp.concatenate([p['moe_w_gate'][j].astype(BF16), p['moe_w_up'][j].astype(BF16)], axis=2),
                p['moe_w_down'][j].astype(BF16)]
    q['norm_mix'] = [_row(p['norm_mix'][i]) for i in range(DEPTH)]
    q['norm_ffn'] = [_row(p['norm_ffn'][i]) for i in range(DEPTH)]
    q['norm_ple'] = [_row(p['norm_ple'][i]) for i in range(DEPTH)]
    q['w_ple_gate'] = [p['w_ple_gate'][i].astype(BF16) for i in range(DEPTH)]
    q['w_ple_proj'] = [p['w_ple_proj'][i].astype(BF16) for i in range(DEPTH)]
    q['norm_final'] = _row(p['norm_final'])
    return q


def _trunk(x, p, states, q):
    lru_conv, lru_h, rwkv_shift, rwkv_wkv, ssd_conv, ssd_st, gla_st = states
    B, L, D = x.shape
    M = B * L
    h = x.reshape(M, D)
    p2 = p.reshape(DEPTH, M, D_PLE)

    proj_dtype = BF16 if L > 1 else F32
    proj = _norm_mm(h, q['norm_mix'][0], q['w_in_ab'], IN_AB, proj_dtype).reshape(B, L, IN_AB)
    y_a, h_new = _lru(proj, lru_conv[0], lru_h[0], *q['lru'])
    y_b, wkv_new = _rwkv(proj, rwkv_shift[0], rwkv_wkv[0], *q['rwkv'])
    if L >= CONV_W - 1:
        conv_new = proj[:, L - (CONV_W - 1):, :D_A].astype(F32)
    else:
        conv_new = jnp.concatenate([lru_conv[0][:, L:], proj[:, :, :D_A]], axis=1)
    shift_new = proj[:, L - 1, 2 * D_A:].astype(F32)
    w_out = q['w_out_ab']
    h = _channel_mix(h, y_a.reshape(M, D_A), y_b.reshape(M, D_B), w_out[:D_A], w_out[D_A:],
                     q['norm_ffn'][0], *q['ffn'], q['norm_ple'][0], q['w_ple_gate'][0], p2, 0, q['w_ple_proj'][0])

    proj = _norm_mm(h, q['norm_mix'][1], q['w_in_cd'], IN_CD_PAD, proj_dtype).reshape(B, L, IN_CD_PAD)
    y_cd, ssd_new, gla_new = _cd(proj, ssd_conv[0], ssd_st[0], gla_st[0], *q['cd'])
    if L >= CONV_W - 1:
        sconv_new = proj[:, L - (CONV_W - 1):, CD_XBC:CD_XBC + D_XBC].astype(F32)
    else:
        sconv_new = jnp.concatenate([ssd_conv[0][:, L:], proj[:, :, CD_XBC:CD_XBC + D_XBC]], axis=1)
    h, route, routeT, meta = _router(h, y_cd.reshape(M, D_C + V_D), q['w_out_cd'], q['norm_ffn'][1], q['router'])
    h = _moe(h, q['norm_ffn'][1], route, routeT, meta, *q['moe'])
    y = _ple(h, q['norm_ple'][1], q['w_ple_gate'][1], p2, 1, q['w_ple_proj'][1], q['norm_final'], True)

    new_states = (conv_new[None], h_new[None], shift_new[None], wkv_new[None],
                  sconv_new[None], ssd_new[None], gla_new[None])
    return y.reshape(B, L, D), new_states


def kernel(x_prompt, x_sample, p_prompt, p_sample, state_lru_conv, state_lru_h, state_rwkv_shift,
           state_rwkv_wkv, state_ssd_conv, state_ssd, state_gla, norm_mix, norm_ffn, norm_ple,
           w_ple_gate, w_ple_proj, norm_final, w_in_ab, w_out_ab, lru_conv_w, lru_conv_b, lru_w_r,
           lru_b_r, lru_w_i, lru_b_i, lru_lambda, rwkv_mu, rwkv_w0, rwkv_w2, rwkv_a0, rwkv_a2,
           rwkv_g2, rwkv_k_k, rwkv_k_a, rwkv_r_k, rwkv_ln_w, rwkv_ln_b, ffn_w_gate, ffn_w_up,
           ffn_w_down, w_in_cd, w_out_cd, ssd_conv_w, ssd_conv_b, ssd_dt_bias, ssd_a_log, ssd_d,
           ssd_norm_w, gla_alpha_up, gla_alpha_b, gla_norm_w, moe_router, moe_w_gate, moe_w_up,
           moe_w_down):
    prm = dict(
        norm_mix=norm_mix, norm_ffn=norm_ffn, norm_ple=norm_ple, w_ple_gate=w_ple_gate,
        w_ple_proj=w_ple_proj, norm_final=norm_final, w_in_ab=w_in_ab, w_out_ab=w_out_ab,
        lru_conv_w=lru_conv_w, lru_conv_b=lru_conv_b, lru_w_r=lru_w_r, lru_b_r=lru_b_r,
        lru_w_i=lru_w_i, lru_b_i=lru_b_i, lru_lambda=lru_lambda, rwkv_mu=rwkv_mu, rwkv_w0=rwkv_w0,
        rwkv_w2=rwkv_w2, rwkv_a0=rwkv_a0, rwkv_a2=rwkv_a2, rwkv_g2=rwkv_g2, rwkv_k_k=rwkv_k_k,
        rwkv_k_a=rwkv_k_a, rwkv_r_k=rwkv_r_k, rwkv_ln_w=rwkv_ln_w, rwkv_ln_b=rwkv_ln_b,
        ffn_w_gate=ffn_w_gate, ffn_w_up=ffn_w_up, ffn_w_down=ffn_w_down, w_in_cd=w_in_cd,
        w_out_cd=w_out_cd, ssd_conv_w=ssd_conv_w, ssd_conv_b=ssd_conv_b, ssd_dt_bias=ssd_dt_bias,
        ssd_a_log=ssd_a_log, ssd_d=ssd_d, ssd_norm_w=ssd_norm_w, gla_alpha_up=gla_alpha_up,
        gla_alpha_b=gla_alpha_b, gla_norm_w=gla_norm_w, moe_router=moe_router,
        moe_w_gate=moe_w_gate, moe_w_up=moe_w_up, moe_w_down=moe_w_down)
    q = _prep_params(prm)
    states_s = (state_lru_conv, state_lru_h, state_rwkv_shift, state_rwkv_wkv,
                state_ssd_conv, state_ssd, state_gla)
    n_prompt = x_prompt.shape[0]
    states_p = tuple(jnp.zeros((s.shape[0], n_prompt) + s.shape[2:], s.dtype) for s in states_s)
    y_p, st_p = _trunk(x_prompt, p_prompt, states_p, q)
    y_s, st_s = _trunk(x_sample, p_sample, states_s, q)
    return (y_p, y_s) + tuple(st_p) + tuple(st_s)
```

```python
import functools
import math

import jax
import jax.numpy as jnp
from jax import lax
from jax.experimental import pallas as pl
from jax.experimental.pallas import tpu as pltpu

F32, BF16 = jnp.float32, jnp.bfloat16

D_MODEL = 1024
DEPTH = 2
D_PLE = 256
EPS = 1e-6
CONV_W = 4
D_A = 512
N_BLK_A = 8
BW_A = D_A // N_BLK_A
LRU_C = 8.0
H_B = 8
HD_B = 64
D_B = H_B * HD_B
R_W = 64
R_A = 64
R_G = 128
D_BPROJ = 3 * D_B + R_W + R_A + R_G
GN_EPS_B = 64e-5
IN_AB = 2 * D_A + D_BPROJ
H_C = 8
P_C = 64
D_C = H_C * P_C
G_C = 2
HG_C = H_C // G_C
N_C = 128
D_XBC = D_C + 2 * G_C * N_C
H_D = 4
DK_D = 64
DV_D = 128
K_D = H_D * DK_D
V_D = H_D * DV_D
R_ALPHA = 16
GATE_NORM = 16.0
D_FF = 2816
N_EXP = 8
D_FF_E = 1408
IN_CD_PAD = D_C + D_XBC + 2 * K_D + 2 * V_D + 128

LANES = 128
SUBLANES = 8
VMEM_LIMIT = 56 * 1024 * 1024

G_ROWS = SUBLANES
G_ROWS_STEP = 2 * SUBLANES
T_CHUNK = 16
TL_BLOCK = 64
TM_TOK = 1024
TM_FFN = 512
TM_MOE = 1024
MOE_RB = 64
MOE_STEP_BLOCKS = 4
MOE_SORT_ROWS = 512
MOE_UNSORT_COLS = 512
TOP_K = 2
HEAD_SUM_PIECES = 2
CONV_COLS = 256
LRU_TL = 256

MASKED_LOG = -1e30

NN = ((1,), (0,))
NT = ((1,), (1,))
TN = ((0,), (0,))


def _cparams(sem):
    return pltpu.CompilerParams(dimension_semantics=sem, vmem_limit_bytes=VMEM_LIMIT)


def _rms(x, g):
    return x * lax.rsqrt(jnp.mean(x * x, axis=-1, keepdims=True) + EPS) * g


def _sigmoid(x):
    return jax.nn.sigmoid(x)


def _silu(x):
    return x * jax.nn.sigmoid(x)


def _softplus(x):
    return jnp.maximum(x, 0.0) + jnp.log1p(jnp.exp(-jnp.abs(x)))


def _gelu_tanh(x):
    c = math.sqrt(2.0 / math.pi)
    return 0.5 * x * (1.0 + jnp.tanh(c * (x + 0.044715 * (x * x * x))))


def _d(a, b, dims=NN):
    return lax.dot_general(a, b, (dims, ((), ())), preferred_element_type=F32)


def _mm(a, b, dims=NN):
    return _d(a.astype(BF16), b.astype(BF16), dims)


def _split2(x):
    hi = x.astype(BF16)
    lo = (x - hi.astype(F32)).astype(BF16)
    return hi, lo


def _mm3(a, b, dims=NN):
    ah, al = _split2(a)
    bh, bl = _split2(b)
    return _d(ah, bh, dims) + (_d(ah, bl, dims) + _d(al, bh, dims))


def _mm_sel(sel, x, pieces, sel_left=True, dims=NN):
    out = None
    rest = x
    for i in range(pieces):
        p = rest.astype(BF16)
        if i + 1 < pieces:
            rest = rest - p.astype(F32)
        t = _d(sel, p, dims) if sel_left else _d(p, sel, dims)
        out = t if out is None else out + t
    return out


def _chunk_masks(G, T):
    R = G * T
    ri = lax.broadcasted_iota(jnp.int32, (R, R), 0)
    ci = lax.broadcasted_iota(jnp.int32, (R, R), 1)
    same = (ri // T) == (ci // T)
    incl = same & (ci <= ri)
    strict = same & (ci < ri)
    last = ci == (ri // T) * T + (T - 1)
    return incl, strict, last


def _to_bf16_mask(m):
    return jnp.where(m, 1.0, 0.0).astype(BF16)


def _batch_lane_mask(n_stack, G, T, W):
    R = G * T
    r = lax.broadcasted_iota(jnp.int32, (n_stack * R, G * W), 0)
    c = lax.broadcasted_iota(jnp.int32, (n_stack * R, G * W), 1)
    return ((r % R) // T) == (c // W)


def _head_sum_sel(width, head):
    r = lax.broadcasted_iota(jnp.int32, (width, width), 0)
    c = lax.broadcasted_iota(jnp.int32, (width, width), 1)
    return _to_bf16_mask((r // head) == (c // head))


def _norm_mm_kernel(x_ref, g_ref, w_ref, o_ref, xn_ref):
    @pl.when(pl.program_id(1) == 0)
    def _():
        xn_ref[...] = _rms(x_ref[...], g_ref[...]).astype(BF16)

    o_ref[...] = _d(xn_ref[...], w_ref[...]).astype(o_ref.dtype)


def _norm_mm(x, g, w, tn, out_dtype):
    M, K = x.shape
    N = w.shape[1]
    tm = min(TM_TOK, M)
    mode = dict(pipeline_mode=pl.Buffered(1)) if tn == N else {}
    return pl.pallas_call(
        _norm_mm_kernel,
        grid=(M // tm, N // tn),
        in_specs=[pl.BlockSpec((tm, K), lambda i, j: (i, 0)),
                  pl.BlockSpec((1, K), lambda i, j: (0, 0)),
                  pl.BlockSpec((K, tn), lambda i, j: (0, j), **mode)],
        out_specs=pl.BlockSpec((tm, tn), lambda i, j: (i, j)),
        out_shape=jax.ShapeDtypeStruct((M, N), out_dtype),
        scratch_shapes=[pltpu.VMEM((tm, K), BF16)],
        compiler_params=_cparams(("parallel", "arbitrary")),
    )(x, g, w)


def _ple_update(h, g, wg, p, wp):
    return h + _sigmoid(_mm(_rms(h, g), wg)) * _mm(p, wp)


def _channel_mix_kernel(h_ref, ya_ref, yb_ref, woa_ref, wob_ref, g_ref, wg_ref, wu_ref, wd_ref,
                        gp_ref, wpg_ref, p_ref, wpp_ref, o_ref):
    h = h_ref[...] + _mm(ya_ref[...], woa_ref[...]) + _mm(yb_ref[...], wob_ref[...])
    xn = _rms(h, g_ref[...]).astype(BF16)
    act = _silu(_d(xn, wg_ref[...])) * _d(xn, wu_ref[...])
    h = h + _mm(act, wd_ref[...])
    o_ref[...] = _ple_update(h, gp_ref[...], wpg_ref[...], p_ref[...], wpp_ref[...])


def _channel_mix(h, ya, yb, woa, wob, g, wg, wu, wd, gp, wpg, p, layer, wpp):
    M, D = h.shape
    tm = min(TM_FFN, M)
    tok = lambda a: pl.BlockSpec((tm, a.shape[1]), lambda i: (i, 0))
    res = lambda a: pl.BlockSpec(a.shape, lambda i: (0, 0), pipeline_mode=pl.Buffered(1))
    return pl.pallas_call(
        _channel_mix_kernel,
        grid=(M // tm,),
        in_specs=[tok(h), tok(ya), tok(yb), res(woa), res(wob), res(g), res(wg), res(wu), res(wd),
                  res(gp), res(wpg),
                  pl.BlockSpec((None, tm, D_PLE), lambda i: (layer, i, 0)), res(wpp)],
        out_specs=pl.BlockSpec((tm, D), lambda i: (i, 0)),
        out_shape=jax.ShapeDtypeStruct((M, D), F32),
        compiler_params=_cparams(("parallel",)),
    )(h, ya, yb, woa, wob, g, wg, wu, wd, gp, wpg, p, wpp)


def _router_kernel(h_ref, y_ref, wo_ref, g_ref, wr_ref, hout_ref, route_ref, routeT_ref, meta_ref):
    h = h_ref[...] + _mm(y_ref[...], wo_ref[...])
    hout_ref[...] = h
    xn = _rms(h, g_ref[...])
    tm = xn.shape[0]
    p = _mm3(wr_ref[...], xn, NT)[:N_EXP]
    p = jnp.exp(p - jnp.max(p, axis=0, keepdims=True))
    p = p / jnp.sum(p, axis=0, keepdims=True)
    ex = lax.broadcasted_iota(jnp.int32, (N_EXP, tm), 0).astype(F32)
    m1 = jnp.max(p, axis=0, keepdims=True)
    i1 = jnp.min(jnp.where(p == m1, ex, float(N_EXP)), axis=0, keepdims=True)
    p2 = jnp.where(ex == i1, -1.0, p)
    m2 = jnp.max(p2, axis=0, keepdims=True)
    i2 = jnp.min(jnp.where(p2 == m2, ex, float(N_EXP)), axis=0, keepdims=True)
    tot = m1 + m2
    sel1 = ex == i1
    sel2 = ex == i2
    assign = jnp.where(sel1 | sel2, 1.0, 0.0)
    ri = lax.broadcasted_iota(jnp.int32, (tm, tm), 0)
    ci = lax.broadcasted_iota(jnp.int32, (tm, tm), 1)
    rank = _d(assign.astype(BF16), _to_bf16_mask(ri < ci))
    cnt = jnp.sum(assign, axis=1, keepdims=True)
    padded = jnp.floor((cnt + (MOE_RB - 1)) * (1.0 / MOE_RB)) * MOE_RB
    padded_b = jnp.broadcast_to(padded, (N_EXP, LANES))
    esub = lax.broadcasted_iota(jnp.int32, (N_EXP, LANES), 0)
    off_b = jnp.zeros((N_EXP, LANES), F32)
    for e in range(N_EXP - 1):
        off_b = off_b + jnp.where(esub > e, padded_b[e:e + 1, :], 0.0)
    off = off_b[:, 0:1]
    pos = off + rank
    dest1 = jnp.sum(jnp.where(sel1, pos, 0.0), axis=0, keepdims=True)
    dest2 = jnp.sum(jnp.where(sel2, pos, 0.0), axis=0, keepdims=True)
    routeT = jnp.where(ex == 0.0, dest1, jnp.where(ex == 1.0, dest2,
                       jnp.where(ex == 2.0, m1 / tot, jnp.where(ex == 3.0, m2 / tot, 0.0))))
    routeT_ref[...] = routeT
    route_ref[...] = jnp.concatenate([routeT, jnp.zeros((LANES - N_EXP, tm), F32)], axis=0).T
    mlane = lax.broadcasted_iota(jnp.int32, (N_EXP, LANES), 1)
    meta_ref[...] = jnp.where(mlane == 0, padded * (1.0 / MOE_RB), jnp.where(mlane == 1, off * (1.0 / MOE_RB), 0.0))


def _router(h, y, wo, g, wr):
    M, D = h.shape
    tm = min(TM_MOE, M)
    nt = M // tm
    return pl.pallas_call(
        _router_kernel,
        grid=(nt,),
        in_specs=[pl.BlockSpec((tm, D), lambda i: (i, 0)),
                  pl.BlockSpec((tm, y.shape[1]), lambda i: (i, 0)),
                  pl.BlockSpec(wo.shape, lambda i: (0, 0)),
                  pl.BlockSpec((1, D), lambda i: (0, 0)),
                  pl.BlockSpec((LANES, D), lambda i: (0, 0))],
        out_specs=[pl.BlockSpec((tm, D), lambda i: (i, 0)),
                   pl.BlockSpec((tm, LANES), lambda i: (i, 0)),
                   pl.BlockSpec((None, SUBLANES, tm), lambda i: (i, 0, 0)),
                   pl.BlockSpec((None, SUBLANES, LANES), lambda i: (i, 0, 0))],
        out_shape=[jax.ShapeDtypeStruct((M, D), F32),
                   jax.ShapeDtypeStruct((M, LANES), F32),
                   jax.ShapeDtypeStruct((nt, SUBLANES, tm), F32),
                   jax.ShapeDtypeStruct((nt, SUBLANES, LANES), F32)],
        compiler_params=_cparams(("parallel",)),
    )(h, y, wo, g, wr)


def _moe_kernel(nblk_ref, offb_ref, h_ref, g_ref, route_ref, routeT_ref, wgu_ref, wd_ref, o_ref,
                xs_ref, gs_ref, xn_ref):
    i = pl.program_id(0)
    e = pl.program_id(1)
    tm = h_ref.shape[0]
    cap = xs_ref.shape[0]

    last_e = pl.num_programs(1) - 1
    used_rows = (offb_ref[i, last_e] + nblk_ref[i, last_e]) * MOE_RB

    @pl.when(e == 0)
    def _():
        xn_ref[...] = _rms(h_ref[...], g_ref[...]).astype(BF16)
        xs_ref[...] = jnp.zeros_like(xs_ref)
        row0 = lax.broadcasted_iota(jnp.int32, (MOE_SORT_ROWS, tm), 0).astype(F32)

        def sort_block(lo):
            d1, d2 = routeT_ref[0:1, :], routeT_ref[1:2, :]
            g1, g2 = routeT_ref[2:3, :], routeT_ref[3:4, :]
            rows = row0 + float(lo)
            m1 = rows == d1
            m2 = rows == d2
            xs_ref[lo:lo + MOE_SORT_ROWS, :] = _d(_to_bf16_mask(m1 | m2), xn_ref[...]).astype(BF16)
            gate = jnp.sum(jnp.where(m1, g1, 0.0) + jnp.where(m2, g2, 0.0), axis=-1, keepdims=True)
            gs_ref[lo:lo + MOE_SORT_ROWS, :] = jnp.broadcast_to(gate, (MOE_SORT_ROWS, LANES))

        for lo in range(0, cap, MOE_SORT_ROWS):
            if lo < TOP_K * tm:
                sort_block(lo)
            else:
                pl.when(lo < used_rows)(functools.partial(sort_block, lo))

    base = offb_ref[i, e]
    nb = nblk_ref[i, e]

    def expert_rows(blk, rows):
        r0 = pl.multiple_of(blk * MOE_RB, MOE_RB)
        x = xs_ref[pl.ds(r0, rows), :]
        gu = _d(x, wgu_ref[...])
        ff = gu.shape[1] // 2
        act = _silu(gu[:, :ff]) * gu[:, ff:]
        act = act * gs_ref[pl.ds(r0, rows), 0:1]
        xs_ref[pl.ds(r0, rows), :] = _mm(act, wd_ref[...]).astype(BF16)

    def full_step(j, carry):
        expert_rows(base + MOE_STEP_BLOCKS * j, MOE_STEP_BLOCKS * MOE_RB)
        return carry

    lax.fori_loop(0, nb // MOE_STEP_BLOCKS, full_step, 0)
    done = (nb // MOE_STEP_BLOCKS) * MOE_STEP_BLOCKS
    piece = MOE_STEP_BLOCKS // 2
    while piece >= 1:
        take = ((nb - done) // piece) * piece

        @pl.when(take > 0)
        def _(done=done, piece=piece):
            expert_rows(base + done, piece * MOE_RB)

        done = done + take
        piece //= 2

    @pl.when(e == last_e)
    def _():
        col0 = lax.broadcasted_iota(jnp.int32, (tm, MOE_UNSORT_COLS), 1).astype(F32)

        def unsort_block(lo):
            d1, d2 = route_ref[:, 0:1], route_ref[:, 1:2]
            cols = col0 + float(lo)
            return _d(_to_bf16_mask((cols == d1) | (cols == d2)), xs_ref[lo:lo + MOE_UNSORT_COLS, :])

        acc = h_ref[...]
        for lo in range(0, TOP_K * tm, MOE_UNSORT_COLS):
            acc = acc + unsort_block(lo)
        o_ref[...] = acc
        for lo in range(-(-TOP_K * tm // MOE_UNSORT_COLS) * MOE_UNSORT_COLS, cap, MOE_UNSORT_COLS):
            @pl.when(lo < used_rows)
            def _():
                o_ref[...] += unsort_block(lo)


def _moe(h, g, route, routeT, meta, wgu, wd):
    M, D = h.shape
    E, FF, _ = wd.shape
    tm = min(TM_MOE, M)
    nt = M // tm
    cap = TOP_K * tm + E * MOE_RB
    cap = -(-cap // MOE_UNSORT_COLS) * MOE_UNSORT_COLS
    meta_i = meta.astype(jnp.int32)
    grid_spec = pltpu.PrefetchScalarGridSpec(
        num_scalar_prefetch=2,
        grid=(nt, E),
        in_specs=[pl.BlockSpec((tm, D), lambda i, e, nb, ob: (i, 0)),
                  pl.BlockSpec((1, D), lambda i, e, nb, ob: (0, 0)),
                  pl.BlockSpec((tm, LANES), lambda i, e, nb, ob: (i, 0)),
                  pl.BlockSpec((None, SUBLANES, tm), lambda i, e, nb, ob: (i, 0, 0)),
                  pl.BlockSpec((None, D, 2 * FF), lambda i, e, nb, ob: (e, 0, 0)),
                  pl.BlockSpec((None, FF, D), lambda i, e, nb, ob: (e, 0, 0))],
        out_specs=pl.BlockSpec((tm, D), lambda i, e, nb, ob: (i, 0)),
        scratch_shapes=[pltpu.VMEM((cap, D), BF16), pltpu.VMEM((cap, LANES), F32), pltpu.VMEM((tm, D), BF16)])
    return pl.pallas_call(
        _moe_kernel,
        grid_spec=grid_spec,
        out_shape=jax.ShapeDtypeStruct((M, D), F32),
        compiler_params=_cparams(("parallel", "arbitrary")),
    )(meta_i[:, :, 0], meta_i[:, :, 1], h, g, route, routeT, wgu, wd)


def _ple_kernel(final, h_ref, g_ref, wg_ref, p_ref, wp_ref, gf_ref, o_ref):
    out = _ple_update(h_ref[...], g_ref[...], wg_ref[...], p_ref[...], wp_ref[...])
    if final:
        out = _rms(out, gf_ref[...])
    o_ref[...] = out


def _ple(h, g, wg, p, layer, wp, gf, final):
    M, D = h.shape
    tm = min(TM_TOK, M)
    return pl.pallas_call(
        functools.partial(_ple_kernel, final),
        grid=(M // tm,),
        in_specs=[pl.BlockSpec((tm, D), lambda i: (i, 0)),
                  pl.BlockSpec((1, D), lambda i: (0, 0)),
                  pl.BlockSpec((D, D), lambda i: (0, 0)),
                  pl.BlockSpec((None, tm, D_PLE), lambda i: (layer, i, 0)),
                  pl.BlockSpec((D_PLE, D), lambda i: (0, 0)),
                  pl.BlockSpec((1, D), lambda i: (0, 0))],
        out_specs=pl.BlockSpec((tm, D), lambda i: (i, 0)),
        out_shape=jax.ShapeDtypeStruct((M, D), F32),
        compiler_params=_cparams(("parallel",)),
    )(h, g, wg, p, wp, gf)


def _lru_gates(xc, wr, br, wi, bi, lam):
    r = _sigmoid(_mm(xc, wr) + br)
    i = _sigmoid(_mm(xc, wi) + bi)
    log_a = -LRU_C * r * _softplus(-lam)
    a = jnp.exp(log_a)
    u = jnp.sqrt(jnp.tanh(-log_a) * (a * a + 1.0)) * (i * xc)
    return a, u


def _lru_seq_kernel(TL, proj_ref, cs_ref, h0_ref, cw_ref, cb_ref, wr_ref, br_ref, wi_ref, bi_ref, lam_ref,
                    y_ref, hT_ref, pad_ref, a_ref, u_ref, hs_ref, hc_ref):
    ti = pl.program_id(0)
    G = pad_ref.shape[0]

    @pl.when(ti == 0)
    def _():
        pad_ref[:, SUBLANES - (CONV_W - 1):SUBLANES, :] = cs_ref[...]
        hc_ref[...] = h0_ref[...]

    pad_ref[:, SUBLANES:, :] = proj_ref[:, :, :D_A].astype(F32)
    cw = cw_ref[...]
    xc = cb_ref[...][None]
    for k in range(CONV_W):
        off = SUBLANES - (CONV_W - 1) + k
        xc = xc + pad_ref[:, off:off + TL, :] * cw[k:k + 1, :][None]
    pad_ref[:, :SUBLANES, :] = pad_ref[:, TL:TL + SUBLANES, :]

    a, u = _lru_gates(xc.reshape(G * TL, D_A), wr_ref[...], br_ref[...], wi_ref[...], bi_ref[...],
                      lam_ref[...])
    a_ref[...] = a.reshape(G, TL, D_A)
    u_ref[...] = u.reshape(G, TL, D_A)

    def step(t, h):
        h = a_ref[:, pl.ds(t, 1), :] * h + u_ref[:, pl.ds(t, 1), :]
        hs_ref[:, pl.ds(t, 1), :] = h
        return h

    hc_ref[...] = lax.fori_loop(0, TL, step, hc_ref[...], unroll=8)
    y_ref[...] = (_gelu_tanh(proj_ref[:, :, D_A:].astype(F32)) * hs_ref[...]).astype(y_ref.dtype)

    @pl.when(ti == pl.num_programs(0) - 1)
    def _():
        hT_ref[...] = hc_ref[...]


def _lru_step_kernel(xa_ref, ga_ref, cs_ref, h0_ref, cw_ref, cb_ref, wr_ref, br_ref, wi_ref, bi_ref, lam_ref,
                     y_ref, hT_ref):
    cw = cw_ref[...]
    x = xa_ref[...]
    xc = cb_ref[...] + x * cw[CONV_W - 1:CONV_W, :]
    for k in range(CONV_W - 1):
        xc = xc + cs_ref[k] * cw[k:k + 1, :]
    a, u = _lru_gates(xc, wr_ref[...], br_ref[...], wi_ref[...], bi_ref[...], lam_ref[...])
    h = a * h0_ref[...] + u
    hT_ref[...] = h
    y_ref[...] = _gelu_tanh(ga_ref[...]) * h


def _lru(proj, conv_state, h0, cw, cb, wr, br, wi, bi, lam):
    B, L, _ = proj.shape
    small = [cw, cb, wr, br, wi, bi, lam]
    if L == 1:
        proj2 = proj.reshape(B, -1)
        cs = jnp.transpose(conv_state, (1, 0, 2))
        tb = min(B, 128)
        full = lambda s: pl.BlockSpec(s.shape, lambda i: (0,) * s.ndim)
        y, hT = pl.pallas_call(
            _lru_step_kernel,
            grid=(B // tb,),
            in_specs=[pl.BlockSpec((tb, D_A), lambda i: (i, 0)),
                      pl.BlockSpec((tb, D_A), lambda i: (i, 1)),
                      pl.BlockSpec((CONV_W - 1, tb, D_A), lambda i: (0, i, 0)),
                      pl.BlockSpec((tb, D_A), lambda i: (i, 0))] + [full(s) for s in small],
            out_specs=[pl.BlockSpec((tb, D_A), lambda i: (i, 0)), pl.BlockSpec((tb, D_A), lambda i: (i, 0))],
            out_shape=[jax.ShapeDtypeStruct((B, D_A), F32), jax.ShapeDtypeStruct((B, D_A), F32)],
            compiler_params=_cparams(("parallel",)),
        )(proj2, proj2, cs, h0, *small)
        return y.reshape(B, 1, D_A), hT
    TL = min(LRU_TL, L)
    G = B
    full = lambda s: pl.BlockSpec(s.shape, lambda i: (0,) * s.ndim)
    y, hT = pl.pallas_call(
        functools.partial(_lru_seq_kernel, TL),
        grid=(L // TL,),
        in_specs=[pl.BlockSpec((G, TL, 2 * D_A), lambda i: (0, i, 0)),
                  pl.BlockSpec((G, CONV_W - 1, D_A), lambda i: (0, 0, 0)),
                  pl.BlockSpec((G, 1, D_A), lambda i: (0, 0, 0))] + [full(s) for s in small],
        out_specs=[pl.BlockSpec((G, TL, D_A), lambda i: (0, i, 0)),
                   pl.BlockSpec((G, 1, D_A), lambda i: (0, 0, 0))],
        out_shape=[jax.ShapeDtypeStruct((B, L, D_A), BF16), jax.ShapeDtypeStruct((B, 1, D_A), F32)],
        scratch_shapes=[pltpu.VMEM((G, SUBLANES + TL, D_A), F32),
                        pltpu.VMEM((G, TL, D_A), F32), pltpu.VMEM((G, TL, D_A), F32),
                        pltpu.VMEM((G, TL, D_A), F32), pltpu.VMEM((G, 1, D_A), F32)],
        compiler_params=_cparams(("arbitrary",)),
    )(proj, conv_state, h0.reshape(B, 1, D_A), *small)
    return y, hT.reshape(B, D_A)


def _rwkv_prep(xs, w0, w2p, a0, a2p, g2, k_k, k_a, r_k, hsel):
    r = xs[:, 0:D_B]
    k = xs[:, D_B:2 * D_B]
    v = xs[:, 2 * D_B:3 * D_B]
    wa = xs[:, 3 * D_B:3 * D_B + R_W + R_A]
    gl = xs[:, 3 * D_B + R_W + R_A:]
    w_in = w0 + _mm(jnp.tanh(wa), w2p)
    w_log = -(jnp.maximum(-w_in, 0.0) + jnp.log(1.0 + jnp.exp(-jnp.abs(w_in)))) - 0.5
    logd = -jnp.exp(w_log)
    a = _sigmoid(a0 + _mm(wa, a2p))
    g = _mm(_sigmoid(gl), g2)
    kk = k * k_k
    kk = kk * lax.rsqrt(jnp.maximum(_mm_sel(hsel, kk * kk, HEAD_SUM_PIECES, sel_left=False), 1e-12))
    k2 = k * (1.0 + (a - 1.0) * k_a)
    beta = kk * a
    bonus = _mm_sel(hsel, r * k2 * r_k, HEAD_SUM_PIECES, sel_left=False) * v
    return r, k2, v, logd, kk, beta, g, bonus


def _rwkv_post(y, bonus, g, ln_w, ln_b, hsel):
    mean = _mm_sel(hsel, y, HEAD_SUM_PIECES, sel_left=False) * (1.0 / HD_B)
    d = y - mean
    var = _mm_sel(hsel, d * d, HEAD_SUM_PIECES, sel_left=False) * (1.0 / HD_B)
    yn = d * lax.rsqrt(var + GN_EPS_B) * ln_w + ln_b
    return (yn + bonus) * g


def _rwkv_chunk(G, T, ops, st_ref, masks, emask, store_y):
    r, k2, v, logd, kk, beta = ops
    incl, strict, lmask_bf, lastmask_bf = masks
    R = G * T
    cum = _mm_sel(lmask_bf, logd, 2)
    cend = _mm_sel(lastmask_bf, cum, 2)
    ecn = jnp.exp(-cum)
    P = jnp.exp(cum - logd) * kk
    Q = ecn * beta
    Kt = ecn * k2
    Rt = jnp.exp(cum) * r
    eend = jnp.exp(cend)
    lo = lax.broadcasted_iota(jnp.int32, (R, LANES), 1) < HD_B
    row_lo = lax.broadcasted_iota(jnp.int32, (LANES, G * HD_B), 0) < HD_B
    eye = None
    if T > 1:
        ri = lax.broadcasted_iota(jnp.int32, (R, R), 0)
        ci = lax.broadcasted_iota(jnp.int32, (R, R), 1)
        eye = jnp.where(ri == ci, 1.0, 0.0)
    n_rep = G * HD_B // LANES
    NP = H_B // 2
    heads = [(c, hh) for c in range(NP) for hh in range(2)]
    own = {0: lo, 1: ~lo}

    def tiled(x, xr, hh):
        t = jnp.where(lo, x, xr) if hh == 0 else jnp.where(lo, xr, x)
        return jnp.where(emask, jnp.concatenate([t] * n_rep, axis=1), 0.0)

    pair = []
    for c in range(NP):
        sl = slice(c * LANES, (c + 1) * LANES)
        vals = (P[:, sl], Rt[:, sl], Q[:, sl], Kt[:, sl], eend[:, sl])
        pair.append(dict(x=vals, xr=[pltpu.roll(x, HD_B, 1) for x in vals], V=v[:, sl]))
    AM = [_mm(jnp.concatenate([jnp.where(own[hh], pair[c]['x'][0], 0.0),
                               jnp.where(own[hh], pair[c]['x'][1], 0.0)], axis=0),
              jnp.concatenate([pair[c]['x'][2], pair[c]['x'][3]], axis=0), NT) for c, hh in heads]
    Mqr = [jnp.where(incl, a[R:, :R], 0.0) for a in AM]
    Mkr = [jnp.where(incl, a[R:, R:], 0.0) for a in AM]
    MV = [_mm(m, pair[c]['V']) for m, (c, hh) in zip(Mkr, heads)]
    if T > 1:
        Npow = [-jnp.where(strict, a[:R, :R], 0.0) for a in AM]
        Akp = [jnp.where(strict, a[:R, R:], 0.0) for a in AM]
        AV = [_mm(m, pair[c]['V']) for m, (c, hh) in zip(Akp, heads)]
        inv = [eye + n for n in Npow]
        for _ in range(int(math.log2(T)) - 1):
            Npow = [_mm(n, n) for n in Npow]
            inv = [i + _mm(i, n) for i, n in zip(inv, Npow)]
    ex = [[tiled(x, xr, hh) for x, xr in zip(pair[c]['x'], pair[c]['xr'])] for c, hh in heads]
    WT = [st_ref[c] for c in range(NP)]
    PWRW = [_mm(jnp.concatenate([e[0], e[1]], axis=0), WT[c], NT) for e, (c, hh) in zip(ex, heads)]
    if T > 1:
        E = [_mm(i, pw[:R] + av) for i, pw, av in zip(inv, PWRW, AV)]
    else:
        E = [pw[:R] for pw in PWRW]
    Y = [pw[R:] + mv - _mm(m, e) for pw, mv, m, e in zip(PWRW, MV, Mqr, E)]
    for c in range(NP):
        lhs = jnp.concatenate([jnp.where(own[hh], x, 0.0) for hh in range(2)
                               for x in (-E[2 * c + hh], pair[c]['V'])], axis=0)
        rhs = jnp.concatenate([ex[2 * c + hh][j] for hh in range(2) for j in (2, 3)], axis=0)
        dW = _mm(lhs, rhs, TN)
        erow = [jnp.sum(ex[2 * c + hh][4], axis=0, keepdims=True) * (1.0 / T) for hh in range(2)]
        st_ref[c] = (WT[c] + dW) * jnp.where(row_lo, erow[0], erow[1])
        store_y(c, jnp.where(lo, Y[2 * c], Y[2 * c + 1]))


def _rwkv_kernel(G, T, NC, seq, proj_ref, sh_ref, s0_ref, mu_ref, w0_ref, w2_ref, a0_ref, a2_ref, g2_ref,
                 kk_ref, ka_ref, rk_ref, lnw_ref, lnb_ref, y_ref, sT_ref, *scratch):
    ti = pl.program_id(1)
    TL = T * NC
    R = G * T
    st_ref = scratch[0]
    hsel = _head_sum_sel(D_B, HD_B)

    @pl.when(ti == 0)
    def _():
        for h in range(H_B):
            r0 = (h % 2) * HD_B
            for j in range(G // 2):
                st_ref[h // 2, r0:r0 + HD_B, j * LANES:(j + 1) * LANES] = jnp.concatenate(
                    [s0_ref[2 * j, h], s0_ref[2 * j + 1, h]], axis=1)

    if seq:
        pad_ref, ops_ref, gb_ref, ys_ref = scratch[1:]

        @pl.when(ti == 0)
        def _():
            pad_ref[:, SUBLANES - 1:SUBLANES, :] = sh_ref[...]

        pad_ref[:, SUBLANES:, :] = proj_ref[:, :, 2 * D_A:].astype(F32)
        pb = pad_ref[:, SUBLANES:, :]
        prev = pad_ref[:, SUBLANES - 1:SUBLANES - 1 + TL, :]
        xs = (pb + (prev - pb) * mu_ref[...][None]).reshape(G * TL, D_BPROJ)
        pad_ref[:, :SUBLANES, :] = pad_ref[:, TL:TL + SUBLANES, :]
    else:
        pb = proj_ref[:, 2 * D_A:]
        xs = pb + (sh_ref[...] - pb) * mu_ref[...]

    r, k2, v, logd, kk, beta, g, bonus = _rwkv_prep(
        xs, w0_ref[...], w2_ref[...], a0_ref[...], a2_ref[...], g2_ref[...], kk_ref[...], ka_ref[...],
        rk_ref[...], hsel)

    incl, strict, last = _chunk_masks(G, T)
    masks = (incl, strict, _to_bf16_mask(incl), _to_bf16_mask(last))
    emask = _batch_lane_mask(1, G, T, HD_B)

    if seq:
        for n, val in enumerate((r, k2, v, logd, kk, beta)):
            ops_ref[n] = val.reshape(G, TL, D_B)
        gb_ref[0] = g.reshape(G, TL, D_B)
        gb_ref[1] = bonus.reshape(G, TL, D_B)

        def chunk(c, carry):
            t0 = pl.multiple_of(c * T, T)
            ops = tuple(ops_ref[n, :, pl.ds(t0, T), :].reshape(R, D_B) for n in range(6))

            def store_y(hp, Y):
                ys_ref[:, pl.ds(t0, T), hp * LANES:(hp + 1) * LANES] = Y.reshape(G, T, LANES)

            _rwkv_chunk(G, T, ops, st_ref, masks, emask, store_y)
            return carry

        lax.fori_loop(0, NC, chunk, 0)
        y = ys_ref[...].reshape(G * TL, D_B)
        out = _rwkv_post(y, gb_ref[1].reshape(G * TL, D_B), gb_ref[0].reshape(G * TL, D_B),
                         lnw_ref[...], lnb_ref[...], hsel)
        y_ref[...] = out.reshape(G, TL, D_B).astype(y_ref.dtype)
    else:
        ys_ref = scratch[1]

        def store_y(hp, Y):
            ys_ref[:, hp * LANES:(hp + 1) * LANES] = Y

        _rwkv_chunk(G, T, (r, k2, v, logd, kk, beta), st_ref, masks, emask, store_y)
        y_ref[...] = _rwkv_post(ys_ref[...], bonus, g, lnw_ref[...], lnb_ref[...], hsel)

    @pl.when(ti == pl.num_programs(1) - 1)
    def _():
        for h in range(H_B):
            r0 = (h % 2) * HD_B
            for b in range(G):
                sT_ref[b, h] = st_ref[h // 2, r0:r0 + HD_B, b * HD_B:(b + 1) * HD_B]


def _rwkv(proj, shift0, s0, mu, w0, w2p, a0, a2p, g2, k_k, k_a, r_k, ln_w, ln_b):
    B, L, _ = proj.shape
    G = G_ROWS if L > 1 else min(G_ROWS_STEP, B)
    NB = B // G
    small = [mu, w0, w2p, a0, a2p, g2, k_k, k_a, r_k, ln_w, ln_b]
    full = lambda s: pl.BlockSpec(s.shape, lambda b, i: (0,) * s.ndim)
    st_dims = (H_B // 2, 2 * HD_B, G * HD_B)
    s0l = s0
    st_spec = pl.BlockSpec((G, H_B, HD_B, HD_B), lambda b, i: (b, 0, 0, 0))
    st_shape = jax.ShapeDtypeStruct(s0.shape, F32)
    unpair = lambda s: s
    if L == 1:
        T, NC = 1, 1
        kern = functools.partial(_rwkv_kernel, G, T, NC, False)
        y, sT = pl.pallas_call(
            kern,
            grid=(NB, 1),
            in_specs=[pl.BlockSpec((G, IN_AB), lambda b, i: (b, 0)),
                      pl.BlockSpec((G, D_BPROJ), lambda b, i: (b, 0)), st_spec] + [full(s) for s in small],
            out_specs=[pl.BlockSpec((G, D_B), lambda b, i: (b, 0)), st_spec],
            out_shape=[jax.ShapeDtypeStruct((B, D_B), F32), st_shape],
            scratch_shapes=[pltpu.VMEM(st_dims, F32), pltpu.VMEM((G, D_B), F32)],
            compiler_params=_cparams(("parallel", "arbitrary")),
        )(proj.reshape(B, IN_AB), shift0, s0l, *small)
        return y.reshape(B, 1, D_B), unpair(sT)
    T = min(T_CHUNK, L)
    TL = min(TL_BLOCK, L)
    NC = TL // T
    kern = functools.partial(_rwkv_kernel, G, T, NC, True)
    y, sT = pl.pallas_call(
        kern,
        grid=(NB, L // TL),
        in_specs=[pl.BlockSpec((G, TL, IN_AB), lambda b, i: (b, i, 0)),
                  pl.BlockSpec((G, 1, D_BPROJ), lambda b, i: (b, 0, 0)), st_spec] + [full(s) for s in small],
        out_specs=[pl.BlockSpec((G, TL, D_B), lambda b, i: (b, i, 0)), st_spec],
        out_shape=[jax.ShapeDtypeStruct((B, L, D_B), BF16), st_shape],
        scratch_shapes=[pltpu.VMEM(st_dims, F32),
                        pltpu.VMEM((G, SUBLANES + TL, D_BPROJ), F32),
                        pltpu.VMEM((6, G, TL, D_B), F32),
                        pltpu.VMEM((2, G, TL, D_B), F32),
                        pltpu.VMEM((G, TL, D_B), F32)],
        compiler_params=_cparams(("parallel", "arbitrary")),
    )(proj, shift0.reshape(B, 1, D_BPROJ), s0l, *small)
    return y, unpair(sT)


CD_Z, CD_XBC, CD_Q, CD_K, CD_V, CD_G, CD_S = 0, 512, 1536, 1792, 2048, 2560, 3072


def _ssd_chunk(G, T, xs, Bm, Cm, dt_all, la_all, st_ref, masks, emaskN, store_y):
    incl, lmask_bf, lastmask_bf = masks
    R = G * T
    cum = _mm_sel(lmask_bf, la_all, 3)
    cend = _mm_sel(lastmask_bf, cum, 3)
    cumT = cum.T
    wend = jnp.exp(cend - cum) * dt_all
    ecum = jnp.exp(cum)
    dend = jnp.exp(cend)
    lo = lax.broadcasted_iota(jnp.int32, (R, LANES), 1) < P_C

    def per_head(x):
        return jnp.concatenate([jnp.where(lo, x[:, 2 * c:2 * c + 1], x[:, 2 * c + 1:2 * c + 2])
                                for c in range(H_C // 2)], axis=1)

    tile = lambda x: jnp.where(emaskN, jnp.concatenate([x] * G, axis=1), 0.0)
    xdt = xs * per_head(dt_all)
    xw = xs * per_head(wend)
    ecf = per_head(ecum)
    Bg = [Bm[:, g * N_C:(g + 1) * N_C] for g in range(G_C)]
    Cg = [Cm[:, g * N_C:(g + 1) * N_C] for g in range(G_C)]
    CB = [_mm(c, b, NT) for c, b in zip(Cg, Bg)]
    ST = [st_ref[g] for g in range(G_C)]
    CS = [_mm(tile(c), s, NT) for c, s in zip(Cg, ST)]
    W = [CB[h // HG_C] * jnp.exp(jnp.where(incl, cum[:, h:h + 1] - cumT[h:h + 1, :], MASKED_LOG))
         for h in range(H_C)]
    intra = [_mm(W[h], xdt[:, (h // 2) * LANES:(h // 2 + 1) * LANES]) for h in range(H_C)]
    drow = [jnp.sum(jnp.where(emaskN, dend[:, h:h + 1], 0.0), axis=0, keepdims=True) * (1.0 / T)
            for h in range(H_C)]
    for c in range(H_C // 2):
        sl = slice(c * LANES, (c + 1) * LANES)
        gsl = slice((c % (HG_C // 2)) * LANES, (c % (HG_C // 2) + 1) * LANES)
        store_y(c, jnp.where(lo, intra[2 * c], intra[2 * c + 1]) + ecf[:, sl] * CS[c // (HG_C // 2)][:, gsl])
    for g in range(G_C):
        W_g = HG_C * P_C
        dS = _mm(xw[:, g * W_g:(g + 1) * W_g], tile(Bg[g]), TN)
        st_ref[g] = jnp.concatenate(
            [ST[g][hh * P_C:(hh + 1) * P_C] * drow[g * HG_C + hh] + dS[hh * P_C:(hh + 1) * P_C]
             for hh in range(HG_C)], axis=0)


def _gla_chunk(G, T, q, k, v, la, st_ref, masks, emask, store_o):
    incl, lmask_bf, lastmask_bf = masks
    R = G * T
    bc = _mm_sel(lmask_bf, la, 3)
    bend = _mm_sel(lastmask_bf, bc, 3)
    qe = q * jnp.exp(bc)
    kn = k * jnp.exp(-bc)
    kw = k * jnp.exp(bend - bc)
    dend = jnp.exp(bend)
    lo = lax.broadcasted_iota(jnp.int32, (R, LANES), 1) < DK_D
    own = {0: lo, 1: ~lo}
    n_rep = G * DK_D // LANES

    def tiled(x, xr, hh):
        t = jnp.where(lo, x, xr) if hh == 0 else jnp.where(lo, xr, x)
        return jnp.where(emask, jnp.concatenate([t] * n_rep, axis=1), 0.0)

    heads = [(c, hh) for c in range(H_D // 2) for hh in range(2)]
    pair = []
    for c in range(H_D // 2):
        sl = slice(c * LANES, (c + 1) * LANES)
        vals = (qe[:, sl], kw[:, sl], dend[:, sl])
        pair.append(dict(x=vals, xr=[pltpu.roll(x, DK_D, 1) for x in vals], kn=kn[:, sl]))
    vh = [v[:, h * DV_D:(h + 1) * DV_D] for h in range(H_D)]
    att = [jnp.where(incl, _mm(jnp.where(own[hh], pair[c]['x'][0], 0.0), pair[c]['kn'], NT), 0.0)
           for c, hh in heads]
    ex = [[tiled(x, xr, hh) for x, xr in zip(pair[c]['x'], pair[c]['xr'])] for c, hh in heads]
    ST = [st_ref[h] for h in range(H_D)]
    o = [_mm(a, vv) + _mm(e[0], s, NT) for a, vv, e, s in zip(att, vh, ex, ST)]
    for h in range(H_D):
        drow = jnp.sum(ex[h][2], axis=0, keepdims=True) * (1.0 / T)
        st_ref[h] = ST[h] * drow + _mm(vh[h], ex[h][1], TN)
        store_o(h, o[h])


def _cd_prep_small(small, dtb, a_neg, aup, ab):
    dt_all = _softplus(small + dtb)
    la_all = dt_all * a_neg
    la_gla = -_softplus(-(_mm(small, aup) + ab)) * (1.0 / GATE_NORM)
    return dt_all, la_all, la_gla


def _cd_post(y, xs, z, o, gg, dskip, ssd_norm, gla_norm):
    y_c = _rms((y + dskip * xs) * _silu(z), ssd_norm)
    outs = [y_c]
    for h in range(H_D):
        sl = slice(h * DV_D, (h + 1) * DV_D)
        outs.append(_rms(o[:, sl], gla_norm) * _silu(gg[:, sl]))
    return outs


def _cd_kernel(G, T, NC, seq, proj_ref, cs_ref, ssd0_ref, gla0_ref, cw_ref, cb_ref, dtb_ref, aneg_ref, dskip_ref,
               ssdn_ref, aup_ref, ab_ref, glan_ref, y_ref, ssdT_ref, glaT_ref, *scratch):
    ti = pl.program_id(1)
    TL = T * NC
    R = G * T
    sst_ref, gst_ref = scratch[0], scratch[1]

    @pl.when(ti == 0)
    def _():
        for h in range(H_C):
            r0 = (h % HG_C) * P_C
            for b in range(G):
                sst_ref[h // HG_C, r0:r0 + P_C, b * N_C:(b + 1) * N_C] = ssd0_ref[b, h]
        for h in range(H_D):
            for j in range(G // 2):
                two = jnp.concatenate([gla0_ref[2 * j, h], gla0_ref[2 * j + 1, h]], axis=0)
                gst_ref[h, :, j * LANES:(j + 1) * LANES] = two.T

    cw = cw_ref[...]
    if seq:
        pad_ref, xbc_ref, sm_ref, ys_ref, os_ref = scratch[2:]

        @pl.when(ti == 0)
        def _():
            pad_ref[:, SUBLANES - (CONV_W - 1):SUBLANES, :] = cs_ref[...]

        pad_ref[:, SUBLANES:, :] = proj_ref[:, :, CD_XBC:CD_XBC + D_XBC].astype(F32)

        def conv_b(b, carry):
            for c0 in range(0, D_XBC, CONV_COLS):
                cols = slice(c0, c0 + CONV_COLS)
                acc = cb_ref[:, cols]
                for kq in range(CONV_W):
                    off = SUBLANES - (CONV_W - 1) + kq
                    acc = acc + pad_ref[b, off:off + TL, cols] * cw_ref[kq:kq + 1, cols]
                xbc_ref[b, :, cols] = _silu(acc)
            return carry

        lax.fori_loop(0, G, conv_b, 0)
        pad_ref[:, :SUBLANES, :] = pad_ref[:, TL:TL + SUBLANES, :]
        small = proj_ref[:, :, CD_S:].astype(F32).reshape(G * TL, LANES)
    else:
        x = proj_ref[:, CD_XBC:CD_XBC + D_XBC]
        xc = cb_ref[...] + x * cw[CONV_W - 1:CONV_W, :]
        for kq in range(CONV_W - 1):
            xc = xc + cs_ref[kq] * cw[kq:kq + 1, :]
        xbc = _silu(xc)
        small = proj_ref[:, CD_S:]

    dt_all, la_all, la_gla = _cd_prep_small(small, dtb_ref[...], aneg_ref[...], aup_ref[...], ab_ref[...])

    incl, _, last = _chunk_masks(G, T)
    masks = (incl, _to_bf16_mask(incl), _to_bf16_mask(last))
    emaskN = _batch_lane_mask(1, G, T, N_C)
    emask1 = _batch_lane_mask(1, G, T, DK_D)
    scale = DK_D ** -0.5

    if seq:
        sm_ref[0] = dt_all.reshape(G, TL, LANES)
        sm_ref[1] = la_all.reshape(G, TL, LANES)
        sm_ref[2] = la_gla[:, :LANES].reshape(G, TL, LANES)
        sm_ref[3] = la_gla[:, LANES:].reshape(G, TL, LANES)

        def chunk(c, carry):
            t0 = pl.multiple_of(c * T, T)
            rows = lambda ref, lo, hi: ref[:, pl.ds(t0, T), lo:hi].astype(F32).reshape(R, hi - lo)
            xs = rows(xbc_ref, 0, D_C)
            Bm = rows(xbc_ref, D_C, D_C + G_C * N_C)
            Cm = rows(xbc_ref, D_C + G_C * N_C, D_XBC)
            dtc = sm_ref[0, :, pl.ds(t0, T), :].reshape(R, LANES)
            lac = sm_ref[1, :, pl.ds(t0, T), :].reshape(R, LANES)

            def store_y(hp, yv):
                ys_ref[:, pl.ds(t0, T), hp * LANES:(hp + 1) * LANES] = yv.reshape(G, T, LANES)

            _ssd_chunk(G, T, xs, Bm, Cm, dtc, lac, sst_ref, masks, emaskN, store_y)

            q = rows(proj_ref, CD_Q, CD_Q + K_D) * scale
            k = rows(proj_ref, CD_K, CD_K + K_D)
            v = rows(proj_ref, CD_V, CD_V + V_D)
            lag = jnp.concatenate([sm_ref[2, :, pl.ds(t0, T), :].reshape(R, LANES),
                                   sm_ref[3, :, pl.ds(t0, T), :].reshape(R, LANES)], axis=1)

            def store_o(h, ov):
                os_ref[:, pl.ds(t0, T), h * DV_D:(h + 1) * DV_D] = ov.reshape(G, T, DV_D)

            _gla_chunk(G, T, q, k, v, lag, gst_ref, masks, emask1, store_o)
            return carry

        lax.fori_loop(0, NC, chunk, 0)
        def post_b(b, carry):
            outs = _cd_post(ys_ref[b], xbc_ref[b, :, :D_C], proj_ref[b, :, CD_Z:CD_Z + D_C].astype(F32),
                            os_ref[b], proj_ref[b, :, CD_G:CD_G + V_D].astype(F32),
                            dskip_ref[...], ssdn_ref[...], glan_ref[...])
            y_ref[b, :, :D_C] = outs[0].astype(y_ref.dtype)
            for h in range(H_D):
                y_ref[b, :, D_C + h * DV_D:D_C + (h + 1) * DV_D] = outs[1 + h].astype(y_ref.dtype)
            return carry

        lax.fori_loop(0, G, post_b, 0)
    else:
        ys_ref, os_ref = scratch[2:]

        def store_y(hp, yv):
            ys_ref[:, hp * LANES:(hp + 1) * LANES] = yv

        def store_o(h, ov):
            os_ref[:, h * DV_D:(h + 1) * DV_D] = ov

        _ssd_chunk(G, T, xbc[:, :D_C], xbc[:, D_C:D_C + G_C * N_C], xbc[:, D_C + G_C * N_C:], dt_all, la_all,
                   sst_ref, masks, emaskN, store_y)
        _gla_chunk(G, T, proj_ref[:, CD_Q:CD_Q + K_D] * scale, proj_ref[:, CD_K:CD_K + K_D],
                   proj_ref[:, CD_V:CD_V + V_D], la_gla, gst_ref, masks, emask1, store_o)
        outs = _cd_post(ys_ref[...], xbc[:, :D_C], proj_ref[:, CD_Z:CD_Z + D_C], os_ref[...],
                        proj_ref[:, CD_G:CD_G + V_D], dskip_ref[...], ssdn_ref[...], glan_ref[...])
        y_ref[:, :D_C] = outs[0]
        for h in range(H_D):
            y_ref[:, D_C + h * DV_D:D_C + (h + 1) * DV_D] = outs[1 + h]

    @pl.when(ti == pl.num_programs(1) - 1)
    def _():
        for h in range(H_C):
            r0 = (h % HG_C) * P_C
            for b in range(G):
                ssdT_ref[b, h] = sst_ref[h // HG_C, r0:r0 + P_C, b * N_C:(b + 1) * N_C]
        for h in range(H_D):
            for j in range(G // 2):
                two = gst_ref[h, :, j * LANES:(j + 1) * LANES].T
                glaT_ref[2 * j, h] = two[:DK_D]
                glaT_ref[2 * j + 1, h] = two[DK_D:]


def _cd(proj, conv_state, ssd0, gla0, cw, cb, dtb, a_neg, dskip, ssd_norm, aup, ab, gla_norm):
    B, L, _ = proj.shape
    G = G_ROWS if L > 1 else min(G_ROWS_STEP, B)
    NB = B // G
    small = [cw, cb, dtb, a_neg, dskip, ssd_norm, aup, ab, gla_norm]
    full = lambda s: pl.BlockSpec(s.shape, lambda b, i: (0,) * s.ndim)
    ssd_dims = (G_C, HG_C * P_C, G * N_C)
    ssd0l, gla0l = ssd0, gla0
    ssd_spec = pl.BlockSpec((G, H_C, P_C, N_C), lambda b, i: (b, 0, 0, 0))
    gla_spec = pl.BlockSpec((G, H_D, DK_D, DV_D), lambda b, i: (b, 0, 0, 0))
    st_shapes = [jax.ShapeDtypeStruct(ssd0.shape, F32), jax.ShapeDtypeStruct(gla0.shape, F32)]
    st_scratch = [pltpu.VMEM(ssd_dims, F32), pltpu.VMEM((H_D, DV_D, G * DK_D), F32)]
    DO = D_C + V_D
    if L == 1:
        kern = functools.partial(_cd_kernel, G, 1, 1, False)
        y, ssdT, glaT = pl.pallas_call(
            kern,
            grid=(NB, 1),
            in_specs=[pl.BlockSpec((G, IN_CD_PAD), lambda b, i: (b, 0)),
                      pl.BlockSpec((CONV_W - 1, G, D_XBC), lambda b, i: (0, b, 0)),
                      ssd_spec, gla_spec] + [full(s) for s in small],
            out_specs=[pl.BlockSpec((G, DO), lambda b, i: (b, 0)), ssd_spec, gla_spec],
            out_shape=[jax.ShapeDtypeStruct((B, DO), F32)] + st_shapes,
            scratch_shapes=st_scratch + [pltpu.VMEM((G, D_C), F32), pltpu.VMEM((G, V_D), F32)],
            compiler_params=_cparams(("parallel", "arbitrary")),
        )(proj.reshape(B, IN_CD_PAD), jnp.transpose(conv_state, (1, 0, 2)), ssd0l, gla0l, *small)
        y = y.reshape(B, 1, DO)
    else:
        T = min(T_CHUNK, L)
        TL = min(TL_BLOCK, L)
        NC = TL // T
        kern = functools.partial(_cd_kernel, G, T, NC, True)
        y, ssdT, glaT = pl.pallas_call(
            kern,
            grid=(NB, L // TL),
            in_specs=[pl.BlockSpec((G, TL, IN_CD_PAD), lambda b, i: (b, i, 0)),
                      pl.BlockSpec((G, CONV_W - 1, D_XBC), lambda b, i: (b, 0, 0)),
                      ssd_spec, gla_spec] + [full(s) for s in small],
            out_specs=[pl.BlockSpec((G, TL, DO), lambda b, i: (b, i, 0)), ssd_spec, gla_spec],
            out_shape=[jax.ShapeDtypeStruct((B, L, DO), BF16)] + st_shapes,
            scratch_shapes=st_scratch + [pltpu.VMEM((G, SUBLANES + TL, D_XBC), F32),
                                         pltpu.VMEM((G, TL, D_XBC), F32),
                                         pltpu.VMEM((4, G, TL, LANES), F32),
                                         pltpu.VMEM((G, TL, D_C), F32),
                                         pltpu.VMEM((G, TL, V_D), F32)],
            compiler_params=_cparams(("parallel", "arbitrary")),
        )(proj, conv_state, ssd0l, gla0l, *small)
    return y, ssdT, glaT


def _regroup_kernel(groups, w_ref, o_ref):
    at = 0
    for lo, hi in groups:
        o_ref[:, at:at + hi - lo] = w_ref[:, lo:hi].astype(o_ref.dtype)
        at += hi - lo
    o_ref[:, at:] = jnp.zeros((o_ref.shape[0], o_ref.shape[1] - at), o_ref.dtype)


def _regroup_cols(w, groups, n_out):
    K = w.shape[0]
    return pl.pallas_call(
        functools.partial(_regroup_kernel, groups),
        out_shape=jax.ShapeDtypeStruct((K, n_out), BF16),
        compiler_params=pltpu.CompilerParams(vmem_limit_bytes=VMEM_LIMIT),
    )(w)


def _block_diag(w):
    eye = jnp.eye(N_BLK_A, dtype=w.dtype)
    return jnp.einsum('nij,nm->nimj', w, eye).reshape(D_A, D_A)


def _row(v):
    return v.reshape(1, -1).astype(F32)


def _prep_params(p):
    q = {}
    j = 0
    q['w_in_ab'] = p['w_in_ab'][j].astype(BF16)
    q['w_out_ab'] = p['w_out_ab'][j].astype(BF16)
    q['lru'] = [p['lru_conv_w'][j], _row(p['lru_conv_b'][j]),
                _block_diag(p['lru_w_r'][j]).astype(BF16), _row(p['lru_b_r'][j]),
                _block_diag(p['lru_w_i'][j]).astype(BF16), _row(p['lru_b_i'][j]), _row(p['lru_lambda'][j])]
    zw = jnp.zeros((R_W, D_B), F32)
    w2p = jnp.concatenate([p['rwkv_w2'][j], zw], axis=0).astype(BF16)
    a2p = jnp.concatenate([zw, p['rwkv_a2'][j]], axis=0).astype(BF16)
    q['rwkv'] = [_row(p['rwkv_mu'][j]), _row(p['rwkv_w0'][j]), w2p, _row(p['rwkv_a0'][j]), a2p,
                 p['rwkv_g2'][j].astype(BF16), _row(p['rwkv_k_k'][j]), _row(p['rwkv_k_a'][j]),
                 _row(p['rwkv_r_k'][j]), _row(p['rwkv_ln_w'][j]), _row(p['rwkv_ln_b'][j])]
    q['ffn'] = [p['ffn_w_gate'][j].astype(BF16), p['ffn_w_up'][j].astype(BF16), p['ffn_w_down'][j].astype(BF16)]
    w = p['w_in_cd'][j]
    o_z, o_xbc, o_dt = 0, D_C, D_C + D_XBC
    o_q = o_dt + H_C
    o_k = o_q + K_D
    o_v = o_k + K_D
    o_g = o_v + V_D
    o_al = o_g + V_D
    q['w_in_cd'] = _regroup_cols(w, [(o_z, o_dt), (o_q, o_al), (o_dt, o_q), (o_al, o_al + R_ALPHA)], IN_CD_PAD)
    q['w_out_cd'] = p['w_out_cd'][j].astype(BF16)
    lane_pad = lambda v: jnp.concatenate([v.astype(F32), jnp.zeros((LANES - v.shape[0],), F32)]).reshape(1, LANES)
    aup = jnp.zeros((LANES, K_D), F32).at[H_C:H_C + R_ALPHA].set(p['gla_alpha_up'][j]).astype(BF16)
    q['cd'] = [p['ssd_conv_w'][j], _row(p['ssd_conv_b'][j]), lane_pad(p['ssd_dt_bias'][j]),
               lane_pad(-jnp.exp(p['ssd_a_log'][j].astype(F32))), _row(jnp.repeat(p['ssd_d'][j], P_C)),
               _row(p['ssd_norm_w'][j]), aup, _row(p['gla_alpha_b'][j]), _row(p['gla_norm_w'][j])]
    q['router'] = jnp.concatenate([p['moe_router'][j].T, jnp.zeros((LANES - N_EXP, D_MODEL), F32)], axis=0)
    q['moe'] = [jnp.concatenate([p['moe_w_gate'][j], p['moe_w_up'][j]], axis=2).astype(BF16),
                p['moe_w_down'][j].astype(BF16)]
    q['norm_mix'] = [_row(p['norm_mix'][i]) for i in range(DEPTH)]
    q['norm_ffn'] = [_row(p['norm_ffn'][i]) for i in range(DEPTH)]
    q['norm_ple'] = [_row(p['norm_ple'][i]) for i in range(DEPTH)]
    q['w_ple_gate'] = [p['w_ple_gate'][i].astype(BF16) for i in range(DEPTH)]
    q['w_ple_proj'] = [p['w_ple_proj'][i].astype(BF16) for i in range(DEPTH)]
    q['norm_final'] = _row(p['norm_final'])
    return q


def _trunk(x, p, states, q):
    lru_conv, lru_h, rwkv_shift, rwkv_wkv, ssd_conv, ssd_st, gla_st = states
    B, L, D = x.shape
    M = B * L
    h = x.reshape(M, D)
    p2 = p.reshape(DEPTH, M, D_PLE)

    proj_dtype = BF16 if L > 1 else F32
    proj = _norm_mm(h, q['norm_mix'][0], q['w_in_ab'], IN_AB, proj_dtype).reshape(B, L, IN_AB)
    y_a, h_new = _lru(proj, lru_conv[0], lru_h[0], *q['lru'])
    y_b, wkv_new = _rwkv(proj, rwkv_shift[0], rwkv_wkv[0], *q['rwkv'])
    if L >= CONV_W - 1:
        conv_new = proj[:, L - (CONV_W - 1):, :D_A].astype(F32)
    else:
        conv_new = jnp.concatenate([lru_conv[0][:, L:], proj[:, :, :D_A]], axis=1)
    shift_new = proj[:, L - 1, 2 * D_A:].astype(F32)
    w_out = q['w_out_ab']
    h = _channel_mix(h, y_a.reshape(M, D_A), y_b.reshape(M, D_B), w_out[:D_A], w_out[D_A:],
                     q['norm_ffn'][0], *q['ffn'], q['norm_ple'][0], q['w_ple_gate'][0], p2, 0, q['w_ple_proj'][0])

    proj = _norm_mm(h, q['norm_mix'][1], q['w_in_cd'], IN_CD_PAD, proj_dtype).reshape(B, L, IN_CD_PAD)
    y_cd, ssd_new, gla_new = _cd(proj, ssd_conv[0], ssd_st[0], gla_st[0], *q['cd'])
    if L >= CONV_W - 1:
        sconv_new = proj[:, L - (CONV_W - 1):, CD_XBC:CD_XBC + D_XBC].astype(F32)
    else:
        sconv_new = jnp.concatenate([ssd_conv[0][:, L:], proj[:, :, CD_XBC:CD_XBC + D_XBC]], axis=1)
    h, route, routeT, meta = _router(h, y_cd.reshape(M, D_C + V_D), q['w_out_cd'], q['norm_ffn'][1], q['router'])
    h = _moe(h, q['norm_ffn'][1], route, routeT, meta, *q['moe'])
    y = _ple(h, q['norm_ple'][1], q['w_ple_gate'][1], p2, 1, q['w_ple_proj'][1], q['norm_final'], True)

    new_states = (conv_new[None], h_new[None], shift_new[None], wkv_new[None],
                  sconv_new[None], ssd_new[None], gla_new[None])
    return y.reshape(B, L, D), new_states


def kernel(x_prompt, x_sample, p_prompt, p_sample, state_lru_conv, state_lru_h, state_rwkv_shift,
           state_rwkv_wkv, state_ssd_conv, state_ssd, state_gla, norm_mix, norm_ffn, norm_ple,
           w_ple_gate, w_ple_proj, norm_final, w_in_ab, w_out_ab, lru_conv_w, lru_conv_b, lru_w_r,
           lru_b_r, lru_w_i, lru_b_i, lru_lambda, rwkv_mu, rwkv_w0, rwkv_w2, rwkv_a0, rwkv_a2,
           rwkv_g2, rwkv_k_k, rwkv_k_a, rwkv_r_k, rwkv_ln_w, rwkv_ln_b, ffn_w_gate, ffn_w_up,
           ffn_w_down, w_in_cd, w_out_cd, ssd_conv_w, ssd_conv_b, ssd_dt_bias, ssd_a_log, ssd_d,
           ssd_norm_w, gla_alpha_up, gla_alpha_b, gla_norm_w, moe_router, moe_w_gate, moe_w_up,
           moe_w_down):
    prm = dict(
        norm_mix=norm_mix, norm_ffn=norm_ffn, norm_ple=norm_ple, w_ple_gate=w_ple_gate,
        w_ple_proj=w_ple_proj, norm_final=norm_final, w_in_ab=w_in_ab, w_out_ab=w_out_ab,
        lru_conv_w=lru_conv_w, lru_conv_b=lru_conv_b, lru_w_r=lru_w_r, lru_b_r=lru_b_r,
        lru_w_i=lru_w_i, lru_b_i=lru_b_i, lru_lambda=lru_lambda, rwkv_mu=rwkv_mu, rwkv_w0=rwkv_w0,
        rwkv_w2=rwkv_w2, rwkv_a0=rwkv_a0, rwkv_a2=rwkv_a2, rwkv_g2=rwkv_g2, rwkv_k_k=rwkv_k_k,
        rwkv_k_a=rwkv_k_a, rwkv_r_k=rwkv_r_k, rwkv_ln_w=rwkv_ln_w, rwkv_ln_b=rwkv_ln_b,
        ffn_w_gate=ffn_w_gate, ffn_w_up=ffn_w_up, ffn_w_down=ffn_w_down, w_in_cd=w_in_cd,
        w_out_cd=w_out_cd, ssd_conv_w=ssd_conv_w, ssd_conv_b=ssd_conv_b, ssd_dt_bias=ssd_dt_bias,
        ssd_a_log=ssd_a_log, ssd_d=ssd_d, ssd_norm_w=ssd_norm_w, gla_alpha_up=gla_alpha_up,
        gla_alpha_b=gla_alpha_b, gla_norm_w=gla_norm_w, moe_router=moe_router,
        moe_w_gate=moe_w_gate, moe_w_up=moe_w_up, moe_w_down=moe_w_down)
    q = _prep_params(prm)
    states_s = (state_lru_conv, state_lru_h, state_rwkv_shift, state_rwkv_wkv,
                state_ssd_conv, state_ssd, state_gla)
    n_prompt = x_prompt.shape[0]
    states_p = tuple(jnp.zeros((s.shape[0], n_prompt) + s.shape[2:], s.dtype) for s in states_s)
    y_p, st_p = _trunk(x_prompt, p_prompt, states_p, q)
    y_s, st_s = _trunk(x_sample, p_sample, states_s, q)
    return (y_p, y_s) + tuple(st_p) + tuple(st_s)
```
